```python
import math
import jax, jax.numpy as jnp
from jax import lax
import numpy as np

D_MODEL = 1024
BATCH = 2
SEQ = 8192
DEPTH = 1

GRID_W = 64
MEM_LEN = 256
MEM_HEADS = 4
MEM_HEAD_DIM = 128
HEAD_DIM = 64
NA_HEADS = 8
NA_WIN_ROWS = 8
NA_WIN_COLS = 16
DIL_GROUPS = ((128, 1), (512, 4), (2048, 16))
DIL_HEADS_PER_GROUP = 4
DIL_HEADS = DIL_HEADS_PER_GROUP * len(DIL_GROUPS)
DIL_BLOCK = 128
T5_BUCKETS = 32
T5_MAX_DIST = 1024
N_EXPERTS = 32
TOP_K = 4
D_FF = D_MODEL
SWIGLU_ALPHA = 1.702
SWIGLU_LIMIT = 7.0
MOE_BLOCK = 128
RMS_EPS = 1e-6
NEG_INF = -1e30

N_BRANCH = 3
WIDTH_A = NA_HEADS * HEAD_DIM
WIDTH_B = DIL_HEADS * HEAD_DIM
WIDTH_B_OUT = DIL_HEADS_PER_GROUP * HEAD_DIM
WIDTH_C = MEM_HEADS * MEM_HEAD_DIM
IN_COLS = 3 * WIDTH_A + 3 * WIDTH_B + WIDTH_C + N_BRANCH * D_MODEL

kernel_name = "hybrid_gated_na_dilated_mem_moe_encoder"


def rms_norm(x, g):
    xf = x.astype(jnp.float32)
    y = xf * lax.rsqrt(jnp.mean(xf * xf, axis=-1, keepdims=True) + RMS_EPS)
    return (y * g.astype(jnp.float32)).astype(x.dtype)


def neighbourhood_attention(q, k, v, rpb):
    B, S, H, dh = q.shape
    rows = S // GRID_W
    wr = min(NA_WIN_ROWS, rows)
    wc = NA_WIN_COLS
    qg = q.reshape(B, rows, GRID_W, H, dh)
    kg = k.reshape(B, rows, GRID_W, H, dh)
    vg = v.reshape(B, rows, GRID_W, H, dh)
    col = jnp.arange(GRID_W)
    col_start = jnp.clip(col - wc // 2, 0, GRID_W - wc)
    col_idx = col_start[:, None] + jnp.arange(wc)[None, :]
    dc = col_idx - col[:, None]
    scale = dh ** -0.5

    def one_row(r):
        rs = jnp.clip(r - wr // 2, 0, rows - wr)
        k_rows = lax.dynamic_slice_in_dim(kg, rs, wr, axis=1)
        v_rows = lax.dynamic_slice_in_dim(vg, rs, wr, axis=1)
        k_win = k_rows[:, :, col_idx]
        v_win = v_rows[:, :, col_idx]
        q_r = lax.dynamic_index_in_dim(qg, r, axis=1, keepdims=False)
        s = jnp.einsum('bqhd,bnqchd->bhqnc', q_r, k_win).astype(jnp.float32) * scale
        dr = rs + jnp.arange(wr) - r
        bias = rpb[:, dr[None, :, None] + NA_WIN_ROWS - 1, dc[:, None, :] + NA_WIN_COLS - 1]
        s = s + bias[None].astype(jnp.float32)
        p = jax.nn.softmax(s.reshape(B, H, GRID_W, wr * wc), axis=-1)
        p = p.reshape(B, H, GRID_W, wr, wc).astype(v.dtype)
        return jnp.einsum('bhqnc,bnqchd->bqhd', p, v_win)

    out = lax.map(one_row, jnp.arange(rows))
    return out.transpose(1, 0, 2, 3, 4).reshape(B, S, H * dh)


def t5_bucket(rel):
    half = T5_BUCKETS // 2
    max_exact = half // 2
    ret = jnp.where(rel > 0, half, 0)
    n = jnp.abs(rel)
    nf = jnp.maximum(n, 1).astype(jnp.float32)
    large = max_exact + (jnp.log(nf / max_exact) / math.log(T5_MAX_DIST / max_exact)
                         * (half - max_exact)).astype(jnp.int32)
    large = jnp.minimum(large, half - 1)
    return ret + jnp.where(n < max_exact, n, large)


def dilated_group_attention(q, k, v, t5_bias, window, dil):
    B, S, H, dh = q.shape
    half = (window // 2) // dil
    L = S // dil
    def split(t):
        return t.reshape(B, L, dil, H, dh).transpose(0, 2, 1, 3, 4)
    qs, ks, vs = split(q), split(k), split(v)
    nb = -(-L // DIL_BLOCK)
    Lp = nb * DIL_BLOCK
    nk = DIL_BLOCK + 2 * half
    qs = jnp.pad(qs, ((0, 0), (0, 0), (0, Lp - L), (0, 0), (0, 0)))
    kp = jnp.pad(ks, ((0, 0), (0, 0), (half, Lp - L + half), (0, 0), (0, 0)))
    vp = jnp.pad(vs, ((0, 0), (0, 0), (half, Lp - L + half), (0, 0), (0, 0)))
    key_idx = jnp.arange(nb)[:, None] * DIL_BLOCK + jnp.arange(nk)[None, :]
    kb = kp[:, :, key_idx]
    vb = vp[:, :, key_idx]
    qb = qs.reshape(B, dil, nb, DIL_BLOCK, H, dh)
    s = jnp.einsum('brnqhd,brnkhd->brnhqk', qb, kb).astype(jnp.float32) * (dh ** -0.5)
    qi = jnp.arange(nb)[:, None] * DIL_BLOCK + jnp.arange(DIL_BLOCK)[None, :]
    ki = key_idx - half
    rel = ki[:, None, :] - qi[:, :, None]
    valid = (jnp.abs(rel) <= half) & (ki[:, None, :] >= 0) & (ki[:, None, :] < L)
    bias = t5_bias[t5_bucket(rel * dil)].astype(jnp.float32)
    s = s + bias.transpose(0, 3, 1, 2)[None, None]
    s = jnp.where(valid[None, None, :, None], s, NEG_INF)
    lse = jax.nn.logsumexp(s, axis=-1)
    p = jnp.exp(s - lse[..., None]).astype(v.dtype)
    o = jnp.einsum('brnhqk,brnkhd->brnqhd', p, vb)
    o = o.reshape(B, dil, Lp, H, dh)[:, :, :L].transpose(0, 2, 1, 3, 4).reshape(B, S, H, dh)
    lse = lse.transpose(0, 1, 2, 4, 3).reshape(B, dil, Lp, H)[:, :, :L]
    lse = lse.transpose(0, 2, 1, 3).reshape(B, S, H)
    return o, lse


def dilated_mixture(q, k, v, t5_bias):
    B, S, _, dh = q.shape
    G = DIL_HEADS_PER_GROUP
    outs, lses = [], []
    for gi, (window, dil) in enumerate(DIL_GROUPS):
        sl = slice(gi * G, (gi + 1) * G)
        o, l = dilated_group_attention(q[:, :, sl], k[:, :, sl], v[:, :, sl],
                                       t5_bias[:, sl], window, dil)
        outs.append(o)
        lses.append(l)
    w = jax.nn.softmax(jnp.stack(lses, axis=0), axis=0).astype(q.dtype)
    o = jnp.sum(w[..., None] * jnp.stack(outs, axis=0), axis=0)
    return o.reshape(B, S, G * dh)


def memory_attention(q, mk, mv):
    B, S, H, dh = q.shape
    s = jnp.einsum('bshd,bmhd->bhsm', q, mk).astype(jnp.float32) * (dh ** -0.5)
    p = jax.nn.softmax(s, axis=-1).astype(mv.dtype)
    return jnp.einsum('bhsm,bmhd->bshd', p, mv).reshape(B, S, H * dh)


def moe_ffn(h, router_w, router_b, w_gu, b_gu, w_dn, b_dn):
    N, D = h.shape
    logits = (h @ router_w).astype(jnp.float32) + router_b.astype(jnp.float32)
    top_val, top_idx = lax.top_k(logits, TOP_K)
    top_w = jax.nn.softmax(top_val, axis=-1)
    A = N * TOP_K
    e_flat = top_idx.reshape(-1)
    tok_flat = jnp.repeat(jnp.arange(N, dtype=jnp.int32), TOP_K)
    w_flat = top_w.reshape(-1)
    order = jnp.argsort(e_flat)
    e_sorted = e_flat[order]
    counts = jnp.bincount(e_flat, length=N_EXPERTS)
    starts = jnp.cumsum(counts) - counts
    padded = (counts + MOE_BLOCK - 1) // MOE_BLOCK * MOE_BLOCK
    pad_ends = jnp.cumsum(padded)
    pad_starts = pad_ends - padded
    dest = pad_starts[e_sorted] + (jnp.arange(A) - starts[e_sorted])
    n_slots = (-(-A // MOE_BLOCK)) * MOE_BLOCK + N_EXPERTS * MOE_BLOCK
    n_blocks = n_slots // MOE_BLOCK
    slot_tok = jnp.zeros((n_slots,), jnp.int32).at[dest].set(tok_flat[order])
    slot_w = jnp.zeros((n_slots,), h.dtype).at[dest].set(w_flat[order].astype(h.dtype))
    block_expert = jnp.minimum(
        jnp.searchsorted(pad_ends, jnp.arange(n_blocks) * MOE_BLOCK, side='right'), N_EXPERTS - 1)
    xs = h[slot_tok].reshape(n_blocks, MOE_BLOCK, D)

    def expert_block(args):
        xb, e = args
        gu = xb @ w_gu[e] + b_gu[e]
        g = jnp.minimum(gu[:, :D_FF], SWIGLU_LIMIT)
        u = jnp.clip(gu[:, D_FF:], -SWIGLU_LIMIT, SWIGLU_LIMIT)
        a = (u + 1.0) * (g * jax.nn.sigmoid(SWIGLU_ALPHA * g))
        return a @ w_dn[e] + b_dn[e]

    ys = lax.map(expert_block, (xs, block_expert)).reshape(n_slots, D)
    return jnp.zeros_like(h).at[slot_tok].add(ys * slot_w[:, None])


def setup_inputs(seed: int = 0) -> dict:
    key = jax.random.key(seed)
    ks = jax.random.split(key, 24)
    f32 = jnp.float32
    nrm = lambda k, shape, s: jax.random.normal(k, shape, f32) * s
    L = DEPTH
    return {
        "x": nrm(ks[0], (BATCH, SEQ, D_MODEL), 1.0),
        "mem": nrm(ks[1], (BATCH, MEM_LEN, D_MODEL), 1.0),
        "attn_norm_g": 1.0 + nrm(ks[2], (L, D_MODEL), 0.02),
        "mem_norm_g": 1.0 + nrm(ks[3], (L, D_MODEL), 0.02),
        "w_in": nrm(ks[4], (L, D_MODEL, IN_COLS), D_MODEL ** -0.5),
        "w_mem_kv": nrm(ks[5], (L, D_MODEL, 2 * WIDTH_C), D_MODEL ** -0.5),
        "na_rpb": nrm(ks[6], (L, NA_HEADS, 2 * NA_WIN_ROWS - 1, 2 * NA_WIN_COLS - 1), 0.2),
        "t5_rel_bias": nrm(ks[7], (T5_BUCKETS, DIL_HEADS), 0.2),
        "w_branch_a": nrm(ks[8], (L, WIDTH_A, D_MODEL), WIDTH_A ** -0.5),
        "w_branch_b": nrm(ks[9], (L, WIDTH_B_OUT, D_MODEL), WIDTH_B_OUT ** -0.5),
        "w_branch_c": nrm(ks[10], (L, WIDTH_C, D_MODEL), WIDTH_C ** -0.5),
        "w_out": nrm(ks[11], (L, D_MODEL, D_MODEL), D_MODEL ** -0.5),
        "ffn_norm_g": 1.0 + nrm(ks[12], (L, D_MODEL), 0.02),
        "router_w": nrm(ks[13], (L, D_MODEL, N_EXPERTS), D_MODEL ** -0.5),
        "router_b": nrm(ks[14], (L, N_EXPERTS), 0.01),
        "expert_w_gate_up": nrm(ks[15], (L, N_EXPERTS, D_MODEL, 2 * D_FF), D_MODEL ** -0.5),
        "expert_b_gate_up": nrm(ks[16], (L, N_EXPERTS, 2 * D_FF), 0.02),
        "expert_w_down": nrm(ks[17], (L, N_EXPERTS, D_FF, D_MODEL), D_FF ** -0.5),
        "expert_b_down": nrm(ks[18], (L, N_EXPERTS, D_MODEL), 0.02),
        "final_norm_g": 1.0 + nrm(ks[19], (D_MODEL,), 0.02),
    }


def reference(x, mem, attn_norm_g, mem_norm_g, w_in, w_mem_kv, na_rpb, t5_rel_bias,
              w_branch_a, w_branch_b, w_branch_c, w_out, ffn_norm_g, router_w, router_b,
              expert_w_gate_up, expert_b_gate_up, expert_w_down, expert_b_down, final_norm_g):
    B, S, D = x.shape
    M = mem.shape[1]
    split_points = list(np.cumsum([WIDTH_A, WIDTH_A, WIDTH_A, WIDTH_B, WIDTH_B, WIDTH_B, WIDTH_C])[:])
    for l in range(DEPTH):
        h = rms_norm(x, attn_norm_g[l])
        proj = h @ w_in[l]
        qa, ka, va, qb, kb, vb, qc, gate_logits = jnp.split(proj, split_points, axis=-1)
        qa = qa.reshape(B, S, NA_HEADS, HEAD_DIM)
        ka = ka.reshape(B, S, NA_HEADS, HEAD_DIM)
        va = va.reshape(B, S, NA_HEADS, HEAD_DIM)
        qb = qb.reshape(B, S, DIL_HEADS, HEAD_DIM)
        kb = kb.reshape(B, S, DIL_HEADS, HEAD_DIM)
        vb = vb.reshape(B, S, DIL_HEADS, HEAD_DIM)
        qc = qc.reshape(B, S, MEM_HEADS, MEM_HEAD_DIM)
        gates = jax.nn.sigmoid(gate_logits.astype(jnp.float32)).astype(x.dtype)
        gates = gates.reshape(B, S, N_BRANCH, D)

        m = rms_norm(mem, mem_norm_g[l])
        mk, mv = jnp.split(m @ w_mem_kv[l], 2, axis=-1)
        mk = mk.reshape(B, M, MEM_HEADS, MEM_HEAD_DIM)
        mv = mv.reshape(B, M, MEM_HEADS, MEM_HEAD_DIM)

        y_a = neighbourhood_attention(qa, ka, va, na_rpb[l]) @ w_branch_a[l]
        y_b = dilated_mixture(qb, kb, vb, t5_rel_bias) @ w_branch_b[l]
        y_c = memory_attention(qc, mk, mv) @ w_branch_c[l]
        merged = gates[:, :, 0] * y_a + gates[:, :, 1] * y_b + gates[:, :, 2] * y_c
        x = x + merged @ w_out[l]

        h = rms_norm(x, ffn_norm_g[l])
        y = moe_ffn(h.reshape(B * S, D), router_w[l], router_b[l], expert_w_gate_up[l],
                    expert_b_gate_up[l], expert_w_down[l], expert_b_down[l])
        x = x + y.reshape(B, S, D)
    return rms_norm(x, final_norm_g)
```

```python
import functools
import math

import numpy as np
import jax
import jax.numpy as jnp
from jax import lax
from jax.experimental import pallas as pl
from jax.experimental.pallas import tpu as pltpu

D_MODEL = 1024
GRID_W = 64
MEM_HEADS = 4
MEM_HEAD_DIM = 128
HEAD_DIM = 64
NA_HEADS = 8
NA_WIN_ROWS = 8
NA_WIN_COLS = 16
DIL_GROUPS = ((128, 1), (512, 4), (2048, 16))
DIL_HEADS_PER_GROUP = 4
DIL_BLOCK = 128
T5_BUCKETS = 32
T5_MAX_DIST = 1024
N_EXPERTS = 32
TOP_K = 4
D_FF = D_MODEL
SWIGLU_ALPHA = 1.702
SWIGLU_LIMIT = 7.0
RMS_EPS = 1e-6
NEG_INF = -1e30

WIDTH_A = NA_HEADS * HEAD_DIM
WIDTH_B = DIL_HEADS_PER_GROUP * len(DIL_GROUPS) * HEAD_DIM
WIDTH_G = DIL_HEADS_PER_GROUP * HEAD_DIM
WIDTH_C = MEM_HEADS * MEM_HEAD_DIM
OFF_QA, OFF_KA, OFF_VA = 0, WIDTH_A, 2 * WIDTH_A
OFF_QB = 3 * WIDTH_A
OFF_KB = OFF_QB + WIDTH_B
OFF_VB = OFF_KB + WIDTH_B
OFF_QC = OFF_VB + WIDTH_B
OFF_GATE = OFF_QC + WIDTH_C
IN_COLS = OFF_GATE + 3 * D_MODEL

LANES = 128
VMEM_LIMIT = 56 * 1024 * 1024

PROJ_TILE = 512
NA_ROWS_PER_STEP = 8
MOE_BLOCK = 256
COMBINE_TILE = 256


def _cparams(n_axes):
    return pltpu.CompilerParams(
        dimension_semantics=("arbitrary",) * n_axes, vmem_limit_bytes=VMEM_LIMIT)


def _rms(x, g):
    return x * lax.rsqrt(jnp.mean(x * x, axis=-1, keepdims=True) + RMS_EPS) * g


def _sigmoid(x):
    return 1.0 / (1.0 + jnp.exp(-x))


def _dot(a, b):
    return jnp.dot(a, b, preferred_element_type=jnp.float32)


def _dot_nt(a, b):
    return lax.dot_general(a, b, (((1,), (1,)), ((), ())), preferred_element_type=jnp.float32)


def _mem_kv_kernel(mem_ref, g_ref, w_ref, out_ref):
    h = _rms(mem_ref[...], g_ref[...]).astype(jnp.bfloat16)
    out_ref[...] = _dot(h, w_ref[...]).astype(jnp.bfloat16)


def _mem_kv(mem2d, g, w):
    rows = mem2d.shape[0]
    return pl.pallas_call(
        _mem_kv_kernel,
        out_shape=jax.ShapeDtypeStruct((rows, 2 * WIDTH_C), jnp.bfloat16),
        name="mem_kv",
        compiler_params=pltpu.CompilerParams(vmem_limit_bytes=VMEM_LIMIT),
    )(mem2d, g, w)


def _in_proj_kernel(x_ref, g_ref, w_ref, na_ref, d0_ref, d1_ref, d2_ref, qc_ref, gate_ref, acc_ref):
    T = x_ref.shape[0]
    h = _rms(x_ref[...], g_ref[...]).astype(jnp.bfloat16)
    q_scale = HEAD_DIM ** -0.5

    for c in range(3):
        acc = _dot(h, w_ref[:, c * WIDTH_A:(c + 1) * WIDTH_A])
        if c == 0:
            acc = acc * q_scale
        na_ref[:, c * WIDTH_A:(c + 1) * WIDTH_A] = acc.astype(jnp.bfloat16)

    for gi, ((_, dil), d_ref) in enumerate(zip(DIL_GROUPS, (d0_ref, d1_ref, d2_ref))):
        for kind, off in enumerate((OFF_QB, OFF_KB, OFF_VB)):
            lo = off + gi * WIDTH_G
            acc = _dot(h, w_ref[:, lo:lo + WIDTH_G])
            if kind == 0:
                acc = acc * q_scale
            cols = slice(kind * WIDTH_G, (kind + 1) * WIDTH_G)
            if dil == 1:
                d_ref[0, 0, :, cols] = acc.astype(jnp.bfloat16)
            else:
                for c in range(WIDTH_G // LANES):
                    acc_ref[c] = acc[:, c * LANES:(c + 1) * LANES]
                for rho in range(dil):
                    d_ref[0, rho, :, cols] = jnp.concatenate(
                        [acc_ref[c, pl.ds(rho, T // dil, stride=dil), :] for c in range(WIDTH_G // LANES)],
                        axis=-1).astype(jnp.bfloat16)

    qc_ref[...] = _dot(h, w_ref[:, OFF_QC:OFF_QC + WIDTH_C]).astype(jnp.bfloat16)

    chunk = 512
    for c in range(3 * D_MODEL // chunk):
        lo = OFF_GATE + c * chunk
        gate_ref[:, c * chunk:(c + 1) * chunk] = _sigmoid(_dot(h, w_ref[:, lo:lo + chunk])).astype(jnp.bfloat16)


def _in_proj(x2d, g, w_bf16, batch, seq):
    n = x2d.shape[0]
    T = PROJ_TILE
    tiles_per_batch = seq // T
    out_shape = [jax.ShapeDtypeStruct((n, 3 * WIDTH_A), jnp.bfloat16)]
    out_specs = [pl.BlockSpec((T, 3 * WIDTH_A), lambda i: (i, 0))]
    for _, dil in DIL_GROUPS:
        out_shape.append(jax.ShapeDtypeStruct((batch, dil, seq // dil, 3 * WIDTH_G), jnp.bfloat16))
        out_specs.append(pl.BlockSpec((1, dil, T // dil, 3 * WIDTH_G),
                                      lambda i: (i // tiles_per_batch, 0, i % tiles_per_batch, 0)))
    out_shape += [jax.ShapeDtypeStruct((n, WIDTH_C), jnp.bfloat16),
                  jax.ShapeDtypeStruct((n, 3 * D_MODEL), jnp.bfloat16)]
    out_specs += [pl.BlockSpec((T, WIDTH_C), lambda i: (i, 0)),
                  pl.BlockSpec((T, 3 * D_MODEL), lambda i: (i, 0))]
    return pl.pallas_call(
        _in_proj_kernel,
        grid=(n // T,),
        in_specs=[pl.BlockSpec((T, D_MODEL), lambda i: (i, 0)),
                  pl.BlockSpec((1, D_MODEL), lambda i: (0, 0)),
                  pl.BlockSpec((D_MODEL, IN_COLS), lambda i: (0, 0))],
        out_specs=out_specs,
        out_shape=out_shape,
        scratch_shapes=[pltpu.VMEM((WIDTH_G // LANES, T, LANES), jnp.float32)],
        compiler_params=_cparams(1),
        name="in_proj",
    )(x2d, g, w_bf16)


def _na_kernel(q_ref, k_ref, v_ref, bias_ref, out_ref, *, rows):
    i = pl.program_id(1)
    lane = lax.broadcasted_iota(jnp.int32, (GRID_W, LANES), 1)
    low_half = lane < HEAD_DIM
    n_win = NA_WIN_ROWS * GRID_W

    def row_body(j, carry):
        qr = i * NA_ROWS_PER_STEP + j
        rs = jnp.clip(qr - NA_WIN_ROWS // 2, 0, rows - NA_WIN_ROWS)
        shift = rs - qr + NA_WIN_ROWS - 1
        q_off = pl.multiple_of(j * GRID_W, GRID_W)
        k_off = pl.multiple_of(rs * GRID_W, GRID_W)
        for hp in range(NA_HEADS // 2):
            cols = slice(hp * LANES, (hp + 1) * LANES)
            qp = q_ref[pl.ds(q_off, GRID_W), cols]
            kp = k_ref[pl.ds(k_off, n_win), cols]
            vp = v_ref[pl.ds(k_off, n_win), cols]
            outs = []
            for hh in range(2):
                h = 2 * hp + hh
                keep = low_half if hh == 0 else jnp.logical_not(low_half)
                qm = jnp.where(keep, qp, jnp.zeros_like(qp))
                s = _dot_nt(qm, kp)
                bias = jnp.concatenate(
                    [bias_ref[shift + 2 * p, h] for p in range(NA_WIN_ROWS // 2)], axis=-1)
                s = s + bias
                m = jnp.max(s, axis=-1, keepdims=True)
                e = jnp.exp(s - m)
                l = jnp.sum(e, axis=-1, keepdims=True)
                pv = _dot(e.astype(jnp.bfloat16), vp)
                outs.append(pv / l)
            out_ref[pl.ds(q_off, GRID_W), cols] = jnp.where(low_half, outs[0], outs[1]).astype(jnp.bfloat16)
        return carry

    lax.fori_loop(0, NA_ROWS_PER_STEP, row_body, 0)


def _na_bias_table(rpb):
    col = np.arange(GRID_W)
    cs = np.clip(col - NA_WIN_COLS // 2, 0, GRID_W - NA_WIN_COLS)
    dc = col[None, :] - col[:, None]
    valid = (col[None, :] >= cs[:, None]) & (col[None, :] < cs[:, None] + NA_WIN_COLS)
    idx = np.clip(dc + NA_WIN_COLS - 1, 0, 2 * NA_WIN_COLS - 2)
    b = rpb[:, :, idx]
    b = jnp.where(valid[None, None], b, NEG_INF).astype(jnp.float32)
    b = b.transpose(1, 0, 2, 3)
    return jnp.concatenate([b[:-1], b[1:]], axis=-1)


def _na_attn(na_qkv, bias_tab, batch, seq):
    rows = seq // GRID_W
    steps = rows // NA_ROWS_PER_STEP
    tq = NA_ROWS_PER_STEP * GRID_W
    n = na_qkv.shape[0]
    return pl.pallas_call(
        functools.partial(_na_kernel, rows=rows),
        grid=(batch, steps),
        in_specs=[pl.BlockSpec((tq, WIDTH_A), lambda b, i: (b * steps + i, 0)),
                  pl.BlockSpec((seq, WIDTH_A), lambda b, i: (b, 1)),
                  pl.BlockSpec((seq, WIDTH_A), lambda b, i: (b, 2)),
                  pl.BlockSpec(bias_tab.shape, lambda b, i: (0, 0, 0, 0))],
        out_specs=pl.BlockSpec((tq, WIDTH_A), lambda b, i: (b * steps + i, 0)),
        out_shape=jax.ShapeDtypeStruct((n, WIDTH_A), jnp.bfloat16),
        compiler_params=_cparams(2),
        name="na_attn",
    )(na_qkv, na_qkv, na_qkv, bias_tab)


def _t5_bucket_np(rel):
    half = T5_BUCKETS // 2
    max_exact = half // 2
    ret = np.where(rel > 0, half, 0)
    n = np.abs(rel)
    nf = np.maximum(n, 1).astype(np.float32)
    large = max_exact + (np.log(nf / np.float32(max_exact)) / np.float32(math.log(T5_MAX_DIST / max_exact))
                         * np.float32(half - max_exact)).astype(np.int32)
    large = np.minimum(large, half - 1)
    return ret + np.where(n < max_exact, n, large)


def _dil_bias_table(t5_group, window, dil):
    half = (window // 2) // dil
    assert half == DIL_BLOCK // 2
    q = np.arange(DIL_BLOCK)[:, None]
    j = np.arange(2 * DIL_BLOCK)[None, :]
    rel = (j - half) - q
    bucket = _t5_bucket_np(rel * dil)
    in_win = np.abs(rel) <= half
    bias = t5_group[bucket]
    bias = bias.transpose(2, 0, 1).astype(jnp.float32)
    variants = []
    for in_range in (j >= half, j >= 0, j < 2 * DIL_BLOCK - half):
        ok = np.broadcast_to(in_win & in_range, rel.shape)
        variants.append(jnp.where(ok[None], bias, NEG_INF))
    return jnp.stack(variants, axis=0)


def _dil_kernel(q_ref, kp_ref, kc_ref, kn_ref, vp_ref, vc_ref, vn_ref, bias_ref, o_ref, lse_ref, *, dil):
    lane = lax.broadcasted_iota(jnp.int32, (DIL_BLOCK, LANES), 1)
    low_half = lane < HEAD_DIM
    hb = DIL_BLOCK // 2

    def class_body(rho, carry):
        for hp in range(DIL_HEADS_PER_GROUP // 2):
            cols = slice(hp * LANES, (hp + 1) * LANES)
            qp = q_ref[0, rho, :, cols]
            kw = jnp.concatenate([kp_ref[0, rho, hb:, cols], kc_ref[0, rho, :, cols],
                                  kn_ref[0, rho, :hb, cols]], axis=0)
            vw = jnp.concatenate([vp_ref[0, rho, hb:, cols], vc_ref[0, rho, :, cols],
                                  vn_ref[0, rho, :hb, cols]], axis=0)
            outs, lses = [], []
            for hh in range(2):
                h = 2 * hp + hh
                keep = low_half if hh == 0 else jnp.logical_not(low_half)
                qm = jnp.where(keep, qp, jnp.zeros_like(qp))
                s = _dot_nt(qm, kw) + bias_ref[0, h]
                m = jnp.max(s, axis=-1, keepdims=True)
                e = jnp.exp(s - m)
                l = jnp.sum(e, axis=-1, keepdims=True)
                pv = _dot(e.astype(jnp.bfloat16), vw)
                outs.append(pv / l)
                lses.append(jnp.broadcast_to(m + jnp.log(l), (DIL_BLOCK, LANES)))
            o_pair = jnp.where(low_half, outs[0], outs[1])
            lse_pair = jnp.where(low_half, lses[0], lses[1])
            if dil == 1:
                o_ref[hp] = o_pair
                lse_ref[hp] = lse_pair
            else:
                o_ref[hp, pl.ds(rho, DIL_BLOCK, stride=dil), :] = o_pair
                lse_ref[hp, pl.ds(rho, DIL_BLOCK, stride=dil), :] = lse_pair
        return carry

    if dil == 1:
        class_body(0, 0)
    else:
        lax.fori_loop(0, dil, class_body, 0)


def _dil_attn(qkv, bias_tab, dil, batch, seq):
    L = seq // dil
    nb = L // DIL_BLOCK
    assert nb >= 2
    tq = DIL_BLOCK * dil

    def blk(col, shift):
        return pl.BlockSpec((1, dil, DIL_BLOCK, WIDTH_G),
                            lambda b, n: (b, 0, jnp.clip(n + shift, 0, nb - 1), col))

    def variant(n):
        return jnp.where(n == 0, 0, jnp.where(n == nb - 1, 2, 1))

    n_pairs = WIDTH_G // LANES
    out_spec = pl.BlockSpec((n_pairs, tq, LANES), lambda b, n: (0, b * nb + n, 0))
    return pl.pallas_call(
        functools.partial(_dil_kernel, dil=dil),
        grid=(batch, nb),
        in_specs=[blk(0, 0), blk(1, -1), blk(1, 0), blk(1, 1), blk(2, -1), blk(2, 0), blk(2, 1),
                  pl.BlockSpec((1, DIL_HEADS_PER_GROUP, DIL_BLOCK, 2 * DIL_BLOCK),
                               lambda b, n: (variant(n), 0, 0, 0))],
        out_specs=[out_spec, out_spec],
        out_shape=[jax.ShapeDtypeStruct((n_pairs, batch * seq, LANES), jnp.float32)] * 2,
        compiler_params=_cparams(2),
        name=f"dil_attn_d{dil}",
    )(qkv, qkv, qkv, qkv, qkv, qkv, qkv, bias_tab)


def _mix_kernel(x_ref, na_ref, o0_ref, l0_ref, o1_ref, l1_ref, o2_ref, l2_ref, qc_ref, gate_ref,
                mkv_ref, wa_ref, wb_ref, wc_ref, wo_ref, fg_ref, rw_ref, rb_ref,
                x1_ref, h_ref, idx_ref, wt_ref):
    T = x_ref.shape[0]
    ob = []
    for hp in range(WIDTH_G // LANES):
        l0, l1, l2 = l0_ref[hp], l1_ref[hp], l2_ref[hp]
        m = jnp.maximum(jnp.maximum(l0, l1), l2)
        e0, e1, e2 = jnp.exp(l0 - m), jnp.exp(l1 - m), jnp.exp(l2 - m)
        ob.append((e0 * o0_ref[hp] + e1 * o1_ref[hp] + e2 * o2_ref[hp]) / (e0 + e1 + e2))
    ob = jnp.concatenate(ob, axis=-1)

    mem_scale = MEM_HEAD_DIM ** -0.5
    oc = []
    for h in range(MEM_HEADS):
        cols = slice(h * MEM_HEAD_DIM, (h + 1) * MEM_HEAD_DIM)
        s = _dot_nt(qc_ref[:, cols], mkv_ref[:, cols]) * mem_scale
        mx = jnp.max(s, axis=-1, keepdims=True)
        e = jnp.exp(s - mx)
        l = jnp.sum(e, axis=-1, keepdims=True)
        pv = _dot(e.astype(jnp.bfloat16), mkv_ref[:, WIDTH_C + h * MEM_HEAD_DIM:WIDTH_C + (h + 1) * MEM_HEAD_DIM])
        oc.append(pv / l)
    oc = jnp.concatenate(oc, axis=-1).astype(jnp.bfloat16)

    y_a = _dot(na_ref[...], wa_ref[...])
    y_b = _dot(ob.astype(jnp.bfloat16), wb_ref[...])
    y_c = _dot(oc, wc_ref[...])
    merged = (gate_ref[:, 0:D_MODEL].astype(jnp.float32) * y_a
              + gate_ref[:, D_MODEL:2 * D_MODEL].astype(jnp.float32) * y_b
              + gate_ref[:, 2 * D_MODEL:3 * D_MODEL].astype(jnp.float32) * y_c)
    x1 = x_ref[...] + _dot(merged.astype(jnp.bfloat16), wo_ref[...])
    x1_ref[...] = x1

    h = _rms(x1, fg_ref[...])
    h_ref[...] = h
    logits = jnp.dot(h, rw_ref[...], preferred_element_type=jnp.float32,
                     precision=lax.Precision.HIGHEST) + rb_ref[...]
    lane = lax.broadcasted_iota(jnp.int32, (T, N_EXPERTS), 1)
    vals, idxs = [], []
    for _ in range(TOP_K):
        mx = jnp.max(logits, axis=-1, keepdims=True)
        sel = jnp.min(jnp.where(logits == mx, lane, N_EXPERTS), axis=-1, keepdims=True)
        vals.append(mx)
        idxs.append(sel)
        logits = jnp.where(lane == sel, -jnp.inf, logits)
    ev = [jnp.exp(v - vals[0]) for v in vals]
    den = ev[0] + ev[1] + ev[2] + ev[3]
    idx_ref[...] = jnp.concatenate(idxs, axis=-1)
    wt_ref[...] = jnp.concatenate([e / den for e in ev], axis=-1)


def _mix_out(x2d, na_out, dil_outs, qc, gates, mkv, wa, wb, wc, wo, ffn_g, router_w, router_b, batch, seq):
    n = x2d.shape[0]
    T = PROJ_TILE
    tiles_per_batch = seq // T
    mem_len = mkv.shape[0] // batch
    row = lambda w: pl.BlockSpec((T, w), lambda i: (i, 0))
    full = lambda a: pl.BlockSpec(a.shape, lambda i: (0,) * a.ndim)
    in_specs = [row(D_MODEL), row(WIDTH_A)]
    args = [x2d, na_out]
    for o, l in dil_outs:
        pair_rows = pl.BlockSpec((WIDTH_G // LANES, T, LANES), lambda i: (0, i, 0))
        in_specs += [pair_rows, pair_rows]
        args += [o, l]
    in_specs += [row(WIDTH_C), row(3 * D_MODEL),
                 pl.BlockSpec((mem_len, 2 * WIDTH_C), lambda i: (i // tiles_per_batch, 0)),
                 full(wa), full(wb), full(wc), full(wo), full(ffn_g), full(router_w), full(router_b)]
    args += [qc, gates, mkv, wa, wb, wc, wo, ffn_g, router_w, router_b]
    return pl.pallas_call(
        _mix_kernel,
        grid=(n // T,),
        in_specs=in_specs,
        out_specs=[row(D_MODEL), row(D_MODEL), row(TOP_K), row(TOP_K)],
        out_shape=[jax.ShapeDtypeStruct((n, D_MODEL), jnp.float32),
                   jax.ShapeDtypeStruct((n, D_MODEL), jnp.float32),
                   jax.ShapeDtypeStruct((n, TOP_K), jnp.int32),
                   jax.ShapeDtypeStruct((n, TOP_K), jnp.float32)],
        compiler_params=_cparams(1),
        name="mix_out",
    )(*args)


def _row_gather(src_hbm, idx_ref, dst_ref, sem, n_rows):
    def body(r, carry):
        pltpu.make_async_copy(src_hbm.at[pl.ds(idx_ref[r], 1), :],
                              dst_ref.at[pl.ds(r, 1), :], sem).start()
        return carry
    lax.fori_loop(0, n_rows, body, 0, unroll=8)


def _row_gather_wait(src_hbm, dst_ref, sem, n_rows):
    def body(r, carry):
        pltpu.make_async_copy(src_hbm.at[pl.ds(0, 1), :], dst_ref.at[pl.ds(0, 1), :], sem).wait()
        return carry
    lax.fori_loop(0, n_rows, body, 0, unroll=8)


def _moe_kernel(be_ref, nv_ref, tok_ref, tok_next_ref, h_hbm, wgu_ref, bgu_ref, wdn_ref, bdn_ref,
                y_ref, xbuf, sem):
    i = pl.program_id(0)
    n_valid = nv_ref[0]
    slot = i % 2

    @pl.when(jnp.logical_and(i == 0, n_valid > 0))
    def _():
        _row_gather(h_hbm, tok_ref, xbuf.at[0], sem.at[0], MOE_BLOCK)

    @pl.when(i + 1 < n_valid)
    def _():
        _row_gather(h_hbm, tok_next_ref, xbuf.at[1 - slot], sem.at[1 - slot], MOE_BLOCK)

    @pl.when(i < n_valid)
    def _():
        _row_gather_wait(h_hbm, xbuf.at[slot], sem.at[slot], MOE_BLOCK)
        x = xbuf[slot].astype(jnp.bfloat16)
        gu = _dot(x, wgu_ref[0]) + bgu_ref[0]
        g = jnp.minimum(gu[:, :D_FF], SWIGLU_LIMIT)
        u = jnp.clip(gu[:, D_FF:], -SWIGLU_LIMIT, SWIGLU_LIMIT)
        a = (u + 1.0) * (g * _sigmoid(SWIGLU_ALPHA * g))
        y_ref[...] = _dot(a.astype(jnp.bfloat16), wdn_ref[0]) + bdn_ref[0]

    @pl.when(i >= n_valid)
    def _():
        y_ref[...] = jnp.zeros_like(y_ref)


def _moe_experts(h, block_expert, n_valid, slot_tok, wgu, bgu, wdn, bdn):
    n_blocks = block_expert.shape[0]
    tok3 = slot_tok.reshape(n_blocks, 1, MOE_BLOCK)
    smem_blk = lambda shift: pl.BlockSpec(
        (None, 1, MOE_BLOCK), lambda i, be, nv: (jnp.minimum(i + shift, n_blocks - 1), 0, 0),
        memory_space=pltpu.SMEM)

    def kernel(be_ref, nv_ref, tok_ref, tok_next_ref, *rest):
        _moe_kernel(be_ref, nv_ref, tok_ref.at[0], tok_next_ref.at[0], *rest)

    grid_spec = pltpu.PrefetchScalarGridSpec(
        num_scalar_prefetch=2,
        grid=(n_blocks,),
        in_specs=[smem_blk(0), smem_blk(1),
                  pl.BlockSpec(memory_space=pl.ANY),
                  pl.BlockSpec((1, D_MODEL, 2 * D_FF), lambda i, be, nv: (be[i], 0, 0)),
                  pl.BlockSpec((1, 1, 2 * D_FF), lambda i, be, nv: (be[i], 0, 0)),
                  pl.BlockSpec((1, D_FF, D_MODEL), lambda i, be, nv: (be[i], 0, 0)),
                  pl.BlockSpec((1, 1, D_MODEL), lambda i, be, nv: (be[i], 0, 0))],
        out_specs=pl.BlockSpec((MOE_BLOCK, D_MODEL), lambda i, be, nv: (i, 0)),
        scratch_shapes=[pltpu.VMEM((2, MOE_BLOCK, D_MODEL), jnp.float32),
                        pltpu.SemaphoreType.DMA((2,))],
    )
    return pl.pallas_call(
        kernel,
        grid_spec=grid_spec,
        out_shape=jax.ShapeDtypeStruct((n_blocks * MOE_BLOCK, D_MODEL), jnp.float32),
        compiler_params=_cparams(1),
        name="moe_experts",
    )(block_expert, n_valid, tok3, tok3, h, wgu, bgu, wdn, bdn)


def _combine_kernel(pos_ref, pos_next_ref, ys_hbm, wt_ref, x1_ref, g_ref, out_ref, gbuf, sem):
    i = pl.program_id(0)
    n_steps = pl.num_programs(0)
    slot = i % 2
    T = COMBINE_TILE
    n_rows = T * TOP_K

    @pl.when(i == 0)
    def _():
        _row_gather(ys_hbm, pos_ref, gbuf.at[0], sem.at[0], n_rows)

    @pl.when(i + 1 < n_steps)
    def _():
        _row_gather(ys_hbm, pos_next_ref, gbuf.at[1 - slot], sem.at[1 - slot], n_rows)

    _row_gather_wait(ys_hbm, gbuf.at[slot], sem.at[slot], n_rows)
    wt = wt_ref[...]
    y = x1_ref[...]
    for k in range(TOP_K):
        y = y + wt[:, k:k + 1] * gbuf[slot, k * T:(k + 1) * T, :]
    out_ref[...] = _rms(y, g_ref[...])


def _moe_combine(ys, pos_tiles, wts, x1, final_g):
    n = x1.shape[0]
    T = COMBINE_TILE
    n_steps = n // T
    smem_blk = lambda shift: pl.BlockSpec(
        (None, 1, T * TOP_K), lambda i: (jnp.minimum(i + shift, n_steps - 1), 0, 0),
        memory_space=pltpu.SMEM)

    def kernel(pos_ref, pos_next_ref, *rest):
        _combine_kernel(pos_ref.at[0], pos_next_ref.at[0], *rest)

    return pl.pallas_call(
        kernel,
        grid=(n_steps,),
        in_specs=[smem_blk(0), smem_blk(1),
                  pl.BlockSpec(memory_space=pl.ANY),
                  pl.BlockSpec((T, TOP_K), lambda i: (i, 0)),
                  pl.BlockSpec((T, D_MODEL), lambda i: (i, 0)),
                  pl.BlockSpec((1, D_MODEL), lambda i: (0, 0))],
        out_specs=pl.BlockSpec((T, D_MODEL), lambda i: (i, 0)),
        out_shape=jax.ShapeDtypeStruct((n, D_MODEL), jnp.float32),
        scratch_shapes=[pltpu.VMEM((2, T * TOP_K, D_MODEL), jnp.float32),
                        pltpu.SemaphoreType.DMA((2,))],
        compiler_params=_cparams(1),
        name="moe_combine",
    )(pos_tiles, pos_tiles, ys, wts, x1, final_g)


def _route(topk_idx):
    n = topk_idx.shape[0]
    a = n * TOP_K
    e_flat = topk_idx.reshape(-1)
    onehot = (e_flat[:, None] == jnp.arange(N_EXPERTS, dtype=jnp.int32)[None, :]).astype(jnp.int32)
    csum = jnp.cumsum(onehot, axis=0)
    rank = jnp.sum(csum * onehot, axis=1) - 1
    counts = csum[-1]
    padded = (counts + MOE_BLOCK - 1) // MOE_BLOCK * MOE_BLOCK
    pad_ends = jnp.cumsum(padded)
    pad_starts = pad_ends - padded
    slot = pad_starts[e_flat] + rank
    n_blocks = a // MOE_BLOCK + N_EXPERTS
    slot_tok = jnp.zeros((n_blocks * MOE_BLOCK,), jnp.int32).at[slot].set(
        jnp.arange(a, dtype=jnp.int32) // TOP_K)
    block_expert = jnp.minimum(
        jnp.searchsorted(pad_ends, jnp.arange(n_blocks, dtype=jnp.int32) * MOE_BLOCK, side="right"),
        N_EXPERTS - 1).astype(jnp.int32)
    n_valid = (pad_ends[-1:] // MOE_BLOCK).astype(jnp.int32)
    return slot.astype(jnp.int32), slot_tok, block_expert, n_valid


def kernel(x, mem, attn_norm_g, mem_norm_g, w_in, w_mem_kv, na_rpb, t5_rel_bias, w_branch_a, w_branch_b,
           w_branch_c, w_out, ffn_norm_g, router_w, router_b, expert_w_gate_up, expert_b_gate_up,
           expert_w_down, expert_b_down, final_norm_g):
    B, S, D = x.shape
    depth = w_in.shape[0]
    bf = jnp.bfloat16
    x2d = x.reshape(B * S, D)
    mem2d = mem.reshape(B * mem.shape[1], D)
    G = DIL_HEADS_PER_GROUP
    dil_bias = [_dil_bias_table(t5_rel_bias[:, gi * G:(gi + 1) * G], window, dil)
                for gi, (window, dil) in enumerate(DIL_GROUPS)]

    assert depth == 1, "single-layer block (per-layer parameters carry a leading axis of 1)"
    l = 0
    mkv = _mem_kv(mem2d, mem_norm_g[l][None], w_mem_kv[l].astype(bf))
    na_qkv, d0, d1, d2, qc, gates = _in_proj(x2d, attn_norm_g[l][None], w_in[l].astype(bf), B, S)
    na_out = _na_attn(na_qkv, _na_bias_table(na_rpb[l]), B, S)
    dil_outs = [_dil_attn(qkv, bias, dil, B, S)
                for qkv, bias, (_, dil) in zip((d0, d1, d2), dil_bias, DIL_GROUPS)]
    x1, h, topk_idx, topk_w = _mix_out(
        x2d, na_out, dil_outs, qc, gates, mkv, w_branch_a[l].astype(bf), w_branch_b[l].astype(bf),
        w_branch_c[l].astype(bf), w_out[l].astype(bf), ffn_norm_g[l][None], router_w[l],
        router_b[l][None], B, S)

    slot, slot_tok, block_expert, n_valid = _route(topk_idx)
    ys = _moe_experts(h, block_expert, n_valid, slot_tok,
                      expert_w_gate_up[l].astype(bf), expert_b_gate_up[l][:, None, :],
                      expert_w_down[l].astype(bf), expert_b_down[l][:, None, :])
    n_tiles = (B * S) // COMBINE_TILE
    pos_tiles = slot.reshape(n_tiles, COMBINE_TILE, TOP_K).transpose(0, 2, 1).reshape(
        n_tiles, 1, COMBINE_TILE * TOP_K)
    out = _moe_combine(ys, pos_tiles, topk_w, x1, final_norm_g[None])
    return out.reshape(B, S, D)
```

```python
import functools
import math

import numpy as np
import jax
import jax.numpy as jnp
from jax import lax
from jax.experimental import pallas as pl
from jax.experimental.pallas import tpu as pltpu

D_MODEL = 1024
GRID_W = 64
MEM_HEADS = 4
MEM_HEAD_DIM = 128
HEAD_DIM = 64
NA_HEADS = 8
NA_WIN_ROWS = 8
NA_WIN_COLS = 16
DIL_GROUPS = ((128, 1), (512, 4), (2048, 16))
DIL_HEADS_PER_GROUP = 4
DIL_BLOCK = 128
T5_BUCKETS = 32
T5_MAX_DIST = 1024
N_EXPERTS = 32
TOP_K = 4
D_FF = D_MODEL
SWIGLU_ALPHA = 1.702
SWIGLU_LIMIT = 7.0
RMS_EPS = 1e-6
NEG_INF = -1e30

WIDTH_A = NA_HEADS * HEAD_DIM
WIDTH_B = DIL_HEADS_PER_GROUP * len(DIL_GROUPS) * HEAD_DIM
WIDTH_G = DIL_HEADS_PER_GROUP * HEAD_DIM
WIDTH_C = MEM_HEADS * MEM_HEAD_DIM
OFF_QA, OFF_KA, OFF_VA = 0, WIDTH_A, 2 * WIDTH_A
OFF_QB = 3 * WIDTH_A
OFF_KB = OFF_QB + WIDTH_B
OFF_VB = OFF_KB + WIDTH_B
OFF_QC = OFF_VB + WIDTH_B
OFF_GATE = OFF_QC + WIDTH_C
IN_COLS = OFF_GATE + 3 * D_MODEL

LANES = 128
VMEM_LIMIT = 56 * 1024 * 1024

PROJ_TILE = 512
NA_ROWS_PER_STEP = 8
MOE_BLOCK = 256
COMBINE_TILE = 256


def _cparams(n_axes):
    return pltpu.CompilerParams(
        dimension_semantics=("arbitrary",) * n_axes, vmem_limit_bytes=VMEM_LIMIT)


def _rms(x, g):
    return x * lax.rsqrt(jnp.mean(x * x, axis=-1, keepdims=True) + RMS_EPS) * g


def _sigmoid(x):
    return 1.0 / (1.0 + jnp.exp(-x))


def _dot(a, b):
    return jnp.dot(a, b, preferred_element_type=jnp.float32)


def _dot_nt(a, b):
    return lax.dot_general(a, b, (((1,), (1,)), ((), ())), preferred_element_type=jnp.float32)


def _mem_kv_kernel(mem_ref, g_ref, w_ref, out_ref):
    h = _rms(mem_ref[...], g_ref[...]).astype(jnp.bfloat16)
    out_ref[...] = _dot(h, w_ref[...]).astype(jnp.bfloat16)


def _mem_kv(mem2d, g, w):
    rows = mem2d.shape[0]
    return pl.pallas_call(
        _mem_kv_kernel,
        out_shape=jax.ShapeDtypeStruct((rows, 2 * WIDTH_C), jnp.bfloat16),
        name="mem_kv",
        compiler_params=pltpu.CompilerParams(vmem_limit_bytes=VMEM_LIMIT),
    )(mem2d, g, w)


def _in_proj_kernel(x_ref, g_ref, w_ref, na_ref, d0_ref, d1_ref, d2_ref, qc_ref, gate_ref, acc_ref):
    T = x_ref.shape[0]
    h = _rms(x_ref[...], g_ref[...]).astype(jnp.bfloat16)
    q_scale = HEAD_DIM ** -0.5

    for c in range(3):
        acc = _dot(h, w_ref[:, c * WIDTH_A:(c + 1) * WIDTH_A])
        if c == 0:
            acc = acc * q_scale
        na_ref[:, c * WIDTH_A:(c + 1) * WIDTH_A] = acc.astype(jnp.bfloat16)

    for gi, ((_, dil), d_ref) in enumerate(zip(DIL_GROUPS, (d0_ref, d1_ref, d2_ref))):
        for kind, off in enumerate((OFF_QB, OFF_KB, OFF_VB)):
            lo = off + gi * WIDTH_G
            acc = _dot(h, w_ref[:, lo:lo + WIDTH_G])
            if kind == 0:
                acc = acc * q_scale
            cols = slice(kind * WIDTH_G, (kind + 1) * WIDTH_G)
            if dil == 1:
                d_ref[0, 0, :, cols] = acc.astype(jnp.bfloat16)
            else:
                for c in range(WIDTH_G // LANES):
                    acc_ref[c] = acc[:, c * LANES:(c + 1) * LANES]
                for rho in range(dil):
                    d_ref[0, rho, :, cols] = jnp.concatenate(
                        [acc_ref[c, pl.ds(rho, T // dil, stride=dil), :] for c in range(WIDTH_G // LANES)],
                        axis=-1).astype(jnp.bfloat16)

    qc_ref[...] = _dot(h, w_ref[:, OFF_QC:OFF_QC + WIDTH_C]).astype(jnp.bfloat16)

    chunk = 512
    for c in range(3 * D_MODEL // chunk):
        lo = OFF_GATE + c * chunk
        gate_ref[:, c * chunk:(c + 1) * chunk] = _sigmoid(_dot(h, w_ref[:, lo:lo + chunk])).astype(jnp.bfloat16)


def _in_proj(x2d, g, w_bf16, batch, seq):
    n = x2d.shape[0]
    T = PROJ_TILE
    tiles_per_batch = seq // T
    out_shape = [jax.ShapeDtypeStruct((n, 3 * WIDTH_A), jnp.bfloat16)]
    out_specs = [pl.BlockSpec((T, 3 * WIDTH_A), lambda i: (i, 0))]
    for _, dil in DIL_GROUPS:
        out_shape.append(jax.ShapeDtypeStruct((batch, dil, seq // dil, 3 * WIDTH_G), jnp.bfloat16))
        out_specs.append(pl.BlockSpec((1, dil, T // dil, 3 * WIDTH_G),
                                      lambda i: (i // tiles_per_batch, 0, i % tiles_per_batch, 0)))
    out_shape += [jax.ShapeDtypeStruct((n, WIDTH_C), jnp.bfloat16),
                  jax.ShapeDtypeStruct((n, 3 * D_MODEL), jnp.bfloat16)]
    out_specs += [pl.BlockSpec((T, WIDTH_C), lambda i: (i, 0)),
                  pl.BlockSpec((T, 3 * D_MODEL), lambda i: (i, 0))]
    return pl.pallas_call(
        _in_proj_kernel,
        grid=(n // T,),
        in_specs=[pl.BlockSpec((T, D_MODEL), lambda i: (i, 0)),
                  pl.BlockSpec((1, D_MODEL), lambda i: (0, 0)),
                  pl.BlockSpec((D_MODEL, IN_COLS), lambda i: (0, 0))],
        out_specs=out_specs,
        out_shape=out_shape,
        scratch_shapes=[pltpu.VMEM((WIDTH_G // LANES, T, LANES), jnp.float32)],
        compiler_params=_cparams(1),
        name="in_proj",
    )(x2d, g, w_bf16)


def _na_kernel(q_ref, k_ref, v_ref, bias_ref, out_ref, *, rows):
    i = pl.program_id(1)
    lane = lax.broadcasted_iota(jnp.int32, (GRID_W, LANES), 1)
    low_half = lane < HEAD_DIM
    n_win = NA_WIN_ROWS * GRID_W

    def row_body(j, carry):
        qr = i * NA_ROWS_PER_STEP + j
        rs = jnp.clip(qr - NA_WIN_ROWS // 2, 0, rows - NA_WIN_ROWS)
        shift = rs - qr + NA_WIN_ROWS - 1
        q_off = pl.multiple_of(j * GRID_W, GRID_W)
        k_off = pl.multiple_of(rs * GRID_W, GRID_W)
        for hp in range(NA_HEADS // 2):
            cols = slice(hp * LANES, (hp + 1) * LANES)
            qp = q_ref[pl.ds(q_off, GRID_W), cols]
            kp = k_ref[pl.ds(k_off, n_win), cols]
            vp = v_ref[pl.ds(k_off, n_win), cols]
            outs = []
            for hh in range(2):
                h = 2 * hp + hh
                keep = low_half if hh == 0 else jnp.logical_not(low_half)
                qm = jnp.where(keep, qp, jnp.zeros_like(qp))
                s = _dot_nt(qm, kp)
                bias = jnp.concatenate(
                    [bias_ref[shift + 2 * p, h] for p in range(NA_WIN_ROWS // 2)], axis=-1)
                s = s + bias
                m = jnp.max(s, axis=-1, keepdims=True)
                e = jnp.exp(s - m)
                l = jnp.sum(e, axis=-1, keepdims=True)
                pv = _dot(e.astype(jnp.bfloat16), vp)
                outs.append(pv / l)
            out_ref[pl.ds(q_off, GRID_W), cols] = jnp.where(low_half, outs[0], outs[1]).astype(jnp.bfloat16)
        return carry

    lax.fori_loop(0, NA_ROWS_PER_STEP, row_body, 0)


def _na_attn(na_qkv, bias_tab, batch, seq):
    rows = seq // GRID_W
    steps = rows // NA_ROWS_PER_STEP
    tq = NA_ROWS_PER_STEP * GRID_W
    n = na_qkv.shape[0]
    return pl.pallas_call(
        functools.partial(_na_kernel, rows=rows),
        grid=(batch, steps),
        in_specs=[pl.BlockSpec((tq, WIDTH_A), lambda b, i: (b * steps + i, 0)),
                  pl.BlockSpec((seq, WIDTH_A), lambda b, i: (b, 1)),
                  pl.BlockSpec((seq, WIDTH_A), lambda b, i: (b, 2)),
                  pl.BlockSpec(bias_tab.shape, lambda b, i: (0, 0, 0, 0))],
        out_specs=pl.BlockSpec((tq, WIDTH_A), lambda b, i: (b * steps + i, 0)),
        out_shape=jax.ShapeDtypeStruct((n, WIDTH_A), jnp.bfloat16),
        compiler_params=_cparams(2),
        name="na_attn",
    )(na_qkv, na_qkv, na_qkv, bias_tab)


def _t5_bucket_np(rel):
    half = T5_BUCKETS // 2
    max_exact = half // 2
    ret = np.where(rel > 0, half, 0)
    n = np.abs(rel)
    nf = np.maximum(n, 1).astype(np.float32)
    large = max_exact + (np.log(nf / np.float32(max_exact)) / np.float32(math.log(T5_MAX_DIST / max_exact))
                         * np.float32(half - max_exact)).astype(np.int32)
    large = np.minimum(large, half - 1)
    return ret + np.where(n < max_exact, n, large)


def _dil_bucket_table():
    q = np.arange(DIL_BLOCK)[:, None]
    j = np.arange(2 * DIL_BLOCK)[None, :]
    tabs = []
    for window, dil in DIL_GROUPS:
        half = (window // 2) // dil
        assert half == DIL_BLOCK // 2
        rel = (j - half) - q
        tabs.append(np.where(np.abs(rel) <= half, _t5_bucket_np(rel * dil), -1))
    return np.stack(tabs).astype(np.int32)


def _bias_prep_kernel(rpb_ref, t5_ref, bucket_ref, na_ref, dil_ref):
    n_dc = 2 * NA_WIN_COLS - 1
    n_dr = 2 * NA_WIN_ROWS - 1
    qc = lax.broadcasted_iota(jnp.int32, (GRID_W, 2 * GRID_W), 0)
    lane = lax.broadcasted_iota(jnp.int32, (GRID_W, 2 * GRID_W), 1)
    second = lane >= GRID_W
    kc = jnp.where(second, lane - GRID_W, lane)
    diff = kc - qc
    cs = jnp.clip(qc - NA_WIN_COLS // 2, 0, GRID_W - NA_WIN_COLS)
    valid = jnp.logical_and(kc >= cs, kc < cs + NA_WIN_COLS)

    def na_tile(mh, carry):
        m = mh // NA_HEADS
        h = mh - m * NA_HEADS
        base = (h * n_dr + m) * n_dc
        acc = jnp.zeros((GRID_W, 2 * GRID_W), jnp.float32)
        for d in range(n_dc):
            v = jnp.where(second, rpb_ref[base + n_dc + d], rpb_ref[base + d])
            acc = jnp.where(diff == d - (NA_WIN_COLS - 1), v, acc)
        na_ref[m, h] = jnp.where(valid, acc, NEG_INF)
        return carry

    lax.fori_loop(0, (n_dr - 1) * NA_HEADS, na_tile, 0)

    j = lax.broadcasted_iota(jnp.int32, (DIL_BLOCK, 2 * DIL_BLOCK), 1)
    half = DIL_BLOCK // 2
    n_heads = DIL_HEADS_PER_GROUP * len(DIL_GROUPS)
    for gi in range(len(DIL_GROUPS)):
        bucket = bucket_ref[gi]
        in_win = bucket >= 0
        for h in range(DIL_HEADS_PER_GROUP):
            acc = jnp.zeros((DIL_BLOCK, 2 * DIL_BLOCK), jnp.float32)
            for b in range(T5_BUCKETS):
                acc = jnp.where(bucket == b, t5_ref[b * n_heads + gi * DIL_HEADS_PER_GROUP + h], acc)
            for v, ok in enumerate((jnp.logical_and(in_win, j >= half), in_win,
                                    jnp.logical_and(in_win, j < 2 * DIL_BLOCK - half))):
                dil_ref[gi, v, h] = jnp.where(ok, acc, NEG_INF)


def _bias_prep(rpb, t5):
    smem = pl.BlockSpec(memory_space=pltpu.SMEM)
    return pl.pallas_call(
        _bias_prep_kernel,
        in_specs=[smem, smem, pl.BlockSpec(memory_space=pltpu.VMEM)],
        out_shape=[jax.ShapeDtypeStruct((2 * NA_WIN_ROWS - 2, NA_HEADS, GRID_W, 2 * GRID_W), jnp.float32),
                   jax.ShapeDtypeStruct((len(DIL_GROUPS), 3, DIL_HEADS_PER_GROUP, DIL_BLOCK, 2 * DIL_BLOCK),
                                        jnp.float32)],
        compiler_params=pltpu.CompilerParams(vmem_limit_bytes=VMEM_LIMIT),
        name="bias_prep",
    )(rpb.reshape(-1), t5.reshape(-1), jnp.asarray(_dil_bucket_table()))


def _dil_kernel(q_ref, kp_ref, kc_ref, kn_ref, vp_ref, vc_ref, vn_ref, bias_ref, o_ref, lse_ref, *, dil):
    lane = lax.broadcasted_iota(jnp.int32, (DIL_BLOCK, LANES), 1)
    low_half = lane < HEAD_DIM
    hb = DIL_BLOCK // 2

    def class_body(rho, carry):
        for hp in range(DIL_HEADS_PER_GROUP // 2):
            cols = slice(hp * LANES, (hp + 1) * LANES)
            qp = q_ref[0, rho, :, cols]
            kw = jnp.concatenate([kp_ref[0, rho, hb:, cols], kc_ref[0, rho, :, cols],
                                  kn_ref[0, rho, :hb, cols]], axis=0)
            vw = jnp.concatenate([vp_ref[0, rho, hb:, cols], vc_ref[0, rho, :, cols],
                                  vn_ref[0, rho, :hb, cols]], axis=0)
            outs, lses = [], []
            for hh in range(2):
                h = 2 * hp + hh
                keep = low_half if hh == 0 else jnp.logical_not(low_half)
                qm = jnp.where(keep, qp, jnp.zeros_like(qp))
                s = _dot_nt(qm, kw) + bias_ref[h]
                m = jnp.max(s, axis=-1, keepdims=True)
                e = jnp.exp(s - m)
                l = jnp.sum(e, axis=-1, keepdims=True)
                pv = _dot(e.astype(jnp.bfloat16), vw)
                outs.append(pv / l)
                lses.append(jnp.broadcast_to(m + jnp.log(l), (DIL_BLOCK, LANES)))
            o_pair = jnp.where(low_half, outs[0], outs[1])
            lse_pair = jnp.where(low_half, lses[0], lses[1])
            if dil == 1:
                o_ref[hp] = o_pair
                lse_ref[hp] = lse_pair
            else:
                o_ref[hp, pl.ds(rho, DIL_BLOCK, stride=dil), :] = o_pair
                lse_ref[hp, pl.ds(rho, DIL_BLOCK, stride=dil), :] = lse_pair
        return carry

    if dil == 1:
        class_body(0, 0)
    else:
        lax.fori_loop(0, dil, class_body, 0)


def _dil_attn(qkv, bias_tab, group, dil, batch, seq):
    L = seq // dil
    nb = L // DIL_BLOCK
    assert nb >= 2
    tq = DIL_BLOCK * dil

    def blk(col, shift):
        return pl.BlockSpec((1, dil, DIL_BLOCK, WIDTH_G),
                            lambda b, n: (b, 0, jnp.clip(n + shift, 0, nb - 1), col))

    def variant(n):
        return jnp.where(n == 0, 0, jnp.where(n == nb - 1, 2, 1))

    n_pairs = WIDTH_G // LANES
    out_spec = pl.BlockSpec((n_pairs, tq, LANES), lambda b, n: (0, b * nb + n, 0))
    return pl.pallas_call(
        functools.partial(_dil_kernel, dil=dil),
        grid=(batch, nb),
        in_specs=[blk(0, 0), blk(1, -1), blk(1, 0), blk(1, 1), blk(2, -1), blk(2, 0), blk(2, 1),
                  pl.BlockSpec((None, None, DIL_HEADS_PER_GROUP, DIL_BLOCK, 2 * DIL_BLOCK),
                               lambda b, n: (group, variant(n), 0, 0, 0))],
        out_specs=[out_spec, out_spec],
        out_shape=[jax.ShapeDtypeStruct((n_pairs, batch * seq, LANES), jnp.float32)] * 2,
        compiler_params=_cparams(2),
        name=f"dil_attn_d{dil}",
    )(qkv, qkv, qkv, qkv, qkv, qkv, qkv, bias_tab)


def _mix_kernel(x_ref, na_ref, o0_ref, l0_ref, o1_ref, l1_ref, o2_ref, l2_ref, qc_ref, gate_ref,
                mkv_ref, wa_ref, wb_ref, wc_ref, wo_ref, fg_ref, rw_ref, rb_ref,
                x1_ref, h_ref, idx_ref, wt_ref):
    T = x_ref.shape[0]
    ob = []
    for hp in range(WIDTH_G // LANES):
        l0, l1, l2 = l0_ref[hp], l1_ref[hp], l2_ref[hp]
        m = jnp.maximum(jnp.maximum(l0, l1), l2)
        e0, e1, e2 = jnp.exp(l0 - m), jnp.exp(l1 - m), jnp.exp(l2 - m)
        ob.append((e0 * o0_ref[hp] + e1 * o1_ref[hp] + e2 * o2_ref[hp]) / (e0 + e1 + e2))
    ob = jnp.concatenate(ob, axis=-1)

    mem_scale = MEM_HEAD_DIM ** -0.5
    oc = []
    for h in range(MEM_HEADS):
        cols = slice(h * MEM_HEAD_DIM, (h + 1) * MEM_HEAD_DIM)
        s = _dot_nt(qc_ref[:, cols], mkv_ref[:, cols]) * mem_scale
        mx = jnp.max(s, axis=-1, keepdims=True)
        e = jnp.exp(s - mx)
        l = jnp.sum(e, axis=-1, keepdims=True)
        pv = _dot(e.astype(jnp.bfloat16), mkv_ref[:, WIDTH_C + h * MEM_HEAD_DIM:WIDTH_C + (h + 1) * MEM_HEAD_DIM])
        oc.append(pv / l)
    oc = jnp.concatenate(oc, axis=-1).astype(jnp.bfloat16)

    y_a = _dot(na_ref[...], wa_ref[...])
    y_b = _dot(ob.astype(jnp.bfloat16), wb_ref[...])
    y_c = _dot(oc, wc_ref[...])
    merged = (gate_ref[:, 0:D_MODEL].astype(jnp.float32) * y_a
              + gate_ref[:, D_MODEL:2 * D_MODEL].astype(jnp.float32) * y_b
              + gate_ref[:, 2 * D_MODEL:3 * D_MODEL].astype(jnp.float32) * y_c)
    x1 = x_ref[...] + _dot(merged.astype(jnp.bfloat16), wo_ref[...])
    x1_ref[...] = x1

    h = _rms(x1, fg_ref[...])
    h_ref[...] = h
    logits = jnp.dot(h, rw_ref[...], preferred_element_type=jnp.float32,
                     precision=lax.Precision.HIGHEST) + rb_ref[...]
    lane = lax.broadcasted_iota(jnp.int32, (T, N_EXPERTS), 1)
    vals, idxs = [], []
    for _ in range(TOP_K):
        mx = jnp.max(logits, axis=-1, keepdims=True)
        sel = jnp.min(jnp.where(logits == mx, lane, N_EXPERTS), axis=-1, keepdims=True)
        vals.append(mx)
        idxs.append(sel)
        logits = jnp.where(lane == sel, -jnp.inf, logits)
    ev = [jnp.exp(v - vals[0]) for v in vals]
    den = ev[0] + ev[1] + ev[2] + ev[3]
    idx_ref[...] = jnp.concatenate(idxs, axis=-1)
    wt_ref[...] = jnp.concatenate([e / den for e in ev], axis=-1)


def _mix_out(x2d, na_out, dil_outs, qc, gates, mkv, wa, wb, wc, wo, ffn_g, router_w, router_b, batch, seq):
    n = x2d.shape[0]
    T = PROJ_TILE
    tiles_per_batch = seq // T
    mem_len = mkv.shape[0] // batch
    row = lambda w: pl.BlockSpec((T, w), lambda i: (i, 0))
    full = lambda a: pl.BlockSpec(a.shape, lambda i: (0,) * a.ndim)
    in_specs = [row(D_MODEL), row(WIDTH_A)]
    args = [x2d, na_out]
    for o, l in dil_outs:
        pair_rows = pl.BlockSpec((WIDTH_G // LANES, T, LANES), lambda i: (0, i, 0))
        in_specs += [pair_rows, pair_rows]
        args += [o, l]
    in_specs += [row(WIDTH_C), row(3 * D_MODEL),
                 pl.BlockSpec((mem_len, 2 * WIDTH_C), lambda i: (i // tiles_per_batch, 0)),
                 full(wa), full(wb), full(wc), full(wo), full(ffn_g), full(router_w), full(router_b)]
    args += [qc, gates, mkv, wa, wb, wc, wo, ffn_g, router_w, router_b]
    return pl.pallas_call(
        _mix_kernel,
        grid=(n // T,),
        in_specs=in_specs,
        out_specs=[row(D_MODEL), row(D_MODEL), row(TOP_K), row(TOP_K)],
        out_shape=[jax.ShapeDtypeStruct((n, D_MODEL), jnp.float32),
                   jax.ShapeDtypeStruct((n, D_MODEL), jnp.float32),
                   jax.ShapeDtypeStruct((n, TOP_K), jnp.int32),
                   jax.ShapeDtypeStruct((n, TOP_K), jnp.float32)],
        compiler_params=_cparams(1),
        name="mix_out",
    )(*args)


def _route_kernel(idx_ref, slot_ref, be_ref, pe_ref, counts_ref, carry_ref, pstart_ref):
    phase = pl.program_id(0)
    i = pl.program_id(1)
    T = idx_ref.shape[0]
    n_blocks = be_ref.shape[0]
    lane = lax.broadcasted_iota(jnp.int32, (T, N_EXPERTS), 1)
    idx = idx_ref[...]
    onehots = [lane == idx[:, k:k + 1] for k in range(TOP_K)]
    member = sum(o.astype(jnp.float32) for o in onehots)
    colsum = jnp.sum(member, axis=0, keepdims=True)

    @pl.when(jnp.logical_and(phase == 0, i == 0))
    def _():
        counts_ref[...] = jnp.zeros_like(counts_ref)

    @pl.when(phase == 0)
    def _():
        counts_ref[0:1, :] += colsum

    @pl.when(jnp.logical_and(phase == 1, i == 0))
    def _():
        counts = counts_ref[...]
        padded = jnp.floor((counts + (MOE_BLOCK - 1)) * (1.0 / MOE_BLOCK)) * MOE_BLOCK
        r = lax.broadcasted_iota(jnp.int32, (N_EXPERTS, N_EXPERTS), 0)
        c = lax.broadcasted_iota(jnp.int32, (N_EXPERTS, N_EXPERTS), 1)
        upper = (r <= c).astype(jnp.float32)
        pad_ends = jnp.dot(padded, upper, preferred_element_type=jnp.float32,
                           precision=lax.Precision.HIGHEST)
        pstart_ref[...] = pad_ends - padded
        carry_ref[...] = jnp.zeros_like(carry_ref)
        pe_ref[...] = pad_ends[0:1, :].astype(jnp.int32)
        blk_start = (lax.broadcasted_iota(jnp.int32, (n_blocks, N_EXPERTS), 0) * MOE_BLOCK).astype(jnp.float32)
        be = jnp.sum((pad_ends[0:1, :] <= blk_start).astype(jnp.int32), axis=1, keepdims=True)
        be_ref[...] = jnp.minimum(be, N_EXPERTS - 1)

    @pl.when(phase == 1)
    def _():
        rr = lax.broadcasted_iota(jnp.int32, (T, T), 0)
        cc = lax.broadcasted_iota(jnp.int32, (T, T), 1)
        lower = (rr > cc).astype(jnp.bfloat16)
        before = _dot(lower, member.astype(jnp.bfloat16))
        base = pstart_ref[0:1, :] + carry_ref[0:1, :] + before
        slots = [jnp.sum(jnp.where(o, base, 0.0), axis=1, keepdims=True) for o in onehots]
        slot_ref[...] = jnp.concatenate(slots, axis=1).astype(jnp.int32)
        carry_ref[0:1, :] += colsum


def _moe_route(topk_idx, n_blocks):
    n = topk_idx.shape[0]
    T = PROJ_TILE
    acc = pltpu.VMEM((8, N_EXPERTS), jnp.float32)
    return pl.pallas_call(
        _route_kernel,
        grid=(2, n // T),
        in_specs=[pl.BlockSpec((T, TOP_K), lambda p, i: (i, 0))],
        out_specs=[pl.BlockSpec((T, TOP_K), lambda p, i: (p * i, 0)),
                   pl.BlockSpec((n_blocks, 1), lambda p, i: (0, 0)),
                   pl.BlockSpec((1, N_EXPERTS), lambda p, i: (0, 0))],
        out_shape=[jax.ShapeDtypeStruct((n, TOP_K), jnp.int32),
                   jax.ShapeDtypeStruct((n_blocks, 1), jnp.int32),
                   jax.ShapeDtypeStruct((1, N_EXPERTS), jnp.int32)],
        scratch_shapes=[acc, acc, acc],
        compiler_params=_cparams(2),
        name="moe_route",
    )(topk_idx)


def _wait_rows(src_hbm, dst_hbm, sem, n_rows):
    def body(r, carry):
        pltpu.make_async_copy(src_hbm.at[pl.ds(0, 1), :], dst_hbm.at[pl.ds(0, 1), :], sem).wait()
        return carry
    lax.fori_loop(0, n_rows, body, 0, unroll=8)


def _dispatch_kernel(pe_ref, slot_ref, h_hbm, xs_hbm, zbuf, sem, zsem):
    i = pl.program_id(0)
    n_steps = pl.num_programs(0)
    T = COMBINE_TILE

    def zero_copy(e):
        end = pe_ref[e]
        start = pl.multiple_of(end - MOE_BLOCK, MOE_BLOCK)
        return pltpu.make_async_copy(zbuf, xs_hbm.at[pl.ds(start, MOE_BLOCK), :], zsem)

    def nonempty(e):
        return pe_ref[e] > (pe_ref[e - 1] if e else 0)

    @pl.when(i == 0)
    def _():
        zbuf[...] = jnp.zeros_like(zbuf)
        for e in range(N_EXPERTS):
            @pl.when(nonempty(e))
            def _():
                zero_copy(e).start()
        for e in range(N_EXPERTS):
            @pl.when(nonempty(e))
            def _():
                zero_copy(e).wait()

        def tail_copy(b):
            return pltpu.make_async_copy(
                zbuf, xs_hbm.at[pl.ds(pl.multiple_of(b * MOE_BLOCK, MOE_BLOCK), MOE_BLOCK), :], zsem)
        first_tail = pe_ref[N_EXPERTS - 1] // MOE_BLOCK
        n_blocks = xs_hbm.shape[0] // MOE_BLOCK
        lax.fori_loop(first_tail, n_blocks, lambda b, c: (tail_copy(b).start(), c)[1], 0)
        lax.fori_loop(first_tail, n_blocks, lambda b, c: (tail_copy(b).wait(), c)[1], 0)

    def token_body(t, carry):
        src = h_hbm.at[pl.ds(i * T + t, 1), :]
        for k in range(TOP_K):
            pltpu.make_async_copy(src, xs_hbm.at[pl.ds(slot_ref[t * TOP_K + k], 1), :],
                                  sem.at[i % 2]).start(priority=k % 2)
        return carry

    lax.fori_loop(0, T, token_body, 0, unroll=2)

    @pl.when(i > 0)
    def _():
        _wait_rows(h_hbm, xs_hbm, sem.at[(i - 1) % 2], T * TOP_K)

    @pl.when(i == n_steps - 1)
    def _():
        _wait_rows(h_hbm, xs_hbm, sem.at[i % 2], T * TOP_K)


def _moe_dispatch(h, slot, pad_ends, n_blocks):
    n = h.shape[0]
    T = COMBINE_TILE
    n_steps = n // T
    slot3 = slot.reshape(n_steps, 1, T * TOP_K)

    def kernel(pe_ref, slot_ref, *rest):
        _dispatch_kernel(pe_ref, slot_ref.at[0], *rest)

    grid_spec = pltpu.PrefetchScalarGridSpec(
        num_scalar_prefetch=1,
        grid=(n_steps,),
        in_specs=[pl.BlockSpec((None, 1, T * TOP_K), lambda i, pe: (i, 0, 0), memory_space=pltpu.SMEM),
                  pl.BlockSpec(memory_space=pl.ANY)],
        out_specs=pl.BlockSpec(memory_space=pl.ANY),
        scratch_shapes=[pltpu.VMEM((MOE_BLOCK, D_MODEL), jnp.float32),
                        pltpu.SemaphoreType.DMA((2,)),
                        pltpu.SemaphoreType.DMA(())],
    )
    return pl.pallas_call(
        kernel,
        grid_spec=grid_spec,
        out_shape=jax.ShapeDtypeStruct((n_blocks * MOE_BLOCK, D_MODEL), jnp.float32),
        compiler_params=_cparams(1),
        name="moe_dispatch",
    )(pad_ends, slot3, h)


def _moe_kernel(be_ref, pe_ref, xs_ref, wgu_ref, bgu_ref, wdn_ref, bdn_ref, y_ref, wgu_bf, wdn_bf):
    i = pl.program_id(0)
    n_valid = pe_ref[N_EXPERTS - 1] // MOE_BLOCK
    new_expert = jnp.logical_or(i == 0, be_ref[i] != be_ref[jnp.maximum(i - 1, 0)])

    @pl.when(jnp.logical_and(i < n_valid, new_expert))
    def _():
        wgu_bf[...] = wgu_ref[0].astype(jnp.bfloat16)
        wdn_bf[...] = wdn_ref[0].astype(jnp.bfloat16)

    @pl.when(i < n_valid)
    def _():
        x = xs_ref[...].astype(jnp.bfloat16)
        gu = _dot(x, wgu_bf[...]) + bgu_ref[0]
        g = jnp.minimum(gu[:, :D_FF], SWIGLU_LIMIT)
        u = jnp.clip(gu[:, D_FF:], -SWIGLU_LIMIT, SWIGLU_LIMIT)
        a = (u + 1.0) * (g * _sigmoid(SWIGLU_ALPHA * g))
        y_ref[...] = _dot(a.astype(jnp.bfloat16), wdn_bf[...]) + bdn_ref[0]

    @pl.when(i >= n_valid)
    def _():
        y_ref[...] = jnp.zeros_like(y_ref)


def _moe_experts(xs, block_expert, pad_ends, wgu, bgu, wdn, bdn):
    n_blocks = block_expert.shape[0]

    def row_blk(i, be, pe):
        return (jnp.minimum(i, jnp.maximum(pe[N_EXPERTS - 1] // MOE_BLOCK - 1, 0)), 0)

    grid_spec = pltpu.PrefetchScalarGridSpec(
        num_scalar_prefetch=2,
        grid=(n_blocks,),
        in_specs=[pl.BlockSpec((MOE_BLOCK, D_MODEL), row_blk),
                  pl.BlockSpec((1, D_MODEL, 2 * D_FF), lambda i, be, pe: (be[i], 0, 0)),
                  pl.BlockSpec((1, 1, 2 * D_FF), lambda i, be, pe: (be[i], 0, 0)),
                  pl.BlockSpec((1, D_FF, D_MODEL), lambda i, be, pe: (be[i], 0, 0)),
                  pl.BlockSpec((1, 1, D_MODEL), lambda i, be, pe: (be[i], 0, 0))],
        out_specs=pl.BlockSpec((MOE_BLOCK, D_MODEL), lambda i, be, pe: (i, 0)),
        scratch_shapes=[pltpu.VMEM((D_MODEL, 2 * D_FF), jnp.bfloat16),
                        pltpu.VMEM((D_FF, D_MODEL), jnp.bfloat16)],
    )
    return pl.pallas_call(
        _moe_kernel,
        grid_spec=grid_spec,
        out_shape=jax.ShapeDtypeStruct((n_blocks * MOE_BLOCK, D_MODEL), jnp.float32),
        compiler_params=_cparams(1),
        name="moe_experts",
    )(block_expert, pad_ends, xs, wgu, bgu, wdn, bdn)


def _gather_rows(src_hbm, idx_ref, dst_ref, sem, n_rows):
    def body(r2, carry):
        for u in range(2):
            r = r2 * 2 + u
            pltpu.make_async_copy(src_hbm.at[pl.ds(idx_ref[r], 1), :],
                                  dst_ref.at[pl.ds(r, 1), :], sem).start(priority=u)
        return carry
    lax.fori_loop(0, n_rows // 2, body, 0, unroll=4)


def _combine_kernel(pos_ref, pos_next_ref, ys_hbm, wt_ref, x1_ref, g_ref, out_ref, gbuf, sem):
    i = pl.program_id(0)
    n_steps = pl.num_programs(0)
    slot = i % 2
    T = COMBINE_TILE
    n_rows = T * TOP_K

    @pl.when(i == 0)
    def _():
        _gather_rows(ys_hbm, pos_ref, gbuf.at[0], sem.at[0], n_rows)

    @pl.when(i + 1 < n_steps)
    def _():
        _gather_rows(ys_hbm, pos_next_ref, gbuf.at[1 - slot], sem.at[1 - slot], n_rows)

    _wait_rows(ys_hbm, gbuf.at[slot], sem.at[slot], n_rows)
    wt = wt_ref[...]
    y = x1_ref[...]
    for k in range(TOP_K):
        y = y + wt[:, k:k + 1] * gbuf[slot, k * T:(k + 1) * T, :]
    out_ref[...] = _rms(y, g_ref[...])


def _moe_combine(ys, pos_tiles, wts, x1, final_g):
    n = x1.shape[0]
    T = COMBINE_TILE
    n_steps = n // T
    smem_blk = lambda shift: pl.BlockSpec(
        (None, 1, T * TOP_K), lambda i: (jnp.minimum(i + shift, n_steps - 1), 0, 0),
        memory_space=pltpu.SMEM)

    def kernel(pos_ref, pos_next_ref, *rest):
        _combine_kernel(pos_ref.at[0], pos_next_ref.at[0], *rest)

    return pl.pallas_call(
        kernel,
        grid=(n_steps,),
        in_specs=[smem_blk(0), smem_blk(1),
                  pl.BlockSpec(memory_space=pl.ANY),
                  pl.BlockSpec((T, TOP_K), lambda i: (i, 0)),
                  pl.BlockSpec((T, D_MODEL), lambda i: (i, 0)),
                  pl.BlockSpec((1, D_MODEL), lambda i: (0, 0))],
        out_specs=pl.BlockSpec((T, D_MODEL), lambda i: (i, 0)),
        out_shape=jax.ShapeDtypeStruct((n, D_MODEL), jnp.float32),
        scratch_shapes=[pltpu.VMEM((2, T * TOP_K, D_MODEL), jnp.float32),
                        pltpu.SemaphoreType.DMA((2,))],
        compiler_params=_cparams(1),
        name="moe_combine",
    )(pos_tiles, pos_tiles, ys, wts, x1, final_g)


def kernel(x, mem, attn_norm_g, mem_norm_g, w_in, w_mem_kv, na_rpb, t5_rel_bias, w_branch_a, w_branch_b,
           w_branch_c, w_out, ffn_norm_g, router_w, router_b, expert_w_gate_up, expert_b_gate_up,
           expert_w_down, expert_b_down, final_norm_g):
    B, S, D = x.shape
    depth = w_in.shape[0]
    bf = jnp.bfloat16
    x2d = x.reshape(B * S, D)
    mem2d = mem.reshape(B * mem.shape[1], D)
    assert depth == 1, "single-layer block (per-layer parameters carry a leading axis of 1)"
    l = 0
    na_bias, dil_bias = _bias_prep(na_rpb[l], t5_rel_bias)
    mkv = _mem_kv(mem2d, mem_norm_g[l][None], w_mem_kv[l].astype(bf))
    na_qkv, d0, d1, d2, qc, gates = _in_proj(x2d, attn_norm_g[l][None], w_in[l].astype(bf), B, S)
    na_out = _na_attn(na_qkv, na_bias, B, S)
    dil_outs = [_dil_attn(qkv, dil_bias, gi, dil, B, S)
                for gi, (qkv, (_, dil)) in enumerate(zip((d0, d1, d2), DIL_GROUPS))]
    x1, h, topk_idx, topk_w = _mix_out(
        x2d, na_out, dil_outs, qc, gates, mkv, w_branch_a[l].astype(bf), w_branch_b[l].astype(bf),
        w_branch_c[l].astype(bf), w_out[l].astype(bf), ffn_norm_g[l][None], router_w[l],
        router_b[l][None], B, S)

    n_blocks = (B * S * TOP_K) // MOE_BLOCK + N_EXPERTS
    slot, block_expert, pad_ends = _moe_route(topk_idx, n_blocks)
    block_expert = block_expert.reshape(n_blocks)
    pad_ends = pad_ends.reshape(N_EXPERTS)
    xs = _moe_dispatch(h, slot, pad_ends, n_blocks)
    ys = _moe_experts(xs, block_expert, pad_ends, expert_w_gate_up[l], expert_b_gate_up[l][:, None, :],
                      expert_w_down[l], expert_b_down[l][:, None, :])
    n_tiles = (B * S) // COMBINE_TILE
    pos_tiles = slot.reshape(n_tiles, COMBINE_TILE, TOP_K).transpose(0, 2, 1).reshape(
        n_tiles, 1, COMBINE_TILE * TOP_K)
    out = _moe_combine(ys, pos_tiles, topk_w, x1, final_norm_g[None])
    return out.reshape(B, S, D)
```

```python
import functools
import math

import numpy as np
import jax
import jax.numpy as jnp
from jax import lax
from jax.experimental import pallas as pl
from jax.experimental.pallas import tpu as pltpu

D_MODEL = 1024
GRID_W = 64
MEM_HEADS = 4
MEM_HEAD_DIM = 128
HEAD_DIM = 64
NA_HEADS = 8
NA_WIN_ROWS = 8
NA_WIN_COLS = 16
DIL_GROUPS = ((128, 1), (512, 4), (2048, 16))
DIL_HEADS_PER_GROUP = 4
DIL_BLOCK = 128
T5_BUCKETS = 32
T5_MAX_DIST = 1024
N_EXPERTS = 32
TOP_K = 4
D_FF = D_MODEL
SWIGLU_ALPHA = 1.702
SWIGLU_LIMIT = 7.0
RMS_EPS = 1e-6
NEG_INF = -1e30

WIDTH_A = NA_HEADS * HEAD_DIM
WIDTH_B = DIL_HEADS_PER_GROUP * len(DIL_GROUPS) * HEAD_DIM
WIDTH_G = DIL_HEADS_PER_GROUP * HEAD_DIM
WIDTH_C = MEM_HEADS * MEM_HEAD_DIM
OFF_QA, OFF_KA, OFF_VA = 0, WIDTH_A, 2 * WIDTH_A
OFF_QB = 3 * WIDTH_A
OFF_KB = OFF_QB + WIDTH_B
OFF_VB = OFF_KB + WIDTH_B
OFF_QC = OFF_VB + WIDTH_B
OFF_GATE = OFF_QC + WIDTH_C
IN_COLS = OFF_GATE + 3 * D_MODEL

LANES = 128
VMEM_LIMIT = 56 * 1024 * 1024

PROJ_TILE = 512
NA_ROWS_PER_STEP = 8
DIL_STEP_TOKENS = 2048
MOE_BLOCK = 256
COMBINE_TILE = 256


def _cparams(n_axes):
    return pltpu.CompilerParams(
        dimension_semantics=("arbitrary",) * n_axes, vmem_limit_bytes=VMEM_LIMIT)


def _rms(x, g):
    return x * lax.rsqrt(jnp.mean(x * x, axis=-1, keepdims=True) + RMS_EPS) * g


def _sigmoid(x):
    return 1.0 / (1.0 + jnp.exp(-x))


def _dot(a, b):
    return jnp.dot(a, b, preferred_element_type=jnp.float32)


def _dot_nt(a, b):
    return lax.dot_general(a, b, (((1,), (1,)), ((), ())), preferred_element_type=jnp.float32)


def _mem_kv_kernel(mem_ref, g_ref, w_ref, out_ref):
    h = _rms(mem_ref[...], g_ref[...]).astype(jnp.bfloat16)
    out_ref[...] = _dot(h, w_ref[...]).astype(jnp.bfloat16)


def _mem_kv(mem2d, g, w):
    rows = mem2d.shape[0]
    return pl.pallas_call(
        _mem_kv_kernel,
        out_shape=jax.ShapeDtypeStruct((rows, 2 * WIDTH_C), jnp.bfloat16),
        name="mem_kv",
        compiler_params=pltpu.CompilerParams(vmem_limit_bytes=VMEM_LIMIT),
    )(mem2d, g, w)


def _in_proj_kernel(x_ref, g_ref, w_ref, na_ref, d0_ref, d1_ref, d2_ref, qc_ref, gate_ref, acc_ref):
    T = x_ref.shape[0]
    h = _rms(x_ref[...], g_ref[...]).astype(jnp.bfloat16)
    q_scale = HEAD_DIM ** -0.5

    for c in range(3):
        acc = _dot(h, w_ref[:, c * WIDTH_A:(c + 1) * WIDTH_A])
        if c == 0:
            acc = acc * q_scale
        na_ref[:, c * WIDTH_A:(c + 1) * WIDTH_A] = acc.astype(jnp.bfloat16)

    for gi, ((_, dil), d_ref) in enumerate(zip(DIL_GROUPS, (d0_ref, d1_ref, d2_ref))):
        for kind, off in enumerate((OFF_QB, OFF_KB, OFF_VB)):
            lo = off + gi * WIDTH_G
            acc = _dot(h, w_ref[:, lo:lo + WIDTH_G])
            if kind == 0:
                acc = acc * q_scale
            cols = slice(kind * WIDTH_G, (kind + 1) * WIDTH_G)
            if dil == 1:
                d_ref[0, 0, :, cols] = acc.astype(jnp.bfloat16)
            else:
                for c in range(WIDTH_G // LANES):
                    acc_ref[c] = acc[:, c * LANES:(c + 1) * LANES]
                for rho in range(dil):
                    d_ref[0, rho, :, cols] = jnp.concatenate(
                        [acc_ref[c, pl.ds(rho, T // dil, stride=dil), :] for c in range(WIDTH_G // LANES)],
                        axis=-1).astype(jnp.bfloat16)

    qc_ref[...] = _dot(h, w_ref[:, OFF_QC:OFF_QC + WIDTH_C]).astype(jnp.bfloat16)

    chunk = 512
    for c in range(3 * D_MODEL // chunk):
        lo = OFF_GATE + c * chunk
        gate_ref[:, c * chunk:(c + 1) * chunk] = _sigmoid(_dot(h, w_ref[:, lo:lo + chunk])).astype(jnp.bfloat16)


def _in_proj(x2d, g, w_bf16, batch, seq):
    n = x2d.shape[0]
    T = PROJ_TILE
    tiles_per_batch = seq // T
    out_shape = [jax.ShapeDtypeStruct((n, 3 * WIDTH_A), jnp.bfloat16)]
    out_specs = [pl.BlockSpec((T, 3 * WIDTH_A), lambda i: (i, 0))]
    for _, dil in DIL_GROUPS:
        out_shape.append(jax.ShapeDtypeStruct((batch, dil, seq // dil, 3 * WIDTH_G), jnp.bfloat16))
        out_specs.append(pl.BlockSpec((1, dil, T // dil, 3 * WIDTH_G),
                                      lambda i: (i // tiles_per_batch, 0, i % tiles_per_batch, 0)))
    out_shape += [jax.ShapeDtypeStruct((n, WIDTH_C), jnp.bfloat16),
                  jax.ShapeDtypeStruct((n, 3 * D_MODEL), jnp.bfloat16)]
    out_specs += [pl.BlockSpec((T, WIDTH_C), lambda i: (i, 0)),
                  pl.BlockSpec((T, 3 * D_MODEL), lambda i: (i, 0))]
    return pl.pallas_call(
        _in_proj_kernel,
        grid=(n // T,),
        in_specs=[pl.BlockSpec((T, D_MODEL), lambda i: (i, 0)),
                  pl.BlockSpec((1, D_MODEL), lambda i: (0, 0)),
                  pl.BlockSpec((D_MODEL, IN_COLS), lambda i: (0, 0))],
        out_specs=out_specs,
        out_shape=out_shape,
        scratch_shapes=[pltpu.VMEM((WIDTH_G // LANES, T, LANES), jnp.float32)],
        compiler_params=_cparams(1),
        name="in_proj",
    )(x2d, g, w_bf16)


def _na_kernel(q_ref, k_ref, v_ref, bias_ref, out_ref, *, rows):
    i = pl.program_id(1)
    lane = lax.broadcasted_iota(jnp.int32, (GRID_W, LANES), 1)
    low_half = lane < HEAD_DIM
    n_win = NA_WIN_ROWS * GRID_W

    def row_body(j, carry):
        qr = i * NA_ROWS_PER_STEP + j
        rs = jnp.clip(qr - NA_WIN_ROWS // 2, 0, rows - NA_WIN_ROWS)
        shift = rs - qr + NA_WIN_ROWS - 1
        q_off = pl.multiple_of(j * GRID_W, GRID_W)
        k_off = pl.multiple_of(rs * GRID_W, GRID_W)
        scores = []
        for h in range(NA_HEADS):
            cols = slice((h // 2) * LANES, (h // 2 + 1) * LANES)
            qp = q_ref[pl.ds(q_off, GRID_W), cols]
            kp = k_ref[pl.ds(k_off, n_win), cols]
            keep = low_half if h % 2 == 0 else jnp.logical_not(low_half)
            qm = jnp.where(keep, qp, jnp.zeros_like(qp))
            bias = jnp.concatenate(
                [bias_ref[shift + 2 * p, h] for p in range(NA_WIN_ROWS // 2)], axis=-1)
            scores.append(_dot_nt(qm, kp) + bias)
        probs = []
        for s in scores:
            e = jnp.exp(s - jnp.max(s, axis=-1, keepdims=True))
            probs.append((e.astype(jnp.bfloat16), jnp.sum(e, axis=-1, keepdims=True)))
        outs = []
        for h, (p, l) in enumerate(probs):
            cols = slice((h // 2) * LANES, (h // 2 + 1) * LANES)
            vp = v_ref[pl.ds(k_off, n_win), cols]
            outs.append(_dot(p, vp) / l)
        for hp in range(NA_HEADS // 2):
            cols = slice(hp * LANES, (hp + 1) * LANES)
            out_ref[pl.ds(q_off, GRID_W), cols] = jnp.where(
                low_half, outs[2 * hp], outs[2 * hp + 1]).astype(jnp.bfloat16)
        return carry

    lax.fori_loop(0, NA_ROWS_PER_STEP, row_body, 0)


def _na_attn(na_qkv, bias_tab, batch, seq):
    rows = seq // GRID_W
    steps = rows // NA_ROWS_PER_STEP
    tq = NA_ROWS_PER_STEP * GRID_W
    n = na_qkv.shape[0]
    return pl.pallas_call(
        functools.partial(_na_kernel, rows=rows),
        grid=(batch, steps),
        in_specs=[pl.BlockSpec((tq, WIDTH_A), lambda b, i: (b * steps + i, 0)),
                  pl.BlockSpec((seq, WIDTH_A), lambda b, i: (b, 1)),
                  pl.BlockSpec((seq, WIDTH_A), lambda b, i: (b, 2)),
                  pl.BlockSpec(bias_tab.shape, lambda b, i: (0, 0, 0, 0))],
        out_specs=pl.BlockSpec((tq, WIDTH_A), lambda b, i: (b * steps + i, 0)),
        out_shape=jax.ShapeDtypeStruct((n, WIDTH_A), jnp.bfloat16),
        compiler_params=_cparams(2),
        name="na_attn",
    )(na_qkv, na_qkv, na_qkv, bias_tab)


def _t5_bucket_np(rel):
    half = T5_BUCKETS // 2
    max_exact = half // 2
    ret = np.where(rel > 0, half, 0)
    n = np.abs(rel)
    nf = np.maximum(n, 1).astype(np.float32)
    large = max_exact + (np.log(nf / np.float32(max_exact)) / np.float32(math.log(T5_MAX_DIST / max_exact))
                         * np.float32(half - max_exact)).astype(np.int32)
    large = np.minimum(large, half - 1)
    return ret + np.where(n < max_exact, n, large)


def _dil_bucket_table():
    q = np.arange(DIL_BLOCK)[:, None]
    j = np.arange(2 * DIL_BLOCK)[None, :]
    tabs = []
    for window, dil in DIL_GROUPS:
        half = (window // 2) // dil
        assert half == DIL_BLOCK // 2
        rel = (j - half) - q
        tabs.append(np.where(np.abs(rel) <= half, _t5_bucket_np(rel * dil), -1))
    return np.stack(tabs).astype(np.int32)


def _bias_prep_kernel(rpb_ref, t5_ref, bucket_ref, na_ref, dil_ref):
    n_dc = 2 * NA_WIN_COLS - 1
    n_dr = 2 * NA_WIN_ROWS - 1
    qc = lax.broadcasted_iota(jnp.int32, (GRID_W, 2 * GRID_W), 0)
    lane = lax.broadcasted_iota(jnp.int32, (GRID_W, 2 * GRID_W), 1)
    second = lane >= GRID_W
    kc = jnp.where(second, lane - GRID_W, lane)
    diff = kc - qc
    cs = jnp.clip(qc - NA_WIN_COLS // 2, 0, GRID_W - NA_WIN_COLS)
    valid = jnp.logical_and(kc >= cs, kc < cs + NA_WIN_COLS)

    def na_tile(mh, carry):
        m = mh // NA_HEADS
        h = mh - m * NA_HEADS
        base = (h * n_dr + m) * n_dc
        acc = jnp.zeros((GRID_W, 2 * GRID_W), jnp.float32)
        for d in range(n_dc):
            v = jnp.where(second, rpb_ref[base + n_dc + d], rpb_ref[base + d])
            acc = jnp.where(diff == d - (NA_WIN_COLS - 1), v, acc)
        na_ref[m, h] = jnp.where(valid, acc, NEG_INF)
        return carry

    lax.fori_loop(0, (n_dr - 1) * NA_HEADS, na_tile, 0)

    j = lax.broadcasted_iota(jnp.int32, (DIL_BLOCK, 2 * DIL_BLOCK), 1)
    half = DIL_BLOCK // 2
    n_heads = DIL_HEADS_PER_GROUP * len(DIL_GROUPS)
    for gi in range(len(DIL_GROUPS)):
        bucket = bucket_ref[gi]
        in_win = bucket >= 0
        for h in range(DIL_HEADS_PER_GROUP):
            acc = jnp.zeros((DIL_BLOCK, 2 * DIL_BLOCK), jnp.float32)
            for b in range(T5_BUCKETS):
                acc = jnp.where(bucket == b, t5_ref[b * n_heads + gi * DIL_HEADS_PER_GROUP + h], acc)
            for v, ok in enumerate((jnp.logical_and(in_win, j >= half), in_win,
                                    jnp.logical_and(in_win, j < 2 * DIL_BLOCK - half))):
                dil_ref[gi, v, h] = jnp.where(ok, acc, NEG_INF)


def _bias_prep(rpb, t5):
    smem = pl.BlockSpec(memory_space=pltpu.SMEM)
    return pl.pallas_call(
        _bias_prep_kernel,
        in_specs=[smem, smem, pl.BlockSpec(memory_space=pltpu.VMEM)],
        out_shape=[jax.ShapeDtypeStruct((2 * NA_WIN_ROWS - 2, NA_HEADS, GRID_W, 2 * GRID_W), jnp.float32),
                   jax.ShapeDtypeStruct((len(DIL_GROUPS), 3, DIL_HEADS_PER_GROUP, DIL_BLOCK, 2 * DIL_BLOCK),
                                        jnp.float32)],
        compiler_params=pltpu.CompilerParams(vmem_limit_bytes=VMEM_LIMIT),
        name="bias_prep",
    )(rpb.reshape(-1), t5.reshape(-1), jnp.asarray(_dil_bucket_table()))


def _dil_kernel(q_ref, kp_ref, kc_ref, kn_ref, vp_ref, vc_ref, vn_ref, bias_ref, o_ref, lse_ref, *, dil):
    lane = lax.broadcasted_iota(jnp.int32, (DIL_BLOCK, LANES), 1)
    low_half = lane < HEAD_DIM
    hb = DIL_BLOCK // 2

    span = pl.program_id(1)
    n_spans = pl.num_programs(1)
    nbs = kc_ref.shape[2] // DIL_BLOCK

    def window(prev_ref, cur_ref, next_ref, rho, blk, cols):
        lo = blk * DIL_BLOCK
        first = prev_ref[0, rho, hb:, cols] if blk == 0 else cur_ref[0, rho, lo - hb:lo, cols]
        last = (next_ref[0, rho, :hb, cols] if blk == nbs - 1
                else cur_ref[0, rho, lo + DIL_BLOCK:lo + DIL_BLOCK + hb, cols])
        return jnp.concatenate([first, cur_ref[0, rho, lo:lo + DIL_BLOCK, cols], last], axis=0)

    def unit(rho, blk):
        variant = 1
        if blk == nbs - 1:
            variant = jnp.where(span == n_spans - 1, 2, variant)
        if blk == 0:
            variant = jnp.where(span == 0, 0, variant)
        scores = []
        for h in range(DIL_HEADS_PER_GROUP):
            cols = slice((h // 2) * LANES, (h // 2 + 1) * LANES)
            qp = q_ref[0, rho, blk * DIL_BLOCK:(blk + 1) * DIL_BLOCK, cols]
            keep = low_half if h % 2 == 0 else jnp.logical_not(low_half)
            qm = jnp.where(keep, qp, jnp.zeros_like(qp))
            kw = window(kp_ref, kc_ref, kn_ref, rho, blk, cols)
            scores.append(_dot_nt(qm, kw) + bias_ref[variant, h])
        probs = []
        for s in scores:
            m = jnp.max(s, axis=-1, keepdims=True)
            e = jnp.exp(s - m)
            probs.append((e.astype(jnp.bfloat16), m, jnp.sum(e, axis=-1, keepdims=True)))
        outs, lses = [], []
        for h, (p, m, l) in enumerate(probs):
            cols = slice((h // 2) * LANES, (h // 2 + 1) * LANES)
            outs.append(_dot(p, window(vp_ref, vc_ref, vn_ref, rho, blk, cols)) / l)
            lses.append(jnp.broadcast_to(m + jnp.log(l), (DIL_BLOCK, LANES)))
        for hp in range(DIL_HEADS_PER_GROUP // 2):
            o_pair = jnp.where(low_half, outs[2 * hp], outs[2 * hp + 1])
            lse_pair = jnp.where(low_half, lses[2 * hp], lses[2 * hp + 1])
            if dil == 1:
                rows = slice(blk * DIL_BLOCK, (blk + 1) * DIL_BLOCK)
                o_ref[hp, rows, :] = o_pair
                lse_ref[hp, rows, :] = lse_pair
            else:
                rows = pl.ds(blk * DIL_BLOCK * dil + rho, DIL_BLOCK, stride=dil)
                o_ref[hp, rows, :] = o_pair
                lse_ref[hp, rows, :] = lse_pair

    def class_body(rho, carry):
        for blk in range(nbs):
            unit(rho, blk)
        return carry

    if dil == 1:
        class_body(0, 0)
    else:
        lax.fori_loop(0, dil, class_body, 0, unroll=2 if nbs == 1 else 1)


def _dil_attn(qkv, bias_tab, group, dil, batch, seq):
    L = seq // dil
    nb = L // DIL_BLOCK
    assert nb >= 2
    tq = DIL_STEP_TOKENS
    span = tq // dil
    nbs = span // DIL_BLOCK
    n_spans = L // span

    def cur(col):
        return pl.BlockSpec((1, dil, span, WIDTH_G), lambda b, n: (b, 0, n, col))

    def edge(col, shift):
        return pl.BlockSpec((1, dil, DIL_BLOCK, WIDTH_G),
                            lambda b, n: (b, 0, jnp.clip(n * nbs + (shift if shift < 0 else nbs), 0, nb - 1), col))

    n_pairs = WIDTH_G // LANES
    out_spec = pl.BlockSpec((n_pairs, tq, LANES), lambda b, n: (0, b * n_spans + n, 0))
    return pl.pallas_call(
        functools.partial(_dil_kernel, dil=dil),
        grid=(batch, n_spans),
        in_specs=[cur(0), edge(1, -1), cur(1), edge(1, 1), edge(2, -1), cur(2), edge(2, 1),
                  pl.BlockSpec((None, 3, DIL_HEADS_PER_GROUP, DIL_BLOCK, 2 * DIL_BLOCK),
                               lambda b, n: (group, 0, 0, 0, 0))],
        out_specs=[out_spec, out_spec],
        out_shape=[jax.ShapeDtypeStruct((n_pairs, batch * seq, LANES), jnp.float32)] * 2,
        compiler_params=_cparams(2),
        name=f"dil_attn_d{dil}",
    )(qkv, qkv, qkv, qkv, qkv, qkv, qkv, bias_tab)


def _mix_kernel(x_ref, na_ref, o0_ref, l0_ref, o1_ref, l1_ref, o2_ref, l2_ref, qc_ref, gate_ref,
                mkv_ref, wa_ref, wb_ref, wc_ref, wo_ref, fg_ref, rw_ref, rb_ref,
                x1_ref, h_ref, idx_ref, wt_ref):
    T = x_ref.shape[0]
    ob = []
    for hp in range(WIDTH_G // LANES):
        l0, l1, l2 = l0_ref[hp], l1_ref[hp], l2_ref[hp]
        m = jnp.maximum(jnp.maximum(l0, l1), l2)
        e0, e1, e2 = jnp.exp(l0 - m), jnp.exp(l1 - m), jnp.exp(l2 - m)
        ob.append((e0 * o0_ref[hp] + e1 * o1_ref[hp] + e2 * o2_ref[hp]) / (e0 + e1 + e2))
    ob = jnp.concatenate(ob, axis=-1)

    mem_scale = MEM_HEAD_DIM ** -0.5
    scores = []
    for h in range(MEM_HEADS):
        cols = slice(h * MEM_HEAD_DIM, (h + 1) * MEM_HEAD_DIM)
        scores.append(_dot_nt(qc_ref[:, cols], mkv_ref[:, cols]) * mem_scale)
    probs = []
    for s in scores:
        e = jnp.exp(s - jnp.max(s, axis=-1, keepdims=True))
        probs.append((e.astype(jnp.bfloat16), jnp.sum(e, axis=-1, keepdims=True)))
    oc = []
    for h, (p, l) in enumerate(probs):
        oc.append(_dot(p, mkv_ref[:, WIDTH_C + h * MEM_HEAD_DIM:WIDTH_C + (h + 1) * MEM_HEAD_DIM]) / l)
    oc = jnp.concatenate(oc, axis=-1).astype(jnp.bfloat16)

    y_a = _dot(na_ref[...], wa_ref[...])
    y_b = _dot(ob.astype(jnp.bfloat16), wb_ref[...])
    y_c = _dot(oc, wc_ref[...])
    merged = (gate_ref[:, 0:D_MODEL].astype(jnp.float32) * y_a
              + gate_ref[:, D_MODEL:2 * D_MODEL].astype(jnp.float32) * y_b
              + gate_ref[:, 2 * D_MODEL:3 * D_MODEL].astype(jnp.float32) * y_c)
    x1 = x_ref[...] + _dot(merged.astype(jnp.bfloat16), wo_ref[...])
    x1_ref[...] = x1

    h = _rms(x1, fg_ref[...])
    h_ref[...] = h
    logits = lax.dot_general(rw_ref[...], h, (((1,), (1,)), ((), ())), preferred_element_type=jnp.float32,
                             precision=lax.Precision.HIGHEST) + rb_ref[...]
    expert = lax.broadcasted_iota(jnp.int32, (N_EXPERTS, T), 0)
    vals, idxs = [], []
    for _ in range(TOP_K):
        mx = jnp.max(logits, axis=0, keepdims=True)
        sel = jnp.min(jnp.where(logits == mx, expert, N_EXPERTS), axis=0, keepdims=True)
        vals.append(mx)
        idxs.append(sel)
        logits = jnp.where(expert == sel, -jnp.inf, logits)
    ev = [jnp.exp(v - vals[0]) for v in vals]
    den = ev[0] + ev[1] + ev[2] + ev[3]
    idx_ref[...] = jnp.concatenate(idxs, axis=0)
    wt_ref[...] = jnp.concatenate([e / den for e in ev], axis=0)


def _mix_out(x2d, na_out, dil_outs, qc, gates, mkv, wa, wb, wc, wo, ffn_g, router_w, router_b, batch, seq):
    n = x2d.shape[0]
    T = PROJ_TILE
    tiles_per_batch = seq // T
    mem_len = mkv.shape[0] // batch
    row = lambda w: pl.BlockSpec((T, w), lambda i: (i, 0))
    full = lambda a: pl.BlockSpec(a.shape, lambda i: (0,) * a.ndim)
    in_specs = [row(D_MODEL), row(WIDTH_A)]
    args = [x2d, na_out]
    for o, l in dil_outs:
        pair_rows = pl.BlockSpec((WIDTH_G // LANES, T, LANES), lambda i: (0, i, 0))
        in_specs += [pair_rows, pair_rows]
        args += [o, l]
    in_specs += [row(WIDTH_C), row(3 * D_MODEL),
                 pl.BlockSpec((mem_len, 2 * WIDTH_C), lambda i: (i // tiles_per_batch, 0)),
                 full(wa), full(wb), full(wc), full(wo), full(ffn_g), full(router_w), full(router_b)]
    args += [qc, gates, mkv, wa, wb, wc, wo, ffn_g, router_w, router_b]
    return pl.pallas_call(
        _mix_kernel,
        grid=(n // T,),
        in_specs=in_specs,
        out_specs=[row(D_MODEL), row(D_MODEL), pl.BlockSpec((TOP_K, T), lambda i: (0, i)),
                   pl.BlockSpec((TOP_K, T), lambda i: (0, i))],
        out_shape=[jax.ShapeDtypeStruct((n, D_MODEL), jnp.float32),
                   jax.ShapeDtypeStruct((n, D_MODEL), jnp.float32),
                   jax.ShapeDtypeStruct((TOP_K, n), jnp.int32),
                   jax.ShapeDtypeStruct((TOP_K, n), jnp.float32)],
        compiler_params=_cparams(1),
        name="mix_out",
    )(*args)


def _route_kernel(idx_ref, slot_ref, be_ref, pe_ref, counts_ref, carry_ref, pstart_ref):
    phase = pl.program_id(0)
    i = pl.program_id(1)
    T = idx_ref.shape[1]
    n_blocks = be_ref.shape[1]
    expert = lax.broadcasted_iota(jnp.int32, (N_EXPERTS, T), 0)
    idx = idx_ref[...]
    onehots = [expert == idx[k:k + 1, :] for k in range(TOP_K)]
    member = sum(o.astype(jnp.float32) for o in onehots)
    rowsum = jnp.sum(member, axis=1, keepdims=True)

    @pl.when(jnp.logical_and(phase == 0, i == 0))
    def _():
        counts_ref[...] = jnp.zeros_like(counts_ref)

    @pl.when(phase == 0)
    def _():
        counts_ref[...] += rowsum

    @pl.when(jnp.logical_and(phase == 1, i == 0))
    def _():
        counts = counts_ref[...]
        padded = jnp.floor((counts + (MOE_BLOCK - 1)) * (1.0 / MOE_BLOCK)) * MOE_BLOCK
        r = lax.broadcasted_iota(jnp.int32, (N_EXPERTS, N_EXPERTS), 0)
        c = lax.broadcasted_iota(jnp.int32, (N_EXPERTS, N_EXPERTS), 1)
        lower_incl = (r >= c).astype(jnp.float32)
        pad_ends = jnp.dot(lower_incl, padded, preferred_element_type=jnp.float32,
                           precision=lax.Precision.HIGHEST)
        pstart_ref[...] = pad_ends - padded
        carry_ref[...] = jnp.zeros_like(carry_ref)
        pe_ref[...] = pad_ends[:, 0:1].astype(jnp.int32)
        blk_start = (lax.broadcasted_iota(jnp.int32, (N_EXPERTS, n_blocks), 1) * MOE_BLOCK).astype(jnp.float32)
        be = jnp.sum((pad_ends[:, 0:1] <= blk_start).astype(jnp.int32), axis=0, keepdims=True)
        be_ref[...] = jnp.minimum(be, N_EXPERTS - 1)

    @pl.when(phase == 1)
    def _():
        rr = lax.broadcasted_iota(jnp.int32, (T, T), 0)
        cc = lax.broadcasted_iota(jnp.int32, (T, T), 1)
        earlier = (rr < cc).astype(jnp.bfloat16)
        before = _dot(member.astype(jnp.bfloat16), earlier)
        base = pstart_ref[:, 0:1] + carry_ref[:, 0:1] + before
        slots = [jnp.sum(jnp.where(o, base, 0.0), axis=0, keepdims=True) for o in onehots]
        slot_ref[...] = jnp.concatenate(slots, axis=0).astype(jnp.int32)
        carry_ref[...] += rowsum


def _moe_route(topk_idx, n_blocks):
    n = topk_idx.shape[1]
    T = PROJ_TILE
    acc = pltpu.VMEM((N_EXPERTS, LANES), jnp.float32)
    return pl.pallas_call(
        _route_kernel,
        grid=(2, n // T),
        in_specs=[pl.BlockSpec((TOP_K, T), lambda p, i: (0, i))],
        out_specs=[pl.BlockSpec((TOP_K, T), lambda p, i: (0, p * i)),
                   pl.BlockSpec((1, n_blocks), lambda p, i: (0, 0)),
                   pl.BlockSpec((N_EXPERTS, 1), lambda p, i: (0, 0))],
        out_shape=[jax.ShapeDtypeStruct((TOP_K, n), jnp.int32),
                   jax.ShapeDtypeStruct((1, n_blocks), jnp.int32),
                   jax.ShapeDtypeStruct((N_EXPERTS, 1), jnp.int32)],
        scratch_shapes=[acc, acc, acc],
        compiler_params=_cparams(2),
        name="moe_route",
    )(topk_idx)


def _wait_rows(src_hbm, dst_hbm, sem, n_rows):
    def body(r, carry):
        pltpu.make_async_copy(src_hbm.at[pl.ds(0, 1), :], dst_hbm.at[pl.ds(0, 1), :], sem).wait()
        return carry
    lax.fori_loop(0, n_rows, body, 0, unroll=8)


def _dispatch_kernel(pe_ref, slot_ref, h_ref, xs_hbm, zbuf, sem, zsem):
    i = pl.program_id(0)
    T = h_ref.shape[0]

    def zero_copy(e):
        end = pe_ref[e]
        start = pl.multiple_of(end - MOE_BLOCK, MOE_BLOCK)
        return pltpu.make_async_copy(zbuf, xs_hbm.at[pl.ds(start, MOE_BLOCK), :], zsem)

    def nonempty(e):
        return pe_ref[e] > (pe_ref[e - 1] if e else 0)

    @pl.when(i == 0)
    def _():
        zbuf[...] = jnp.zeros_like(zbuf)
        for e in range(N_EXPERTS):
            @pl.when(nonempty(e))
            def _():
                zero_copy(e).start()
        for e in range(N_EXPERTS):
            @pl.when(nonempty(e))
            def _():
                zero_copy(e).wait()

        def tail_copy(b):
            return pltpu.make_async_copy(
                zbuf, xs_hbm.at[pl.ds(pl.multiple_of(b * MOE_BLOCK, MOE_BLOCK), MOE_BLOCK), :], zsem)
        first_tail = pe_ref[N_EXPERTS - 1] // MOE_BLOCK
        n_blocks = xs_hbm.shape[0] // MOE_BLOCK
        lax.fori_loop(first_tail, n_blocks, lambda b, c: (tail_copy(b).start(), c)[1], 0)
        lax.fori_loop(first_tail, n_blocks, lambda b, c: (tail_copy(b).wait(), c)[1], 0)

    def token_body(t, carry):
        src = h_ref.at[pl.ds(t, 1), :]
        for k in range(TOP_K):
            pltpu.make_async_copy(src, xs_hbm.at[pl.ds(slot_ref[k * T + t], 1), :],
                                  sem).start(priority=k % 2)
        return carry

    lax.fori_loop(0, T, token_body, 0, unroll=2)
    _wait_rows(h_ref, xs_hbm, sem, T * TOP_K)


def _moe_dispatch(h, slot3, pad_ends, n_blocks):
    n = h.shape[0]
    T = COMBINE_TILE
    n_steps = n // T

    def kernel(pe_ref, slot_ref, *rest):
        _dispatch_kernel(pe_ref, slot_ref.at[0], *rest)

    grid_spec = pltpu.PrefetchScalarGridSpec(
        num_scalar_prefetch=1,
        grid=(n_steps,),
        in_specs=[pl.BlockSpec((None, 1, T * TOP_K), lambda i, pe: (i, 0, 0), memory_space=pltpu.SMEM),
                  pl.BlockSpec((T, D_MODEL), lambda i, pe: (i, 0))],
        out_specs=pl.BlockSpec(memory_space=pl.ANY),
        scratch_shapes=[pltpu.VMEM((MOE_BLOCK, D_MODEL), jnp.float32),
                        pltpu.SemaphoreType.DMA(()),
                        pltpu.SemaphoreType.DMA(())],
    )
    return pl.pallas_call(
        kernel,
        grid_spec=grid_spec,
        out_shape=jax.ShapeDtypeStruct((n_blocks * MOE_BLOCK, D_MODEL), jnp.float32),
        compiler_params=_cparams(1),
        name="moe_dispatch",
    )(pad_ends, slot3, h)


def _moe_kernel(be_ref, pe_ref, xs_ref, wgu_ref, bgu_ref, wdn_ref, bdn_ref, y_ref, wgu_bf, wdn_bf):
    i = pl.program_id(0)
    n_valid = pe_ref[N_EXPERTS - 1] // MOE_BLOCK
    new_expert = jnp.logical_or(i == 0, be_ref[i] != be_ref[jnp.maximum(i - 1, 0)])

    @pl.when(jnp.logical_and(i < n_valid, new_expert))
    def _():
        wgu_bf[...] = wgu_ref[0].astype(jnp.bfloat16)
        wdn_bf[...] = wdn_ref[0].astype(jnp.bfloat16)

    @pl.when(i < n_valid)
    def _():
        x = xs_ref[...].astype(jnp.bfloat16)
        gu = _dot(x, wgu_bf[...]) + bgu_ref[0]
        g = jnp.minimum(gu[:, :D_FF], SWIGLU_LIMIT)
        u = jnp.clip(gu[:, D_FF:], -SWIGLU_LIMIT, SWIGLU_LIMIT)
        a = (u + 1.0) * (g * _sigmoid(SWIGLU_ALPHA * g))
        y_ref[...] = _dot(a.astype(jnp.bfloat16), wdn_bf[...]) + bdn_ref[0]

    @pl.when(i >= n_valid)
    def _():
        y_ref[...] = jnp.zeros_like(y_ref)


def _moe_experts(xs, block_expert, pad_ends, wgu, bgu, wdn, bdn):
    n_blocks = block_expert.shape[0]

    def row_blk(i, be, pe):
        return (jnp.minimum(i, jnp.maximum(pe[N_EXPERTS - 1] // MOE_BLOCK - 1, 0)), 0)

    grid_spec = pltpu.PrefetchScalarGridSpec(
        num_scalar_prefetch=2,
        grid=(n_blocks,),
        in_specs=[pl.BlockSpec((MOE_BLOCK, D_MODEL), row_blk),
                  pl.BlockSpec((1, D_MODEL, 2 * D_FF), lambda i, be, pe: (be[i], 0, 0)),
                  pl.BlockSpec((1, 1, 2 * D_FF), lambda i, be, pe: (be[i], 0, 0)),
                  pl.BlockSpec((1, D_FF, D_MODEL), lambda i, be, pe: (be[i], 0, 0)),
                  pl.BlockSpec((1, 1, D_MODEL), lambda i, be, pe: (be[i], 0, 0))],
        out_specs=pl.BlockSpec((MOE_BLOCK, D_MODEL), lambda i, be, pe: (i, 0)),
        scratch_shapes=[pltpu.VMEM((D_MODEL, 2 * D_FF), jnp.bfloat16),
                        pltpu.VMEM((D_FF, D_MODEL), jnp.bfloat16)],
    )
    return pl.pallas_call(
        _moe_kernel,
        grid_spec=grid_spec,
        out_shape=jax.ShapeDtypeStruct((n_blocks * MOE_BLOCK, D_MODEL), jnp.float32),
        compiler_params=_cparams(1),
        name="moe_experts",
    )(block_expert, pad_ends, xs, wgu, bgu, wdn, bdn)


def _gather_rows(src_hbm, idx_ref, dst_ref, sem, n_rows):
    def body(r2, carry):
        for u in range(2):
            r = r2 * 2 + u
            pltpu.make_async_copy(src_hbm.at[pl.ds(idx_ref[r], 1), :],
                                  dst_ref.at[pl.ds(r, 1), :], sem).start(priority=u)
        return carry
    lax.fori_loop(0, n_rows // 2, body, 0, unroll=4)


def _combine_kernel(pos_ref, pos_next_ref, ys_hbm, wt_ref, x1_ref, g_ref, out_ref, gbuf, sem):
    i = pl.program_id(0)
    n_steps = pl.num_programs(0)
    slot = i % 2
    T = COMBINE_TILE
    n_rows = T * TOP_K

    @pl.when(i == 0)
    def _():
        _gather_rows(ys_hbm, pos_ref, gbuf.at[0], sem.at[0], n_rows)

    @pl.when(i + 1 < n_steps)
    def _():
        _gather_rows(ys_hbm, pos_next_ref, gbuf.at[1 - slot], sem.at[1 - slot], n_rows)

    _wait_rows(ys_hbm, gbuf.at[slot], sem.at[slot], n_rows)
    wt = wt_ref[...]
    y = x1_ref[...]
    for k in range(TOP_K):
        y = y + wt[:, k:k + 1] * gbuf[slot, k * T:(k + 1) * T, :]
    out_ref[...] = _rms(y, g_ref[...])


def _moe_combine(ys, pos_tiles, wts, x1, final_g):
    n = x1.shape[0]
    T = COMBINE_TILE
    n_steps = n // T
    smem_blk = lambda shift: pl.BlockSpec(
        (None, 1, T * TOP_K), lambda i: (jnp.minimum(i + shift, n_steps - 1), 0, 0),
        memory_space=pltpu.SMEM)

    def kernel(pos_ref, pos_next_ref, *rest):
        _combine_kernel(pos_ref.at[0], pos_next_ref.at[0], *rest)

    return pl.pallas_call(
        kernel,
        grid=(n_steps,),
        in_specs=[smem_blk(0), smem_blk(1),
                  pl.BlockSpec(memory_space=pl.ANY),
                  pl.BlockSpec((T, TOP_K), lambda i: (i, 0)),
                  pl.BlockSpec((T, D_MODEL), lambda i: (i, 0)),
                  pl.BlockSpec((1, D_MODEL), lambda i: (0, 0))],
        out_specs=pl.BlockSpec((T, D_MODEL), lambda i: (i, 0)),
        out_shape=jax.ShapeDtypeStruct((n, D_MODEL), jnp.float32),
        scratch_shapes=[pltpu.VMEM((2, T * TOP_K, D_MODEL), jnp.float32),
                        pltpu.SemaphoreType.DMA((2,))],
        compiler_params=_cparams(1),
        name="moe_combine",
    )(pos_tiles, pos_tiles, ys, wts, x1, final_g)


def kernel(x, mem, attn_norm_g, mem_norm_g, w_in, w_mem_kv, na_rpb, t5_rel_bias, w_branch_a, w_branch_b,
           w_branch_c, w_out, ffn_norm_g, router_w, router_b, expert_w_gate_up, expert_b_gate_up,
           expert_w_down, expert_b_down, final_norm_g):
    B, S, D = x.shape
    depth = w_in.shape[0]
    bf = jnp.bfloat16
    x2d = x.reshape(B * S, D)
    mem2d = mem.reshape(B * mem.shape[1], D)
    assert depth == 1, "single-layer block (per-layer parameters carry a leading axis of 1)"
    l = 0
    na_bias, dil_bias = _bias_prep(na_rpb[l], t5_rel_bias)
    mkv = _mem_kv(mem2d, mem_norm_g[l][None], w_mem_kv[l].astype(bf))
    na_qkv, d0, d1, d2, qc, gates = _in_proj(x2d, attn_norm_g[l][None], w_in[l].astype(bf), B, S)
    na_out = _na_attn(na_qkv, na_bias, B, S)
    dil_outs = [_dil_attn(qkv, dil_bias, gi, dil, B, S)
                for gi, (qkv, (_, dil)) in enumerate(zip((d0, d1, d2), DIL_GROUPS))]
    x1, h, topk_idx, topk_w = _mix_out(
        x2d, na_out, dil_outs, qc, gates, mkv, w_branch_a[l].astype(bf), w_branch_b[l].astype(bf),
        w_branch_c[l].astype(bf), w_out[l].astype(bf), ffn_norm_g[l][None], router_w[l].T,
        router_b[l][:, None], B, S)

    n_blocks = (B * S * TOP_K) // MOE_BLOCK + N_EXPERTS
    slot, block_expert, pad_ends = _moe_route(topk_idx, n_blocks)
    block_expert = block_expert.reshape(n_blocks)
    pad_ends = pad_ends.reshape(N_EXPERTS)
    n_tiles = (B * S) // COMBINE_TILE
    slot_tiles = slot.reshape(TOP_K, n_tiles, COMBINE_TILE).transpose(1, 0, 2).reshape(
        n_tiles, 1, TOP_K * COMBINE_TILE)
    xs = _moe_dispatch(h, slot_tiles, pad_ends, n_blocks)
    ys = _moe_experts(xs, block_expert, pad_ends, expert_w_gate_up[l], expert_b_gate_up[l][:, None, :],
                      expert_w_down[l], expert_b_down[l][:, None, :])
    out = _moe_combine(ys, slot_tiles, topk_w.T, x1, final_norm_g[None])
    return out.reshape(B, S, D)
```

```python
import functools
import math

import numpy as np
import jax
import jax.numpy as jnp
from jax import lax
from jax.experimental import pallas as pl
from jax.experimental.pallas import tpu as pltpu

D_MODEL = 1024
GRID_W = 64
MEM_HEADS = 4
MEM_HEAD_DIM = 128
HEAD_DIM = 64
NA_HEADS = 8
NA_WIN_ROWS = 8
NA_WIN_COLS = 16
DIL_GROUPS = ((128, 1), (512, 4), (2048, 16))
DIL_HEADS_PER_GROUP = 4
DIL_BLOCK = 128
T5_BUCKETS = 32
T5_MAX_DIST = 1024
N_EXPERTS = 32
TOP_K = 4
D_FF = D_MODEL
SWIGLU_ALPHA = 1.702
SWIGLU_LIMIT = 7.0
RMS_EPS = 1e-6
NEG_INF = -1e30

WIDTH_A = NA_HEADS * HEAD_DIM
WIDTH_B = DIL_HEADS_PER_GROUP * len(DIL_GROUPS) * HEAD_DIM
WIDTH_G = DIL_HEADS_PER_GROUP * HEAD_DIM
WIDTH_C = MEM_HEADS * MEM_HEAD_DIM
OFF_QA, OFF_KA, OFF_VA = 0, WIDTH_A, 2 * WIDTH_A
OFF_QB = 3 * WIDTH_A
OFF_KB = OFF_QB + WIDTH_B
OFF_VB = OFF_KB + WIDTH_B
OFF_QC = OFF_VB + WIDTH_B
OFF_GATE = OFF_QC + WIDTH_C
IN_COLS = OFF_GATE + 3 * D_MODEL

LANES = 128
VMEM_LIMIT = 56 * 1024 * 1024

IN_PROJ_TILE = 1024
PROJ_TILE = 512
NA_ROWS_PER_STEP = 8
DIL_STEP_TOKENS = 2048
MOE_BLOCK = 512
COMBINE_TILE = 256


def _cparams(n_axes):
    return pltpu.CompilerParams(
        dimension_semantics=("arbitrary",) * n_axes, vmem_limit_bytes=VMEM_LIMIT)


def _rms(x, g):
    return x * lax.rsqrt(jnp.mean(x * x, axis=-1, keepdims=True) + RMS_EPS) * g


def _sigmoid(x):
    return 1.0 / (1.0 + jnp.exp(-x))


def _pack_bf16_pairs(x):
    w = x.shape[1] // 2
    bits = lax.bitcast_convert_type(x.astype(jnp.bfloat16).astype(jnp.float32), jnp.uint32)
    return (bits[:, w:] & jnp.uint32(0xFFFF0000)) | (bits[:, :w] >> 16)


def _unpack_bf16_pairs(p):
    lo = lax.bitcast_convert_type(p << 16, jnp.float32)
    hi = lax.bitcast_convert_type(p & jnp.uint32(0xFFFF0000), jnp.float32)
    return jnp.concatenate([lo, hi], axis=1).astype(jnp.bfloat16)


def _dot(a, b):
    return jnp.dot(a, b, preferred_element_type=jnp.float32)


def _dot_nt(a, b):
    return lax.dot_general(a, b, (((1,), (1,)), ((), ())), preferred_element_type=jnp.float32)


def _mem_kv_kernel(mem_ref, g_ref, w_ref, out_ref):
    h = _rms(mem_ref[...], g_ref[...]).astype(jnp.bfloat16)
    out_ref[...] = _dot(h, w_ref[...]).astype(jnp.bfloat16)


def _mem_kv(mem2d, g, w):
    rows = mem2d.shape[0]
    return pl.pallas_call(
        _mem_kv_kernel,
        out_shape=jax.ShapeDtypeStruct((rows, 2 * WIDTH_C), jnp.bfloat16),
        name="mem_kv",
        compiler_params=pltpu.CompilerParams(vmem_limit_bytes=VMEM_LIMIT),
    )(mem2d, g, w)


def _in_proj_kernel(x_ref, g_ref, w_ref, na_ref, d0_ref, d1_ref, d2_ref, qc_ref, acc_ref):
    T = x_ref.shape[0]
    h = _rms(x_ref[...], g_ref[...]).astype(jnp.bfloat16)
    q_scale = HEAD_DIM ** -0.5

    for c in range(3):
        acc = _dot(h, w_ref[:, c * WIDTH_A:(c + 1) * WIDTH_A])
        if c == 0:
            acc = acc * q_scale
        na_ref[:, c * WIDTH_A:(c + 1) * WIDTH_A] = acc.astype(jnp.bfloat16)

    for gi, ((_, dil), d_ref) in enumerate(zip(DIL_GROUPS, (d0_ref, d1_ref, d2_ref))):
        for kind, off in enumerate((OFF_QB, OFF_KB, OFF_VB)):
            lo = off + gi * WIDTH_G
            acc = _dot(h, w_ref[:, lo:lo + WIDTH_G])
            if kind == 0:
                acc = acc * q_scale
            cols = slice(kind * WIDTH_G, (kind + 1) * WIDTH_G)
            if dil == 1:
                d_ref[0, 0, :, cols] = acc.astype(jnp.bfloat16)
            else:
                for c in range(WIDTH_G // LANES):
                    acc_ref[c] = acc[:, c * LANES:(c + 1) * LANES]
                for rho in range(dil):
                    d_ref[0, rho, :, cols] = jnp.concatenate(
                        [acc_ref[c, pl.ds(rho, T // dil, stride=dil), :] for c in range(WIDTH_G // LANES)],
                        axis=-1).astype(jnp.bfloat16)

    qc_ref[...] = _dot(h, w_ref[:, OFF_QC:OFF_QC + WIDTH_C]).astype(jnp.bfloat16)


def _gate_proj_kernel(x_ref, g_ref, w_ref, gate_ref):
    h = _rms(x_ref[...], g_ref[...]).astype(jnp.bfloat16)
    chunk = 512
    for c in range(3 * D_MODEL // chunk):
        gate_ref[:, c * chunk:(c + 1) * chunk] = _sigmoid(
            _dot(h, w_ref[:, c * chunk:(c + 1) * chunk])).astype(jnp.bfloat16)


def _gate_proj(x2d, g, w_gate_bf16):
    n = x2d.shape[0]
    T = IN_PROJ_TILE
    return pl.pallas_call(
        _gate_proj_kernel,
        grid=(n // T,),
        in_specs=[pl.BlockSpec((T, D_MODEL), lambda i: (i, 0)),
                  pl.BlockSpec((1, D_MODEL), lambda i: (0, 0)),
                  pl.BlockSpec((D_MODEL, 3 * D_MODEL), lambda i: (0, 0))],
        out_specs=pl.BlockSpec((T, 3 * D_MODEL), lambda i: (i, 0)),
        out_shape=jax.ShapeDtypeStruct((n, 3 * D_MODEL), jnp.bfloat16),
        compiler_params=_cparams(1),
        name="gate_proj",
    )(x2d, g, w_gate_bf16)


def _in_proj(x2d, g, w_bf16, batch, seq):
    n = x2d.shape[0]
    T = IN_PROJ_TILE
    tiles_per_batch = seq // T
    out_shape = [jax.ShapeDtypeStruct((n, 3 * WIDTH_A), jnp.bfloat16)]
    out_specs = [pl.BlockSpec((T, 3 * WIDTH_A), lambda i: (i, 0))]
    for _, dil in DIL_GROUPS:
        out_shape.append(jax.ShapeDtypeStruct((batch, dil, seq // dil, 3 * WIDTH_G), jnp.bfloat16))
        out_specs.append(pl.BlockSpec((1, dil, T // dil, 3 * WIDTH_G),
                                      lambda i: (i // tiles_per_batch, 0, i % tiles_per_batch, 0)))
    out_shape += [jax.ShapeDtypeStruct((n, WIDTH_C), jnp.bfloat16)]
    out_specs += [pl.BlockSpec((T, WIDTH_C), lambda i: (i, 0))]
    return pl.pallas_call(
        _in_proj_kernel,
        grid=(n // T,),
        in_specs=[pl.BlockSpec((T, D_MODEL), lambda i: (i, 0)),
                  pl.BlockSpec((1, D_MODEL), lambda i: (0, 0)),
                  pl.BlockSpec((D_MODEL, OFF_GATE), lambda i: (0, 0))],
        out_specs=out_specs,
        out_shape=out_shape,
        scratch_shapes=[pltpu.VMEM((WIDTH_G // LANES, T, LANES), jnp.float32)],
        compiler_params=_cparams(1),
        name="in_proj",
    )(x2d, g, w_bf16)


def _na_kernel(q_ref, k_ref, v_ref, bias_ref, out_ref, *, rows):
    i = pl.program_id(1)
    lane = lax.broadcasted_iota(jnp.int32, (GRID_W, LANES), 1)
    low_half = lane < HEAD_DIM
    n_win = NA_WIN_ROWS * GRID_W

    def row_body(j, carry):
        qr = i * NA_ROWS_PER_STEP + j
        rs = jnp.clip(qr - NA_WIN_ROWS // 2, 0, rows - NA_WIN_ROWS)
        shift = rs - qr + NA_WIN_ROWS - 1
        q_off = pl.multiple_of(j * GRID_W, GRID_W)
        k_off = pl.multiple_of(rs * GRID_W, GRID_W)
        scores = []
        for h in range(NA_HEADS):
            cols = slice((h // 2) * LANES, (h // 2 + 1) * LANES)
            qp = q_ref[pl.ds(q_off, GRID_W), cols]
            kp = k_ref[pl.ds(k_off, n_win), cols]
            keep = low_half if h % 2 == 0 else jnp.logical_not(low_half)
            qm = jnp.where(keep, qp, jnp.zeros_like(qp))
            bias = jnp.concatenate(
                [bias_ref[shift + 2 * p, h] for p in range(NA_WIN_ROWS // 2)], axis=-1)
            scores.append(_dot_nt(qm, kp) + bias)
        probs = []
        for s in scores:
            e = jnp.exp(s - jnp.max(s, axis=-1, keepdims=True))
            probs.append((e.astype(jnp.bfloat16), jnp.sum(e, axis=-1, keepdims=True)))
        outs = []
        for h, (p, l) in enumerate(probs):
            cols = slice((h // 2) * LANES, (h // 2 + 1) * LANES)
            vp = v_ref[pl.ds(k_off, n_win), cols]
            outs.append(_dot(p, vp) / l)
        for hp in range(NA_HEADS // 2):
            cols = slice(hp * LANES, (hp + 1) * LANES)
            out_ref[pl.ds(q_off, GRID_W), cols] = jnp.where(
                low_half, outs[2 * hp], outs[2 * hp + 1]).astype(jnp.bfloat16)
        return carry

    lax.fori_loop(0, NA_ROWS_PER_STEP, row_body, 0)


def _na_attn(na_qkv, bias_tab, batch, seq):
    rows = seq // GRID_W
    steps = rows // NA_ROWS_PER_STEP
    tq = NA_ROWS_PER_STEP * GRID_W
    n = na_qkv.shape[0]
    return pl.pallas_call(
        functools.partial(_na_kernel, rows=rows),
        grid=(batch, steps),
        in_specs=[pl.BlockSpec((tq, WIDTH_A), lambda b, i: (b * steps + i, 0)),
                  pl.BlockSpec((seq, WIDTH_A), lambda b, i: (b, 1)),
                  pl.BlockSpec((seq, WIDTH_A), lambda b, i: (b, 2)),
                  pl.BlockSpec(bias_tab.shape, lambda b, i: (0, 0, 0, 0))],
        out_specs=pl.BlockSpec((tq, WIDTH_A), lambda b, i: (b * steps + i, 0)),
        out_shape=jax.ShapeDtypeStruct((n, WIDTH_A), jnp.bfloat16),
        compiler_params=_cparams(2),
        name="na_attn",
    )(na_qkv, na_qkv, na_qkv, bias_tab)


def _t5_bucket_np(rel):
    half = T5_BUCKETS // 2
    max_exact = half // 2
    ret = np.where(rel > 0, half, 0)
    n = np.abs(rel)
    nf = np.maximum(n, 1).astype(np.float32)
    large = max_exact + (np.log(nf / np.float32(max_exact)) / np.float32(math.log(T5_MAX_DIST / max_exact))
                         * np.float32(half - max_exact)).astype(np.int32)
    large = np.minimum(large, half - 1)
    return ret + np.where(n < max_exact, n, large)


def _dil_bucket_table():
    q = np.arange(DIL_BLOCK)[:, None]
    j = np.arange(2 * DIL_BLOCK)[None, :]
    tabs = []
    for window, dil in DIL_GROUPS:
        half = (window // 2) // dil
        assert half == DIL_BLOCK // 2
        rel = (j - half) - q
        tabs.append(np.where(np.abs(rel) <= half, _t5_bucket_np(rel * dil), -1))
    return np.stack(tabs).astype(np.int32)


def _bias_prep_kernel(rpb_ref, t5_ref, bucket_ref, na_ref, dil_ref):
    n_dc = 2 * NA_WIN_COLS - 1
    n_dr = 2 * NA_WIN_ROWS - 1
    qc = lax.broadcasted_iota(jnp.int32, (GRID_W, 2 * GRID_W), 0)
    lane = lax.broadcasted_iota(jnp.int32, (GRID_W, 2 * GRID_W), 1)
    second = lane >= GRID_W
    kc = jnp.where(second, lane - GRID_W, lane)
    diff = kc - qc
    cs = jnp.clip(qc - NA_WIN_COLS // 2, 0, GRID_W - NA_WIN_COLS)
    valid = jnp.logical_and(kc >= cs, kc < cs + NA_WIN_COLS)

    def na_tile(mh, carry):
        m = mh // NA_HEADS
        h = mh - m * NA_HEADS
        base = (h * n_dr + m) * n_dc
        acc = jnp.zeros((GRID_W, 2 * GRID_W), jnp.float32)
        for d in range(n_dc):
            v = jnp.where(second, rpb_ref[base + n_dc + d], rpb_ref[base + d])
            acc = jnp.where(diff == d - (NA_WIN_COLS - 1), v, acc)
        na_ref[m, h] = jnp.where(valid, acc, NEG_INF)
        return carry

    lax.fori_loop(0, (n_dr - 1) * NA_HEADS, na_tile, 0)

    j = lax.broadcasted_iota(jnp.int32, (DIL_BLOCK, 2 * DIL_BLOCK), 1)
    half = DIL_BLOCK // 2
    n_heads = DIL_HEADS_PER_GROUP * len(DIL_GROUPS)
    for gi in range(len(DIL_GROUPS)):
        bucket = bucket_ref[gi]
        in_win = bucket >= 0
        for h in range(DIL_HEADS_PER_GROUP):
            acc = jnp.zeros((DIL_BLOCK, 2 * DIL_BLOCK), jnp.float32)
            for b in range(T5_BUCKETS):
                acc = jnp.where(bucket == b, t5_ref[b * n_heads + gi * DIL_HEADS_PER_GROUP + h], acc)
            for v, ok in enumerate((jnp.logical_and(in_win, j >= half), in_win,
                                    jnp.logical_and(in_win, j < 2 * DIL_BLOCK - half))):
                dil_ref[gi, v, h] = jnp.where(ok, acc, NEG_INF)


def _bias_prep(rpb, t5):
    smem = pl.BlockSpec(memory_space=pltpu.SMEM)
    return pl.pallas_call(
        _bias_prep_kernel,
        in_specs=[smem, smem, pl.BlockSpec(memory_space=pltpu.VMEM)],
        out_shape=[jax.ShapeDtypeStruct((2 * NA_WIN_ROWS - 2, NA_HEADS, GRID_W, 2 * GRID_W), jnp.float32),
                   jax.ShapeDtypeStruct((len(DIL_GROUPS), 3, DIL_HEADS_PER_GROUP, DIL_BLOCK, 2 * DIL_BLOCK),
                                        jnp.float32)],
        compiler_params=pltpu.CompilerParams(vmem_limit_bytes=VMEM_LIMIT),
        name="bias_prep",
    )(rpb.reshape(-1), t5.reshape(-1), jnp.asarray(_dil_bucket_table()))


def _dil_kernel(q_ref, kp_ref, kc_ref, kn_ref, vp_ref, vc_ref, vn_ref, bias_ref, o_ref, lse_ref, *, dil):
    lane = lax.broadcasted_iota(jnp.int32, (DIL_BLOCK, LANES), 1)
    low_half = lane < HEAD_DIM
    hb = DIL_BLOCK // 2

    span = pl.program_id(1)
    n_spans = pl.num_programs(1)
    nbs = kc_ref.shape[2] // DIL_BLOCK

    def window(prev_ref, cur_ref, next_ref, rho, blk, cols):
        lo = blk * DIL_BLOCK
        first = prev_ref[0, rho, hb:, cols] if blk == 0 else cur_ref[0, rho, lo - hb:lo, cols]
        last = (next_ref[0, rho, :hb, cols] if blk == nbs - 1
                else cur_ref[0, rho, lo + DIL_BLOCK:lo + DIL_BLOCK + hb, cols])
        return jnp.concatenate([first, cur_ref[0, rho, lo:lo + DIL_BLOCK, cols], last], axis=0)

    def unit(rho, blk):
        variant = 1
        if blk == nbs - 1:
            variant = jnp.where(span == n_spans - 1, 2, variant)
        if blk == 0:
            variant = jnp.where(span == 0, 0, variant)
        scores = []
        for h in range(DIL_HEADS_PER_GROUP):
            cols = slice((h // 2) * LANES, (h // 2 + 1) * LANES)
            qp = q_ref[0, rho, blk * DIL_BLOCK:(blk + 1) * DIL_BLOCK, cols]
            keep = low_half if h % 2 == 0 else jnp.logical_not(low_half)
            qm = jnp.where(keep, qp, jnp.zeros_like(qp))
            kw = window(kp_ref, kc_ref, kn_ref, rho, blk, cols)
            scores.append(_dot_nt(qm, kw) + bias_ref[variant, h])
        probs = []
        for s in scores:
            m = jnp.max(s, axis=-1, keepdims=True)
            e = jnp.exp(s - m)
            probs.append((e.astype(jnp.bfloat16), m, jnp.sum(e, axis=-1, keepdims=True)))
        outs, lses = [], []
        for h, (p, m, l) in enumerate(probs):
            cols = slice((h // 2) * LANES, (h // 2 + 1) * LANES)
            outs.append(_dot(p, window(vp_ref, vc_ref, vn_ref, rho, blk, cols)) / l)
            lses.append(jnp.broadcast_to(m + jnp.log(l), (DIL_BLOCK, LANES)))
        for hp in range(DIL_HEADS_PER_GROUP // 2):
            o_pair = jnp.where(low_half, outs[2 * hp], outs[2 * hp + 1])
            lse_pair = jnp.where(low_half, lses[2 * hp], lses[2 * hp + 1])
            if dil == 1:
                rows = slice(blk * DIL_BLOCK, (blk + 1) * DIL_BLOCK)
                o_ref[hp, rows, :] = o_pair
                lse_ref[hp, rows, :] = lse_pair
            else:
                rows = pl.ds(blk * DIL_BLOCK * dil + rho, DIL_BLOCK, stride=dil)
                o_ref[hp, rows, :] = o_pair
                lse_ref[hp, rows, :] = lse_pair

    def class_body(rho, carry):
        for blk in range(nbs):
            unit(rho, blk)
        return carry

    if dil == 1:
        class_body(0, 0)
    else:
        lax.fori_loop(0, dil, class_body, 0, unroll=2 if nbs == 1 else 1)


def _dil_attn(qkv, bias_tab, group, dil, batch, seq):
    L = seq // dil
    nb = L // DIL_BLOCK
    assert nb >= 2
    tq = DIL_STEP_TOKENS
    span = tq // dil
    nbs = span // DIL_BLOCK
    n_spans = L // span

    def cur(col):
        return pl.BlockSpec((1, dil, span, WIDTH_G), lambda b, n: (b, 0, n, col))

    def edge(col, shift):
        return pl.BlockSpec((1, dil, DIL_BLOCK, WIDTH_G),
                            lambda b, n: (b, 0, jnp.clip(n * nbs + (shift if shift < 0 else nbs), 0, nb - 1), col))

    n_pairs = WIDTH_G // LANES
    out_spec = pl.BlockSpec((n_pairs, tq, LANES), lambda b, n: (0, b * n_spans + n, 0))
    return pl.pallas_call(
        functools.partial(_dil_kernel, dil=dil),
        grid=(batch, n_spans),
        in_specs=[cur(0), edge(1, -1), cur(1), edge(1, 1), edge(2, -1), cur(2), edge(2, 1),
                  pl.BlockSpec((None, 3, DIL_HEADS_PER_GROUP, DIL_BLOCK, 2 * DIL_BLOCK),
                               lambda b, n: (group, 0, 0, 0, 0))],
        out_specs=[out_spec, out_spec],
        out_shape=[jax.ShapeDtypeStruct((n_pairs, batch * seq, LANES), jnp.float32)] * 2,
        compiler_params=_cparams(2),
        name=f"dil_attn_d{dil}",
    )(qkv, qkv, qkv, qkv, qkv, qkv, qkv, bias_tab)


def _mix_kernel(x_ref, na_ref, o0_ref, l0_ref, o1_ref, l1_ref, o2_ref, l2_ref, qc_ref, gate_ref,
                mkv_ref, wa_ref, wb_ref, wc_ref, wo_ref, fg_ref, rw_ref, rb_ref,
                x1_ref, h_ref, idx_ref, wt_ref):
    T = x_ref.shape[0]
    ob = []
    for hp in range(WIDTH_G // LANES):
        l0, l1, l2 = l0_ref[hp], l1_ref[hp], l2_ref[hp]
        m = jnp.maximum(jnp.maximum(l0, l1), l2)
        e0, e1, e2 = jnp.exp(l0 - m), jnp.exp(l1 - m), jnp.exp(l2 - m)
        ob.append((e0 * o0_ref[hp] + e1 * o1_ref[hp] + e2 * o2_ref[hp]) / (e0 + e1 + e2))
    ob = jnp.concatenate(ob, axis=-1)

    mem_scale = MEM_HEAD_DIM ** -0.5
    scores = []
    for h in range(MEM_HEADS):
        cols = slice(h * MEM_HEAD_DIM, (h + 1) * MEM_HEAD_DIM)
        scores.append(_dot_nt(qc_ref[:, cols], mkv_ref[:, cols]) * mem_scale)
    probs = []
    for s in scores:
        e = jnp.exp(s - jnp.max(s, axis=-1, keepdims=True))
        probs.append((e.astype(jnp.bfloat16), jnp.sum(e, axis=-1, keepdims=True)))
    oc = []
    for h, (p, l) in enumerate(probs):
        oc.append(_dot(p, mkv_ref[:, WIDTH_C + h * MEM_HEAD_DIM:WIDTH_C + (h + 1) * MEM_HEAD_DIM]) / l)
    oc = jnp.concatenate(oc, axis=-1).astype(jnp.bfloat16)

    y_a = _dot(na_ref[...], wa_ref[...])
    y_b = _dot(ob.astype(jnp.bfloat16), wb_ref[...])
    y_c = _dot(oc, wc_ref[...])
    merged = (gate_ref[:, 0:D_MODEL].astype(jnp.float32) * y_a
              + gate_ref[:, D_MODEL:2 * D_MODEL].astype(jnp.float32) * y_b
              + gate_ref[:, 2 * D_MODEL:3 * D_MODEL].astype(jnp.float32) * y_c)
    x1 = x_ref[...] + _dot(merged.astype(jnp.bfloat16), wo_ref[...])
    x1_ref[...] = x1

    h = _rms(x1, fg_ref[...])
    h_ref[...] = _pack_bf16_pairs(h)
    logits = lax.dot_general(rw_ref[...], h, (((1,), (1,)), ((), ())), preferred_element_type=jnp.float32,
                             precision=lax.Precision.HIGHEST) + rb_ref[...]
    expert = lax.broadcasted_iota(jnp.int32, (N_EXPERTS, T), 0)
    vals, idxs = [], []
    for _ in range(TOP_K):
        mx = jnp.max(logits, axis=0, keepdims=True)
        sel = jnp.min(jnp.where(logits == mx, expert, N_EXPERTS), axis=0, keepdims=True)
        vals.append(mx)
        idxs.append(sel)
        logits = jnp.where(expert == sel, -jnp.inf, logits)
    ev = [jnp.exp(v - vals[0]) for v in vals]
    den = ev[0] + ev[1] + ev[2] + ev[3]
    idx_ref[...] = jnp.concatenate(idxs, axis=0)
    wt_ref[...] = jnp.concatenate([e / den for e in ev], axis=0)


def _mix_out(x2d, na_out, dil_outs, qc, gates, mkv, wa, wb, wc, wo, ffn_g, router_w, router_b, batch, seq):
    n = x2d.shape[0]
    T = PROJ_TILE
    tiles_per_batch = seq // T
    mem_len = mkv.shape[0] // batch
    row = lambda w: pl.BlockSpec((T, w), lambda i: (i, 0))
    full = lambda a: pl.BlockSpec(a.shape, lambda i: (0,) * a.ndim)
    in_specs = [row(D_MODEL), row(WIDTH_A)]
    args = [x2d, na_out]
    for o, l in dil_outs:
        pair_rows = pl.BlockSpec((WIDTH_G // LANES, T, LANES), lambda i: (0, i, 0))
        in_specs += [pair_rows, pair_rows]
        args += [o, l]
    in_specs += [row(WIDTH_C), row(3 * D_MODEL),
                 pl.BlockSpec((mem_len, 2 * WIDTH_C), lambda i: (i // tiles_per_batch, 0)),
                 full(wa), full(wb), full(wc), full(wo), full(ffn_g), full(router_w), full(router_b)]
    args += [qc, gates, mkv, wa, wb, wc, wo, ffn_g, router_w, router_b]
    return pl.pallas_call(
        _mix_kernel,
        grid=(n // T,),
        in_specs=in_specs,
        out_specs=[row(D_MODEL), row(D_MODEL // 2), pl.BlockSpec((TOP_K, T), lambda i: (0, i)),
                   pl.BlockSpec((TOP_K, T), lambda i: (0, i))],
        out_shape=[jax.ShapeDtypeStruct((n, D_MODEL), jnp.float32),
                   jax.ShapeDtypeStruct((n, D_MODEL // 2), jnp.uint32),
                   jax.ShapeDtypeStruct((TOP_K, n), jnp.int32),
                   jax.ShapeDtypeStruct((TOP_K, n), jnp.float32)],
        compiler_params=_cparams(1),
        name="mix_out",
    )(*args)


def _route_kernel(idx_ref, slot_ref, be_ref, pe_ref, counts_ref, carry_ref, pstart_ref):
    phase = pl.program_id(0)
    i = pl.program_id(1)
    T = idx_ref.shape[1]
    n_blocks = be_ref.shape[1]
    expert = lax.broadcasted_iota(jnp.int32, (N_EXPERTS, T), 0)
    idx = idx_ref[...]
    onehots = [expert == idx[k:k + 1, :] for k in range(TOP_K)]
    member = sum(o.astype(jnp.float32) for o in onehots)
    rowsum = jnp.sum(member, axis=1, keepdims=True)

    @pl.when(jnp.logical_and(phase == 0, i == 0))
    def _():
        counts_ref[...] = jnp.zeros_like(counts_ref)

    @pl.when(phase == 0)
    def _():
        counts_ref[...] += rowsum

    @pl.when(jnp.logical_and(phase == 1, i == 0))
    def _():
        counts = counts_ref[...]
        padded = jnp.floor((counts + (MOE_BLOCK - 1)) * (1.0 / MOE_BLOCK)) * MOE_BLOCK
        r = lax.broadcasted_iota(jnp.int32, (N_EXPERTS, N_EXPERTS), 0)
        c = lax.broadcasted_iota(jnp.int32, (N_EXPERTS, N_EXPERTS), 1)
        lower_incl = (r >= c).astype(jnp.float32)
        pad_ends = jnp.dot(lower_incl, padded, preferred_element_type=jnp.float32,
                           precision=lax.Precision.HIGHEST)
        pstart_ref[...] = pad_ends - padded
        carry_ref[...] = jnp.zeros_like(carry_ref)
        pe_ref[...] = pad_ends[:, 0:1].astype(jnp.int32)
        blk_start = (lax.broadcasted_iota(jnp.int32, (N_EXPERTS, n_blocks), 1) * MOE_BLOCK).astype(jnp.float32)
        be = jnp.sum((pad_ends[:, 0:1] <= blk_start).astype(jnp.int32), axis=0, keepdims=True)
        be_ref[...] = jnp.minimum(be, N_EXPERTS - 1)

    @pl.when(phase == 1)
    def _():
        rr = lax.broadcasted_iota(jnp.int32, (T, T), 0)
        cc = lax.broadcasted_iota(jnp.int32, (T, T), 1)
        earlier = (rr < cc).astype(jnp.bfloat16)
        before = _dot(member.astype(jnp.bfloat16), earlier)
        base = pstart_ref[:, 0:1] + carry_ref[:, 0:1] + before
        slots = [jnp.sum(jnp.where(o, base, 0.0), axis=0, keepdims=True) for o in onehots]
        slot_ref[...] = jnp.concatenate(slots, axis=0).astype(jnp.int32)
        carry_ref[...] += rowsum


def _moe_route(topk_idx, n_blocks):
    n = topk_idx.shape[1]
    T = PROJ_TILE
    acc = pltpu.VMEM((N_EXPERTS, LANES), jnp.float32)
    return pl.pallas_call(
        _route_kernel,
        grid=(2, n // T),
        in_specs=[pl.BlockSpec((TOP_K, T), lambda p, i: (0, i))],
        out_specs=[pl.BlockSpec((TOP_K, T), lambda p, i: (0, p * i)),
                   pl.BlockSpec((1, n_blocks), lambda p, i: (0, 0)),
                   pl.BlockSpec((N_EXPERTS, 1), lambda p, i: (0, 0))],
        out_shape=[jax.ShapeDtypeStruct((TOP_K, n), jnp.int32),
                   jax.ShapeDtypeStruct((1, n_blocks), jnp.int32),
                   jax.ShapeDtypeStruct((N_EXPERTS, 1), jnp.int32)],
        scratch_shapes=[acc, acc, acc],
        compiler_params=_cparams(2),
        name="moe_route",
    )(topk_idx)


def _wait_rows(src_hbm, dst_hbm, sem, n_rows):
    def body(r, carry):
        pltpu.make_async_copy(src_hbm.at[pl.ds(0, 1), :], dst_hbm.at[pl.ds(0, 1), :], sem).wait()
        return carry
    lax.fori_loop(0, n_rows, body, 0, unroll=8)


def _dispatch_kernel(pe_ref, slot_ref, h_ref, xs_hbm, zbuf, sem, zsem):
    i = pl.program_id(0)
    T = h_ref.shape[0]

    def zero_copy(e):
        end = pe_ref[e]
        start = pl.multiple_of(end - MOE_BLOCK, MOE_BLOCK)
        return pltpu.make_async_copy(zbuf, xs_hbm.at[pl.ds(start, MOE_BLOCK), :], zsem)

    def nonempty(e):
        return pe_ref[e] > (pe_ref[e - 1] if e else 0)

    @pl.when(i == 0)
    def _():
        zbuf[...] = jnp.zeros_like(zbuf)
        for e in range(N_EXPERTS):
            @pl.when(nonempty(e))
            def _():
                zero_copy(e).start()
        for e in range(N_EXPERTS):
            @pl.when(nonempty(e))
            def _():
                zero_copy(e).wait()

        def tail_copy(b):
            return pltpu.make_async_copy(
                zbuf, xs_hbm.at[pl.ds(pl.multiple_of(b * MOE_BLOCK, MOE_BLOCK), MOE_BLOCK), :], zsem)
        first_tail = pe_ref[N_EXPERTS - 1] // MOE_BLOCK
        n_blocks = xs_hbm.shape[0] // MOE_BLOCK
        lax.fori_loop(first_tail, n_blocks, lambda b, c: (tail_copy(b).start(), c)[1], 0)
        lax.fori_loop(first_tail, n_blocks, lambda b, c: (tail_copy(b).wait(), c)[1], 0)

    def token_body(t, carry):
        src = h_ref.at[pl.ds(t, 1), :]
        for k in range(TOP_K):
            pltpu.make_async_copy(src, xs_hbm.at[pl.ds(slot_ref[k * T + t], 1), :],
                                  sem).start(priority=k % 2)
        return carry

    lax.fori_loop(0, T, token_body, 0, unroll=2)
    _wait_rows(h_ref, xs_hbm, sem, T * TOP_K)


def _moe_dispatch(h, slot3, pad_ends, n_blocks):
    n, width = h.shape
    T = COMBINE_TILE
    n_steps = n // T

    def kernel(pe_ref, slot_ref, *rest):
        _dispatch_kernel(pe_ref, slot_ref.at[0], *rest)

    grid_spec = pltpu.PrefetchScalarGridSpec(
        num_scalar_prefetch=1,
        grid=(n_steps,),
        in_specs=[pl.BlockSpec((None, 1, T * TOP_K), lambda i, pe: (i, 0, 0), memory_space=pltpu.SMEM),
                  pl.BlockSpec((T, width), lambda i, pe: (i, 0))],
        out_specs=pl.BlockSpec(memory_space=pl.ANY),
        scratch_shapes=[pltpu.VMEM((MOE_BLOCK, width), h.dtype),
                        pltpu.SemaphoreType.DMA(()),
                        pltpu.SemaphoreType.DMA(())],
    )
    return pl.pallas_call(
        kernel,
        grid_spec=grid_spec,
        out_shape=jax.ShapeDtypeStruct((n_blocks * MOE_BLOCK, width), h.dtype),
        compiler_params=_cparams(1),
        name="moe_dispatch",
    )(pad_ends, slot3, h)


def _moe_kernel(be_ref, pe_ref, xs_ref, wgu_ref, bgu_ref, wdn_ref, bdn_ref, y_ref, wgu_bf, wdn_bf):
    i = pl.program_id(0)
    n_valid = pe_ref[N_EXPERTS - 1] // MOE_BLOCK
    new_expert = jnp.logical_or(i == 0, be_ref[i] != be_ref[jnp.maximum(i - 1, 0)])

    @pl.when(jnp.logical_and(i < n_valid, new_expert))
    def _():
        wgu_bf[...] = wgu_ref[0].astype(jnp.bfloat16)
        wdn_bf[...] = wdn_ref[0].astype(jnp.bfloat16)

    @pl.when(i < n_valid)
    def _():
        x = _unpack_bf16_pairs(xs_ref[...])
        gu = _dot(x, wgu_bf[...]) + bgu_ref[0]
        g = jnp.minimum(gu[:, :D_FF], SWIGLU_LIMIT)
        u = jnp.clip(gu[:, D_FF:], -SWIGLU_LIMIT, SWIGLU_LIMIT)
        a = (u + 1.0) * (g * _sigmoid(SWIGLU_ALPHA * g))
        y_ref[...] = _dot(a.astype(jnp.bfloat16), wdn_bf[...]) + bdn_ref[0]

    @pl.when(i >= n_valid)
    def _():
        y_ref[...] = jnp.zeros_like(y_ref)


def _moe_experts(xs, block_expert, pad_ends, wgu, bgu, wdn, bdn):
    n_blocks = block_expert.shape[0]

    def row_blk(i, be, pe):
        return (jnp.minimum(i, jnp.maximum(pe[N_EXPERTS - 1] // MOE_BLOCK - 1, 0)), 0)

    grid_spec = pltpu.PrefetchScalarGridSpec(
        num_scalar_prefetch=2,
        grid=(n_blocks,),
        in_specs=[pl.BlockSpec((MOE_BLOCK, D_MODEL // 2), row_blk),
                  pl.BlockSpec((1, D_MODEL, 2 * D_FF), lambda i, be, pe: (be[i], 0, 0)),
                  pl.BlockSpec((1, 1, 2 * D_FF), lambda i, be, pe: (be[i], 0, 0)),
                  pl.BlockSpec((1, D_FF, D_MODEL), lambda i, be, pe: (be[i], 0, 0)),
                  pl.BlockSpec((1, 1, D_MODEL), lambda i, be, pe: (be[i], 0, 0))],
        out_specs=pl.BlockSpec((MOE_BLOCK, D_MODEL), lambda i, be, pe: (i, 0)),
        scratch_shapes=[pltpu.VMEM((D_MODEL, 2 * D_FF), jnp.bfloat16),
                        pltpu.VMEM((D_FF, D_MODEL), jnp.bfloat16)],
    )
    return pl.pallas_call(
        _moe_kernel,
        grid_spec=grid_spec,
        out_shape=jax.ShapeDtypeStruct((n_blocks * MOE_BLOCK, D_MODEL), jnp.float32),
        compiler_params=_cparams(1),
        name="moe_experts",
    )(block_expert, pad_ends, xs, wgu, bgu, wdn, bdn)


def _gather_rows(src_hbm, idx_ref, dst_ref, sem, n_rows):
    def body(r2, carry):
        for u in range(2):
            r = r2 * 2 + u
            pltpu.make_async_copy(src_hbm.at[pl.ds(idx_ref[r], 1), :],
                                  dst_ref.at[pl.ds(r, 1), :], sem).start(priority=u)
        return carry
    lax.fori_loop(0, n_rows // 2, body, 0, unroll=4)


def _combine_kernel(pos_ref, pos_next_ref, ys_hbm, wt_ref, x1_ref, g_ref, out_ref, gbuf, sem):
    i = pl.program_id(0)
    n_steps = pl.num_programs(0)
    slot = i % 2
    T = COMBINE_TILE
    n_rows = T * TOP_K

    @pl.when(i == 0)
    def _():
        _gather_rows(ys_hbm, pos_ref, gbuf.at[0], sem.at[0], n_rows)

    @pl.when(i + 1 < n_steps)
    def _():
        _gather_rows(ys_hbm, pos_next_ref, gbuf.at[1 - slot], sem.at[1 - slot], n_rows)

    _wait_rows(ys_hbm, gbuf.at[slot], sem.at[slot], n_rows)
    wt = wt_ref[...]
    y = x1_ref[...]
    for k in range(TOP_K):
        y = y + wt[:, k:k + 1] * gbuf[slot, k * T:(k + 1) * T, :]
    out_ref[...] = _rms(y, g_ref[...])


def _moe_combine(ys, pos_tiles, wts, x1, final_g):
    n = x1.shape[0]
    T = COMBINE_TILE
    n_steps = n // T
    smem_blk = lambda shift: pl.BlockSpec(
        (None, 1, T * TOP_K), lambda i: (jnp.minimum(i + shift, n_steps - 1), 0, 0),
        memory_space=pltpu.SMEM)

    def kernel(pos_ref, pos_next_ref, *rest):
        _combine_kernel(pos_ref.at[0], pos_next_ref.at[0], *rest)

    return pl.pallas_call(
        kernel,
        grid=(n_steps,),
        in_specs=[smem_blk(0), smem_blk(1),
                  pl.BlockSpec(memory_space=pl.ANY),
                  pl.BlockSpec((T, TOP_K), lambda i: (i, 0)),
                  pl.BlockSpec((T, D_MODEL), lambda i: (i, 0)),
                  pl.BlockSpec((1, D_MODEL), lambda i: (0, 0))],
        out_specs=pl.BlockSpec((T, D_MODEL), lambda i: (i, 0)),
        out_shape=jax.ShapeDtypeStruct((n, D_MODEL), jnp.float32),
        scratch_shapes=[pltpu.VMEM((2, T * TOP_K, D_MODEL), jnp.float32),
                        pltpu.SemaphoreType.DMA((2,))],
        compiler_params=_cparams(1),
        name="moe_combine",
    )(pos_tiles, pos_tiles, ys, wts, x1, final_g)


def kernel(x, mem, attn_norm_g, mem_norm_g, w_in, w_mem_kv, na_rpb, t5_rel_bias, w_branch_a, w_branch_b,
           w_branch_c, w_out, ffn_norm_g, router_w, router_b, expert_w_gate_up, expert_b_gate_up,
           expert_w_down, expert_b_down, final_norm_g):
    B, S, D = x.shape
    depth = w_in.shape[0]
    bf = jnp.bfloat16
    x2d = x.reshape(B * S, D)
    mem2d = mem.reshape(B * mem.shape[1], D)
    assert depth == 1, "single-layer block (per-layer parameters carry a leading axis of 1)"
    l = 0
    na_bias, dil_bias = _bias_prep(na_rpb[l], t5_rel_bias)
    mkv = _mem_kv(mem2d, mem_norm_g[l][None], w_mem_kv[l].astype(bf))
    na_qkv, d0, d1, d2, qc = _in_proj(x2d, attn_norm_g[l][None], w_in[l][:, :OFF_GATE].astype(bf), B, S)
    gates = _gate_proj(x2d, attn_norm_g[l][None], w_in[l][:, OFF_GATE:].astype(bf))
    na_out = _na_attn(na_qkv, na_bias, B, S)
    dil_outs = [_dil_attn(qkv, dil_bias, gi, dil, B, S)
                for gi, (qkv, (_, dil)) in enumerate(zip((d0, d1, d2), DIL_GROUPS))]
    x1, h, topk_idx, topk_w = _mix_out(
        x2d, na_out, dil_outs, qc, gates, mkv, w_branch_a[l].astype(bf), w_branch_b[l].astype(bf),
        w_branch_c[l].astype(bf), w_out[l].astype(bf), ffn_norm_g[l][None], router_w[l].T,
        router_b[l][:, None], B, S)

    n_blocks = (B * S * TOP_K) // MOE_BLOCK + N_EXPERTS
    slot, block_expert, pad_ends = _moe_route(topk_idx, n_blocks)
    block_expert = block_expert.reshape(n_blocks)
    pad_ends = pad_ends.reshape(N_EXPERTS)
    n_tiles = (B * S) // COMBINE_TILE
    slot_tiles = slot.reshape(TOP_K, n_tiles, COMBINE_TILE).transpose(1, 0, 2).reshape(
        n_tiles, 1, TOP_K * COMBINE_TILE)
    xs = _moe_dispatch(h, slot_tiles, pad_ends, n_blocks)
    ys = _moe_experts(xs, block_expert, pad_ends, expert_w_gate_up[l], expert_b_gate_up[l][:, None, :],
                      expert_w_down[l], expert_b_down[l][:, None, :])
    out = _moe_combine(ys, slot_tiles, topk_w.T, x1, final_norm_g[None])
    return out.reshape(B, S, D)
```

```python
import functools
import math

import numpy as np
import jax
import jax.numpy as jnp
from jax import lax
from jax.experimental import pallas as pl
from jax.experimental.pallas import tpu as pltpu

D_MODEL = 1024
GRID_W = 64
MEM_HEADS = 4
MEM_HEAD_DIM = 128
HEAD_DIM = 64
NA_HEADS = 8
NA_WIN_ROWS = 8
NA_WIN_COLS = 16
DIL_GROUPS = ((128, 1), (512, 4), (2048, 16))
DIL_HEADS_PER_GROUP = 4
DIL_BLOCK = 128
T5_BUCKETS = 32
T5_MAX_DIST = 1024
N_EXPERTS = 32
TOP_K = 4
D_FF = D_MODEL
SWIGLU_ALPHA = 1.702
SWIGLU_LIMIT = 7.0
RMS_EPS = 1e-6
NEG_INF = -1e30

WIDTH_A = NA_HEADS * HEAD_DIM
WIDTH_B = DIL_HEADS_PER_GROUP * len(DIL_GROUPS) * HEAD_DIM
WIDTH_G = DIL_HEADS_PER_GROUP * HEAD_DIM
WIDTH_C = MEM_HEADS * MEM_HEAD_DIM
OFF_QA, OFF_KA, OFF_VA = 0, WIDTH_A, 2 * WIDTH_A
OFF_QB = 3 * WIDTH_A
OFF_KB = OFF_QB + WIDTH_B
OFF_VB = OFF_KB + WIDTH_B
OFF_QC = OFF_VB + WIDTH_B
OFF_GATE = OFF_QC + WIDTH_C
IN_COLS = OFF_GATE + 3 * D_MODEL

LANES = 128
VMEM_LIMIT = 56 * 1024 * 1024

IN_PROJ_TILE = 1024
PROJ_TILE = 512
NA_ROWS_PER_STEP = 8
DIL_STEP_TOKENS = 2048
MOE_BLOCK = 512
MOE_TILE = 256
MOE_PIECE = 8
MOE_TILE_ROWS = -(-(TOP_K * MOE_TILE + N_EXPERTS * (MOE_PIECE - 1)) // 256) * 256
MOE_PIECE_TAB = 256
assert MOE_TILE_ROWS // MOE_PIECE < MOE_PIECE_TAB


def _cparams(n_axes):
    return pltpu.CompilerParams(
        dimension_semantics=("arbitrary",) * n_axes, vmem_limit_bytes=VMEM_LIMIT)


def _rms(x, g):
    return x * lax.rsqrt(jnp.mean(x * x, axis=-1, keepdims=True) + RMS_EPS) * g


def _sigmoid(x):
    return 1.0 / (1.0 + jnp.exp(-x))


def _pack_bf16_pairs(x):
    w = x.shape[1] // 2
    bits = lax.bitcast_convert_type(x.astype(jnp.bfloat16).astype(jnp.float32), jnp.uint32)
    return (bits[:, w:] & jnp.uint32(0xFFFF0000)) | (bits[:, :w] >> 16)


def _unpack_bf16_pairs(p):
    lo = lax.bitcast_convert_type(p << 16, jnp.float32)
    hi = lax.bitcast_convert_type(p & jnp.uint32(0xFFFF0000), jnp.float32)
    return jnp.concatenate([lo, hi], axis=1).astype(jnp.bfloat16)


def _dot(a, b):
    return jnp.dot(a, b, preferred_element_type=jnp.float32)


def _dot_nt(a, b):
    return lax.dot_general(a, b, (((1,), (1,)), ((), ())), preferred_element_type=jnp.float32)


def _mem_kv_kernel(mem_ref, g_ref, w_ref, out_ref):
    h = _rms(mem_ref[...], g_ref[...]).astype(jnp.bfloat16)
    out_ref[...] = _dot(h, w_ref[...]).astype(jnp.bfloat16)


def _mem_kv(mem2d, g, w):
    rows = mem2d.shape[0]
    return pl.pallas_call(
        _mem_kv_kernel,
        out_shape=jax.ShapeDtypeStruct((rows, 2 * WIDTH_C), jnp.bfloat16),
        name="mem_kv",
        compiler_params=pltpu.CompilerParams(vmem_limit_bytes=VMEM_LIMIT),
    )(mem2d, g, w)


def _in_proj_kernel(x_ref, g_ref, w_ref, na_ref, d0_ref, d1_ref, d2_ref, qc_ref, acc_ref):
    T = x_ref.shape[0]
    h = _rms(x_ref[...], g_ref[...]).astype(jnp.bfloat16)
    q_scale = HEAD_DIM ** -0.5

    for c in range(3):
        acc = _dot(h, w_ref[:, c * WIDTH_A:(c + 1) * WIDTH_A])
        if c == 0:
            acc = acc * q_scale
        na_ref[:, c * WIDTH_A:(c + 1) * WIDTH_A] = acc.astype(jnp.bfloat16)

    for gi, ((_, dil), d_ref) in enumerate(zip(DIL_GROUPS, (d0_ref, d1_ref, d2_ref))):
        for kind, off in enumerate((OFF_QB, OFF_KB, OFF_VB)):
            lo = off + gi * WIDTH_G
            acc = _dot(h, w_ref[:, lo:lo + WIDTH_G])
            if kind == 0:
                acc = acc * q_scale
            cols = slice(kind * WIDTH_G, (kind + 1) * WIDTH_G)
            if dil == 1:
                d_ref[0, 0, :, cols] = acc.astype(jnp.bfloat16)
            else:
                for c in range(WIDTH_G // LANES):
                    acc_ref[c] = acc[:, c * LANES:(c + 1) * LANES]
                for rho in range(dil):
                    d_ref[0, rho, :, cols] = jnp.concatenate(
                        [acc_ref[c, pl.ds(rho, T // dil, stride=dil), :] for c in range(WIDTH_G // LANES)],
                        axis=-1).astype(jnp.bfloat16)

    qc_ref[...] = _dot(h, w_ref[:, OFF_QC:OFF_QC + WIDTH_C]).astype(jnp.bfloat16)


def _gate_proj_kernel(x_ref, g_ref, w_ref, gate_ref):
    h = _rms(x_ref[...], g_ref[...]).astype(jnp.bfloat16)
    chunk = 512
    for c in range(3 * D_MODEL // chunk):
        gate_ref[:, c * chunk:(c + 1) * chunk] = _sigmoid(
            _dot(h, w_ref[:, c * chunk:(c + 1) * chunk])).astype(jnp.bfloat16)


def _gate_proj(x2d, g, w_gate_bf16):
    n = x2d.shape[0]
    T = IN_PROJ_TILE
    return pl.pallas_call(
        _gate_proj_kernel,
        grid=(n // T,),
        in_specs=[pl.BlockSpec((T, D_MODEL), lambda i: (i, 0)),
                  pl.BlockSpec((1, D_MODEL), lambda i: (0, 0)),
                  pl.BlockSpec((D_MODEL, 3 * D_MODEL), lambda i: (0, 0))],
        out_specs=pl.BlockSpec((T, 3 * D_MODEL), lambda i: (i, 0)),
        out_shape=jax.ShapeDtypeStruct((n, 3 * D_MODEL), jnp.bfloat16),
        compiler_params=_cparams(1),
        name="gate_proj",
    )(x2d, g, w_gate_bf16)


def _in_proj(x2d, g, w_bf16, batch, seq):
    n = x2d.shape[0]
    T = IN_PROJ_TILE
    tiles_per_batch = seq // T
    out_shape = [jax.ShapeDtypeStruct((n, 3 * WIDTH_A), jnp.bfloat16)]
    out_specs = [pl.BlockSpec((T, 3 * WIDTH_A), lambda i: (i, 0))]
    for _, dil in DIL_GROUPS:
        out_shape.append(jax.ShapeDtypeStruct((batch, dil, seq // dil, 3 * WIDTH_G), jnp.bfloat16))
        out_specs.append(pl.BlockSpec((1, dil, T // dil, 3 * WIDTH_G),
                                      lambda i: (i // tiles_per_batch, 0, i % tiles_per_batch, 0)))
    out_shape += [jax.ShapeDtypeStruct((n, WIDTH_C), jnp.bfloat16)]
    out_specs += [pl.BlockSpec((T, WIDTH_C), lambda i: (i, 0))]
    return pl.pallas_call(
        _in_proj_kernel,
        grid=(n // T,),
        in_specs=[pl.BlockSpec((T, D_MODEL), lambda i: (i, 0)),
                  pl.BlockSpec((1, D_MODEL), lambda i: (0, 0)),
                  pl.BlockSpec((D_MODEL, OFF_GATE), lambda i: (0, 0))],
        out_specs=out_specs,
        out_shape=out_shape,
        scratch_shapes=[pltpu.VMEM((WIDTH_G // LANES, T, LANES), jnp.float32)],
        compiler_params=_cparams(1),
        name="in_proj",
    )(x2d, g, w_bf16)


def _na_kernel(q_ref, k_ref, v_ref, bias_ref, out_ref, *, rows):
    i = pl.program_id(1)
    lane = lax.broadcasted_iota(jnp.int32, (GRID_W, LANES), 1)
    low_half = lane < HEAD_DIM
    n_win = NA_WIN_ROWS * GRID_W

    def row_body(j, carry):
        qr = i * NA_ROWS_PER_STEP + j
        rs = jnp.clip(qr - NA_WIN_ROWS // 2, 0, rows - NA_WIN_ROWS)
        shift = rs - qr + NA_WIN_ROWS - 1
        q_off = pl.multiple_of(j * GRID_W, GRID_W)
        k_off = pl.multiple_of(rs * GRID_W, GRID_W)
        scores = []
        for h in range(NA_HEADS):
            cols = slice((h // 2) * LANES, (h // 2 + 1) * LANES)
            qp = q_ref[pl.ds(q_off, GRID_W), cols]
            kp = k_ref[pl.ds(k_off, n_win), cols]
            keep = low_half if h % 2 == 0 else jnp.logical_not(low_half)
            qm = jnp.where(keep, qp, jnp.zeros_like(qp))
            bias = jnp.concatenate(
                [bias_ref[shift + 2 * p, h] for p in range(NA_WIN_ROWS // 2)], axis=-1)
            scores.append(_dot_nt(qm, kp) + bias)
        probs = []
        for s in scores:
            e = jnp.exp(s - jnp.max(s, axis=-1, keepdims=True))
            probs.append((e.astype(jnp.bfloat16), jnp.sum(e, axis=-1, keepdims=True)))
        outs = []
        for h, (p, l) in enumerate(probs):
            cols = slice((h // 2) * LANES, (h // 2 + 1) * LANES)
            vp = v_ref[pl.ds(k_off, n_win), cols]
            outs.append(_dot(p, vp) / l)
        for hp in range(NA_HEADS // 2):
            cols = slice(hp * LANES, (hp + 1) * LANES)
            out_ref[pl.ds(q_off, GRID_W), cols] = jnp.where(
                low_half, outs[2 * hp], outs[2 * hp + 1]).astype(jnp.bfloat16)
        return carry

    lax.fori_loop(0, NA_ROWS_PER_STEP, row_body, 0)


def _na_attn(na_qkv, bias_tab, batch, seq):
    rows = seq // GRID_W
    steps = rows // NA_ROWS_PER_STEP
    tq = NA_ROWS_PER_STEP * GRID_W
    n = na_qkv.shape[0]
    return pl.pallas_call(
        functools.partial(_na_kernel, rows=rows),
        grid=(batch, steps),
        in_specs=[pl.BlockSpec((tq, WIDTH_A), lambda b, i: (b * steps + i, 0)),
                  pl.BlockSpec((seq, WIDTH_A), lambda b, i: (b, 1)),
                  pl.BlockSpec((seq, WIDTH_A), lambda b, i: (b, 2)),
                  pl.BlockSpec(bias_tab.shape, lambda b, i: (0, 0, 0, 0))],
        out_specs=pl.BlockSpec((tq, WIDTH_A), lambda b, i: (b * steps + i, 0)),
        out_shape=jax.ShapeDtypeStruct((n, WIDTH_A), jnp.bfloat16),
        compiler_params=_cparams(2),
        name="na_attn",
    )(na_qkv, na_qkv, na_qkv, bias_tab)


def _t5_bucket_np(rel):
    half = T5_BUCKETS // 2
    max_exact = half // 2
    ret = np.where(rel > 0, half, 0)
    n = np.abs(rel)
    nf = np.maximum(n, 1).astype(np.float32)
    large = max_exact + (np.log(nf / np.float32(max_exact)) / np.float32(math.log(T5_MAX_DIST / max_exact))
                         * np.float32(half - max_exact)).astype(np.int32)
    large = np.minimum(large, half - 1)
    return ret + np.where(n < max_exact, n, large)


def _dil_bucket_table():
    q = np.arange(DIL_BLOCK)[:, None]
    j = np.arange(2 * DIL_BLOCK)[None, :]
    tabs = []
    for window, dil in DIL_GROUPS:
        half = (window // 2) // dil
        assert half == DIL_BLOCK // 2
        rel = (j - half) - q
        tabs.append(np.where(np.abs(rel) <= half, _t5_bucket_np(rel * dil), -1))
    return np.stack(tabs).astype(np.int32)


def _bias_prep_kernel(rpb_ref, t5_ref, bucket_ref, na_ref, dil_ref):
    n_dc = 2 * NA_WIN_COLS - 1
    n_dr = 2 * NA_WIN_ROWS - 1
    qc = lax.broadcasted_iota(jnp.int32, (GRID_W, 2 * GRID_W), 0)
    lane = lax.broadcasted_iota(jnp.int32, (GRID_W, 2 * GRID_W), 1)
    second = lane >= GRID_W
    kc = jnp.where(second, lane - GRID_W, lane)
    diff = kc - qc
    cs = jnp.clip(qc - NA_WIN_COLS // 2, 0, GRID_W - NA_WIN_COLS)
    valid = jnp.logical_and(kc >= cs, kc < cs + NA_WIN_COLS)

    def na_tile(mh, carry):
        m = mh // NA_HEADS
        h = mh - m * NA_HEADS
        base = (h * n_dr + m) * n_dc
        acc = jnp.zeros((GRID_W, 2 * GRID_W), jnp.float32)
        for d in range(n_dc):
            v = jnp.where(second, rpb_ref[base + n_dc + d], rpb_ref[base + d])
            acc = jnp.where(diff == d - (NA_WIN_COLS - 1), v, acc)
        na_ref[m, h] = jnp.where(valid, acc, NEG_INF)
        return carry

    lax.fori_loop(0, (n_dr - 1) * NA_HEADS, na_tile, 0)

    j = lax.broadcasted_iota(jnp.int32, (DIL_BLOCK, 2 * DIL_BLOCK), 1)
    half = DIL_BLOCK // 2
    n_heads = DIL_HEADS_PER_GROUP * len(DIL_GROUPS)
    for gi in range(len(DIL_GROUPS)):
        bucket = bucket_ref[gi]
        in_win = bucket >= 0
        for h in range(DIL_HEADS_PER_GROUP):
            acc = jnp.zeros((DIL_BLOCK, 2 * DIL_BLOCK), jnp.float32)
            for b in range(T5_BUCKETS):
                acc = jnp.where(bucket == b, t5_ref[b * n_heads + gi * DIL_HEADS_PER_GROUP + h], acc)
            for v, ok in enumerate((jnp.logical_and(in_win, j >= half), in_win,
                                    jnp.logical_and(in_win, j < 2 * DIL_BLOCK - half))):
                dil_ref[gi, v, h] = jnp.where(ok, acc, NEG_INF)


def _bias_prep(rpb, t5):
    smem = pl.BlockSpec(memory_space=pltpu.SMEM)
    return pl.pallas_call(
        _bias_prep_kernel,
        in_specs=[smem, smem, pl.BlockSpec(memory_space=pltpu.VMEM)],
        out_shape=[jax.ShapeDtypeStruct((2 * NA_WIN_ROWS - 2, NA_HEADS, GRID_W, 2 * GRID_W), jnp.float32),
                   jax.ShapeDtypeStruct((len(DIL_GROUPS), 3, DIL_HEADS_PER_GROUP, DIL_BLOCK, 2 * DIL_BLOCK),
                                        jnp.float32)],
        compiler_params=pltpu.CompilerParams(vmem_limit_bytes=VMEM_LIMIT),
        name="bias_prep",
    )(rpb.reshape(-1), t5.reshape(-1), jnp.asarray(_dil_bucket_table()))


def _dil_kernel(q_ref, kp_ref, kc_ref, kn_ref, vp_ref, vc_ref, vn_ref, bias_ref, o_ref, lse_ref, *, dil):
    lane = lax.broadcasted_iota(jnp.int32, (DIL_BLOCK, LANES), 1)
    low_half = lane < HEAD_DIM
    hb = DIL_BLOCK // 2

    span = pl.program_id(1)
    n_spans = pl.num_programs(1)
    nbs = kc_ref.shape[2] // DIL_BLOCK

    def window(prev_ref, cur_ref, next_ref, rho, blk, cols):
        lo = blk * DIL_BLOCK
        first = prev_ref[0, rho, hb:, cols] if blk == 0 else cur_ref[0, rho, lo - hb:lo, cols]
        last = (next_ref[0, rho, :hb, cols] if blk == nbs - 1
                else cur_ref[0, rho, lo + DIL_BLOCK:lo + DIL_BLOCK + hb, cols])
        return jnp.concatenate([first, cur_ref[0, rho, lo:lo + DIL_BLOCK, cols], last], axis=0)

    def unit(rho, blk):
        variant = 1
        if blk == nbs - 1:
            variant = jnp.where(span == n_spans - 1, 2, variant)
        if blk == 0:
            variant = jnp.where(span == 0, 0, variant)
        scores = []
        for h in range(DIL_HEADS_PER_GROUP):
            cols = slice((h // 2) * LANES, (h // 2 + 1) * LANES)
            qp = q_ref[0, rho, blk * DIL_BLOCK:(blk + 1) * DIL_BLOCK, cols]
            keep = low_half if h % 2 == 0 else jnp.logical_not(low_half)
            qm = jnp.where(keep, qp, jnp.zeros_like(qp))
            kw = window(kp_ref, kc_ref, kn_ref, rho, blk, cols)
            scores.append(_dot_nt(qm, kw) + bias_ref[variant, h])
        probs = []
        for s in scores:
            m = jnp.max(s, axis=-1, keepdims=True)
            e = jnp.exp(s - m)
            probs.append((e.astype(jnp.bfloat16), m, jnp.sum(e, axis=-1, keepdims=True)))
        outs, lses = [], []
        for h, (p, m, l) in enumerate(probs):
            cols = slice((h // 2) * LANES, (h // 2 + 1) * LANES)
            outs.append(_dot(p, window(vp_ref, vc_ref, vn_ref, rho, blk, cols)) / l)
            lses.append(jnp.broadcast_to(m + jnp.log(l), (DIL_BLOCK, LANES)))
        for hp in range(DIL_HEADS_PER_GROUP // 2):
            o_pair = jnp.where(low_half, outs[2 * hp], outs[2 * hp + 1])
            lse_pair = jnp.where(low_half, lses[2 * hp], lses[2 * hp + 1])
            if dil == 1:
                rows = slice(blk * DIL_BLOCK, (blk + 1) * DIL_BLOCK)
                o_ref[hp, rows, :] = o_pair
                lse_ref[hp, rows, :] = lse_pair
            else:
                rows = pl.ds(blk * DIL_BLOCK * dil + rho, DIL_BLOCK, stride=dil)
                o_ref[hp, rows, :] = o_pair
                lse_ref[hp, rows, :] = lse_pair

    def class_body(rho, carry):
        for blk in range(nbs):
            unit(rho, blk)
        return carry

    if dil == 1:
        class_body(0, 0)
    else:
        lax.fori_loop(0, dil, class_body, 0, unroll=2 if nbs == 1 else 1)


def _dil_attn(qkv, bias_tab, group, dil, batch, seq):
    L = seq // dil
    nb = L // DIL_BLOCK
    assert nb >= 2
    tq = DIL_STEP_TOKENS
    span = tq // dil
    nbs = span // DIL_BLOCK
    n_spans = L // span

    def cur(col):
        return pl.BlockSpec((1, dil, span, WIDTH_G), lambda b, n: (b, 0, n, col))

    def edge(col, shift):
        return pl.BlockSpec((1, dil, DIL_BLOCK, WIDTH_G),
                            lambda b, n: (b, 0, jnp.clip(n * nbs + (shift if shift < 0 else nbs), 0, nb - 1), col))

    n_pairs = WIDTH_G // LANES
    out_spec = pl.BlockSpec((n_pairs, tq, LANES), lambda b, n: (0, b * n_spans + n, 0))
    return pl.pallas_call(
        functools.partial(_dil_kernel, dil=dil),
        grid=(batch, n_spans),
        in_specs=[cur(0), edge(1, -1), cur(1), edge(1, 1), edge(2, -1), cur(2), edge(2, 1),
                  pl.BlockSpec((None, 3, DIL_HEADS_PER_GROUP, DIL_BLOCK, 2 * DIL_BLOCK),
                               lambda b, n: (group, 0, 0, 0, 0))],
        out_specs=[out_spec, out_spec],
        out_shape=[jax.ShapeDtypeStruct((n_pairs, batch * seq, LANES), jnp.float32)] * 2,
        compiler_params=_cparams(2),
        name=f"dil_attn_d{dil}",
    )(qkv, qkv, qkv, qkv, qkv, qkv, qkv, bias_tab)


def _mix_kernel(x_ref, na_ref, o0_ref, l0_ref, o1_ref, l1_ref, o2_ref, l2_ref, qc_ref, gate_ref,
                mkv_ref, wa_ref, wb_ref, wc_ref, wo_ref, fg_ref, rw_ref, rb_ref,
                x1_ref, h_ref, idx_ref, wt_ref):
    T = x_ref.shape[0]
    ob = []
    for hp in range(WIDTH_G // LANES):
        l0, l1, l2 = l0_ref[hp], l1_ref[hp], l2_ref[hp]
        m = jnp.maximum(jnp.maximum(l0, l1), l2)
        e0, e1, e2 = jnp.exp(l0 - m), jnp.exp(l1 - m), jnp.exp(l2 - m)
        ob.append((e0 * o0_ref[hp] + e1 * o1_ref[hp] + e2 * o2_ref[hp]) / (e0 + e1 + e2))
    ob = jnp.concatenate(ob, axis=-1)

    mem_scale = MEM_HEAD_DIM ** -0.5
    scores = []
    for h in range(MEM_HEADS):
        cols = slice(h * MEM_HEAD_DIM, (h + 1) * MEM_HEAD_DIM)
        scores.append(_dot_nt(qc_ref[:, cols], mkv_ref[:, cols]) * mem_scale)
    probs = []
    for s in scores:
        e = jnp.exp(s - jnp.max(s, axis=-1, keepdims=True))
        probs.append((e.astype(jnp.bfloat16), jnp.sum(e, axis=-1, keepdims=True)))
    oc = []
    for h, (p, l) in enumerate(probs):
        oc.append(_dot(p, mkv_ref[:, WIDTH_C + h * MEM_HEAD_DIM:WIDTH_C + (h + 1) * MEM_HEAD_DIM]) / l)
    oc = jnp.concatenate(oc, axis=-1).astype(jnp.bfloat16)

    y_a = _dot(na_ref[...], wa_ref[...])
    y_b = _dot(ob.astype(jnp.bfloat16), wb_ref[...])
    y_c = _dot(oc, wc_ref[...])
    merged = (gate_ref[:, 0:D_MODEL].astype(jnp.float32) * y_a
              + gate_ref[:, D_MODEL:2 * D_MODEL].astype(jnp.float32) * y_b
              + gate_ref[:, 2 * D_MODEL:3 * D_MODEL].astype(jnp.float32) * y_c)
    x1 = x_ref[...] + _dot(merged.astype(jnp.bfloat16), wo_ref[...])
    x1_ref[...] = x1

    h = _rms(x1, fg_ref[...])
    h_ref[...] = h.astype(jnp.bfloat16)
    logits = lax.dot_general(rw_ref[...], h, (((1,), (1,)), ((), ())), preferred_element_type=jnp.float32,
                             precision=lax.Precision.HIGHEST) + rb_ref[...]
    expert = lax.broadcasted_iota(jnp.int32, (N_EXPERTS, T), 0)
    vals, idxs = [], []
    for _ in range(TOP_K):
        mx = jnp.max(logits, axis=0, keepdims=True)
        sel = jnp.min(jnp.where(logits == mx, expert, N_EXPERTS), axis=0, keepdims=True)
        vals.append(mx)
        idxs.append(sel)
        logits = jnp.where(expert == sel, -jnp.inf, logits)
    ev = [jnp.exp(v - vals[0]) for v in vals]
    den = ev[0] + ev[1] + ev[2] + ev[3]
    idx_ref[...] = jnp.concatenate(idxs, axis=0)
    wt_ref[...] = jnp.concatenate([e / den for e in ev], axis=0)


def _mix_out(x2d, na_out, dil_outs, qc, gates, mkv, wa, wb, wc, wo, ffn_g, router_w, router_b, batch, seq):
    n = x2d.shape[0]
    T = PROJ_TILE
    tiles_per_batch = seq // T
    mem_len = mkv.shape[0] // batch
    row = lambda w: pl.BlockSpec((T, w), lambda i: (i, 0))
    full = lambda a: pl.BlockSpec(a.shape, lambda i: (0,) * a.ndim)
    in_specs = [row(D_MODEL), row(WIDTH_A)]
    args = [x2d, na_out]
    for o, l in dil_outs:
        pair_rows = pl.BlockSpec((WIDTH_G // LANES, T, LANES), lambda i: (0, i, 0))
        in_specs += [pair_rows, pair_rows]
        args += [o, l]
    in_specs += [row(WIDTH_C), row(3 * D_MODEL),
                 pl.BlockSpec((mem_len, 2 * WIDTH_C), lambda i: (i // tiles_per_batch, 0)),
                 full(wa), full(wb), full(wc), full(wo), full(ffn_g), full(router_w), full(router_b)]
    args += [qc, gates, mkv, wa, wb, wc, wo, ffn_g, router_w, router_b]
    return pl.pallas_call(
        _mix_kernel,
        grid=(n // T,),
        in_specs=in_specs,
        out_specs=[row(D_MODEL), row(D_MODEL), pl.BlockSpec((TOP_K, T), lambda i: (0, i)),
                   pl.BlockSpec((TOP_K, T), lambda i: (0, i))],
        out_shape=[jax.ShapeDtypeStruct((n, D_MODEL), jnp.float32),
                   jax.ShapeDtypeStruct((n, D_MODEL), jnp.bfloat16),
                   jax.ShapeDtypeStruct((TOP_K, n), jnp.int32),
                   jax.ShapeDtypeStruct((TOP_K, n), jnp.float32)],
        compiler_params=_cparams(1),
        name="mix_out",
    )(*args)


def _route_kernel(idx_ref, lpos_ref, ptab_ref, be_ref, pe_ref, counts_ref, carry_ref, pstart_ref):
    phase = pl.program_id(0)
    i = pl.program_id(1)
    T = idx_ref.shape[1]
    n_blocks = be_ref.shape[1]
    expert = lax.broadcasted_iota(jnp.int32, (N_EXPERTS, T), 0)
    idx = idx_ref[...]
    onehots = [expert == idx[k:k + 1, :] for k in range(TOP_K)]
    member = sum(o.astype(jnp.float32) for o in onehots)
    cnt = jnp.sum(member, axis=1, keepdims=True)
    seg = jnp.floor((cnt + (MOE_PIECE - 1)) * (1.0 / MOE_PIECE)) * MOE_PIECE
    r = lax.broadcasted_iota(jnp.int32, (N_EXPERTS, N_EXPERTS), 0)
    c = lax.broadcasted_iota(jnp.int32, (N_EXPERTS, N_EXPERTS), 1)
    lower_incl = (r >= c).astype(jnp.float32)

    def cumsum_experts(v):
        return jnp.dot(lower_incl, v, preferred_element_type=jnp.float32, precision=lax.Precision.HIGHEST)

    @pl.when(jnp.logical_and(phase == 0, i == 0))
    def _():
        counts_ref[...] = jnp.zeros_like(counts_ref)

    @pl.when(phase == 0)
    def _():
        counts_ref[...] += seg

    @pl.when(jnp.logical_and(phase == 1, i == 0))
    def _():
        counts = counts_ref[...]
        padded = jnp.floor((counts + (MOE_BLOCK - 1)) * (1.0 / MOE_BLOCK)) * MOE_BLOCK
        pad_ends = cumsum_experts(padded)
        pstart_ref[...] = pad_ends - padded
        carry_ref[...] = jnp.zeros_like(carry_ref)
        pe_ref[...] = pad_ends[:, 0:1].astype(jnp.int32)
        blk_start = (lax.broadcasted_iota(jnp.int32, (N_EXPERTS, n_blocks), 1) * MOE_BLOCK).astype(jnp.float32)
        be = jnp.sum((pad_ends[:, 0:1] <= blk_start).astype(jnp.int32), axis=0, keepdims=True)
        be_ref[...] = jnp.minimum(be, N_EXPERTS - 1)

    @pl.when(phase == 1)
    def _():
        seg_b = jnp.broadcast_to(seg, (N_EXPERTS, LANES))
        local_end = cumsum_experts(seg_b)[:, 0:1]
        local_start = local_end - seg
        global_start = pstart_ref[:, 0:1] + carry_ref[:, 0:1]
        rr = lax.broadcasted_iota(jnp.int32, (T, T), 0)
        cc = lax.broadcasted_iota(jnp.int32, (T, T), 1)
        earlier = (rr < cc).astype(jnp.bfloat16)
        before = _dot(member.astype(jnp.bfloat16), earlier)
        base = local_start + before
        lpos = [jnp.sum(jnp.where(o, base, 0.0), axis=0, keepdims=True) for o in onehots]
        lpos_ref[...] = jnp.concatenate(lpos, axis=0).astype(jnp.int32)

        inv = 1.0 / MOE_PIECE
        piece = lax.broadcasted_iota(jnp.int32, (N_EXPERTS, MOE_PIECE_TAB), 1).astype(jnp.float32)
        first = local_start * inv
        inside = jnp.logical_and(piece >= first, piece < local_end * inv)
        dst = jnp.sum(jnp.where(inside, global_start * inv + piece - first, 0.0), axis=0, keepdims=True)
        n_pieces = local_end[N_EXPERTS - 1:N_EXPERTS, :] * inv
        dst = jnp.where(piece[0:1, :] == MOE_PIECE_TAB - 1, n_pieces, dst)
        ptab_ref[...] = dst.astype(jnp.int32)
        carry_ref[...] += seg


def _moe_route(topk_idx, n_blocks):
    n = topk_idx.shape[1]
    T = MOE_TILE
    acc = pltpu.VMEM((N_EXPERTS, LANES), jnp.float32)
    return pl.pallas_call(
        _route_kernel,
        grid=(2, n // T),
        in_specs=[pl.BlockSpec((TOP_K, T), lambda p, i: (0, i))],
        out_specs=[pl.BlockSpec((TOP_K, T), lambda p, i: (0, p * i)),
                   pl.BlockSpec((None, 1, MOE_PIECE_TAB), lambda p, i: (p * i, 0, 0)),
                   pl.BlockSpec((1, n_blocks), lambda p, i: (0, 0)),
                   pl.BlockSpec((N_EXPERTS, 1), lambda p, i: (0, 0))],
        out_shape=[jax.ShapeDtypeStruct((TOP_K, n), jnp.int32),
                   jax.ShapeDtypeStruct((n // T, 1, MOE_PIECE_TAB), jnp.int32),
                   jax.ShapeDtypeStruct((1, n_blocks), jnp.int32),
                   jax.ShapeDtypeStruct((N_EXPERTS, 1), jnp.int32)],
        scratch_shapes=[acc, acc, acc],
        compiler_params=_cparams(2),
        name="moe_route",
    )(topk_idx)


def _piece(ref, p):
    return ref.at[pl.ds(pl.multiple_of(p * MOE_PIECE, MOE_PIECE), MOE_PIECE), :]


def _dispatch_kernel(pe_ref, ptab_ref, lpos_ref, h_ref, xs_hbm, zbuf, pbuf, sem, zsem):
    i = pl.program_id(0)
    T = h_ref.shape[0]
    half = h_ref.shape[1] // 2

    def zero_copy(e):
        end = pe_ref[e]
        start = pl.multiple_of(end - MOE_BLOCK, MOE_BLOCK)
        return pltpu.make_async_copy(zbuf, xs_hbm.at[pl.ds(start, MOE_BLOCK), :], zsem)

    def nonempty(e):
        return pe_ref[e] > (pe_ref[e - 1] if e else 0)

    @pl.when(i == 0)
    def _():
        zbuf[...] = jnp.zeros_like(zbuf)
        for e in range(N_EXPERTS):
            @pl.when(nonempty(e))
            def _():
                zero_copy(e).start()
        for e in range(N_EXPERTS):
            @pl.when(nonempty(e))
            def _():
                zero_copy(e).wait()

        def tail_copy(b):
            return pltpu.make_async_copy(
                zbuf, xs_hbm.at[pl.ds(pl.multiple_of(b * MOE_BLOCK, MOE_BLOCK), MOE_BLOCK), :], zsem)
        first_tail = pe_ref[N_EXPERTS - 1] // MOE_BLOCK
        n_blocks = xs_hbm.shape[0] // MOE_BLOCK
        lax.fori_loop(first_tail, n_blocks, lambda b, c: (tail_copy(b).start(), c)[1], 0)
        lax.fori_loop(first_tail, n_blocks, lambda b, c: (tail_copy(b).wait(), c)[1], 0)

    row = lax.broadcasted_iota(jnp.int32, (MOE_TILE_ROWS, T), 0)
    lpos = lpos_ref[...]
    perm = sum((row == lpos[k:k + 1, :]).astype(jnp.bfloat16) for k in range(TOP_K))
    lo = lax.bitcast_convert_type(_dot(perm, h_ref[:, :half]), jnp.uint32)
    hi = lax.bitcast_convert_type(_dot(perm, h_ref[:, half:]), jnp.uint32)
    pbuf[...] = hi | (lo >> 16)

    n_pieces = ptab_ref[MOE_PIECE_TAB - 1]

    def piece_copy(p):
        return pltpu.make_async_copy(_piece(pbuf, p), _piece(xs_hbm, ptab_ref[p]), sem)

    lax.fori_loop(0, n_pieces, lambda p, c: (piece_copy(p).start(), c)[1], 0)
    lax.fori_loop(0, n_pieces, lambda p, c: (piece_copy(p).wait(), c)[1], 0)


def _moe_dispatch(h, lpos, ptab, pad_ends, n_blocks):
    n, width = h.shape
    T = MOE_TILE
    n_steps = n // T

    def kernel(pe_ref, ptab_ref, *rest):
        _dispatch_kernel(pe_ref, ptab_ref.at[0], *rest)

    grid_spec = pltpu.PrefetchScalarGridSpec(
        num_scalar_prefetch=1,
        grid=(n_steps,),
        in_specs=[pl.BlockSpec((None, 1, MOE_PIECE_TAB), lambda i, pe: (i, 0, 0), memory_space=pltpu.SMEM),
                  pl.BlockSpec((TOP_K, T), lambda i, pe: (0, i)),
                  pl.BlockSpec((T, width), lambda i, pe: (i, 0))],
        out_specs=pl.BlockSpec(memory_space=pl.ANY),
        scratch_shapes=[pltpu.VMEM((MOE_BLOCK, width // 2), jnp.uint32),
                        pltpu.VMEM((MOE_TILE_ROWS, width // 2), jnp.uint32),
                        pltpu.SemaphoreType.DMA(()),
                        pltpu.SemaphoreType.DMA(())],
    )
    return pl.pallas_call(
        kernel,
        grid_spec=grid_spec,
        out_shape=jax.ShapeDtypeStruct((n_blocks * MOE_BLOCK, width // 2), jnp.uint32),
        compiler_params=_cparams(1),
        name="moe_dispatch",
    )(pad_ends, ptab, lpos, h)


def _moe_kernel(be_ref, pe_ref, xs_ref, wgu_ref, bgu_ref, wdn_ref, bdn_ref, y_ref, wgu_bf, wdn_bf):
    i = pl.program_id(0)
    n_valid = pe_ref[N_EXPERTS - 1] // MOE_BLOCK
    new_expert = jnp.logical_or(i == 0, be_ref[i] != be_ref[jnp.maximum(i - 1, 0)])

    @pl.when(jnp.logical_and(i < n_valid, new_expert))
    def _():
        wgu_bf[...] = wgu_ref[0].astype(jnp.bfloat16)
        wdn_bf[...] = wdn_ref[0].astype(jnp.bfloat16)

    @pl.when(i < n_valid)
    def _():
        x = _unpack_bf16_pairs(xs_ref[...])
        gu = _dot(x, wgu_bf[...]) + bgu_ref[0]
        g = jnp.minimum(gu[:, :D_FF], SWIGLU_LIMIT)
        u = jnp.clip(gu[:, D_FF:], -SWIGLU_LIMIT, SWIGLU_LIMIT)
        a = (u + 1.0) * (g * _sigmoid(SWIGLU_ALPHA * g))
        y_ref[...] = _pack_bf16_pairs(_dot(a.astype(jnp.bfloat16), wdn_bf[...]) + bdn_ref[0])

    @pl.when(i >= n_valid)
    def _():
        y_ref[...] = jnp.zeros_like(y_ref)


def _moe_experts(xs, block_expert, pad_ends, wgu, bgu, wdn, bdn):
    n_blocks = block_expert.shape[0]

    def row_blk(i, be, pe):
        return (jnp.minimum(i, jnp.maximum(pe[N_EXPERTS - 1] // MOE_BLOCK - 1, 0)), 0)

    grid_spec = pltpu.PrefetchScalarGridSpec(
        num_scalar_prefetch=2,
        grid=(n_blocks,),
        in_specs=[pl.BlockSpec((MOE_BLOCK, D_MODEL // 2), row_blk),
                  pl.BlockSpec((1, D_MODEL, 2 * D_FF), lambda i, be, pe: (be[i], 0, 0)),
                  pl.BlockSpec((1, 1, 2 * D_FF), lambda i, be, pe: (be[i], 0, 0)),
                  pl.BlockSpec((1, D_FF, D_MODEL), lambda i, be, pe: (be[i], 0, 0)),
                  pl.BlockSpec((1, 1, D_MODEL), lambda i, be, pe: (be[i], 0, 0))],
        out_specs=pl.BlockSpec((MOE_BLOCK, D_MODEL // 2), lambda i, be, pe: (i, 0)),
        scratch_shapes=[pltpu.VMEM((D_MODEL, 2 * D_FF), jnp.bfloat16),
                        pltpu.VMEM((D_FF, D_MODEL), jnp.bfloat16)],
    )
    return pl.pallas_call(
        _moe_kernel,
        grid_spec=grid_spec,
        out_shape=jax.ShapeDtypeStruct((n_blocks * MOE_BLOCK, D_MODEL // 2), jnp.uint32),
        compiler_params=_cparams(1),
        name="moe_experts",
    )(block_expert, pad_ends, xs, wgu, bgu, wdn, bdn)


def _gather_pieces(src_hbm, ptab_ref, dst_ref, sem):
    def body(p, carry):
        pltpu.make_async_copy(_piece(src_hbm, ptab_ref[p]), _piece(dst_ref, p), sem).start()
        return carry
    lax.fori_loop(0, MOE_TILE_ROWS // MOE_PIECE, body, 0, unroll=8)


def _wait_pieces(src_hbm, dst_ref, sem):
    def body(p, carry):
        pltpu.make_async_copy(_piece(src_hbm, 0), _piece(dst_ref, 0), sem).wait()
        return carry
    lax.fori_loop(0, MOE_TILE_ROWS // MOE_PIECE, body, 0, unroll=8)


def _combine_kernel(ptab_ref, ptab_next_ref, ys_hbm, lpos_ref, wt_ref, x1_ref, g_ref, out_ref, gbuf, sem):
    i = pl.program_id(0)
    n_steps = pl.num_programs(0)
    slot = i % 2
    T = x1_ref.shape[0]

    @pl.when(i == 0)
    def _():
        _gather_pieces(ys_hbm, ptab_ref, gbuf.at[0], sem.at[0])

    @pl.when(i + 1 < n_steps)
    def _():
        _gather_pieces(ys_hbm, ptab_next_ref, gbuf.at[1 - slot], sem.at[1 - slot])

    _wait_pieces(ys_hbm, gbuf.at[slot], sem.at[slot])
    rows = _unpack_bf16_pairs(gbuf[slot])
    col = lax.broadcasted_iota(jnp.int32, (T, MOE_TILE_ROWS), 1)
    lpos = lpos_ref[...]
    wt = wt_ref[...]
    cmb = jnp.zeros((T, MOE_TILE_ROWS), jnp.float32)
    for k in range(TOP_K):
        cmb = jnp.where(col == lpos[:, k:k + 1], wt[:, k:k + 1], cmb)
    hi = cmb.astype(jnp.bfloat16)
    lo = (cmb - hi.astype(jnp.float32)).astype(jnp.bfloat16)
    y2 = _dot(jnp.concatenate([hi, lo], axis=0), rows)
    y = x1_ref[...] + y2[:T] + y2[T:]
    out_ref[...] = _rms(y, g_ref[...])


def _moe_combine(ys, ptab, lpos_t, wts, x1, final_g):
    n = x1.shape[0]
    T = MOE_TILE
    n_steps = n // T
    smem_blk = lambda shift: pl.BlockSpec(
        (None, 1, MOE_PIECE_TAB), lambda i: (jnp.minimum(i + shift, n_steps - 1), 0, 0),
        memory_space=pltpu.SMEM)

    def kernel(ptab_ref, ptab_next_ref, *rest):
        _combine_kernel(ptab_ref.at[0], ptab_next_ref.at[0], *rest)

    return pl.pallas_call(
        kernel,
        grid=(n_steps,),
        in_specs=[smem_blk(0), smem_blk(1),
                  pl.BlockSpec(memory_space=pl.ANY),
                  pl.BlockSpec((T, TOP_K), lambda i: (i, 0)),
                  pl.BlockSpec((T, TOP_K), lambda i: (i, 0)),
                  pl.BlockSpec((T, D_MODEL), lambda i: (i, 0)),
                  pl.BlockSpec((1, D_MODEL), lambda i: (0, 0))],
        out_specs=pl.BlockSpec((T, D_MODEL), lambda i: (i, 0)),
        out_shape=jax.ShapeDtypeStruct((n, D_MODEL), jnp.float32),
        scratch_shapes=[pltpu.VMEM((2, MOE_TILE_ROWS, D_MODEL // 2), jnp.uint32),
                        pltpu.SemaphoreType.DMA((2,))],
        compiler_params=_cparams(1),
        name="moe_combine",
    )(ptab, ptab, ys, lpos_t, wts, x1, final_g)


def kernel(x, mem, attn_norm_g, mem_norm_g, w_in, w_mem_kv, na_rpb, t5_rel_bias, w_branch_a, w_branch_b,
           w_branch_c, w_out, ffn_norm_g, router_w, router_b, expert_w_gate_up, expert_b_gate_up,
           expert_w_down, expert_b_down, final_norm_g):
    B, S, D = x.shape
    depth = w_in.shape[0]
    bf = jnp.bfloat16
    x2d = x.reshape(B * S, D)
    mem2d = mem.reshape(B * mem.shape[1], D)
    assert depth == 1, "single-layer block (per-layer parameters carry a leading axis of 1)"
    l = 0
    na_bias, dil_bias = _bias_prep(na_rpb[l], t5_rel_bias)
    mkv = _mem_kv(mem2d, mem_norm_g[l][None], w_mem_kv[l].astype(bf))
    na_qkv, d0, d1, d2, qc = _in_proj(x2d, attn_norm_g[l][None], w_in[l][:, :OFF_GATE].astype(bf), B, S)
    gates = _gate_proj(x2d, attn_norm_g[l][None], w_in[l][:, OFF_GATE:].astype(bf))
    na_out = _na_attn(na_qkv, na_bias, B, S)
    dil_outs = [_dil_attn(qkv, dil_bias, gi, dil, B, S)
                for gi, (qkv, (_, dil)) in enumerate(zip((d0, d1, d2), DIL_GROUPS))]
    x1, h, topk_idx, topk_w = _mix_out(
        x2d, na_out, dil_outs, qc, gates, mkv, w_branch_a[l].astype(bf), w_branch_b[l].astype(bf),
        w_branch_c[l].astype(bf), w_out[l].astype(bf), ffn_norm_g[l][None], router_w[l].T,
        router_b[l][:, None], B, S)

    n = B * S
    max_rows = n * TOP_K + (n // MOE_TILE) * N_EXPERTS * (MOE_PIECE - 1)
    n_blocks = -(-max_rows // MOE_BLOCK) + N_EXPERTS
    lpos, ptab, block_expert, pad_ends = _moe_route(topk_idx, n_blocks)
    block_expert = block_expert.reshape(n_blocks)
    pad_ends = pad_ends.reshape(N_EXPERTS)
    xs = _moe_dispatch(h, lpos, ptab, pad_ends, n_blocks)
    ys = _moe_experts(xs, block_expert, pad_ends, expert_w_gate_up[l], expert_b_gate_up[l][:, None, :],
                      expert_w_down[l], expert_b_down[l][:, None, :])
    out = _moe_combine(ys, ptab, lpos.T, topk_w.T, x1, final_norm_g[None])
    return out.reshape(B, S, D)
```

```python
import functools
import math

import numpy as np
import jax
import jax.numpy as jnp
from jax import lax
from jax.experimental import pallas as pl
from jax.experimental.pallas import tpu as pltpu

D_MODEL = 1024
GRID_W = 64
MEM_HEADS = 4
MEM_HEAD_DIM = 128
HEAD_DIM = 64
NA_HEADS = 8
NA_WIN_ROWS = 8
NA_WIN_COLS = 16
DIL_GROUPS = ((128, 1), (512, 4), (2048, 16))
DIL_HEADS_PER_GROUP = 4
DIL_BLOCK = 128
T5_BUCKETS = 32
T5_MAX_DIST = 1024
N_EXPERTS = 32
TOP_K = 4
D_FF = D_MODEL
SWIGLU_ALPHA = 1.702
SWIGLU_LIMIT = 7.0
RMS_EPS = 1e-6
NEG_INF = -1e30

WIDTH_A = NA_HEADS * HEAD_DIM
WIDTH_B = DIL_HEADS_PER_GROUP * len(DIL_GROUPS) * HEAD_DIM
WIDTH_G = DIL_HEADS_PER_GROUP * HEAD_DIM
WIDTH_C = MEM_HEADS * MEM_HEAD_DIM
OFF_QA, OFF_KA, OFF_VA = 0, WIDTH_A, 2 * WIDTH_A
OFF_QB = 3 * WIDTH_A
OFF_KB = OFF_QB + WIDTH_B
OFF_VB = OFF_KB + WIDTH_B
OFF_QC = OFF_VB + WIDTH_B
OFF_GATE = OFF_QC + WIDTH_C
IN_COLS = OFF_GATE + 3 * D_MODEL

LANES = 128
VMEM_LIMIT = 56 * 1024 * 1024

IN_PROJ_TILE = 1024
PROJ_TILE = 512
NA_ROWS_PER_STEP = 8
DIL_STEP_TOKENS = 2048
MOE_BLOCK = 512
MOE_TILE = 256
MOE_PIECE = 8
MOE_TILE_ROWS = -(-(TOP_K * MOE_TILE + N_EXPERTS * (MOE_PIECE - 1)) // 256) * 256
MOE_PIECE_TAB = 256
assert MOE_TILE_ROWS // MOE_PIECE < MOE_PIECE_TAB


def _cparams(n_axes):
    return pltpu.CompilerParams(
        dimension_semantics=("arbitrary",) * n_axes, vmem_limit_bytes=VMEM_LIMIT)


def _rms(x, g):
    return x * lax.rsqrt(jnp.mean(x * x, axis=-1, keepdims=True) + RMS_EPS) * g


def _sigmoid(x):
    return 1.0 / (1.0 + jnp.exp(-x))


def _pack_bf16_pairs(x):
    w = x.shape[1] // 2
    bits = lax.bitcast_convert_type(x.astype(jnp.bfloat16).astype(jnp.float32), jnp.uint32)
    return (bits[:, w:] & jnp.uint32(0xFFFF0000)) | (bits[:, :w] >> 16)


def _unpack_bf16_pairs(p):
    lo = lax.bitcast_convert_type(p << 16, jnp.float32)
    hi = lax.bitcast_convert_type(p & jnp.uint32(0xFFFF0000), jnp.float32)
    return jnp.concatenate([lo, hi], axis=1).astype(jnp.bfloat16)


def _dot(a, b):
    return jnp.dot(a, b, preferred_element_type=jnp.float32)


def _dot_nt(a, b):
    return lax.dot_general(a, b, (((1,), (1,)), ((), ())), preferred_element_type=jnp.float32)


def _mem_kv_kernel(mem_ref, g_ref, w_ref, out_ref):
    h = _rms(mem_ref[...], g_ref[...]).astype(jnp.bfloat16)
    out_ref[...] = _dot(h, w_ref[...]).astype(jnp.bfloat16)


def _mem_kv(mem2d, g, w):
    rows = mem2d.shape[0]
    return pl.pallas_call(
        _mem_kv_kernel,
        out_shape=jax.ShapeDtypeStruct((rows, 2 * WIDTH_C), jnp.bfloat16),
        name="mem_kv",
        compiler_params=pltpu.CompilerParams(vmem_limit_bytes=VMEM_LIMIT),
    )(mem2d, g, w)


def _in_proj_kernel(x_ref, g_ref, w_ref, na_ref, d0_ref, d1_ref, d2_ref, qc_ref, acc_ref):
    T = x_ref.shape[0]
    h = _rms(x_ref[...], g_ref[...]).astype(jnp.bfloat16)
    q_scale = HEAD_DIM ** -0.5

    for c in range(3):
        acc = _dot(h, w_ref[:, c * WIDTH_A:(c + 1) * WIDTH_A])
        if c == 0:
            acc = acc * q_scale
        na_ref[:, c * WIDTH_A:(c + 1) * WIDTH_A] = acc.astype(jnp.bfloat16)

    for gi, ((_, dil), d_ref) in enumerate(zip(DIL_GROUPS, (d0_ref, d1_ref, d2_ref))):
        for kind, off in enumerate((OFF_QB, OFF_KB, OFF_VB)):
            lo = off + gi * WIDTH_G
            acc = _dot(h, w_ref[:, lo:lo + WIDTH_G])
            if kind == 0:
                acc = acc * q_scale
            cols = slice(kind * WIDTH_G, (kind + 1) * WIDTH_G)
            if dil == 1:
                d_ref[0, 0, :, cols] = acc.astype(jnp.bfloat16)
            else:
                for c in range(WIDTH_G // LANES):
                    acc_ref[c] = acc[:, c * LANES:(c + 1) * LANES]
                for rho in range(dil):
                    d_ref[0, rho, :, cols] = jnp.concatenate(
                        [acc_ref[c, pl.ds(rho, T // dil, stride=dil), :] for c in range(WIDTH_G // LANES)],
                        axis=-1).astype(jnp.bfloat16)

    qc_ref[...] = _dot(h, w_ref[:, OFF_QC:OFF_QC + WIDTH_C]).astype(jnp.bfloat16)


def _gate_proj_kernel(x_ref, g_ref, w_ref, gate_ref):
    h = _rms(x_ref[...], g_ref[...]).astype(jnp.bfloat16)
    chunk = 512
    for c in range(3 * D_MODEL // chunk):
        gate_ref[:, c * chunk:(c + 1) * chunk] = _sigmoid(
            _dot(h, w_ref[:, c * chunk:(c + 1) * chunk])).astype(jnp.bfloat16)


def _gate_proj(x2d, g, w_gate_bf16):
    n = x2d.shape[0]
    T = IN_PROJ_TILE
    return pl.pallas_call(
        _gate_proj_kernel,
        grid=(n // T,),
        in_specs=[pl.BlockSpec((T, D_MODEL), lambda i: (i, 0)),
                  pl.BlockSpec((1, D_MODEL), lambda i: (0, 0)),
                  pl.BlockSpec((D_MODEL, 3 * D_MODEL), lambda i: (0, 0))],
        out_specs=pl.BlockSpec((T, 3 * D_MODEL), lambda i: (i, 0)),
        out_shape=jax.ShapeDtypeStruct((n, 3 * D_MODEL), jnp.bfloat16),
        compiler_params=_cparams(1),
        name="gate_proj",
    )(x2d, g, w_gate_bf16)


def _in_proj(x2d, g, w_bf16, batch, seq):
    n = x2d.shape[0]
    T = IN_PROJ_TILE
    tiles_per_batch = seq // T
    out_shape = [jax.ShapeDtypeStruct((n, 3 * WIDTH_A), jnp.bfloat16)]
    out_specs = [pl.BlockSpec((T, 3 * WIDTH_A), lambda i: (i, 0))]
    for _, dil in DIL_GROUPS:
        out_shape.append(jax.ShapeDtypeStruct((batch, dil, seq // dil, 3 * WIDTH_G), jnp.bfloat16))
        out_specs.append(pl.BlockSpec((1, dil, T // dil, 3 * WIDTH_G),
                                      lambda i: (i // tiles_per_batch, 0, i % tiles_per_batch, 0)))
    out_shape += [jax.ShapeDtypeStruct((n, WIDTH_C), jnp.bfloat16)]
    out_specs += [pl.BlockSpec((T, WIDTH_C), lambda i: (i, 0))]
    return pl.pallas_call(
        _in_proj_kernel,
        grid=(n // T,),
        in_specs=[pl.BlockSpec((T, D_MODEL), lambda i: (i, 0)),
                  pl.BlockSpec((1, D_MODEL), lambda i: (0, 0)),
                  pl.BlockSpec((D_MODEL, OFF_GATE), lambda i: (0, 0))],
        out_specs=out_specs,
        out_shape=out_shape,
        scratch_shapes=[pltpu.VMEM((WIDTH_G // LANES, T, LANES), jnp.float32)],
        compiler_params=_cparams(1),
        name="in_proj",
    )(x2d, g, w_bf16)


def _na_kernel(q_ref, k_ref, v_ref, bias_ref, out_ref, *, rows):
    i = pl.program_id(1)
    lane = lax.broadcasted_iota(jnp.int32, (GRID_W, LANES), 1)
    low_half = lane < HEAD_DIM
    n_win = NA_WIN_ROWS * GRID_W

    def row_body(j, carry):
        qr = i * NA_ROWS_PER_STEP + j
        rs = jnp.clip(qr - NA_WIN_ROWS // 2, 0, rows - NA_WIN_ROWS)
        shift = rs - qr + NA_WIN_ROWS - 1
        q_off = pl.multiple_of(j * GRID_W, GRID_W)
        k_off = pl.multiple_of(rs * GRID_W, GRID_W)
        scores = []
        for h in range(NA_HEADS):
            cols = slice((h // 2) * LANES, (h // 2 + 1) * LANES)
            qp = q_ref[pl.ds(q_off, GRID_W), cols]
            kp = k_ref[pl.ds(k_off, n_win), cols]
            keep = low_half if h % 2 == 0 else jnp.logical_not(low_half)
            qm = jnp.where(keep, qp, jnp.zeros_like(qp))
            bias = jnp.concatenate(
                [bias_ref[shift + 2 * p, h] for p in range(NA_WIN_ROWS // 2)], axis=-1)
            scores.append(_dot_nt(qm, kp) + bias)
        probs = []
        for s in scores:
            e = jnp.exp(s - jnp.max(s, axis=-1, keepdims=True))
            probs.append((e.astype(jnp.bfloat16), jnp.sum(e, axis=-1, keepdims=True)))
        outs = []
        for h, (p, l) in enumerate(probs):
            cols = slice((h // 2) * LANES, (h // 2 + 1) * LANES)
            vp = v_ref[pl.ds(k_off, n_win), cols]
            outs.append(_dot(p, vp) / l)
        for hp in range(NA_HEADS // 2):
            cols = slice(hp * LANES, (hp + 1) * LANES)
            out_ref[pl.ds(q_off, GRID_W), cols] = jnp.where(
                low_half, outs[2 * hp], outs[2 * hp + 1]).astype(jnp.bfloat16)
        return carry

    lax.fori_loop(0, NA_ROWS_PER_STEP, row_body, 0)


def _na_attn(na_qkv, bias_tab, batch, seq):
    rows = seq // GRID_W
    steps = rows // NA_ROWS_PER_STEP
    tq = NA_ROWS_PER_STEP * GRID_W
    n = na_qkv.shape[0]
    return pl.pallas_call(
        functools.partial(_na_kernel, rows=rows),
        grid=(batch, steps),
        in_specs=[pl.BlockSpec((tq, WIDTH_A), lambda b, i: (b * steps + i, 0)),
                  pl.BlockSpec((seq, WIDTH_A), lambda b, i: (b, 1)),
                  pl.BlockSpec((seq, WIDTH_A), lambda b, i: (b, 2)),
                  pl.BlockSpec(bias_tab.shape, lambda b, i: (0, 0, 0, 0))],
        out_specs=pl.BlockSpec((tq, WIDTH_A), lambda b, i: (b * steps + i, 0)),
        out_shape=jax.ShapeDtypeStruct((n, WIDTH_A), jnp.bfloat16),
        compiler_params=_cparams(2),
        name="na_attn",
    )(na_qkv, na_qkv, na_qkv, bias_tab)


def _t5_bucket_np(rel):
    half = T5_BUCKETS // 2
    max_exact = half // 2
    ret = np.where(rel > 0, half, 0)
    n = np.abs(rel)
    nf = np.maximum(n, 1).astype(np.float32)
    large = max_exact + (np.log(nf / np.float32(max_exact)) / np.float32(math.log(T5_MAX_DIST / max_exact))
                         * np.float32(half - max_exact)).astype(np.int32)
    large = np.minimum(large, half - 1)
    return ret + np.where(n < max_exact, n, large)


def _dil_bucket_table():
    q = np.arange(DIL_BLOCK)[:, None]
    j = np.arange(2 * DIL_BLOCK)[None, :]
    tabs = []
    for window, dil in DIL_GROUPS:
        half = (window // 2) // dil
        assert half == DIL_BLOCK // 2
        rel = (j - half) - q
        tabs.append(np.where(np.abs(rel) <= half, _t5_bucket_np(rel * dil), -1))
    return np.stack(tabs).astype(np.int32)


def _bias_prep_kernel(rpb_ref, t5_ref, bucket_ref, na_ref, dil_ref):
    n_dc = 2 * NA_WIN_COLS - 1
    n_dr = 2 * NA_WIN_ROWS - 1
    qc = lax.broadcasted_iota(jnp.int32, (GRID_W, 2 * GRID_W), 0)
    lane = lax.broadcasted_iota(jnp.int32, (GRID_W, 2 * GRID_W), 1)
    second = lane >= GRID_W
    kc = jnp.where(second, lane - GRID_W, lane)
    diff = kc - qc
    cs = jnp.clip(qc - NA_WIN_COLS // 2, 0, GRID_W - NA_WIN_COLS)
    valid = jnp.logical_and(kc >= cs, kc < cs + NA_WIN_COLS)

    def na_tile(mh, carry):
        m = mh // NA_HEADS
        h = mh - m * NA_HEADS
        base = (h * n_dr + m) * n_dc
        acc = jnp.zeros((GRID_W, 2 * GRID_W), jnp.float32)
        for d in range(n_dc):
            v = jnp.where(second, rpb_ref[base + n_dc + d], rpb_ref[base + d])
            acc = jnp.where(diff == d - (NA_WIN_COLS - 1), v, acc)
        na_ref[m, h] = jnp.where(valid, acc, NEG_INF)
        return carry

    lax.fori_loop(0, (n_dr - 1) * NA_HEADS, na_tile, 0)

    j = lax.broadcasted_iota(jnp.int32, (DIL_BLOCK, 2 * DIL_BLOCK), 1)
    half = DIL_BLOCK // 2
    n_heads = DIL_HEADS_PER_GROUP * len(DIL_GROUPS)
    for gi in range(len(DIL_GROUPS)):
        bucket = bucket_ref[gi]
        in_win = bucket >= 0
        for h in range(DIL_HEADS_PER_GROUP):
            acc = jnp.zeros((DIL_BLOCK, 2 * DIL_BLOCK), jnp.float32)
            for b in range(T5_BUCKETS):
                acc = jnp.where(bucket == b, t5_ref[b * n_heads + gi * DIL_HEADS_PER_GROUP + h], acc)
            for v, ok in enumerate((jnp.logical_and(in_win, j >= half), in_win,
                                    jnp.logical_and(in_win, j < 2 * DIL_BLOCK - half))):
                dil_ref[gi, v, h] = jnp.where(ok, acc, NEG_INF)


def _bias_prep(rpb, t5):
    smem = pl.BlockSpec(memory_space=pltpu.SMEM)
    return pl.pallas_call(
        _bias_prep_kernel,
        in_specs=[smem, smem, pl.BlockSpec(memory_space=pltpu.VMEM)],
        out_shape=[jax.ShapeDtypeStruct((2 * NA_WIN_ROWS - 2, NA_HEADS, GRID_W, 2 * GRID_W), jnp.float32),
                   jax.ShapeDtypeStruct((len(DIL_GROUPS), 3, DIL_HEADS_PER_GROUP, DIL_BLOCK, 2 * DIL_BLOCK),
                                        jnp.float32)],
        compiler_params=pltpu.CompilerParams(vmem_limit_bytes=VMEM_LIMIT),
        name="bias_prep",
    )(rpb.reshape(-1), t5.reshape(-1), jnp.asarray(_dil_bucket_table()))


def _dil_kernel(q_ref, kp_ref, kc_ref, kn_ref, vp_ref, vc_ref, vn_ref, bias_ref, o_ref, lse_ref, *, dil):
    lane = lax.broadcasted_iota(jnp.int32, (DIL_BLOCK, LANES), 1)
    low_half = lane < HEAD_DIM
    hb = DIL_BLOCK // 2

    span = pl.program_id(1)
    n_spans = pl.num_programs(1)
    nbs = kc_ref.shape[2] // DIL_BLOCK

    def window(prev_ref, cur_ref, next_ref, rho, blk, cols):
        lo = blk * DIL_BLOCK
        first = prev_ref[0, rho, hb:, cols] if blk == 0 else cur_ref[0, rho, lo - hb:lo, cols]
        last = (next_ref[0, rho, :hb, cols] if blk == nbs - 1
                else cur_ref[0, rho, lo + DIL_BLOCK:lo + DIL_BLOCK + hb, cols])
        return jnp.concatenate([first, cur_ref[0, rho, lo:lo + DIL_BLOCK, cols], last], axis=0)

    def unit(rho, blk):
        variant = 1
        if blk == nbs - 1:
            variant = jnp.where(span == n_spans - 1, 2, variant)
        if blk == 0:
            variant = jnp.where(span == 0, 0, variant)
        scores = []
        for h in range(DIL_HEADS_PER_GROUP):
            cols = slice((h // 2) * LANES, (h // 2 + 1) * LANES)
            qp = q_ref[0, rho, blk * DIL_BLOCK:(blk + 1) * DIL_BLOCK, cols]
            keep = low_half if h % 2 == 0 else jnp.logical_not(low_half)
            qm = jnp.where(keep, qp, jnp.zeros_like(qp))
            kw = window(kp_ref, kc_ref, kn_ref, rho, blk, cols)
            scores.append(_dot_nt(qm, kw) + bias_ref[variant, h])
        probs = []
        for s in scores:
            m = jnp.max(s, axis=-1, keepdims=True)
            e = jnp.exp(s - m)
            probs.append((e.astype(jnp.bfloat16), m, jnp.sum(e, axis=-1, keepdims=True)))
        outs, lses = [], []
        for h, (p, m, l) in enumerate(probs):
            cols = slice((h // 2) * LANES, (h // 2 + 1) * LANES)
            outs.append(_dot(p, window(vp_ref, vc_ref, vn_ref, rho, blk, cols)) / l)
            lses.append(jnp.broadcast_to(m + jnp.log(l), (DIL_BLOCK, LANES)))
        for hp in range(DIL_HEADS_PER_GROUP // 2):
            o_pair = jnp.where(low_half, outs[2 * hp], outs[2 * hp + 1])
            lse_pair = jnp.where(low_half, lses[2 * hp], lses[2 * hp + 1])
            if dil == 1:
                rows = slice(blk * DIL_BLOCK, (blk + 1) * DIL_BLOCK)
                o_ref[hp, rows, :] = o_pair
                lse_ref[hp, rows, :] = lse_pair
            else:
                rows = pl.ds(blk * DIL_BLOCK * dil + rho, DIL_BLOCK, stride=dil)
                o_ref[hp, rows, :] = o_pair
                lse_ref[hp, rows, :] = lse_pair

    def class_body(rho, carry):
        for blk in range(nbs):
            unit(rho, blk)
        return carry

    if dil == 1:
        class_body(0, 0)
    else:
        lax.fori_loop(0, dil, class_body, 0, unroll=2 if nbs == 1 else 1)


def _dil_attn(qkv, bias_tab, group, dil, batch, seq):
    L = seq // dil
    nb = L // DIL_BLOCK
    assert nb >= 2
    tq = DIL_STEP_TOKENS
    span = tq // dil
    nbs = span // DIL_BLOCK
    n_spans = L // span

    def cur(col):
        return pl.BlockSpec((1, dil, span, WIDTH_G), lambda b, n: (b, 0, n, col))

    def edge(col, shift):
        return pl.BlockSpec((1, dil, DIL_BLOCK, WIDTH_G),
                            lambda b, n: (b, 0, jnp.clip(n * nbs + (shift if shift < 0 else nbs), 0, nb - 1), col))

    n_pairs = WIDTH_G // LANES
    out_spec = pl.BlockSpec((n_pairs, tq, LANES), lambda b, n: (0, b * n_spans + n, 0))
    return pl.pallas_call(
        functools.partial(_dil_kernel, dil=dil),
        grid=(batch, n_spans),
        in_specs=[cur(0), edge(1, -1), cur(1), edge(1, 1), edge(2, -1), cur(2), edge(2, 1),
                  pl.BlockSpec((None, 3, DIL_HEADS_PER_GROUP, DIL_BLOCK, 2 * DIL_BLOCK),
                               lambda b, n: (group, 0, 0, 0, 0))],
        out_specs=[out_spec, out_spec],
        out_shape=[jax.ShapeDtypeStruct((n_pairs, batch * seq, LANES), jnp.float32)] * 2,
        compiler_params=_cparams(2),
        name=f"dil_attn_d{dil}",
    )(qkv, qkv, qkv, qkv, qkv, qkv, qkv, bias_tab)


def _mix_kernel(x_ref, na_ref, o0_ref, l0_ref, o1_ref, l1_ref, o2_ref, l2_ref, qc_ref, gate_ref,
                mkv_ref, wa_ref, wb_ref, wc_ref, wo_ref, fg_ref, rw_ref, rb_ref,
                x1_ref, h_ref, idx_ref, wt_ref):
    T = x_ref.shape[0]
    ob = []
    for hp in range(WIDTH_G // LANES):
        l0, l1, l2 = l0_ref[hp], l1_ref[hp], l2_ref[hp]
        m = jnp.maximum(jnp.maximum(l0, l1), l2)
        e0, e1, e2 = jnp.exp(l0 - m), jnp.exp(l1 - m), jnp.exp(l2 - m)
        ob.append((e0 * o0_ref[hp] + e1 * o1_ref[hp] + e2 * o2_ref[hp]) / (e0 + e1 + e2))
    ob = jnp.concatenate(ob, axis=-1)

    mem_scale = MEM_HEAD_DIM ** -0.5
    scores = []
    for h in range(MEM_HEADS):
        cols = slice(h * MEM_HEAD_DIM, (h + 1) * MEM_HEAD_DIM)
        scores.append(_dot_nt(qc_ref[:, cols], mkv_ref[:, cols]) * mem_scale)
    probs = []
    for s in scores:
        e = jnp.exp(s - jnp.max(s, axis=-1, keepdims=True))
        probs.append((e.astype(jnp.bfloat16), jnp.sum(e, axis=-1, keepdims=True)))
    oc = []
    for h, (p, l) in enumerate(probs):
        oc.append(_dot(p, mkv_ref[:, WIDTH_C + h * MEM_HEAD_DIM:WIDTH_C + (h + 1) * MEM_HEAD_DIM]) / l)
    oc = jnp.concatenate(oc, axis=-1).astype(jnp.bfloat16)

    y_a = _dot(na_ref[...], wa_ref[...])
    y_b = _dot(ob.astype(jnp.bfloat16), wb_ref[...])
    y_c = _dot(oc, wc_ref[...])
    merged = (gate_ref[:, 0:D_MODEL].astype(jnp.float32) * y_a
              + gate_ref[:, D_MODEL:2 * D_MODEL].astype(jnp.float32) * y_b
              + gate_ref[:, 2 * D_MODEL:3 * D_MODEL].astype(jnp.float32) * y_c)
    x1 = x_ref[...] + _dot(merged.astype(jnp.bfloat16), wo_ref[...])
    x1_ref[...] = x1

    h = _rms(x1, fg_ref[...])
    h_ref[...] = h.astype(jnp.bfloat16)
    logits = lax.dot_general(rw_ref[...], h, (((1,), (1,)), ((), ())), preferred_element_type=jnp.float32,
                             precision=lax.Precision.HIGHEST) + rb_ref[...]
    expert = lax.broadcasted_iota(jnp.int32, (N_EXPERTS, T), 0)
    vals, idxs = [], []
    for _ in range(TOP_K):
        mx = jnp.max(logits, axis=0, keepdims=True)
        sel = jnp.min(jnp.where(logits == mx, expert, N_EXPERTS), axis=0, keepdims=True)
        vals.append(mx)
        idxs.append(sel)
        logits = jnp.where(expert == sel, -jnp.inf, logits)
    ev = [jnp.exp(v - vals[0]) for v in vals]
    den = ev[0] + ev[1] + ev[2] + ev[3]
    idx_ref[...] = jnp.concatenate(idxs, axis=0)
    wt_ref[...] = jnp.concatenate([e / den for e in ev], axis=0)


def _mix_out(x2d, na_out, dil_outs, qc, gates, mkv, wa, wb, wc, wo, ffn_g, router_w, router_b, batch, seq):
    n = x2d.shape[0]
    T = PROJ_TILE
    tiles_per_batch = seq // T
    mem_len = mkv.shape[0] // batch
    row = lambda w: pl.BlockSpec((T, w), lambda i: (i, 0))
    full = lambda a: pl.BlockSpec(a.shape, lambda i: (0,) * a.ndim)
    in_specs = [row(D_MODEL), row(WIDTH_A)]
    args = [x2d, na_out]
    for o, l in dil_outs:
        pair_rows = pl.BlockSpec((WIDTH_G // LANES, T, LANES), lambda i: (0, i, 0))
        in_specs += [pair_rows, pair_rows]
        args += [o, l]
    in_specs += [row(WIDTH_C), row(3 * D_MODEL),
                 pl.BlockSpec((mem_len, 2 * WIDTH_C), lambda i: (i // tiles_per_batch, 0)),
                 full(wa), full(wb), full(wc), full(wo), full(ffn_g), full(router_w), full(router_b)]
    args += [qc, gates, mkv, wa, wb, wc, wo, ffn_g, router_w, router_b]
    return pl.pallas_call(
        _mix_kernel,
        grid=(n // T,),
        in_specs=in_specs,
        out_specs=[row(D_MODEL), row(D_MODEL), pl.BlockSpec((TOP_K, T), lambda i: (0, i)),
                   pl.BlockSpec((TOP_K, T), lambda i: (0, i))],
        out_shape=[jax.ShapeDtypeStruct((n, D_MODEL), jnp.float32),
                   jax.ShapeDtypeStruct((n, D_MODEL), jnp.bfloat16),
                   jax.ShapeDtypeStruct((TOP_K, n), jnp.int32),
                   jax.ShapeDtypeStruct((TOP_K, n), jnp.float32)],
        compiler_params=_cparams(1),
        name="mix_out",
    )(*args)


def _route_kernel(idx_ref, lpos_ref, ptab_ref, be_ref, pe_ref, counts_ref, carry_ref, pstart_ref):
    phase = pl.program_id(0)
    i = pl.program_id(1)
    T = idx_ref.shape[1]
    n_blocks = be_ref.shape[1]
    expert = lax.broadcasted_iota(jnp.int32, (N_EXPERTS, T), 0)
    idx = idx_ref[...]
    onehots = [expert == idx[k:k + 1, :] for k in range(TOP_K)]
    member = sum(o.astype(jnp.float32) for o in onehots)
    cnt = jnp.sum(member, axis=1, keepdims=True)
    seg = jnp.floor((cnt + (MOE_PIECE - 1)) * (1.0 / MOE_PIECE)) * MOE_PIECE
    r = lax.broadcasted_iota(jnp.int32, (N_EXPERTS, N_EXPERTS), 0)
    c = lax.broadcasted_iota(jnp.int32, (N_EXPERTS, N_EXPERTS), 1)
    lower_incl = (r >= c).astype(jnp.float32)

    def cumsum_experts(v):
        return jnp.dot(lower_incl, v, preferred_element_type=jnp.float32, precision=lax.Precision.HIGHEST)

    @pl.when(jnp.logical_and(phase == 0, i == 0))
    def _():
        counts_ref[...] = jnp.zeros_like(counts_ref)

    @pl.when(phase == 0)
    def _():
        counts_ref[...] += seg

    @pl.when(jnp.logical_and(phase == 1, i == 0))
    def _():
        counts = counts_ref[...]
        padded = jnp.floor((counts + (MOE_BLOCK - 1)) * (1.0 / MOE_BLOCK)) * MOE_BLOCK
        pad_ends = cumsum_experts(padded)
        pstart_ref[...] = pad_ends - padded
        carry_ref[...] = jnp.zeros_like(carry_ref)
        pe_ref[...] = pad_ends[:, 0:1].astype(jnp.int32)
        blk_start = (lax.broadcasted_iota(jnp.int32, (N_EXPERTS, n_blocks), 1) * MOE_BLOCK).astype(jnp.float32)
        be = jnp.sum((pad_ends[:, 0:1] <= blk_start).astype(jnp.int32), axis=0, keepdims=True)
        be_ref[...] = jnp.minimum(be, N_EXPERTS - 1)

    @pl.when(phase == 1)
    def _():
        seg_b = jnp.broadcast_to(seg, (N_EXPERTS, LANES))
        local_end = cumsum_experts(seg_b)[:, 0:1]
        local_start = local_end - seg
        global_start = pstart_ref[:, 0:1] + carry_ref[:, 0:1]
        rr = lax.broadcasted_iota(jnp.int32, (T, T), 0)
        cc = lax.broadcasted_iota(jnp.int32, (T, T), 1)
        earlier = (rr < cc).astype(jnp.bfloat16)
        before = _dot(member.astype(jnp.bfloat16), earlier)
        base = local_start + before
        lpos = [jnp.sum(jnp.where(o, base, 0.0), axis=0, keepdims=True) for o in onehots]
        lpos_ref[...] = jnp.concatenate(lpos, axis=0).astype(jnp.int32)

        inv = 1.0 / MOE_PIECE
        piece = lax.broadcasted_iota(jnp.int32, (N_EXPERTS, MOE_PIECE_TAB), 1).astype(jnp.float32)
        first = local_start * inv
        inside = jnp.logical_and(piece >= first, piece < local_end * inv)
        dst = jnp.sum(jnp.where(inside, global_start * inv + piece - first, 0.0), axis=0, keepdims=True)
        n_pieces = local_end[N_EXPERTS - 1:N_EXPERTS, :] * inv
        dst = jnp.where(piece[0:1, :] == MOE_PIECE_TAB - 1, n_pieces, dst)
        ptab_ref[...] = dst.astype(jnp.int32)
        carry_ref[...] += seg


def _moe_route(topk_idx, n_blocks):
    n = topk_idx.shape[1]
    T = MOE_TILE
    acc = pltpu.VMEM((N_EXPERTS, LANES), jnp.float32)
    return pl.pallas_call(
        _route_kernel,
        grid=(2, n // T),
        in_specs=[pl.BlockSpec((TOP_K, T), lambda p, i: (0, i))],
        out_specs=[pl.BlockSpec((TOP_K, T), lambda p, i: (0, p * i)),
                   pl.BlockSpec((None, 1, MOE_PIECE_TAB), lambda p, i: (p * i, 0, 0)),
                   pl.BlockSpec((1, n_blocks), lambda p, i: (0, 0)),
                   pl.BlockSpec((N_EXPERTS, 1), lambda p, i: (0, 0))],
        out_shape=[jax.ShapeDtypeStruct((TOP_K, n), jnp.int32),
                   jax.ShapeDtypeStruct((n // T, 1, MOE_PIECE_TAB), jnp.int32),
                   jax.ShapeDtypeStruct((1, n_blocks), jnp.int32),
                   jax.ShapeDtypeStruct((N_EXPERTS, 1), jnp.int32)],
        scratch_shapes=[acc, acc, acc],
        compiler_params=_cparams(2),
        name="moe_route",
    )(topk_idx)


def _piece(ref, p):
    return ref.at[pl.ds(pl.multiple_of(p * MOE_PIECE, MOE_PIECE), MOE_PIECE), :]


def _dispatch_kernel(pe_ref, ptab_ref, lpos_ref, h_ref, xs_hbm, zbuf, pbuf, n_started, sem, zsem):
    i = pl.program_id(0)
    T = h_ref.shape[0]
    half = h_ref.shape[1] // 2

    def zero_copy(e):
        end = pe_ref[e]
        start = pl.multiple_of(end - MOE_BLOCK, MOE_BLOCK)
        return pltpu.make_async_copy(zbuf, xs_hbm.at[pl.ds(start, MOE_BLOCK), :], zsem)

    def nonempty(e):
        return pe_ref[e] > (pe_ref[e - 1] if e else 0)

    @pl.when(i == 0)
    def _():
        zbuf[...] = jnp.zeros_like(zbuf)
        for e in range(N_EXPERTS):
            @pl.when(nonempty(e))
            def _():
                zero_copy(e).start()
        for e in range(N_EXPERTS):
            @pl.when(nonempty(e))
            def _():
                zero_copy(e).wait()

        def tail_copy(b):
            return pltpu.make_async_copy(
                zbuf, xs_hbm.at[pl.ds(pl.multiple_of(b * MOE_BLOCK, MOE_BLOCK), MOE_BLOCK), :], zsem)
        first_tail = pe_ref[N_EXPERTS - 1] // MOE_BLOCK
        n_blocks = xs_hbm.shape[0] // MOE_BLOCK
        lax.fori_loop(first_tail, n_blocks, lambda b, c: (tail_copy(b).start(), c)[1], 0)
        lax.fori_loop(first_tail, n_blocks, lambda b, c: (tail_copy(b).wait(), c)[1], 0)

    row = lax.broadcasted_iota(jnp.int32, (MOE_TILE_ROWS, T), 0)
    lpos = lpos_ref[...]
    perm = sum((row == lpos[k:k + 1, :]).astype(jnp.bfloat16) for k in range(TOP_K))
    lo = lax.bitcast_convert_type(_dot(perm, h_ref[:, :half]), jnp.uint32)
    hi = lax.bitcast_convert_type(_dot(perm, h_ref[:, half:]), jnp.uint32)
    slot = i % 2
    pbuf[slot] = hi | (lo >> 16)

    n_pieces = ptab_ref[MOE_PIECE_TAB - 1]
    n_started[slot] = n_pieces

    def piece_copy(s, p):
        return pltpu.make_async_copy(_piece(pbuf.at[s], p), _piece(xs_hbm, ptab_ref[p]), sem.at[s])

    def wait_all(s):
        lax.fori_loop(0, n_started[s], lambda p, c: (piece_copy(s, 0).wait(), c)[1], 0)

    lax.fori_loop(0, n_pieces, lambda p, c: (piece_copy(slot, p).start(), c)[1], 0)

    @pl.when(i > 0)
    def _():
        wait_all(1 - slot)

    @pl.when(i == pl.num_programs(0) - 1)
    def _():
        wait_all(slot)


def _moe_dispatch(h, lpos, ptab, pad_ends, n_blocks):
    n, width = h.shape
    T = MOE_TILE
    n_steps = n // T

    def kernel(pe_ref, ptab_ref, *rest):
        _dispatch_kernel(pe_ref, ptab_ref.at[0], *rest)

    grid_spec = pltpu.PrefetchScalarGridSpec(
        num_scalar_prefetch=1,
        grid=(n_steps,),
        in_specs=[pl.BlockSpec((None, 1, MOE_PIECE_TAB), lambda i, pe: (i, 0, 0), memory_space=pltpu.SMEM),
                  pl.BlockSpec((TOP_K, T), lambda i, pe: (0, i)),
                  pl.BlockSpec((T, width), lambda i, pe: (i, 0))],
        out_specs=pl.BlockSpec(memory_space=pl.ANY),
        scratch_shapes=[pltpu.VMEM((MOE_BLOCK, width // 2), jnp.uint32),
                        pltpu.VMEM((2, MOE_TILE_ROWS, width // 2), jnp.uint32),
                        pltpu.SMEM((2,), jnp.int32),
                        pltpu.SemaphoreType.DMA((2,)),
                        pltpu.SemaphoreType.DMA(())],
    )
    return pl.pallas_call(
        kernel,
        grid_spec=grid_spec,
        out_shape=jax.ShapeDtypeStruct((n_blocks * MOE_BLOCK, width // 2), jnp.uint32),
        compiler_params=_cparams(1),
        name="moe_dispatch",
    )(pad_ends, ptab, lpos, h)


def _moe_kernel(be_ref, pe_ref, xs_ref, wgu_ref, bgu_ref, wdn_ref, bdn_ref, y_ref, wgu_bf, wdn_bf):
    i = pl.program_id(0)
    n_valid = pe_ref[N_EXPERTS - 1] // MOE_BLOCK
    new_expert = jnp.logical_or(i == 0, be_ref[i] != be_ref[jnp.maximum(i - 1, 0)])

    @pl.when(jnp.logical_and(i < n_valid, new_expert))
    def _():
        wgu_bf[...] = wgu_ref[0].astype(jnp.bfloat16)
        wdn_bf[...] = wdn_ref[0].astype(jnp.bfloat16)

    @pl.when(i < n_valid)
    def _():
        x = _unpack_bf16_pairs(xs_ref[...])
        gu = _dot(x, wgu_bf[...]) + bgu_ref[0]
        g = jnp.minimum(gu[:, :D_FF], SWIGLU_LIMIT)
        u = jnp.clip(gu[:, D_FF:], -SWIGLU_LIMIT, SWIGLU_LIMIT)
        a = (u + 1.0) * (g * _sigmoid(SWIGLU_ALPHA * g))
        y_ref[...] = _pack_bf16_pairs(_dot(a.astype(jnp.bfloat16), wdn_bf[...]) + bdn_ref[0])

    @pl.when(i >= n_valid)
    def _():
        y_ref[...] = jnp.zeros_like(y_ref)


def _moe_experts(xs, block_expert, pad_ends, wgu, bgu, wdn, bdn):
    n_blocks = block_expert.shape[0]

    def row_blk(i, be, pe):
        return (jnp.minimum(i, jnp.maximum(pe[N_EXPERTS - 1] // MOE_BLOCK - 1, 0)), 0)

    grid_spec = pltpu.PrefetchScalarGridSpec(
        num_scalar_prefetch=2,
        grid=(n_blocks,),
        in_specs=[pl.BlockSpec((MOE_BLOCK, D_MODEL // 2), row_blk),
                  pl.BlockSpec((1, D_MODEL, 2 * D_FF), lambda i, be, pe: (be[i], 0, 0)),
                  pl.BlockSpec((1, 1, 2 * D_FF), lambda i, be, pe: (be[i], 0, 0)),
                  pl.BlockSpec((1, D_FF, D_MODEL), lambda i, be, pe: (be[i], 0, 0)),
                  pl.BlockSpec((1, 1, D_MODEL), lambda i, be, pe: (be[i], 0, 0))],
        out_specs=pl.BlockSpec((MOE_BLOCK, D_MODEL // 2), lambda i, be, pe: (i, 0)),
        scratch_shapes=[pltpu.VMEM((D_MODEL, 2 * D_FF), jnp.bfloat16),
                        pltpu.VMEM((D_FF, D_MODEL), jnp.bfloat16)],
    )
    return pl.pallas_call(
        _moe_kernel,
        grid_spec=grid_spec,
        out_shape=jax.ShapeDtypeStruct((n_blocks * MOE_BLOCK, D_MODEL // 2), jnp.uint32),
        compiler_params=_cparams(1),
        name="moe_experts",
    )(block_expert, pad_ends, xs, wgu, bgu, wdn, bdn)


def _gather_pieces(src_hbm, ptab_ref, dst_ref, sem):
    def body(p, carry):
        pltpu.make_async_copy(_piece(src_hbm, ptab_ref[p]), _piece(dst_ref, p), sem).start()
        return carry
    lax.fori_loop(0, MOE_TILE_ROWS // MOE_PIECE, body, 0, unroll=8)


def _wait_pieces(src_hbm, dst_ref, sem):
    def body(p, carry):
        pltpu.make_async_copy(_piece(src_hbm, 0), _piece(dst_ref, 0), sem).wait()
        return carry
    lax.fori_loop(0, MOE_TILE_ROWS // MOE_PIECE, body, 0, unroll=8)


def _combine_kernel(ptab_ref, ptab_next_ref, ys_hbm, lpos_ref, wt_ref, x1_ref, g_ref, out_ref, gbuf, sem):
    i = pl.program_id(0)
    n_steps = pl.num_programs(0)
    slot = i % 2
    T = x1_ref.shape[0]

    @pl.when(i == 0)
    def _():
        _gather_pieces(ys_hbm, ptab_ref, gbuf.at[0], sem.at[0])

    @pl.when(i + 1 < n_steps)
    def _():
        _gather_pieces(ys_hbm, ptab_next_ref, gbuf.at[1 - slot], sem.at[1 - slot])

    _wait_pieces(ys_hbm, gbuf.at[slot], sem.at[slot])
    rows = _unpack_bf16_pairs(gbuf[slot])
    col = lax.broadcasted_iota(jnp.int32, (T, MOE_TILE_ROWS), 1)
    lpos = lpos_ref[...]
    wt = wt_ref[...]
    cmb = jnp.zeros((T, MOE_TILE_ROWS), jnp.float32)
    for k in range(TOP_K):
        cmb = jnp.where(col == lpos[:, k:k + 1], wt[:, k:k + 1], cmb)
    hi = cmb.astype(jnp.bfloat16)
    lo = (cmb - hi.astype(jnp.float32)).astype(jnp.bfloat16)
    y2 = _dot(jnp.concatenate([hi, lo], axis=0), rows)
    y = x1_ref[...] + y2[:T] + y2[T:]
    out_ref[...] = _rms(y, g_ref[...])


def _moe_combine(ys, ptab, lpos_t, wts, x1, final_g):
    n = x1.shape[0]
    T = MOE_TILE
    n_steps = n // T
    smem_blk = lambda shift: pl.BlockSpec(
        (None, 1, MOE_PIECE_TAB), lambda i: (jnp.minimum(i + shift, n_steps - 1), 0, 0),
        memory_space=pltpu.SMEM)

    def kernel(ptab_ref, ptab_next_ref, *rest):
        _combine_kernel(ptab_ref.at[0], ptab_next_ref.at[0], *rest)

    return pl.pallas_call(
        kernel,
        grid=(n_steps,),
        in_specs=[smem_blk(0), smem_blk(1),
                  pl.BlockSpec(memory_space=pl.ANY),
                  pl.BlockSpec((T, TOP_K), lambda i: (i, 0)),
                  pl.BlockSpec((T, TOP_K), lambda i: (i, 0)),
                  pl.BlockSpec((T, D_MODEL), lambda i: (i, 0)),
                  pl.BlockSpec((1, D_MODEL), lambda i: (0, 0))],
        out_specs=pl.BlockSpec((T, D_MODEL), lambda i: (i, 0)),
        out_shape=jax.ShapeDtypeStruct((n, D_MODEL), jnp.float32),
        scratch_shapes=[pltpu.VMEM((2, MOE_TILE_ROWS, D_MODEL // 2), jnp.uint32),
                        pltpu.SemaphoreType.DMA((2,))],
        compiler_params=_cparams(1),
        name="moe_combine",
    )(ptab, ptab, ys, lpos_t, wts, x1, final_g)


def kernel(x, mem, attn_norm_g, mem_norm_g, w_in, w_mem_kv, na_rpb, t5_rel_bias, w_branch_a, w_branch_b,
           w_branch_c, w_out, ffn_norm_g, router_w, router_b, expert_w_gate_up, expert_b_gate_up,
           expert_w_down, expert_b_down, final_norm_g):
    B, S, D = x.shape
    depth = w_in.shape[0]
    bf = jnp.bfloat16
    x2d = x.reshape(B * S, D)
    mem2d = mem.reshape(B * mem.shape[1], D)
    assert depth == 1, "single-layer block (per-layer parameters carry a leading axis of 1)"
    l = 0
    na_bias, dil_bias = _bias_prep(na_rpb[l], t5_rel_bias)
    mkv = _mem_kv(mem2d, mem_norm_g[l][None], w_mem_kv[l].astype(bf))
    na_qkv, d0, d1, d2, qc = _in_proj(x2d, attn_norm_g[l][None], w_in[l][:, :OFF_GATE].astype(bf), B, S)
    gates = _gate_proj(x2d, attn_norm_g[l][None], w_in[l][:, OFF_GATE:].astype(bf))
    na_out = _na_attn(na_qkv, na_bias, B, S)
    dil_outs = [_dil_attn(qkv, dil_bias, gi, dil, B, S)
                for gi, (qkv, (_, dil)) in enumerate(zip((d0, d1, d2), DIL_GROUPS))]
    x1, h, topk_idx, topk_w = _mix_out(
        x2d, na_out, dil_outs, qc, gates, mkv, w_branch_a[l].astype(bf), w_branch_b[l].astype(bf),
        w_branch_c[l].astype(bf), w_out[l].astype(bf), ffn_norm_g[l][None], router_w[l].T,
        router_b[l][:, None], B, S)

    n = B * S
    max_rows = n * TOP_K + (n // MOE_TILE) * N_EXPERTS * (MOE_PIECE - 1)
    n_blocks = -(-max_rows // MOE_BLOCK) + N_EXPERTS
    lpos, ptab, block_expert, pad_ends = _moe_route(topk_idx, n_blocks)
    block_expert = block_expert.reshape(n_blocks)
    pad_ends = pad_ends.reshape(N_EXPERTS)
    xs = _moe_dispatch(h, lpos, ptab, pad_ends, n_blocks)
    ys = _moe_experts(xs, block_expert, pad_ends, expert_w_gate_up[l], expert_b_gate_up[l][:, None, :],
                      expert_w_down[l], expert_b_down[l][:, None, :])
    out = _moe_combine(ys, ptab, lpos.T, topk_w.T, x1, final_norm_g[None])
    return out.reshape(B, S, D)
```

```python
import functools
import math

import numpy as np
import jax
import jax.numpy as jnp
from jax import lax
from jax.experimental import pallas as pl
from jax.experimental.pallas import tpu as pltpu

D_MODEL = 1024
GRID_W = 64
MEM_HEADS = 4
MEM_HEAD_DIM = 128
HEAD_DIM = 64
NA_HEADS = 8
NA_WIN_ROWS = 8
NA_WIN_COLS = 16
DIL_GROUPS = ((128, 1), (512, 4), (2048, 16))
DIL_HEADS_PER_GROUP = 4
DIL_BLOCK = 128
T5_BUCKETS = 32
T5_MAX_DIST = 1024
N_EXPERTS = 32
TOP_K = 4
D_FF = D_MODEL
SWIGLU_ALPHA = 1.702
SWIGLU_LIMIT = 7.0
RMS_EPS = 1e-6
NEG_INF = -1e30

WIDTH_A = NA_HEADS * HEAD_DIM
WIDTH_B = DIL_HEADS_PER_GROUP * len(DIL_GROUPS) * HEAD_DIM
WIDTH_G = DIL_HEADS_PER_GROUP * HEAD_DIM
WIDTH_C = MEM_HEADS * MEM_HEAD_DIM
OFF_QA, OFF_KA, OFF_VA = 0, WIDTH_A, 2 * WIDTH_A
OFF_QB = 3 * WIDTH_A
OFF_KB = OFF_QB + WIDTH_B
OFF_VB = OFF_KB + WIDTH_B
OFF_QC = OFF_VB + WIDTH_B
OFF_GATE = OFF_QC + WIDTH_C
IN_COLS = OFF_GATE + 3 * D_MODEL

LANES = 128
VMEM_LIMIT = 56 * 1024 * 1024

IN_PROJ_TILE = 1024
PROJ_TILE = 512
NA_ROWS_PER_STEP = 8
DIL_STEP_TOKENS = 2048
MOE_BLOCK = 512
MOE_TILE = 256
MOE_PIECE = 8
MOE_TILE_ROWS = -(-(TOP_K * MOE_TILE + N_EXPERTS * (MOE_PIECE - 1)) // 256) * 256
MOE_PIECE_TAB = 256
assert MOE_TILE_ROWS // MOE_PIECE < MOE_PIECE_TAB
MOE_SPARE_BLOCKS = -(-2 * MOE_TILE_ROWS // MOE_BLOCK)


def _cparams(n_axes):
    return pltpu.CompilerParams(
        dimension_semantics=("arbitrary",) * n_axes, vmem_limit_bytes=VMEM_LIMIT)


def _rms(x, g):
    return x * lax.rsqrt(jnp.mean(x * x, axis=-1, keepdims=True) + RMS_EPS) * g


def _sigmoid(x):
    return 1.0 / (1.0 + jnp.exp(-x))


def _pack_bf16_pairs(x):
    w = x.shape[1] // 2
    bits = lax.bitcast_convert_type(x.astype(jnp.bfloat16).astype(jnp.float32), jnp.uint32)
    return (bits[:, w:] & jnp.uint32(0xFFFF0000)) | (bits[:, :w] >> 16)


def _unpack_bf16_pairs(p):
    lo = lax.bitcast_convert_type(p << 16, jnp.float32)
    hi = lax.bitcast_convert_type(p & jnp.uint32(0xFFFF0000), jnp.float32)
    return jnp.concatenate([lo, hi], axis=1).astype(jnp.bfloat16)


def _dot(a, b):
    return jnp.dot(a, b, preferred_element_type=jnp.float32)


def _dot_nt(a, b):
    return lax.dot_general(a, b, (((1,), (1,)), ((), ())), preferred_element_type=jnp.float32)


def _mem_kv_kernel(mem_ref, g_ref, w_ref, out_ref):
    h = _rms(mem_ref[...], g_ref[...]).astype(jnp.bfloat16)
    out_ref[...] = _dot(h, w_ref[...]).astype(jnp.bfloat16)


def _mem_kv(mem2d, g, w):
    rows = mem2d.shape[0]
    return pl.pallas_call(
        _mem_kv_kernel,
        out_shape=jax.ShapeDtypeStruct((rows, 2 * WIDTH_C), jnp.bfloat16),
        name="mem_kv",
        compiler_params=pltpu.CompilerParams(vmem_limit_bytes=VMEM_LIMIT),
    )(mem2d, g, w)


def _in_proj_kernel(x_ref, g_ref, w_ref, na_ref, d0_ref, d1_ref, d2_ref, qc_ref, acc_ref):
    T = x_ref.shape[0]
    h = _rms(x_ref[...], g_ref[...]).astype(jnp.bfloat16)
    q_scale = HEAD_DIM ** -0.5

    for c in range(3):
        acc = _dot(h, w_ref[:, c * WIDTH_A:(c + 1) * WIDTH_A])
        if c == 0:
            acc = acc * q_scale
        na_ref[:, c * WIDTH_A:(c + 1) * WIDTH_A] = acc.astype(jnp.bfloat16)

    for gi, ((_, dil), d_ref) in enumerate(zip(DIL_GROUPS, (d0_ref, d1_ref, d2_ref))):
        for kind, off in enumerate((OFF_QB, OFF_KB, OFF_VB)):
            lo = off + gi * WIDTH_G
            acc = _dot(h, w_ref[:, lo:lo + WIDTH_G])
            if kind == 0:
                acc = acc * q_scale
            cols = slice(kind * WIDTH_G, (kind + 1) * WIDTH_G)
            if dil == 1:
                d_ref[0, 0, :, cols] = acc.astype(jnp.bfloat16)
            else:
                for c in range(WIDTH_G // LANES):
                    acc_ref[c] = acc[:, c * LANES:(c + 1) * LANES]
                for rho in range(dil):
                    d_ref[0, rho, :, cols] = jnp.concatenate(
                        [acc_ref[c, pl.ds(rho, T // dil, stride=dil), :] for c in range(WIDTH_G // LANES)],
                        axis=-1).astype(jnp.bfloat16)

    qc_ref[...] = _dot(h, w_ref[:, OFF_QC:OFF_QC + WIDTH_C]).astype(jnp.bfloat16)


def _gate_proj_kernel(x_ref, g_ref, w_ref, gate_ref):
    h = _rms(x_ref[...], g_ref[...]).astype(jnp.bfloat16)
    chunk = 512
    for c in range(3 * D_MODEL // chunk):
        gate_ref[:, c * chunk:(c + 1) * chunk] = _sigmoid(
            _dot(h, w_ref[:, c * chunk:(c + 1) * chunk])).astype(jnp.bfloat16)


def _gate_proj(x2d, g, w_gate_bf16):
    n = x2d.shape[0]
    T = IN_PROJ_TILE
    return pl.pallas_call(
        _gate_proj_kernel,
        grid=(n // T,),
        in_specs=[pl.BlockSpec((T, D_MODEL), lambda i: (i, 0)),
                  pl.BlockSpec((1, D_MODEL), lambda i: (0, 0)),
                  pl.BlockSpec((D_MODEL, 3 * D_MODEL), lambda i: (0, 0))],
        out_specs=pl.BlockSpec((T, 3 * D_MODEL), lambda i: (i, 0)),
        out_shape=jax.ShapeDtypeStruct((n, 3 * D_MODEL), jnp.bfloat16),
        compiler_params=_cparams(1),
        name="gate_proj",
    )(x2d, g, w_gate_bf16)


def _in_proj(x2d, g, w_bf16, batch, seq):
    n = x2d.shape[0]
    T = IN_PROJ_TILE
    tiles_per_batch = seq // T
    out_shape = [jax.ShapeDtypeStruct((n, 3 * WIDTH_A), jnp.bfloat16)]
    out_specs = [pl.BlockSpec((T, 3 * WIDTH_A), lambda i: (i, 0))]
    for _, dil in DIL_GROUPS:
        out_shape.append(jax.ShapeDtypeStruct((batch, dil, seq // dil, 3 * WIDTH_G), jnp.bfloat16))
        out_specs.append(pl.BlockSpec((1, dil, T // dil, 3 * WIDTH_G),
                                      lambda i: (i // tiles_per_batch, 0, i % tiles_per_batch, 0)))
    out_shape += [jax.ShapeDtypeStruct((n, WIDTH_C), jnp.bfloat16)]
    out_specs += [pl.BlockSpec((T, WIDTH_C), lambda i: (i, 0))]
    return pl.pallas_call(
        _in_proj_kernel,
        grid=(n // T,),
        in_specs=[pl.BlockSpec((T, D_MODEL), lambda i: (i, 0)),
                  pl.BlockSpec((1, D_MODEL), lambda i: (0, 0)),
                  pl.BlockSpec((D_MODEL, OFF_GATE), lambda i: (0, 0))],
        out_specs=out_specs,
        out_shape=out_shape,
        scratch_shapes=[pltpu.VMEM((WIDTH_G // LANES, T, LANES), jnp.float32)],
        compiler_params=_cparams(1),
        name="in_proj",
    )(x2d, g, w_bf16)


def _na_kernel(q_ref, k_ref, v_ref, bias_ref, out_ref, *, rows):
    i = pl.program_id(1)
    lane = lax.broadcasted_iota(jnp.int32, (GRID_W, LANES), 1)
    low_half = lane < HEAD_DIM
    n_win = NA_WIN_ROWS * GRID_W

    def row_body(j, carry):
        qr = i * NA_ROWS_PER_STEP + j
        rs = jnp.clip(qr - NA_WIN_ROWS // 2, 0, rows - NA_WIN_ROWS)
        shift = rs - qr + NA_WIN_ROWS - 1
        q_off = pl.multiple_of(j * GRID_W, GRID_W)
        k_off = pl.multiple_of(rs * GRID_W, GRID_W)
        scores = []
        for h in range(NA_HEADS):
            cols = slice((h // 2) * LANES, (h // 2 + 1) * LANES)
            qp = q_ref[pl.ds(q_off, GRID_W), cols]
            kp = k_ref[pl.ds(k_off, n_win), cols]
            keep = low_half if h % 2 == 0 else jnp.logical_not(low_half)
            qm = jnp.where(keep, qp, jnp.zeros_like(qp))
            bias = jnp.concatenate(
                [bias_ref[shift + 2 * p, h] for p in range(NA_WIN_ROWS // 2)], axis=-1)
            scores.append(_dot_nt(qm, kp) + bias)
        probs = []
        for s in scores:
            e = jnp.exp(s - jnp.max(s, axis=-1, keepdims=True))
            probs.append((e.astype(jnp.bfloat16), jnp.sum(e, axis=-1, keepdims=True)))
        outs = []
        for h, (p, l) in enumerate(probs):
            cols = slice((h // 2) * LANES, (h // 2 + 1) * LANES)
            vp = v_ref[pl.ds(k_off, n_win), cols]
            outs.append(_dot(p, vp) / l)
        for hp in range(NA_HEADS // 2):
            cols = slice(hp * LANES, (hp + 1) * LANES)
            out_ref[pl.ds(q_off, GRID_W), cols] = jnp.where(
                low_half, outs[2 * hp], outs[2 * hp + 1]).astype(jnp.bfloat16)
        return carry

    lax.fori_loop(0, NA_ROWS_PER_STEP, row_body, 0, unroll=2)


def _na_attn(na_qkv, bias_tab, batch, seq):
    rows = seq // GRID_W
    steps = rows // NA_ROWS_PER_STEP
    tq = NA_ROWS_PER_STEP * GRID_W
    n = na_qkv.shape[0]
    return pl.pallas_call(
        functools.partial(_na_kernel, rows=rows),
        grid=(batch, steps),
        in_specs=[pl.BlockSpec((tq, WIDTH_A), lambda b, i: (b * steps + i, 0)),
                  pl.BlockSpec((seq, WIDTH_A), lambda b, i: (b, 1)),
                  pl.BlockSpec((seq, WIDTH_A), lambda b, i: (b, 2)),
                  pl.BlockSpec(bias_tab.shape, lambda b, i: (0, 0, 0, 0))],
        out_specs=pl.BlockSpec((tq, WIDTH_A), lambda b, i: (b * steps + i, 0)),
        out_shape=jax.ShapeDtypeStruct((n, WIDTH_A), jnp.bfloat16),
        compiler_params=_cparams(2),
        name="na_attn",
    )(na_qkv, na_qkv, na_qkv, bias_tab)


def _t5_bucket_np(rel):
    half = T5_BUCKETS // 2
    max_exact = half // 2
    ret = np.where(rel > 0, half, 0)
    n = np.abs(rel)
    nf = np.maximum(n, 1).astype(np.float32)
    large = max_exact + (np.log(nf / np.float32(max_exact)) / np.float32(math.log(T5_MAX_DIST / max_exact))
                         * np.float32(half - max_exact)).astype(np.int32)
    large = np.minimum(large, half - 1)
    return ret + np.where(n < max_exact, n, large)


def _dil_bucket_table():
    q = np.arange(DIL_BLOCK)[:, None]
    j = np.arange(2 * DIL_BLOCK)[None, :]
    tabs = []
    for window, dil in DIL_GROUPS:
        half = (window // 2) // dil
        assert half == DIL_BLOCK // 2
        rel = (j - half) - q
        tabs.append(np.where(np.abs(rel) <= half, _t5_bucket_np(rel * dil), -1))
    return np.stack(tabs).astype(np.int32)


def _bias_prep_kernel(rpb_ref, t5_ref, bucket_ref, na_ref, dil_ref):
    n_dc = 2 * NA_WIN_COLS - 1
    n_dr = 2 * NA_WIN_ROWS - 1
    qc = lax.broadcasted_iota(jnp.int32, (GRID_W, 2 * GRID_W), 0)
    lane = lax.broadcasted_iota(jnp.int32, (GRID_W, 2 * GRID_W), 1)
    second = lane >= GRID_W
    kc = jnp.where(second, lane - GRID_W, lane)
    diff = kc - qc
    cs = jnp.clip(qc - NA_WIN_COLS // 2, 0, GRID_W - NA_WIN_COLS)
    valid = jnp.logical_and(kc >= cs, kc < cs + NA_WIN_COLS)

    def na_tile(mh, carry):
        m = mh // NA_HEADS
        h = mh - m * NA_HEADS
        base = (h * n_dr + m) * n_dc
        acc = jnp.zeros((GRID_W, 2 * GRID_W), jnp.float32)
        for d in range(n_dc):
            v = jnp.where(second, rpb_ref[base + n_dc + d], rpb_ref[base + d])
            acc = jnp.where(diff == d - (NA_WIN_COLS - 1), v, acc)
        na_ref[m, h] = jnp.where(valid, acc, NEG_INF)
        return carry

    lax.fori_loop(0, (n_dr - 1) * NA_HEADS, na_tile, 0)

    j = lax.broadcasted_iota(jnp.int32, (DIL_BLOCK, 2 * DIL_BLOCK), 1)
    half = DIL_BLOCK // 2
    n_heads = DIL_HEADS_PER_GROUP * len(DIL_GROUPS)
    for gi in range(len(DIL_GROUPS)):
        bucket = bucket_ref[gi]
        in_win = bucket >= 0
        for h in range(DIL_HEADS_PER_GROUP):
            acc = jnp.zeros((DIL_BLOCK, 2 * DIL_BLOCK), jnp.float32)
            for b in range(T5_BUCKETS):
                acc = jnp.where(bucket == b, t5_ref[b * n_heads + gi * DIL_HEADS_PER_GROUP + h], acc)
            for v, ok in enumerate((jnp.logical_and(in_win, j >= half), in_win,
                                    jnp.logical_and(in_win, j < 2 * DIL_BLOCK - half))):
                dil_ref[gi, v, h] = jnp.where(ok, acc, NEG_INF)


def _bias_prep(rpb, t5):
    smem = pl.BlockSpec(memory_space=pltpu.SMEM)
    return pl.pallas_call(
        _bias_prep_kernel,
        in_specs=[smem, smem, pl.BlockSpec(memory_space=pltpu.VMEM)],
        out_shape=[jax.ShapeDtypeStruct((2 * NA_WIN_ROWS - 2, NA_HEADS, GRID_W, 2 * GRID_W), jnp.float32),
                   jax.ShapeDtypeStruct((len(DIL_GROUPS), 3, DIL_HEADS_PER_GROUP, DIL_BLOCK, 2 * DIL_BLOCK),
                                        jnp.float32)],
        compiler_params=pltpu.CompilerParams(vmem_limit_bytes=VMEM_LIMIT),
        name="bias_prep",
    )(rpb.reshape(-1), t5.reshape(-1), jnp.asarray(_dil_bucket_table()))


def _dil_kernel(q_ref, kp_ref, kc_ref, kn_ref, vp_ref, vc_ref, vn_ref, bias_ref, o_ref, lse_ref, *, dil):
    lane = lax.broadcasted_iota(jnp.int32, (DIL_BLOCK, LANES), 1)
    low_half = lane < HEAD_DIM
    hb = DIL_BLOCK // 2

    span = pl.program_id(1)
    n_spans = pl.num_programs(1)
    nbs = kc_ref.shape[2] // DIL_BLOCK

    def window(prev_ref, cur_ref, next_ref, rho, blk, cols):
        lo = blk * DIL_BLOCK
        first = prev_ref[0, rho, hb:, cols] if blk == 0 else cur_ref[0, rho, lo - hb:lo, cols]
        last = (next_ref[0, rho, :hb, cols] if blk == nbs - 1
                else cur_ref[0, rho, lo + DIL_BLOCK:lo + DIL_BLOCK + hb, cols])
        return jnp.concatenate([first, cur_ref[0, rho, lo:lo + DIL_BLOCK, cols], last], axis=0)

    def unit(rho, blk):
        variant = 1
        if blk == nbs - 1:
            variant = jnp.where(span == n_spans - 1, 2, variant)
        if blk == 0:
            variant = jnp.where(span == 0, 0, variant)
        scores = []
        for h in range(DIL_HEADS_PER_GROUP):
            cols = slice((h // 2) * LANES, (h // 2 + 1) * LANES)
            qp = q_ref[0, rho, blk * DIL_BLOCK:(blk + 1) * DIL_BLOCK, cols]
            keep = low_half if h % 2 == 0 else jnp.logical_not(low_half)
            qm = jnp.where(keep, qp, jnp.zeros_like(qp))
            kw = window(kp_ref, kc_ref, kn_ref, rho, blk, cols)
            scores.append(_dot_nt(qm, kw) + bias_ref[variant, h])
        probs = []
        for s in scores:
            m = jnp.max(s, axis=-1, keepdims=True)
            e = jnp.exp(s - m)
            probs.append((e.astype(jnp.bfloat16), m, jnp.sum(e, axis=-1, keepdims=True)))
        outs, lses = [], []
        for h, (p, m, l) in enumerate(probs):
            cols = slice((h // 2) * LANES, (h // 2 + 1) * LANES)
            outs.append(_dot(p, window(vp_ref, vc_ref, vn_ref, rho, blk, cols)) / l)
            lses.append(jnp.broadcast_to(m + jnp.log(l), (DIL_BLOCK, LANES)))
        for hp in range(DIL_HEADS_PER_GROUP // 2):
            o_pair = jnp.where(low_half, outs[2 * hp], outs[2 * hp + 1])
            lse_pair = jnp.where(low_half, lses[2 * hp], lses[2 * hp + 1])
            if dil == 1:
                rows = slice(blk * DIL_BLOCK, (blk + 1) * DIL_BLOCK)
                o_ref[hp, rows, :] = o_pair
                lse_ref[hp, rows, :] = lse_pair
            else:
                rows = pl.ds(blk * DIL_BLOCK * dil + rho, DIL_BLOCK, stride=dil)
                o_ref[hp, rows, :] = o_pair
                lse_ref[hp, rows, :] = lse_pair

    def class_body(rho, carry):
        for blk in range(nbs):
            unit(rho, blk)
        return carry

    if dil == 1:
        class_body(0, 0)
    else:
        lax.fori_loop(0, dil, class_body, 0, unroll=2 if nbs == 1 else 1)


def _dil_attn(qkv, bias_tab, group, dil, batch, seq):
    L = seq // dil
    nb = L // DIL_BLOCK
    assert nb >= 2
    tq = DIL_STEP_TOKENS
    span = tq // dil
    nbs = span // DIL_BLOCK
    n_spans = L // span

    def cur(col):
        return pl.BlockSpec((1, dil, span, WIDTH_G), lambda b, n: (b, 0, n, col))

    def edge(col, shift):
        return pl.BlockSpec((1, dil, DIL_BLOCK, WIDTH_G),
                            lambda b, n: (b, 0, jnp.clip(n * nbs + (shift if shift < 0 else nbs), 0, nb - 1), col))

    n_pairs = WIDTH_G // LANES
    out_spec = pl.BlockSpec((n_pairs, tq, LANES), lambda b, n: (0, b * n_spans + n, 0))
    return pl.pallas_call(
        functools.partial(_dil_kernel, dil=dil),
        grid=(batch, n_spans),
        in_specs=[cur(0), edge(1, -1), cur(1), edge(1, 1), edge(2, -1), cur(2), edge(2, 1),
                  pl.BlockSpec((None, 3, DIL_HEADS_PER_GROUP, DIL_BLOCK, 2 * DIL_BLOCK),
                               lambda b, n: (group, 0, 0, 0, 0))],
        out_specs=[out_spec, out_spec],
        out_shape=[jax.ShapeDtypeStruct((n_pairs, batch * seq, LANES), jnp.float32)] * 2,
        compiler_params=_cparams(2),
        name=f"dil_attn_d{dil}",
    )(qkv, qkv, qkv, qkv, qkv, qkv, qkv, bias_tab)


def _mix_kernel(x_ref, na_ref, o0_ref, l0_ref, o1_ref, l1_ref, o2_ref, l2_ref, qc_ref, gate_ref,
                mkv_ref, wa_ref, wb_ref, wc_ref, wo_ref, fg_ref, rw_ref, rb_ref,
                x1_ref, h_ref, idx_ref, wt_ref, seg_ref):
    T = x_ref.shape[0]
    ob = []
    for hp in range(WIDTH_G // LANES):
        l0, l1, l2 = l0_ref[hp], l1_ref[hp], l2_ref[hp]
        m = jnp.maximum(jnp.maximum(l0, l1), l2)
        e0, e1, e2 = jnp.exp(l0 - m), jnp.exp(l1 - m), jnp.exp(l2 - m)
        ob.append((e0 * o0_ref[hp] + e1 * o1_ref[hp] + e2 * o2_ref[hp]) / (e0 + e1 + e2))
    ob = jnp.concatenate(ob, axis=-1)

    mem_scale = MEM_HEAD_DIM ** -0.5
    scores = []
    for h in range(MEM_HEADS):
        cols = slice(h * MEM_HEAD_DIM, (h + 1) * MEM_HEAD_DIM)
        scores.append(_dot_nt(qc_ref[:, cols], mkv_ref[:, cols]) * mem_scale)
    probs = []
    for s in scores:
        e = jnp.exp(s - jnp.max(s, axis=-1, keepdims=True))
        probs.append((e.astype(jnp.bfloat16), jnp.sum(e, axis=-1, keepdims=True)))
    oc = []
    for h, (p, l) in enumerate(probs):
        oc.append(_dot(p, mkv_ref[:, WIDTH_C + h * MEM_HEAD_DIM:WIDTH_C + (h + 1) * MEM_HEAD_DIM]) / l)
    oc = jnp.concatenate(oc, axis=-1).astype(jnp.bfloat16)

    y_a = _dot(na_ref[...], wa_ref[...])
    y_b = _dot(ob.astype(jnp.bfloat16), wb_ref[...])
    y_c = _dot(oc, wc_ref[...])
    merged = (gate_ref[:, 0:D_MODEL].astype(jnp.float32) * y_a
              + gate_ref[:, D_MODEL:2 * D_MODEL].astype(jnp.float32) * y_b
              + gate_ref[:, 2 * D_MODEL:3 * D_MODEL].astype(jnp.float32) * y_c)
    x1 = x_ref[...] + _dot(merged.astype(jnp.bfloat16), wo_ref[...])
    x1_ref[...] = x1

    h = _rms(x1, fg_ref[...])
    h_ref[...] = h.astype(jnp.bfloat16)
    logits = lax.dot_general(rw_ref[...], h, (((1,), (1,)), ((), ())), preferred_element_type=jnp.float32,
                             precision=lax.Precision.HIGHEST) + rb_ref[...]
    expert = lax.broadcasted_iota(jnp.int32, (N_EXPERTS, T), 0)
    vals, idxs = [], []
    for _ in range(TOP_K):
        mx = jnp.max(logits, axis=0, keepdims=True)
        sel = jnp.min(jnp.where(logits == mx, expert, N_EXPERTS), axis=0, keepdims=True)
        vals.append(mx)
        idxs.append(sel)
        logits = jnp.where(expert == sel, -jnp.inf, logits)
    ev = [jnp.exp(v - vals[0]) for v in vals]
    den = ev[0] + ev[1] + ev[2] + ev[3]
    idx_ref[...] = jnp.concatenate(idxs, axis=0)
    wt_ref[...] = jnp.concatenate([e / den for e in ev], axis=0)
    member = sum((expert == sel).astype(jnp.float32) for sel in idxs)
    for s in range(T // MOE_TILE):
        cnt = jnp.sum(member[:, s * MOE_TILE:(s + 1) * MOE_TILE], axis=1, keepdims=True)
        seg = jnp.floor((cnt + (MOE_PIECE - 1)) * (1.0 / MOE_PIECE)) * MOE_PIECE
        seg_ref[s] = jnp.broadcast_to(seg, (N_EXPERTS, LANES))


def _mix_out(x2d, na_out, dil_outs, qc, gates, mkv, wa, wb, wc, wo, ffn_g, router_w, router_b, batch, seq):
    n = x2d.shape[0]
    T = PROJ_TILE
    tiles_per_batch = seq // T
    mem_len = mkv.shape[0] // batch
    row = lambda w: pl.BlockSpec((T, w), lambda i: (i, 0))
    full = lambda a: pl.BlockSpec(a.shape, lambda i: (0,) * a.ndim)
    in_specs = [row(D_MODEL), row(WIDTH_A)]
    args = [x2d, na_out]
    for o, l in dil_outs:
        pair_rows = pl.BlockSpec((WIDTH_G // LANES, T, LANES), lambda i: (0, i, 0))
        in_specs += [pair_rows, pair_rows]
        args += [o, l]
    in_specs += [row(WIDTH_C), row(3 * D_MODEL),
                 pl.BlockSpec((mem_len, 2 * WIDTH_C), lambda i: (i // tiles_per_batch, 0)),
                 full(wa), full(wb), full(wc), full(wo), full(ffn_g), full(router_w), full(router_b)]
    args += [qc, gates, mkv, wa, wb, wc, wo, ffn_g, router_w, router_b]
    return pl.pallas_call(
        _mix_kernel,
        grid=(n // T,),
        in_specs=in_specs,
        out_specs=[row(D_MODEL), row(D_MODEL), pl.BlockSpec((TOP_K, T), lambda i: (0, i)),
                   pl.BlockSpec((TOP_K, T), lambda i: (0, i)),
                   pl.BlockSpec((T // MOE_TILE, N_EXPERTS, LANES), lambda i: (i, 0, 0))],
        out_shape=[jax.ShapeDtypeStruct((n, D_MODEL), jnp.float32),
                   jax.ShapeDtypeStruct((n, D_MODEL), jnp.bfloat16),
                   jax.ShapeDtypeStruct((TOP_K, n), jnp.int32),
                   jax.ShapeDtypeStruct((TOP_K, n), jnp.float32),
                   jax.ShapeDtypeStruct((n // MOE_TILE, N_EXPERTS, LANES), jnp.float32)],
        compiler_params=_cparams(1),
        name="mix_out",
    )(*args)


def _route_kernel(idx_ref, segs_ref, lpos_ref, ptab_ref, be_ref, pe_ref, carry_ref, pstart_ref):
    i = pl.program_id(0)
    T = idx_ref.shape[1]
    n_blocks = be_ref.shape[1]
    expert = lax.broadcasted_iota(jnp.int32, (N_EXPERTS, T), 0)
    idx = idx_ref[...]
    onehots = [expert == idx[k:k + 1, :] for k in range(TOP_K)]
    member = sum(o.astype(jnp.float32) for o in onehots)
    seg_b = segs_ref[i]
    seg = seg_b[:, 0:1]
    r = lax.broadcasted_iota(jnp.int32, (N_EXPERTS, N_EXPERTS), 0)
    c = lax.broadcasted_iota(jnp.int32, (N_EXPERTS, N_EXPERTS), 1)
    lower_incl = (r >= c).astype(jnp.float32)

    def cumsum_experts(v):
        return jnp.dot(lower_incl, v, preferred_element_type=jnp.float32, precision=lax.Precision.HIGHEST)

    @pl.when(i == 0)
    def _():
        counts = jnp.sum(segs_ref[...], axis=0)
        padded = jnp.floor((counts + (MOE_BLOCK - 1)) * (1.0 / MOE_BLOCK)) * MOE_BLOCK
        pad_ends = cumsum_experts(padded)
        pstart_ref[...] = pad_ends - padded
        carry_ref[...] = jnp.zeros_like(carry_ref)
        pe_ref[...] = pad_ends[:, 0:1].astype(jnp.int32)
        blk_start = (lax.broadcasted_iota(jnp.int32, (N_EXPERTS, n_blocks), 1) * MOE_BLOCK).astype(jnp.float32)
        be = jnp.sum((pad_ends[:, 0:1] <= blk_start).astype(jnp.int32), axis=0, keepdims=True)
        be_ref[...] = jnp.minimum(be, N_EXPERTS - 1)

    local_end = cumsum_experts(seg_b)[:, 0:1]
    local_start = local_end - seg
    global_start = pstart_ref[:, 0:1] + carry_ref[:, 0:1]
    rr = lax.broadcasted_iota(jnp.int32, (T, T), 0)
    cc = lax.broadcasted_iota(jnp.int32, (T, T), 1)
    earlier = (rr < cc).astype(jnp.bfloat16)
    before = _dot(member.astype(jnp.bfloat16), earlier)
    base = local_start + before
    lpos = [jnp.sum(jnp.where(o, base, 0.0), axis=0, keepdims=True) for o in onehots]
    lpos_ref[...] = jnp.concatenate(lpos, axis=0).astype(jnp.int32)

    inv = 1.0 / MOE_PIECE
    piece = lax.broadcasted_iota(jnp.int32, (N_EXPERTS, MOE_PIECE_TAB), 1).astype(jnp.float32)
    first = local_start * inv
    inside = jnp.logical_and(piece >= first, piece < local_end * inv)
    dst = jnp.sum(jnp.where(inside, global_start * inv + piece - first, 0.0), axis=0, keepdims=True)
    n_pieces = local_end[N_EXPERTS - 1:N_EXPERTS, :] * inv
    spare = (n_blocks * MOE_BLOCK // MOE_PIECE
             + (i % 2) * (MOE_TILE_ROWS // MOE_PIECE)).astype(jnp.float32)
    dst = jnp.where(piece[0:1, :] >= n_pieces, spare + piece[0:1, :], dst)
    dst = jnp.where(piece[0:1, :] == MOE_PIECE_TAB - 1, n_pieces, dst)
    ptab_ref[...] = dst.astype(jnp.int32)
    carry_ref[...] += seg


def _moe_route(topk_idx, segs, n_blocks):
    n = topk_idx.shape[1]
    T = MOE_TILE
    acc = pltpu.VMEM((N_EXPERTS, LANES), jnp.float32)
    return pl.pallas_call(
        _route_kernel,
        grid=(n // T,),
        in_specs=[pl.BlockSpec((TOP_K, T), lambda i: (0, i)),
                  pl.BlockSpec(segs.shape, lambda i: (0, 0, 0))],
        out_specs=[pl.BlockSpec((TOP_K, T), lambda i: (0, i)),
                   pl.BlockSpec((None, 1, MOE_PIECE_TAB), lambda i: (i, 0, 0)),
                   pl.BlockSpec((1, n_blocks), lambda i: (0, 0)),
                   pl.BlockSpec((N_EXPERTS, 1), lambda i: (0, 0))],
        out_shape=[jax.ShapeDtypeStruct((TOP_K, n), jnp.int32),
                   jax.ShapeDtypeStruct((n // T, 1, MOE_PIECE_TAB), jnp.int32),
                   jax.ShapeDtypeStruct((1, n_blocks), jnp.int32),
                   jax.ShapeDtypeStruct((N_EXPERTS, 1), jnp.int32)],
        scratch_shapes=[acc, acc],
        compiler_params=_cparams(1),
        name="moe_route",
    )(topk_idx, segs)


def _piece(ref, p):
    return ref.at[pl.ds(pl.multiple_of(p * MOE_PIECE, MOE_PIECE), MOE_PIECE), :]


def _dispatch_kernel(pe_ref, ptab_ref, lpos_ref, h_ref, xs_hbm, zbuf, pbuf, sem, zsem):
    i = pl.program_id(0)
    T = h_ref.shape[0]
    half = h_ref.shape[1] // 2

    def zero_copy(e):
        end = pe_ref[e]
        start = pl.multiple_of(end - MOE_BLOCK, MOE_BLOCK)
        return pltpu.make_async_copy(zbuf, xs_hbm.at[pl.ds(start, MOE_BLOCK), :], zsem)

    def nonempty(e):
        return pe_ref[e] > (pe_ref[e - 1] if e else 0)

    @pl.when(i == 0)
    def _():
        zbuf[...] = jnp.zeros_like(zbuf)
        for e in range(N_EXPERTS):
            @pl.when(nonempty(e))
            def _():
                zero_copy(e).start()
        for e in range(N_EXPERTS):
            @pl.when(nonempty(e))
            def _():
                zero_copy(e).wait()

        def tail_copy(b):
            return pltpu.make_async_copy(
                zbuf, xs_hbm.at[pl.ds(pl.multiple_of(b * MOE_BLOCK, MOE_BLOCK), MOE_BLOCK), :], zsem)
        first_tail = pe_ref[N_EXPERTS - 1] // MOE_BLOCK
        n_blocks = xs_hbm.shape[0] // MOE_BLOCK
        lax.fori_loop(first_tail, n_blocks, lambda b, c: (tail_copy(b).start(), c)[1], 0)
        lax.fori_loop(first_tail, n_blocks, lambda b, c: (tail_copy(b).wait(), c)[1], 0)

    row = lax.broadcasted_iota(jnp.int32, (MOE_TILE_ROWS, T), 0)
    lpos = lpos_ref[...]
    perm = sum((row == lpos[k:k + 1, :]).astype(jnp.bfloat16) for k in range(TOP_K))
    lo = lax.bitcast_convert_type(_dot(perm, h_ref[:, :half]), jnp.uint32)
    hi = lax.bitcast_convert_type(_dot(perm, h_ref[:, half:]), jnp.uint32)
    slot = i % 2
    pbuf[slot] = hi | (lo >> 16)

    n_tile_pieces = MOE_TILE_ROWS // MOE_PIECE

    def piece_copy(s, p):
        return pltpu.make_async_copy(_piece(pbuf.at[s], p), _piece(xs_hbm, ptab_ref[p]), sem.at[s])

    def wait_all(s):
        lax.fori_loop(0, n_tile_pieces, lambda p, c: (piece_copy(s, 0).wait(), c)[1], 0, unroll=8)

    lax.fori_loop(0, n_tile_pieces, lambda p, c: (piece_copy(slot, p).start(), c)[1], 0, unroll=8)

    @pl.when(i > 0)
    def _():
        wait_all(1 - slot)

    @pl.when(i == pl.num_programs(0) - 1)
    def _():
        wait_all(slot)


def _moe_dispatch(h, lpos, ptab, pad_ends, n_blocks):
    n, width = h.shape
    T = MOE_TILE
    n_steps = n // T

    def kernel(pe_ref, ptab_ref, *rest):
        _dispatch_kernel(pe_ref, ptab_ref.at[0], *rest)

    grid_spec = pltpu.PrefetchScalarGridSpec(
        num_scalar_prefetch=1,
        grid=(n_steps,),
        in_specs=[pl.BlockSpec((None, 1, MOE_PIECE_TAB), lambda i, pe: (i, 0, 0), memory_space=pltpu.SMEM),
                  pl.BlockSpec((TOP_K, T), lambda i, pe: (0, i)),
                  pl.BlockSpec((T, width), lambda i, pe: (i, 0))],
        out_specs=pl.BlockSpec(memory_space=pl.ANY),
        scratch_shapes=[pltpu.VMEM((MOE_BLOCK, width // 2), jnp.uint32),
                        pltpu.VMEM((2, MOE_TILE_ROWS, width // 2), jnp.uint32),
                        pltpu.SemaphoreType.DMA((2,)),
                        pltpu.SemaphoreType.DMA(())],
    )
    return pl.pallas_call(
        kernel,
        grid_spec=grid_spec,
        out_shape=jax.ShapeDtypeStruct(((n_blocks + MOE_SPARE_BLOCKS) * MOE_BLOCK, width // 2), jnp.uint32),
        compiler_params=_cparams(1),
        name="moe_dispatch",
    )(pad_ends, ptab, lpos, h)


def _moe_kernel(be_ref, pe_ref, xs_ref, wgu_ref, bgu_ref, wdn_ref, bdn_ref, y_ref, wgu_bf, wdn_bf):
    i = pl.program_id(0)
    n_valid = pe_ref[N_EXPERTS - 1] // MOE_BLOCK
    new_expert = jnp.logical_or(i == 0, be_ref[i] != be_ref[jnp.maximum(i - 1, 0)])

    @pl.when(jnp.logical_and(i < n_valid, new_expert))
    def _():
        wgu_bf[...] = wgu_ref[0].astype(jnp.bfloat16)
        wdn_bf[...] = wdn_ref[0].astype(jnp.bfloat16)

    @pl.when(i < n_valid)
    def _():
        x = _unpack_bf16_pairs(xs_ref[...])
        gu = _dot(x, wgu_bf[...]) + bgu_ref[0]
        g = jnp.minimum(gu[:, :D_FF], SWIGLU_LIMIT)
        u = jnp.clip(gu[:, D_FF:], -SWIGLU_LIMIT, SWIGLU_LIMIT)
        a = (u + 1.0) * (g * _sigmoid(SWIGLU_ALPHA * g))
        y_ref[...] = _pack_bf16_pairs(_dot(a.astype(jnp.bfloat16), wdn_bf[...]) + bdn_ref[0])

    @pl.when(i >= n_valid)
    def _():
        y_ref[...] = jnp.zeros_like(y_ref)


def _moe_experts(xs, block_expert, pad_ends, wgu, bgu, wdn, bdn):
    n_blocks = block_expert.shape[0]

    def row_blk(i, be, pe):
        return (jnp.minimum(i, jnp.maximum(pe[N_EXPERTS - 1] // MOE_BLOCK - 1, 0)), 0)

    grid_spec = pltpu.PrefetchScalarGridSpec(
        num_scalar_prefetch=2,
        grid=(n_blocks,),
        in_specs=[pl.BlockSpec((MOE_BLOCK, D_MODEL // 2), row_blk),
                  pl.BlockSpec((1, D_MODEL, 2 * D_FF), lambda i, be, pe: (be[i], 0, 0)),
                  pl.BlockSpec((1, 1, 2 * D_FF), lambda i, be, pe: (be[i], 0, 0)),
                  pl.BlockSpec((1, D_FF, D_MODEL), lambda i, be, pe: (be[i], 0, 0)),
                  pl.BlockSpec((1, 1, D_MODEL), lambda i, be, pe: (be[i], 0, 0))],
        out_specs=pl.BlockSpec((MOE_BLOCK, D_MODEL // 2), lambda i, be, pe: (i, 0)),
        scratch_shapes=[pltpu.VMEM((D_MODEL, 2 * D_FF), jnp.bfloat16),
                        pltpu.VMEM((D_FF, D_MODEL), jnp.bfloat16)],
    )
    return pl.pallas_call(
        _moe_kernel,
        grid_spec=grid_spec,
        out_shape=jax.ShapeDtypeStruct((n_blocks * MOE_BLOCK, D_MODEL // 2), jnp.uint32),
        compiler_params=_cparams(1),
        name="moe_experts",
    )(block_expert, pad_ends, xs, wgu, bgu, wdn, bdn)


def _gather_pieces(src_hbm, ptab_ref, dst_ref, sem):
    spare = src_hbm.shape[0] // MOE_PIECE

    def body(p, carry):
        src = ptab_ref[p]
        src = jnp.where(src >= spare, 0, src)
        pltpu.make_async_copy(_piece(src_hbm, src), _piece(dst_ref, p), sem).start()
        return carry
    lax.fori_loop(0, MOE_TILE_ROWS // MOE_PIECE, body, 0, unroll=8)


def _wait_pieces(src_hbm, dst_ref, sem):
    def body(p, carry):
        pltpu.make_async_copy(_piece(src_hbm, 0), _piece(dst_ref, 0), sem).wait()
        return carry
    lax.fori_loop(0, MOE_TILE_ROWS // MOE_PIECE, body, 0, unroll=8)


def _combine_kernel(ptab_ref, ptab_next_ref, ys_hbm, lpos_ref, wt_ref, x1_ref, g_ref, out_ref, gbuf, sem):
    i = pl.program_id(0)
    n_steps = pl.num_programs(0)
    slot = i % 2
    T = x1_ref.shape[0]

    @pl.when(i == 0)
    def _():
        _gather_pieces(ys_hbm, ptab_ref, gbuf.at[0], sem.at[0])

    @pl.when(i + 1 < n_steps)
    def _():
        _gather_pieces(ys_hbm, ptab_next_ref, gbuf.at[1 - slot], sem.at[1 - slot])

    _wait_pieces(ys_hbm, gbuf.at[slot], sem.at[slot])
    rows = _unpack_bf16_pairs(gbuf[slot])
    col = lax.broadcasted_iota(jnp.int32, (T, MOE_TILE_ROWS), 1)
    lpos = lpos_ref[...]
    wt = wt_ref[...]
    cmb = jnp.zeros((T, MOE_TILE_ROWS), jnp.float32)
    for k in range(TOP_K):
        cmb = jnp.where(col == lpos[:, k:k + 1], wt[:, k:k + 1], cmb)
    hi = cmb.astype(jnp.bfloat16)
    lo = (cmb - hi.astype(jnp.float32)).astype(jnp.bfloat16)
    y2 = _dot(jnp.concatenate([hi, lo], axis=0), rows)
    y = x1_ref[...] + y2[:T] + y2[T:]
    out_ref[...] = _rms(y, g_ref[...])


def _moe_combine(ys, ptab, lpos_t, wts, x1, final_g):
    n = x1.shape[0]
    T = MOE_TILE
    n_steps = n // T
    smem_blk = lambda shift: pl.BlockSpec(
        (None, 1, MOE_PIECE_TAB), lambda i: (jnp.minimum(i + shift, n_steps - 1), 0, 0),
        memory_space=pltpu.SMEM)

    def kernel(ptab_ref, ptab_next_ref, *rest):
        _combine_kernel(ptab_ref.at[0], ptab_next_ref.at[0], *rest)

    return pl.pallas_call(
        kernel,
        grid=(n_steps,),
        in_specs=[smem_blk(0), smem_blk(1),
                  pl.BlockSpec(memory_space=pl.ANY),
                  pl.BlockSpec((T, TOP_K), lambda i: (i, 0)),
                  pl.BlockSpec((T, TOP_K), lambda i: (i, 0)),
                  pl.BlockSpec((T, D_MODEL), lambda i: (i, 0)),
                  pl.BlockSpec((1, D_MODEL), lambda i: (0, 0))],
        out_specs=pl.BlockSpec((T, D_MODEL), lambda i: (i, 0)),
        out_shape=jax.ShapeDtypeStruct((n, D_MODEL), jnp.float32),
        scratch_shapes=[pltpu.VMEM((2, MOE_TILE_ROWS, D_MODEL // 2), jnp.uint32),
                        pltpu.SemaphoreType.DMA((2,))],
        compiler_params=_cparams(1),
        name="moe_combine",
    )(ptab, ptab, ys, lpos_t, wts, x1, final_g)


def kernel(x, mem, attn_norm_g, mem_norm_g, w_in, w_mem_kv, na_rpb, t5_rel_bias, w_branch_a, w_branch_b,
           w_branch_c, w_out, ffn_norm_g, router_w, router_b, expert_w_gate_up, expert_b_gate_up,
           expert_w_down, expert_b_down, final_norm_g):
    B, S, D = x.shape
    depth = w_in.shape[0]
    bf = jnp.bfloat16
    x2d = x.reshape(B * S, D)
    mem2d = mem.reshape(B * mem.shape[1], D)
    assert depth == 1, "single-layer block (per-layer parameters carry a leading axis of 1)"
    l = 0
    na_bias, dil_bias = _bias_prep(na_rpb[l], t5_rel_bias)
    mkv = _mem_kv(mem2d, mem_norm_g[l][None], w_mem_kv[l].astype(bf))
    na_qkv, d0, d1, d2, qc = _in_proj(x2d, attn_norm_g[l][None], w_in[l][:, :OFF_GATE].astype(bf), B, S)
    gates = _gate_proj(x2d, attn_norm_g[l][None], w_in[l][:, OFF_GATE:].astype(bf))
    na_out = _na_attn(na_qkv, na_bias, B, S)
    dil_outs = [_dil_attn(qkv, dil_bias, gi, dil, B, S)
                for gi, (qkv, (_, dil)) in enumerate(zip((d0, d1, d2), DIL_GROUPS))]
    x1, h, topk_idx, topk_w, segs = _mix_out(
        x2d, na_out, dil_outs, qc, gates, mkv, w_branch_a[l].astype(bf), w_branch_b[l].astype(bf),
        w_branch_c[l].astype(bf), w_out[l].astype(bf), ffn_norm_g[l][None], router_w[l].T,
        router_b[l][:, None], B, S)

    n = B * S
    max_rows = n * TOP_K + (n // MOE_TILE) * N_EXPERTS * (MOE_PIECE - 1)
    n_blocks = -(-max_rows // MOE_BLOCK) + N_EXPERTS
    lpos, ptab, block_expert, pad_ends = _moe_route(topk_idx, segs, n_blocks)
    block_expert = block_expert.reshape(n_blocks)
    pad_ends = pad_ends.reshape(N_EXPERTS)
    xs = _moe_dispatch(h, lpos, ptab, pad_ends, n_blocks)
    ys = _moe_experts(xs, block_expert, pad_ends, expert_w_gate_up[l], expert_b_gate_up[l][:, None, :],
                      expert_w_down[l], expert_b_down[l][:, None, :])
    out = _moe_combine(ys, ptab, lpos.T, topk_w.T, x1, final_norm_g[None])
    return out.reshape(B, S, D)
```

```python
import functools
import math

import numpy as np
import jax
import jax.numpy as jnp
from jax import lax
from jax.experimental import pallas as pl
from jax.experimental.pallas import tpu as pltpu

D_MODEL = 1024
GRID_W = 64
MEM_HEADS = 4
MEM_HEAD_DIM = 128
HEAD_DIM = 64
NA_HEADS = 8
NA_WIN_ROWS = 8
NA_WIN_COLS = 16
DIL_GROUPS = ((128, 1), (512, 4), (2048, 16))
DIL_HEADS_PER_GROUP = 4
DIL_BLOCK = 128
T5_BUCKETS = 32
T5_MAX_DIST = 1024
N_EXPERTS = 32
TOP_K = 4
D_FF = D_MODEL
SWIGLU_ALPHA = 1.702
SWIGLU_LIMIT = 7.0
RMS_EPS = 1e-6
NEG_INF = -1e30

WIDTH_A = NA_HEADS * HEAD_DIM
WIDTH_B = DIL_HEADS_PER_GROUP * len(DIL_GROUPS) * HEAD_DIM
WIDTH_G = DIL_HEADS_PER_GROUP * HEAD_DIM
WIDTH_C = MEM_HEADS * MEM_HEAD_DIM
OFF_QA, OFF_KA, OFF_VA = 0, WIDTH_A, 2 * WIDTH_A
OFF_QB = 3 * WIDTH_A
OFF_KB = OFF_QB + WIDTH_B
OFF_VB = OFF_KB + WIDTH_B
OFF_QC = OFF_VB + WIDTH_B
OFF_GATE = OFF_QC + WIDTH_C
IN_COLS = OFF_GATE + 3 * D_MODEL

LANES = 128
VMEM_LIMIT = 56 * 1024 * 1024

IN_PROJ_TILE = 1024
PROJ_TILE = 512
NA_ROWS_PER_STEP = 8
DIL_STEP_TOKENS = 2048
MOE_BLOCK = 512
MOE_TILE = 256
MOE_PIECE = 8
MOE_TILE_ROWS = -(-(TOP_K * MOE_TILE + N_EXPERTS * (MOE_PIECE - 1)) // 256) * 256
MOE_PIECE_TAB = 256
assert MOE_TILE_ROWS // MOE_PIECE < MOE_PIECE_TAB
MOE_SPARE_BLOCKS = -(-2 * MOE_TILE_ROWS // MOE_BLOCK)


def _cparams(n_axes):
    return pltpu.CompilerParams(
        dimension_semantics=("arbitrary",) * n_axes, vmem_limit_bytes=VMEM_LIMIT)


def _rms(x, g):
    return x * lax.rsqrt(jnp.mean(x * x, axis=-1, keepdims=True) + RMS_EPS) * g


def _sigmoid(x):
    return 1.0 / (1.0 + jnp.exp(-x))


def _pack_bf16_pairs(x):
    w = x.shape[1] // 2
    bits = lax.bitcast_convert_type(x.astype(jnp.bfloat16).astype(jnp.float32), jnp.uint32)
    return (bits[:, w:] & jnp.uint32(0xFFFF0000)) | (bits[:, :w] >> 16)


def _unpack_bf16_pairs(p):
    lo = lax.bitcast_convert_type(p << 16, jnp.float32)
    hi = lax.bitcast_convert_type(p & jnp.uint32(0xFFFF0000), jnp.float32)
    return jnp.concatenate([lo, hi], axis=1).astype(jnp.bfloat16)


def _dot(a, b):
    return jnp.dot(a, b, preferred_element_type=jnp.float32)


def _dot_nt(a, b):
    return lax.dot_general(a, b, (((1,), (1,)), ((), ())), preferred_element_type=jnp.float32)


def _mem_kv_kernel(mem_ref, g_ref, w_ref, out_ref):
    h = _rms(mem_ref[...], g_ref[...]).astype(jnp.bfloat16)
    out_ref[...] = _dot(h, w_ref[...]).astype(jnp.bfloat16)


def _mem_kv(mem2d, g, w):
    rows = mem2d.shape[0]
    return pl.pallas_call(
        _mem_kv_kernel,
        out_shape=jax.ShapeDtypeStruct((rows, 2 * WIDTH_C), jnp.bfloat16),
        name="mem_kv",
        compiler_params=pltpu.CompilerParams(vmem_limit_bytes=VMEM_LIMIT),
    )(mem2d, g, w)


def _in_proj_kernel(x_ref, g_ref, w_ref, na_ref, d0_ref, d1_ref, d2_ref, qc_ref, acc_ref):
    T = x_ref.shape[0]
    h = _rms(x_ref[...], g_ref[...]).astype(jnp.bfloat16)
    q_scale = HEAD_DIM ** -0.5

    for c in range(3):
        acc = _dot(h, w_ref[:, c * WIDTH_A:(c + 1) * WIDTH_A])
        if c == 0:
            acc = acc * q_scale
        na_ref[:, c * WIDTH_A:(c + 1) * WIDTH_A] = acc.astype(jnp.bfloat16)

    for gi, ((_, dil), d_ref) in enumerate(zip(DIL_GROUPS, (d0_ref, d1_ref, d2_ref))):
        for kind, off in enumerate((OFF_QB, OFF_KB, OFF_VB)):
            lo = off + gi * WIDTH_G
            acc = _dot(h, w_ref[:, lo:lo + WIDTH_G])
            if kind == 0:
                acc = acc * q_scale
            cols = slice(kind * WIDTH_G, (kind + 1) * WIDTH_G)
            if dil == 1:
                d_ref[0, 0, :, cols] = acc.astype(jnp.bfloat16)
            else:
                for c in range(WIDTH_G // LANES):
                    acc_ref[c] = acc[:, c * LANES:(c + 1) * LANES]
                for rho in range(dil):
                    d_ref[0, rho, :, cols] = jnp.concatenate(
                        [acc_ref[c, pl.ds(rho, T // dil, stride=dil), :] for c in range(WIDTH_G // LANES)],
                        axis=-1).astype(jnp.bfloat16)

    qc_ref[...] = _dot(h, w_ref[:, OFF_QC:OFF_QC + WIDTH_C]).astype(jnp.bfloat16)


def _gate_proj_kernel(x_ref, g_ref, w_ref, gate_ref):
    h = _rms(x_ref[...], g_ref[...]).astype(jnp.bfloat16)
    chunk = 512
    for c in range(3 * D_MODEL // chunk):
        gate_ref[:, c * chunk:(c + 1) * chunk] = _sigmoid(
            _dot(h, w_ref[:, c * chunk:(c + 1) * chunk])).astype(jnp.bfloat16)


def _gate_proj(x2d, g, w_gate_bf16):
    n = x2d.shape[0]
    T = IN_PROJ_TILE
    return pl.pallas_call(
        _gate_proj_kernel,
        grid=(n // T,),
        in_specs=[pl.BlockSpec((T, D_MODEL), lambda i: (i, 0)),
                  pl.BlockSpec((1, D_MODEL), lambda i: (0, 0)),
                  pl.BlockSpec((D_MODEL, 3 * D_MODEL), lambda i: (0, 0))],
        out_specs=pl.BlockSpec((T, 3 * D_MODEL), lambda i: (i, 0)),
        out_shape=jax.ShapeDtypeStruct((n, 3 * D_MODEL), jnp.bfloat16),
        compiler_params=_cparams(1),
        name="gate_proj",
    )(x2d, g, w_gate_bf16)


def _in_proj(x2d, g, w_bf16, batch, seq):
    n = x2d.shape[0]
    T = IN_PROJ_TILE
    tiles_per_batch = seq // T
    out_shape = [jax.ShapeDtypeStruct((n, 3 * WIDTH_A), jnp.bfloat16)]
    out_specs = [pl.BlockSpec((T, 3 * WIDTH_A), lambda i: (i, 0))]
    for _, dil in DIL_GROUPS:
        out_shape.append(jax.ShapeDtypeStruct((batch, dil, seq // dil, 3 * WIDTH_G), jnp.bfloat16))
        out_specs.append(pl.BlockSpec((1, dil, T // dil, 3 * WIDTH_G),
                                      lambda i: (i // tiles_per_batch, 0, i % tiles_per_batch, 0)))
    out_shape += [jax.ShapeDtypeStruct((n, WIDTH_C), jnp.bfloat16)]
    out_specs += [pl.BlockSpec((T, WIDTH_C), lambda i: (i, 0))]
    return pl.pallas_call(
        _in_proj_kernel,
        grid=(n // T,),
        in_specs=[pl.BlockSpec((T, D_MODEL), lambda i: (i, 0)),
                  pl.BlockSpec((1, D_MODEL), lambda i: (0, 0)),
                  pl.BlockSpec((D_MODEL, OFF_GATE), lambda i: (0, 0))],
        out_specs=out_specs,
        out_shape=out_shape,
        scratch_shapes=[pltpu.VMEM((WIDTH_G // LANES, T, LANES), jnp.float32)],
        compiler_params=_cparams(1),
        name="in_proj",
    )(x2d, g, w_bf16)


def _na_kernel(q_ref, k_ref, v_ref, bias_ref, out_ref, *, rows):
    i = pl.program_id(1)
    lane = lax.broadcasted_iota(jnp.int32, (GRID_W, LANES), 1)
    low_half = lane < HEAD_DIM
    n_win = NA_WIN_ROWS * GRID_W

    def row_body(j, carry):
        qr = i * NA_ROWS_PER_STEP + j
        rs = jnp.clip(qr - NA_WIN_ROWS // 2, 0, rows - NA_WIN_ROWS)
        shift = rs - qr + NA_WIN_ROWS - 1
        q_off = pl.multiple_of(j * GRID_W, GRID_W)
        k_off = pl.multiple_of(rs * GRID_W, GRID_W)
        scores = []
        for hp in range(NA_HEADS // 2):
            cols = slice(hp * LANES, (hp + 1) * LANES)
            qp = q_ref[pl.ds(q_off, GRID_W), cols]
            kp = k_ref[pl.ds(k_off, n_win), cols]
            zero = jnp.zeros_like(qp)
            q2 = jnp.concatenate([jnp.where(low_half, qp, zero), jnp.where(low_half, zero, qp)], axis=0)
            bias = jnp.concatenate(
                [jnp.concatenate([bias_ref[shift + 2 * p, 2 * hp + hh] for p in range(NA_WIN_ROWS // 2)], axis=-1)
                 for hh in range(2)], axis=0)
            scores.append(_dot_nt(q2, kp) + bias)
        probs = []
        for s in scores:
            e = jnp.exp(s - jnp.max(s, axis=-1, keepdims=True))
            probs.append((e.astype(jnp.bfloat16), jnp.sum(e, axis=-1, keepdims=True)))
        for hp, (p, l) in enumerate(probs):
            cols = slice(hp * LANES, (hp + 1) * LANES)
            o2 = _dot(p, v_ref[pl.ds(k_off, n_win), cols]) / l
            out_ref[pl.ds(q_off, GRID_W), cols] = jnp.where(
                low_half, o2[:GRID_W], o2[GRID_W:]).astype(jnp.bfloat16)
        return carry

    lax.fori_loop(0, NA_ROWS_PER_STEP, row_body, 0, unroll=2)


def _na_attn(na_qkv, bias_tab, batch, seq):
    rows = seq // GRID_W
    steps = rows // NA_ROWS_PER_STEP
    tq = NA_ROWS_PER_STEP * GRID_W
    n = na_qkv.shape[0]
    return pl.pallas_call(
        functools.partial(_na_kernel, rows=rows),
        grid=(batch, steps),
        in_specs=[pl.BlockSpec((tq, WIDTH_A), lambda b, i: (b * steps + i, 0)),
                  pl.BlockSpec((seq, WIDTH_A), lambda b, i: (b, 1)),
                  pl.BlockSpec((seq, WIDTH_A), lambda b, i: (b, 2)),
                  pl.BlockSpec(bias_tab.shape, lambda b, i: (0, 0, 0, 0))],
        out_specs=pl.BlockSpec((tq, WIDTH_A), lambda b, i: (b * steps + i, 0)),
        out_shape=jax.ShapeDtypeStruct((n, WIDTH_A), jnp.bfloat16),
        compiler_params=_cparams(2),
        name="na_attn",
    )(na_qkv, na_qkv, na_qkv, bias_tab)


def _t5_bucket_np(rel):
    half = T5_BUCKETS // 2
    max_exact = half // 2
    ret = np.where(rel > 0, half, 0)
    n = np.abs(rel)
    nf = np.maximum(n, 1).astype(np.float32)
    large = max_exact + (np.log(nf / np.float32(max_exact)) / np.float32(math.log(T5_MAX_DIST / max_exact))
                         * np.float32(half - max_exact)).astype(np.int32)
    large = np.minimum(large, half - 1)
    return ret + np.where(n < max_exact, n, large)


def _dil_bucket_table():
    q = np.arange(DIL_BLOCK)[:, None]
    j = np.arange(2 * DIL_BLOCK)[None, :]
    tabs = []
    for window, dil in DIL_GROUPS:
        half = (window // 2) // dil
        assert half == DIL_BLOCK // 2
        rel = (j - half) - q
        tabs.append(np.where(np.abs(rel) <= half, _t5_bucket_np(rel * dil), -1))
    return np.stack(tabs).astype(np.int32)


def _bias_prep_kernel(rpb_ref, t5_ref, bucket_ref, na_ref, dil_ref):
    n_dc = 2 * NA_WIN_COLS - 1
    n_dr = 2 * NA_WIN_ROWS - 1
    qc = lax.broadcasted_iota(jnp.int32, (GRID_W, 2 * GRID_W), 0)
    lane = lax.broadcasted_iota(jnp.int32, (GRID_W, 2 * GRID_W), 1)
    second = lane >= GRID_W
    kc = jnp.where(second, lane - GRID_W, lane)
    diff = kc - qc
    cs = jnp.clip(qc - NA_WIN_COLS // 2, 0, GRID_W - NA_WIN_COLS)
    valid = jnp.logical_and(kc >= cs, kc < cs + NA_WIN_COLS)

    def na_tile(mh, carry):
        m = mh // NA_HEADS
        h = mh - m * NA_HEADS
        base = (h * n_dr + m) * n_dc
        acc = jnp.zeros((GRID_W, 2 * GRID_W), jnp.float32)
        for d in range(n_dc):
            v = jnp.where(second, rpb_ref[base + n_dc + d], rpb_ref[base + d])
            acc = jnp.where(diff == d - (NA_WIN_COLS - 1), v, acc)
        na_ref[m, h] = jnp.where(valid, acc, NEG_INF)
        return carry

    lax.fori_loop(0, (n_dr - 1) * NA_HEADS, na_tile, 0)

    j = lax.broadcasted_iota(jnp.int32, (DIL_BLOCK, 2 * DIL_BLOCK), 1)
    half = DIL_BLOCK // 2
    n_heads = DIL_HEADS_PER_GROUP * len(DIL_GROUPS)
    for gi in range(len(DIL_GROUPS)):
        bucket = bucket_ref[gi]
        in_win = bucket >= 0
        for h in range(DIL_HEADS_PER_GROUP):
            acc = jnp.zeros((DIL_BLOCK, 2 * DIL_BLOCK), jnp.float32)
            for b in range(T5_BUCKETS):
                acc = jnp.where(bucket == b, t5_ref[b * n_heads + gi * DIL_HEADS_PER_GROUP + h], acc)
            for v, ok in enumerate((jnp.logical_and(in_win, j >= half), in_win,
                                    jnp.logical_and(in_win, j < 2 * DIL_BLOCK - half))):
                dil_ref[gi, v, h] = jnp.where(ok, acc, NEG_INF)


def _bias_prep(rpb, t5):
    smem = pl.BlockSpec(memory_space=pltpu.SMEM)
    return pl.pallas_call(
        _bias_prep_kernel,
        in_specs=[smem, smem, pl.BlockSpec(memory_space=pltpu.VMEM)],
        out_shape=[jax.ShapeDtypeStruct((2 * NA_WIN_ROWS - 2, NA_HEADS, GRID_W, 2 * GRID_W), jnp.float32),
                   jax.ShapeDtypeStruct((len(DIL_GROUPS), 3, DIL_HEADS_PER_GROUP, DIL_BLOCK, 2 * DIL_BLOCK),
                                        jnp.float32)],
        compiler_params=pltpu.CompilerParams(vmem_limit_bytes=VMEM_LIMIT),
        name="bias_prep",
    )(rpb.reshape(-1), t5.reshape(-1), jnp.asarray(_dil_bucket_table()))


def _dil_kernel(q_ref, kp_ref, kc_ref, kn_ref, vp_ref, vc_ref, vn_ref, bias_ref, o_ref, lse_ref, *, dil):
    lane = lax.broadcasted_iota(jnp.int32, (DIL_BLOCK, LANES), 1)
    low_half = lane < HEAD_DIM
    hb = DIL_BLOCK // 2

    span = pl.program_id(1)
    n_spans = pl.num_programs(1)
    nbs = kc_ref.shape[2] // DIL_BLOCK

    def window(prev_ref, cur_ref, next_ref, rho, blk, cols):
        lo = blk * DIL_BLOCK
        first = prev_ref[0, rho, hb:, cols] if blk == 0 else cur_ref[0, rho, lo - hb:lo, cols]
        last = (next_ref[0, rho, :hb, cols] if blk == nbs - 1
                else cur_ref[0, rho, lo + DIL_BLOCK:lo + DIL_BLOCK + hb, cols])
        return jnp.concatenate([first, cur_ref[0, rho, lo:lo + DIL_BLOCK, cols], last], axis=0)

    def unit(rho, blk):
        variant = 1
        if blk == nbs - 1:
            variant = jnp.where(span == n_spans - 1, 2, variant)
        if blk == 0:
            variant = jnp.where(span == 0, 0, variant)
        scores = []
        for hp in range(DIL_HEADS_PER_GROUP // 2):
            cols = slice(hp * LANES, (hp + 1) * LANES)
            qp = q_ref[0, rho, blk * DIL_BLOCK:(blk + 1) * DIL_BLOCK, cols]
            zero = jnp.zeros_like(qp)
            q2 = jnp.concatenate([jnp.where(low_half, qp, zero), jnp.where(low_half, zero, qp)], axis=0)
            kw = window(kp_ref, kc_ref, kn_ref, rho, blk, cols)
            bias = jnp.concatenate([bias_ref[variant, 2 * hp], bias_ref[variant, 2 * hp + 1]], axis=0)
            scores.append(_dot_nt(q2, kw) + bias)
        probs = []
        for s in scores:
            m = jnp.max(s, axis=-1, keepdims=True)
            e = jnp.exp(s - m)
            probs.append((e.astype(jnp.bfloat16), m, jnp.sum(e, axis=-1, keepdims=True)))
        for hp, (p, m, l) in enumerate(probs):
            cols = slice(hp * LANES, (hp + 1) * LANES)
            o2 = _dot(p, window(vp_ref, vc_ref, vn_ref, rho, blk, cols)) / l
            lse2 = jnp.broadcast_to(m + jnp.log(l), (2 * DIL_BLOCK, LANES))
            o_pair = jnp.where(low_half, o2[:DIL_BLOCK], o2[DIL_BLOCK:])
            lse_pair = jnp.where(low_half, lse2[:DIL_BLOCK], lse2[DIL_BLOCK:])
            if dil == 1:
                rows = slice(blk * DIL_BLOCK, (blk + 1) * DIL_BLOCK)
                o_ref[hp, rows, :] = o_pair
                lse_ref[hp, rows, :] = lse_pair
            else:
                rows = pl.ds(blk * DIL_BLOCK * dil + rho, DIL_BLOCK, stride=dil)
                o_ref[hp, rows, :] = o_pair
                lse_ref[hp, rows, :] = lse_pair

    def class_body(rho, carry):
        for blk in range(nbs):
            unit(rho, blk)
        return carry

    if dil == 1:
        class_body(0, 0)
    else:
        lax.fori_loop(0, dil, class_body, 0, unroll=2 if nbs == 1 else 1)


def _dil_attn(qkv, bias_tab, group, dil, batch, seq):
    L = seq // dil
    nb = L // DIL_BLOCK
    assert nb >= 2
    tq = DIL_STEP_TOKENS
    span = tq // dil
    nbs = span // DIL_BLOCK
    n_spans = L // span

    def cur(col):
        return pl.BlockSpec((1, dil, span, WIDTH_G), lambda b, n: (b, 0, n, col))

    def edge(col, shift):
        return pl.BlockSpec((1, dil, DIL_BLOCK, WIDTH_G),
                            lambda b, n: (b, 0, jnp.clip(n * nbs + (shift if shift < 0 else nbs), 0, nb - 1), col))

    n_pairs = WIDTH_G // LANES
    out_spec = pl.BlockSpec((n_pairs, tq, LANES), lambda b, n: (0, b * n_spans + n, 0))
    return pl.pallas_call(
        functools.partial(_dil_kernel, dil=dil),
        grid=(batch, n_spans),
        in_specs=[cur(0), edge(1, -1), cur(1), edge(1, 1), edge(2, -1), cur(2), edge(2, 1),
                  pl.BlockSpec((None, 3, DIL_HEADS_PER_GROUP, DIL_BLOCK, 2 * DIL_BLOCK),
                               lambda b, n: (group, 0, 0, 0, 0))],
        out_specs=[out_spec, out_spec],
        out_shape=[jax.ShapeDtypeStruct((n_pairs, batch * seq, LANES), jnp.float32)] * 2,
        compiler_params=_cparams(2),
        name=f"dil_attn_d{dil}",
    )(qkv, qkv, qkv, qkv, qkv, qkv, qkv, bias_tab)


def _mix_kernel(x_ref, na_ref, o0_ref, l0_ref, o1_ref, l1_ref, o2_ref, l2_ref, qc_ref, gate_ref,
                mkv_ref, wa_ref, wb_ref, wc_ref, wo_ref, fg_ref, rw_ref, rb_ref,
                x1_ref, h_ref, idx_ref, wt_ref, seg_ref):
    T = x_ref.shape[0]
    ob = []
    for hp in range(WIDTH_G // LANES):
        l0, l1, l2 = l0_ref[hp], l1_ref[hp], l2_ref[hp]
        m = jnp.maximum(jnp.maximum(l0, l1), l2)
        e0, e1, e2 = jnp.exp(l0 - m), jnp.exp(l1 - m), jnp.exp(l2 - m)
        ob.append((e0 * o0_ref[hp] + e1 * o1_ref[hp] + e2 * o2_ref[hp]) / (e0 + e1 + e2))
    ob = jnp.concatenate(ob, axis=-1)

    mem_scale = MEM_HEAD_DIM ** -0.5
    scores = []
    for h in range(MEM_HEADS):
        cols = slice(h * MEM_HEAD_DIM, (h + 1) * MEM_HEAD_DIM)
        scores.append(_dot_nt(qc_ref[:, cols], mkv_ref[:, cols]) * mem_scale)
    probs = []
    for s in scores:
        e = jnp.exp(s - jnp.max(s, axis=-1, keepdims=True))
        probs.append((e.astype(jnp.bfloat16), jnp.sum(e, axis=-1, keepdims=True)))
    oc = []
    for h, (p, l) in enumerate(probs):
        oc.append(_dot(p, mkv_ref[:, WIDTH_C + h * MEM_HEAD_DIM:WIDTH_C + (h + 1) * MEM_HEAD_DIM]) / l)
    oc = jnp.concatenate(oc, axis=-1).astype(jnp.bfloat16)

    y_a = _dot(na_ref[...], wa_ref[...])
    y_b = _dot(ob.astype(jnp.bfloat16), wb_ref[...])
    y_c = _dot(oc, wc_ref[...])
    merged = (gate_ref[:, 0:D_MODEL].astype(jnp.float32) * y_a
              + gate_ref[:, D_MODEL:2 * D_MODEL].astype(jnp.float32) * y_b
              + gate_ref[:, 2 * D_MODEL:3 * D_MODEL].astype(jnp.float32) * y_c)
    x1 = x_ref[...] + _dot(merged.astype(jnp.bfloat16), wo_ref[...])
    x1_ref[...] = x1

    h = _rms(x1, fg_ref[...])
    h_ref[...] = h.astype(jnp.bfloat16)
    logits = lax.dot_general(rw_ref[...], h, (((1,), (1,)), ((), ())), preferred_element_type=jnp.float32,
                             precision=lax.Precision.HIGHEST) + rb_ref[...]
    expert = lax.broadcasted_iota(jnp.int32, (N_EXPERTS, T), 0)
    vals, idxs = [], []
    for _ in range(TOP_K):
        mx = jnp.max(logits, axis=0, keepdims=True)
        sel = jnp.min(jnp.where(logits == mx, expert, N_EXPERTS), axis=0, keepdims=True)
        vals.append(mx)
        idxs.append(sel)
        logits = jnp.where(expert == sel, -jnp.inf, logits)
    ev = [jnp.exp(v - vals[0]) for v in vals]
    den = ev[0] + ev[1] + ev[2] + ev[3]
    idx_ref[...] = jnp.concatenate(idxs, axis=0)
    wt_ref[...] = jnp.concatenate([e / den for e in ev], axis=0)
    member = sum((expert == sel).astype(jnp.float32) for sel in idxs)
    for s in range(T // MOE_TILE):
        cnt = jnp.sum(member[:, s * MOE_TILE:(s + 1) * MOE_TILE], axis=1, keepdims=True)
        seg = jnp.floor((cnt + (MOE_PIECE - 1)) * (1.0 / MOE_PIECE)) * MOE_PIECE
        seg_ref[s] = jnp.broadcast_to(seg, (N_EXPERTS, LANES))


def _mix_out(x2d, na_out, dil_outs, qc, gates, mkv, wa, wb, wc, wo, ffn_g, router_w, router_b, batch, seq):
    n = x2d.shape[0]
    T = PROJ_TILE
    tiles_per_batch = seq // T
    mem_len = mkv.shape[0] // batch
    row = lambda w: pl.BlockSpec((T, w), lambda i: (i, 0))
    full = lambda a: pl.BlockSpec(a.shape, lambda i: (0,) * a.ndim)
    in_specs = [row(D_MODEL), row(WIDTH_A)]
    args = [x2d, na_out]
    for o, l in dil_outs:
        pair_rows = pl.BlockSpec((WIDTH_G // LANES, T, LANES), lambda i: (0, i, 0))
        in_specs += [pair_rows, pair_rows]
        args += [o, l]
    in_specs += [row(WIDTH_C), row(3 * D_MODEL),
                 pl.BlockSpec((mem_len, 2 * WIDTH_C), lambda i: (i // tiles_per_batch, 0)),
                 full(wa), full(wb), full(wc), full(wo), full(ffn_g), full(router_w), full(router_b)]
    args += [qc, gates, mkv, wa, wb, wc, wo, ffn_g, router_w, router_b]
    return pl.pallas_call(
        _mix_kernel,
        grid=(n // T,),
        in_specs=in_specs,
        out_specs=[row(D_MODEL), row(D_MODEL), pl.BlockSpec((TOP_K, T), lambda i: (0, i)),
                   pl.BlockSpec((TOP_K, T), lambda i: (0, i)),
                   pl.BlockSpec((T // MOE_TILE, N_EXPERTS, LANES), lambda i: (i, 0, 0))],
        out_shape=[jax.ShapeDtypeStruct((n, D_MODEL), jnp.float32),
                   jax.ShapeDtypeStruct((n, D_MODEL), jnp.bfloat16),
                   jax.ShapeDtypeStruct((TOP_K, n), jnp.int32),
                   jax.ShapeDtypeStruct((TOP_K, n), jnp.float32),
                   jax.ShapeDtypeStruct((n // MOE_TILE, N_EXPERTS, LANES), jnp.float32)],
        compiler_params=_cparams(1),
        name="mix_out",
    )(*args)


def _route_kernel(idx_ref, segs_ref, lpos_ref, ptab_ref, be_ref, pe_ref, carry_ref, pstart_ref):
    i = pl.program_id(0)
    T = idx_ref.shape[1]
    n_blocks = be_ref.shape[1]
    expert = lax.broadcasted_iota(jnp.int32, (N_EXPERTS, T), 0)
    idx = idx_ref[...]
    onehots = [expert == idx[k:k + 1, :] for k in range(TOP_K)]
    member = sum(o.astype(jnp.float32) for o in onehots)
    seg_b = segs_ref[i]
    seg = seg_b[:, 0:1]
    r = lax.broadcasted_iota(jnp.int32, (N_EXPERTS, N_EXPERTS), 0)
    c = lax.broadcasted_iota(jnp.int32, (N_EXPERTS, N_EXPERTS), 1)
    lower_incl = (r >= c).astype(jnp.float32)

    def cumsum_experts(v):
        return jnp.dot(lower_incl, v, preferred_element_type=jnp.float32, precision=lax.Precision.HIGHEST)

    @pl.when(i == 0)
    def _():
        counts = jnp.sum(segs_ref[...], axis=0)
        padded = jnp.floor((counts + (MOE_BLOCK - 1)) * (1.0 / MOE_BLOCK)) * MOE_BLOCK
        pad_ends = cumsum_experts(padded)
        pstart_ref[...] = pad_ends - padded
        carry_ref[...] = jnp.zeros_like(carry_ref)
        pe_ref[...] = pad_ends[:, 0:1].astype(jnp.int32)
        blk_start = (lax.broadcasted_iota(jnp.int32, (N_EXPERTS, n_blocks), 1) * MOE_BLOCK).astype(jnp.float32)
        be = jnp.sum((pad_ends[:, 0:1] <= blk_start).astype(jnp.int32), axis=0, keepdims=True)
        be_ref[...] = jnp.minimum(be, N_EXPERTS - 1)

    local_end = cumsum_experts(seg_b)[:, 0:1]
    local_start = local_end - seg
    global_start = pstart_ref[:, 0:1] + carry_ref[:, 0:1]
    rr = lax.broadcasted_iota(jnp.int32, (T, T), 0)
    cc = lax.broadcasted_iota(jnp.int32, (T, T), 1)
    earlier = (rr < cc).astype(jnp.bfloat16)
    before = _dot(member.astype(jnp.bfloat16), earlier)
    base = local_start + before
    lpos = [jnp.sum(jnp.where(o, base, 0.0), axis=0, keepdims=True) for o in onehots]
    lpos_ref[...] = jnp.concatenate(lpos, axis=0).astype(jnp.int32)

    inv = 1.0 / MOE_PIECE
    piece = lax.broadcasted_iota(jnp.int32, (N_EXPERTS, MOE_PIECE_TAB), 1).astype(jnp.float32)
    first = local_start * inv
    inside = jnp.logical_and(piece >= first, piece < local_end * inv)
    dst = jnp.sum(jnp.where(inside, global_start * inv + piece - first, 0.0), axis=0, keepdims=True)
    n_pieces = local_end[N_EXPERTS - 1:N_EXPERTS, :] * inv
    spare = (n_blocks * MOE_BLOCK // MOE_PIECE
             + (i % 2) * (MOE_TILE_ROWS // MOE_PIECE)).astype(jnp.float32)
    dst = jnp.where(piece[0:1, :] >= n_pieces, spare + piece[0:1, :], dst)
    dst = jnp.where(piece[0:1, :] == MOE_PIECE_TAB - 1, n_pieces, dst)
    ptab_ref[...] = dst.astype(jnp.int32)
    carry_ref[...] += seg


def _moe_route(topk_idx, segs, n_blocks):
    n = topk_idx.shape[1]
    T = MOE_TILE
    acc = pltpu.VMEM((N_EXPERTS, LANES), jnp.float32)
    return pl.pallas_call(
        _route_kernel,
        grid=(n // T,),
        in_specs=[pl.BlockSpec((TOP_K, T), lambda i: (0, i)),
                  pl.BlockSpec(segs.shape, lambda i: (0, 0, 0))],
        out_specs=[pl.BlockSpec((TOP_K, T), lambda i: (0, i)),
                   pl.BlockSpec((None, 1, MOE_PIECE_TAB), lambda i: (i, 0, 0)),
                   pl.BlockSpec((1, n_blocks), lambda i: (0, 0)),
                   pl.BlockSpec((N_EXPERTS, 1), lambda i: (0, 0))],
        out_shape=[jax.ShapeDtypeStruct((TOP_K, n), jnp.int32),
                   jax.ShapeDtypeStruct((n // T, 1, MOE_PIECE_TAB), jnp.int32),
                   jax.ShapeDtypeStruct((1, n_blocks), jnp.int32),
                   jax.ShapeDtypeStruct((N_EXPERTS, 1), jnp.int32)],
        scratch_shapes=[acc, acc],
        compiler_params=_cparams(1),
        name="moe_route",
    )(topk_idx, segs)


def _piece(ref, p):
    return ref.at[pl.ds(pl.multiple_of(p * MOE_PIECE, MOE_PIECE), MOE_PIECE), :]


def _dispatch_kernel(pe_ref, ptab_ref, lpos_ref, h_ref, xs_hbm, zbuf, pbuf, sem, zsem):
    i = pl.program_id(0)
    T = h_ref.shape[0]
    half = h_ref.shape[1] // 2

    def zero_copy(e):
        end = pe_ref[e]
        start = pl.multiple_of(end - MOE_BLOCK, MOE_BLOCK)
        return pltpu.make_async_copy(zbuf, xs_hbm.at[pl.ds(start, MOE_BLOCK), :], zsem)

    def nonempty(e):
        return pe_ref[e] > (pe_ref[e - 1] if e else 0)

    @pl.when(i == 0)
    def _():
        zbuf[...] = jnp.zeros_like(zbuf)
        for e in range(N_EXPERTS):
            @pl.when(nonempty(e))
            def _():
                zero_copy(e).start()
        for e in range(N_EXPERTS):
            @pl.when(nonempty(e))
            def _():
                zero_copy(e).wait()

        def tail_copy(b):
            return pltpu.make_async_copy(
                zbuf, xs_hbm.at[pl.ds(pl.multiple_of(b * MOE_BLOCK, MOE_BLOCK), MOE_BLOCK), :], zsem)
        first_tail = pe_ref[N_EXPERTS - 1] // MOE_BLOCK
        n_blocks = xs_hbm.shape[0] // MOE_BLOCK
        lax.fori_loop(first_tail, n_blocks, lambda b, c: (tail_copy(b).start(), c)[1], 0)
        lax.fori_loop(first_tail, n_blocks, lambda b, c: (tail_copy(b).wait(), c)[1], 0)

    row = lax.broadcasted_iota(jnp.int32, (MOE_TILE_ROWS, T), 0)
    lpos = lpos_ref[...]
    perm = sum((row == lpos[k:k + 1, :]).astype(jnp.bfloat16) for k in range(TOP_K))
    lo = lax.bitcast_convert_type(_dot(perm, h_ref[:, :half]), jnp.uint32)
    hi = lax.bitcast_convert_type(_dot(perm, h_ref[:, half:]), jnp.uint32)
    slot = i % 2
    pbuf[slot] = hi | (lo >> 16)

    n_tile_pieces = MOE_TILE_ROWS // MOE_PIECE

    def piece_copy(s, p):
        return pltpu.make_async_copy(_piece(pbuf.at[s], p), _piece(xs_hbm, ptab_ref[p]), sem.at[s])

    def wait_all(s):
        lax.fori_loop(0, n_tile_pieces, lambda p, c: (piece_copy(s, 0).wait(), c)[1], 0, unroll=8)

    lax.fori_loop(0, n_tile_pieces, lambda p, c: (piece_copy(slot, p).start(), c)[1], 0, unroll=8)

    @pl.when(i > 0)
    def _():
        wait_all(1 - slot)

    @pl.when(i == pl.num_programs(0) - 1)
    def _():
        wait_all(slot)


def _moe_dispatch(h, lpos, ptab, pad_ends, n_blocks):
    n, width = h.shape
    T = MOE_TILE
    n_steps = n // T

    def kernel(pe_ref, ptab_ref, *rest):
        _dispatch_kernel(pe_ref, ptab_ref.at[0], *rest)

    grid_spec = pltpu.PrefetchScalarGridSpec(
        num_scalar_prefetch=1,
        grid=(n_steps,),
        in_specs=[pl.BlockSpec((None, 1, MOE_PIECE_TAB), lambda i, pe: (i, 0, 0), memory_space=pltpu.SMEM),
                  pl.BlockSpec((TOP_K, T), lambda i, pe: (0, i)),
                  pl.BlockSpec((T, width), lambda i, pe: (i, 0))],
        out_specs=pl.BlockSpec(memory_space=pl.ANY),
        scratch_shapes=[pltpu.VMEM((MOE_BLOCK, width // 2), jnp.uint32),
                        pltpu.VMEM((2, MOE_TILE_ROWS, width // 2), jnp.uint32),
                        pltpu.SemaphoreType.DMA((2,)),
                        pltpu.SemaphoreType.DMA(())],
    )
    return pl.pallas_call(
        kernel,
        grid_spec=grid_spec,
        out_shape=jax.ShapeDtypeStruct(((n_blocks + MOE_SPARE_BLOCKS) * MOE_BLOCK, width // 2), jnp.uint32),
        compiler_params=_cparams(1),
        name="moe_dispatch",
    )(pad_ends, ptab, lpos, h)


def _moe_kernel(be_ref, pe_ref, xs_ref, wgu_ref, bgu_ref, wdn_ref, bdn_ref, y_ref, wgu_bf, wdn_bf):
    i = pl.program_id(0)
    n_valid = pe_ref[N_EXPERTS - 1] // MOE_BLOCK
    new_expert = jnp.logical_or(i == 0, be_ref[i] != be_ref[jnp.maximum(i - 1, 0)])

    @pl.when(jnp.logical_and(i < n_valid, new_expert))
    def _():
        wgu_bf[...] = wgu_ref[0].astype(jnp.bfloat16)
        wdn_bf[...] = wdn_ref[0].astype(jnp.bfloat16)

    @pl.when(i < n_valid)
    def _():
        x = _unpack_bf16_pairs(xs_ref[...])
        gu = _dot(x, wgu_bf[...]) + bgu_ref[0]
        g = jnp.minimum(gu[:, :D_FF], SWIGLU_LIMIT)
        u = jnp.clip(gu[:, D_FF:], -SWIGLU_LIMIT, SWIGLU_LIMIT)
        a = (u + 1.0) * (g * _sigmoid(SWIGLU_ALPHA * g))
        y_ref[...] = _pack_bf16_pairs(_dot(a.astype(jnp.bfloat16), wdn_bf[...]) + bdn_ref[0])

    @pl.when(i >= n_valid)
    def _():
        y_ref[...] = jnp.zeros_like(y_ref)


def _moe_experts(xs, block_expert, pad_ends, wgu, bgu, wdn, bdn):
    n_blocks = block_expert.shape[0]

    def row_blk(i, be, pe):
        return (jnp.minimum(i, jnp.maximum(pe[N_EXPERTS - 1] // MOE_BLOCK - 1, 0)), 0)

    grid_spec = pltpu.PrefetchScalarGridSpec(
        num_scalar_prefetch=2,
        grid=(n_blocks,),
        in_specs=[pl.BlockSpec((MOE_BLOCK, D_MODEL // 2), row_blk),
                  pl.BlockSpec((1, D_MODEL, 2 * D_FF), lambda i, be, pe: (be[i], 0, 0)),
                  pl.BlockSpec((1, 1, 2 * D_FF), lambda i, be, pe: (be[i], 0, 0)),
                  pl.BlockSpec((1, D_FF, D_MODEL), lambda i, be, pe: (be[i], 0, 0)),
                  pl.BlockSpec((1, 1, D_MODEL), lambda i, be, pe: (be[i], 0, 0))],
        out_specs=pl.BlockSpec((MOE_BLOCK, D_MODEL // 2), lambda i, be, pe: (i, 0)),
        scratch_shapes=[pltpu.VMEM((D_MODEL, 2 * D_FF), jnp.bfloat16),
                        pltpu.VMEM((D_FF, D_MODEL), jnp.bfloat16)],
    )
    return pl.pallas_call(
        _moe_kernel,
        grid_spec=grid_spec,
        out_shape=jax.ShapeDtypeStruct((n_blocks * MOE_BLOCK, D_MODEL // 2), jnp.uint32),
        compiler_params=_cparams(1),
        name="moe_experts",
    )(block_expert, pad_ends, xs, wgu, bgu, wdn, bdn)


def _gather_pieces(src_hbm, ptab_ref, dst_ref, sem):
    spare = src_hbm.shape[0] // MOE_PIECE

    def body(p, carry):
        src = ptab_ref[p]
        src = jnp.where(src >= spare, 0, src)
        pltpu.make_async_copy(_piece(src_hbm, src), _piece(dst_ref, p), sem).start()
        return carry
    lax.fori_loop(0, MOE_TILE_ROWS // MOE_PIECE, body, 0, unroll=8)


def _wait_pieces(src_hbm, dst_ref, sem):
    def body(p, carry):
        pltpu.make_async_copy(_piece(src_hbm, 0), _piece(dst_ref, 0), sem).wait()
        return carry
    lax.fori_loop(0, MOE_TILE_ROWS // MOE_PIECE, body, 0, unroll=8)


def _combine_kernel(ptab_ref, ptab_next_ref, ys_hbm, lpos_ref, wt_ref, x1_ref, g_ref, out_ref, gbuf, sem):
    i = pl.program_id(0)
    n_steps = pl.num_programs(0)
    slot = i % 2
    T = x1_ref.shape[0]

    @pl.when(i == 0)
    def _():
        _gather_pieces(ys_hbm, ptab_ref, gbuf.at[0], sem.at[0])

    @pl.when(i + 1 < n_steps)
    def _():
        _gather_pieces(ys_hbm, ptab_next_ref, gbuf.at[1 - slot], sem.at[1 - slot])

    _wait_pieces(ys_hbm, gbuf.at[slot], sem.at[slot])
    rows = _unpack_bf16_pairs(gbuf[slot])
    col = lax.broadcasted_iota(jnp.int32, (T, MOE_TILE_ROWS), 1)
    lpos = lpos_ref[...]
    wt = wt_ref[...]
    cmb = jnp.zeros((T, MOE_TILE_ROWS), jnp.float32)
    for k in range(TOP_K):
        cmb = jnp.where(col == lpos[:, k:k + 1], wt[:, k:k + 1], cmb)
    hi = cmb.astype(jnp.bfloat16)
    lo = (cmb - hi.astype(jnp.float32)).astype(jnp.bfloat16)
    y2 = _dot(jnp.concatenate([hi, lo], axis=0), rows)
    y = x1_ref[...] + y2[:T] + y2[T:]
    out_ref[...] = _rms(y, g_ref[...])


def _moe_combine(ys, ptab, lpos_t, wts, x1, final_g):
    n = x1.shape[0]
    T = MOE_TILE
    n_steps = n // T
    smem_blk = lambda shift: pl.BlockSpec(
        (None, 1, MOE_PIECE_TAB), lambda i: (jnp.minimum(i + shift, n_steps - 1), 0, 0),
        memory_space=pltpu.SMEM)

    def kernel(ptab_ref, ptab_next_ref, *rest):
        _combine_kernel(ptab_ref.at[0], ptab_next_ref.at[0], *rest)

    return pl.pallas_call(
        kernel,
        grid=(n_steps,),
        in_specs=[smem_blk(0), smem_blk(1),
                  pl.BlockSpec(memory_space=pl.ANY),
                  pl.BlockSpec((T, TOP_K), lambda i: (i, 0)),
                  pl.BlockSpec((T, TOP_K), lambda i: (i, 0)),
                  pl.BlockSpec((T, D_MODEL), lambda i: (i, 0)),
                  pl.BlockSpec((1, D_MODEL), lambda i: (0, 0))],
        out_specs=pl.BlockSpec((T, D_MODEL), lambda i: (i, 0)),
        out_shape=jax.ShapeDtypeStruct((n, D_MODEL), jnp.float32),
        scratch_shapes=[pltpu.VMEM((2, MOE_TILE_ROWS, D_MODEL // 2), jnp.uint32),
                        pltpu.SemaphoreType.DMA((2,))],
        compiler_params=_cparams(1),
        name="moe_combine",
    )(ptab, ptab, ys, lpos_t, wts, x1, final_g)


def kernel(x, mem, attn_norm_g, mem_norm_g, w_in, w_mem_kv, na_rpb, t5_rel_bias, w_branch_a, w_branch_b,
           w_branch_c, w_out, ffn_norm_g, router_w, router_b, expert_w_gate_up, expert_b_gate_up,
           expert_w_down, expert_b_down, final_norm_g):
    B, S, D = x.shape
    depth = w_in.shape[0]
    bf = jnp.bfloat16
    x2d = x.reshape(B * S, D)
    mem2d = mem.reshape(B * mem.shape[1], D)
    assert depth == 1, "single-layer block (per-layer parameters carry a leading axis of 1)"
    l = 0
    na_bias, dil_bias = _bias_prep(na_rpb[l], t5_rel_bias)
    mkv = _mem_kv(mem2d, mem_norm_g[l][None], w_mem_kv[l].astype(bf))
    na_qkv, d0, d1, d2, qc = _in_proj(x2d, attn_norm_g[l][None], w_in[l][:, :OFF_GATE].astype(bf), B, S)
    gates = _gate_proj(x2d, attn_norm_g[l][None], w_in[l][:, OFF_GATE:].astype(bf))
    na_out = _na_attn(na_qkv, na_bias, B, S)
    dil_outs = [_dil_attn(qkv, dil_bias, gi, dil, B, S)
                for gi, (qkv, (_, dil)) in enumerate(zip((d0, d1, d2), DIL_GROUPS))]
    x1, h, topk_idx, topk_w, segs = _mix_out(
        x2d, na_out, dil_outs, qc, gates, mkv, w_branch_a[l].astype(bf), w_branch_b[l].astype(bf),
        w_branch_c[l].astype(bf), w_out[l].astype(bf), ffn_norm_g[l][None], router_w[l].T,
        router_b[l][:, None], B, S)

    n = B * S
    max_rows = n * TOP_K + (n // MOE_TILE) * N_EXPERTS * (MOE_PIECE - 1)
    n_blocks = -(-max_rows // MOE_BLOCK) + N_EXPERTS
    lpos, ptab, block_expert, pad_ends = _moe_route(topk_idx, segs, n_blocks)
    block_expert = block_expert.reshape(n_blocks)
    pad_ends = pad_ends.reshape(N_EXPERTS)
    xs = _moe_dispatch(h, lpos, ptab, pad_ends, n_blocks)
    ys = _moe_experts(xs, block_expert, pad_ends, expert_w_gate_up[l], expert_b_gate_up[l][:, None, :],
                      expert_w_down[l], expert_b_down[l][:, None, :])
    out = _moe_combine(ys, ptab, lpos.T, topk_w.T, x1, final_norm_g[None])
    return out.reshape(B, S, D)
```

```python
import functools
import math

import numpy as np
import jax
import jax.numpy as jnp
from jax import lax
from jax.experimental import pallas as pl
from jax.experimental.pallas import tpu as pltpu

D_MODEL = 1024
GRID_W = 64
MEM_HEADS = 4
MEM_HEAD_DIM = 128
HEAD_DIM = 64
NA_HEADS = 8
NA_WIN_ROWS = 8
NA_WIN_COLS = 16
DIL_GROUPS = ((128, 1), (512, 4), (2048, 16))
DIL_HEADS_PER_GROUP = 4
DIL_BLOCK = 128
T5_BUCKETS = 32
T5_MAX_DIST = 1024
N_EXPERTS = 32
TOP_K = 4
D_FF = D_MODEL
SWIGLU_ALPHA = 1.702
SWIGLU_LIMIT = 7.0
RMS_EPS = 1e-6
NEG_INF = -1e30

WIDTH_A = NA_HEADS * HEAD_DIM
WIDTH_B = DIL_HEADS_PER_GROUP * len(DIL_GROUPS) * HEAD_DIM
WIDTH_G = DIL_HEADS_PER_GROUP * HEAD_DIM
WIDTH_C = MEM_HEADS * MEM_HEAD_DIM
OFF_QA, OFF_KA, OFF_VA = 0, WIDTH_A, 2 * WIDTH_A
OFF_QB = 3 * WIDTH_A
OFF_KB = OFF_QB + WIDTH_B
OFF_VB = OFF_KB + WIDTH_B
OFF_QC = OFF_VB + WIDTH_B
OFF_GATE = OFF_QC + WIDTH_C
IN_COLS = OFF_GATE + 3 * D_MODEL

LANES = 128
VMEM_LIMIT = 56 * 1024 * 1024

IN_PROJ_TILE = 1024
PROJ_TILE = 512
NA_ROWS_PER_STEP = 8
DIL_STEP_TOKENS = 2048
MOE_BLOCK = 512
MOE_TILE = 256
MOE_PIECE = 8
MOE_TILE_ROWS = -(-(TOP_K * MOE_TILE + N_EXPERTS * (MOE_PIECE - 1)) // 256) * 256
MOE_PIECE_TAB = 256
assert MOE_TILE_ROWS // MOE_PIECE < MOE_PIECE_TAB
MOE_SPARE_BLOCKS = -(-2 * MOE_TILE_ROWS // MOE_BLOCK)


def _cparams(n_axes):
    return pltpu.CompilerParams(
        dimension_semantics=("arbitrary",) * n_axes, vmem_limit_bytes=VMEM_LIMIT)


def _rms(x, g):
    return x * lax.rsqrt(jnp.mean(x * x, axis=-1, keepdims=True) + RMS_EPS) * g


def _sigmoid(x):
    return 1.0 / (1.0 + jnp.exp(-x))


def _pack_bf16_pairs(x):
    w = x.shape[1] // 2
    bits = lax.bitcast_convert_type(x.astype(jnp.bfloat16).astype(jnp.float32), jnp.uint32)
    return (bits[:, w:] & jnp.uint32(0xFFFF0000)) | (bits[:, :w] >> 16)


def _unpack_bf16_pairs(p):
    lo = lax.bitcast_convert_type(p << 16, jnp.float32)
    hi = lax.bitcast_convert_type(p & jnp.uint32(0xFFFF0000), jnp.float32)
    return jnp.concatenate([lo, hi], axis=1).astype(jnp.bfloat16)


def _dot(a, b):
    return jnp.dot(a, b, preferred_element_type=jnp.float32)


def _dot_nt(a, b):
    return lax.dot_general(a, b, (((1,), (1,)), ((), ())), preferred_element_type=jnp.float32)


def _mem_kv_kernel(mem_ref, g_ref, w_ref, out_ref):
    h = _rms(mem_ref[...], g_ref[...]).astype(jnp.bfloat16)
    out_ref[...] = _dot(h, w_ref[...]).astype(jnp.bfloat16)


def _mem_kv(mem2d, g, w):
    rows = mem2d.shape[0]
    return pl.pallas_call(
        _mem_kv_kernel,
        out_shape=jax.ShapeDtypeStruct((rows, 2 * WIDTH_C), jnp.bfloat16),
        name="mem_kv",
        compiler_params=pltpu.CompilerParams(vmem_limit_bytes=VMEM_LIMIT),
    )(mem2d, g, w)


def _in_proj_kernel(x_ref, g_ref, w_ref, na_ref, d0_ref, d1_ref, d2_ref, qc_ref, acc_ref):
    T = x_ref.shape[0]
    h = _rms(x_ref[...], g_ref[...]).astype(jnp.bfloat16)
    q_scale = HEAD_DIM ** -0.5

    for c in range(3):
        acc = _dot(h, w_ref[:, c * WIDTH_A:(c + 1) * WIDTH_A])
        if c == 0:
            acc = acc * q_scale
        na_ref[:, c * WIDTH_A:(c + 1) * WIDTH_A] = acc.astype(jnp.bfloat16)

    for gi, ((_, dil), d_ref) in enumerate(zip(DIL_GROUPS, (d0_ref, d1_ref, d2_ref))):
        for kind, off in enumerate((OFF_QB, OFF_KB, OFF_VB)):
            lo = off + gi * WIDTH_G
            acc = _dot(h, w_ref[:, lo:lo + WIDTH_G])
            if kind == 0:
                acc = acc * q_scale
            cols = slice(kind * WIDTH_G, (kind + 1) * WIDTH_G)
            if dil == 1:
                d_ref[0, 0, :, cols] = acc.astype(jnp.bfloat16)
            else:
                for c in range(WIDTH_G // LANES):
                    acc_ref[c] = acc[:, c * LANES:(c + 1) * LANES]
                for rho in range(dil):
                    d_ref[0, rho, :, cols] = jnp.concatenate(
                        [acc_ref[c, pl.ds(rho, T // dil, stride=dil), :] for c in range(WIDTH_G // LANES)],
                        axis=-1).astype(jnp.bfloat16)

    qc_ref[...] = _dot(h, w_ref[:, OFF_QC:OFF_QC + WIDTH_C]).astype(jnp.bfloat16)


def _gate_proj_kernel(x_ref, g_ref, w_ref, gate_ref):
    h = _rms(x_ref[...], g_ref[...]).astype(jnp.bfloat16)
    chunk = 512
    first = OFF_GATE - IN_COLS // 2
    for c in range(3 * D_MODEL // chunk):
        lo = first + c * chunk
        gate_ref[:, c * chunk:(c + 1) * chunk] = _sigmoid(_dot(h, w_ref[:, lo:lo + chunk])).astype(jnp.bfloat16)


def _gate_proj(x2d, g, w_bf16):
    n = x2d.shape[0]
    T = IN_PROJ_TILE
    assert (IN_COLS // 2) % LANES == 0 and IN_COLS // 2 <= OFF_GATE
    return pl.pallas_call(
        _gate_proj_kernel,
        grid=(n // T,),
        in_specs=[pl.BlockSpec((T, D_MODEL), lambda i: (i, 0)),
                  pl.BlockSpec((1, D_MODEL), lambda i: (0, 0)),
                  pl.BlockSpec((D_MODEL, IN_COLS // 2), lambda i: (0, 1))],
        out_specs=pl.BlockSpec((T, 3 * D_MODEL), lambda i: (i, 0)),
        out_shape=jax.ShapeDtypeStruct((n, 3 * D_MODEL), jnp.bfloat16),
        compiler_params=_cparams(1),
        name="gate_proj",
    )(x2d, g, w_bf16)


def _in_proj(x2d, g, w_bf16, batch, seq):
    n = x2d.shape[0]
    T = IN_PROJ_TILE
    tiles_per_batch = seq // T
    out_shape = [jax.ShapeDtypeStruct((n, 3 * WIDTH_A), jnp.bfloat16)]
    out_specs = [pl.BlockSpec((T, 3 * WIDTH_A), lambda i: (i, 0))]
    for _, dil in DIL_GROUPS:
        out_shape.append(jax.ShapeDtypeStruct((batch, dil, seq // dil, 3 * WIDTH_G), jnp.bfloat16))
        out_specs.append(pl.BlockSpec((1, dil, T // dil, 3 * WIDTH_G),
                                      lambda i: (i // tiles_per_batch, 0, i % tiles_per_batch, 0)))
    out_shape += [jax.ShapeDtypeStruct((n, WIDTH_C), jnp.bfloat16)]
    out_specs += [pl.BlockSpec((T, WIDTH_C), lambda i: (i, 0))]
    return pl.pallas_call(
        _in_proj_kernel,
        grid=(n // T,),
        in_specs=[pl.BlockSpec((T, D_MODEL), lambda i: (i, 0)),
                  pl.BlockSpec((1, D_MODEL), lambda i: (0, 0)),
                  pl.BlockSpec((D_MODEL, OFF_GATE), lambda i: (0, 0))],
        out_specs=out_specs,
        out_shape=out_shape,
        scratch_shapes=[pltpu.VMEM((WIDTH_G // LANES, T, LANES), jnp.float32)],
        compiler_params=_cparams(1),
        name="in_proj",
    )(x2d, g, w_bf16)


def _na_kernel(q_ref, k_ref, v_ref, bias_ref, out_ref, *, rows):
    i = pl.program_id(1)
    lane = lax.broadcasted_iota(jnp.int32, (GRID_W, LANES), 1)
    low_half = lane < HEAD_DIM
    n_win = NA_WIN_ROWS * GRID_W

    def row_body(j, carry):
        qr = i * NA_ROWS_PER_STEP + j
        rs = jnp.clip(qr - NA_WIN_ROWS // 2, 0, rows - NA_WIN_ROWS)
        shift = rs - qr + NA_WIN_ROWS - 1
        q_off = pl.multiple_of(j * GRID_W, GRID_W)
        k_off = pl.multiple_of(rs * GRID_W, GRID_W)
        scores = []
        for hp in range(NA_HEADS // 2):
            cols = slice(hp * LANES, (hp + 1) * LANES)
            qp = q_ref[pl.ds(q_off, GRID_W), cols]
            kp = k_ref[pl.ds(k_off, n_win), cols]
            zero = jnp.zeros_like(qp)
            q2 = jnp.concatenate([jnp.where(low_half, qp, zero), jnp.where(low_half, zero, qp)], axis=0)
            bias = jnp.concatenate(
                [jnp.concatenate([bias_ref[shift + 2 * p, 2 * hp + hh] for p in range(NA_WIN_ROWS // 2)], axis=-1)
                 for hh in range(2)], axis=0)
            scores.append(_dot_nt(q2, kp) + bias)
        probs = []
        for s in scores:
            e = jnp.exp(s - jnp.max(s, axis=-1, keepdims=True))
            probs.append((e.astype(jnp.bfloat16), jnp.sum(e, axis=-1, keepdims=True)))
        for hp, (p, l) in enumerate(probs):
            cols = slice(hp * LANES, (hp + 1) * LANES)
            o2 = _dot(p, v_ref[pl.ds(k_off, n_win), cols]) / l
            out_ref[pl.ds(q_off, GRID_W), cols] = jnp.where(
                low_half, o2[:GRID_W], o2[GRID_W:]).astype(jnp.bfloat16)
        return carry

    lax.fori_loop(0, NA_ROWS_PER_STEP, row_body, 0, unroll=2)


def _na_attn(na_qkv, bias_tab, batch, seq):
    rows = seq // GRID_W
    steps = rows // NA_ROWS_PER_STEP
    tq = NA_ROWS_PER_STEP * GRID_W
    n = na_qkv.shape[0]
    return pl.pallas_call(
        functools.partial(_na_kernel, rows=rows),
        grid=(batch, steps),
        in_specs=[pl.BlockSpec((tq, WIDTH_A), lambda b, i: (b * steps + i, 0)),
                  pl.BlockSpec((seq, WIDTH_A), lambda b, i: (b, 1)),
                  pl.BlockSpec((seq, WIDTH_A), lambda b, i: (b, 2)),
                  pl.BlockSpec(bias_tab.shape, lambda b, i: (0, 0, 0, 0))],
        out_specs=pl.BlockSpec((tq, WIDTH_A), lambda b, i: (b * steps + i, 0)),
        out_shape=jax.ShapeDtypeStruct((n, WIDTH_A), jnp.bfloat16),
        compiler_params=_cparams(2),
        name="na_attn",
    )(na_qkv, na_qkv, na_qkv, bias_tab)


def _t5_bucket_np(rel):
    half = T5_BUCKETS // 2
    max_exact = half // 2
    ret = np.where(rel > 0, half, 0)
    n = np.abs(rel)
    nf = np.maximum(n, 1).astype(np.float32)
    large = max_exact + (np.log(nf / np.float32(max_exact)) / np.float32(math.log(T5_MAX_DIST / max_exact))
                         * np.float32(half - max_exact)).astype(np.int32)
    large = np.minimum(large, half - 1)
    return ret + np.where(n < max_exact, n, large)


def _dil_bucket_table():
    q = np.arange(DIL_BLOCK)[:, None]
    j = np.arange(2 * DIL_BLOCK)[None, :]
    tabs = []
    for window, dil in DIL_GROUPS:
        half = (window // 2) // dil
        assert half == DIL_BLOCK // 2
        rel = (j - half) - q
        tabs.append(np.where(np.abs(rel) <= half, _t5_bucket_np(rel * dil), -1))
    return np.stack(tabs).astype(np.int32)


def _bias_prep_kernel(rpb_ref, t5_ref, bucket_ref, na_ref, dil_ref):
    n_dc = 2 * NA_WIN_COLS - 1
    n_dr = 2 * NA_WIN_ROWS - 1
    qc = lax.broadcasted_iota(jnp.int32, (GRID_W, 2 * GRID_W), 0)
    lane = lax.broadcasted_iota(jnp.int32, (GRID_W, 2 * GRID_W), 1)
    second = lane >= GRID_W
    kc = jnp.where(second, lane - GRID_W, lane)
    diff = kc - qc
    cs = jnp.clip(qc - NA_WIN_COLS // 2, 0, GRID_W - NA_WIN_COLS)
    valid = jnp.logical_and(kc >= cs, kc < cs + NA_WIN_COLS)

    def na_tile(mh, carry):
        m = mh // NA_HEADS
        h = mh - m * NA_HEADS
        base = (h * n_dr + m) * n_dc
        acc = jnp.zeros((GRID_W, 2 * GRID_W), jnp.float32)
        for d in range(n_dc):
            v = jnp.where(second, rpb_ref[base + n_dc + d], rpb_ref[base + d])
            acc = jnp.where(diff == d - (NA_WIN_COLS - 1), v, acc)
        na_ref[m, h] = jnp.where(valid, acc, NEG_INF)
        return carry

    lax.fori_loop(0, (n_dr - 1) * NA_HEADS, na_tile, 0)

    j = lax.broadcasted_iota(jnp.int32, (DIL_BLOCK, 2 * DIL_BLOCK), 1)
    half = DIL_BLOCK // 2
    n_heads = DIL_HEADS_PER_GROUP * len(DIL_GROUPS)
    for gi in range(len(DIL_GROUPS)):
        bucket = bucket_ref[gi]
        in_win = bucket >= 0
        for h in range(DIL_HEADS_PER_GROUP):
            acc = jnp.zeros((DIL_BLOCK, 2 * DIL_BLOCK), jnp.float32)
            for b in range(T5_BUCKETS):
                acc = jnp.where(bucket == b, t5_ref[b * n_heads + gi * DIL_HEADS_PER_GROUP + h], acc)
            for v, ok in enumerate((jnp.logical_and(in_win, j >= half), in_win,
                                    jnp.logical_and(in_win, j < 2 * DIL_BLOCK - half))):
                dil_ref[gi, v, h] = jnp.where(ok, acc, NEG_INF)


def _bias_prep(rpb, t5):
    smem = pl.BlockSpec(memory_space=pltpu.SMEM)
    return pl.pallas_call(
        _bias_prep_kernel,
        in_specs=[smem, smem, pl.BlockSpec(memory_space=pltpu.VMEM)],
        out_shape=[jax.ShapeDtypeStruct((2 * NA_WIN_ROWS - 2, NA_HEADS, GRID_W, 2 * GRID_W), jnp.float32),
                   jax.ShapeDtypeStruct((len(DIL_GROUPS), 3, DIL_HEADS_PER_GROUP, DIL_BLOCK, 2 * DIL_BLOCK),
                                        jnp.float32)],
        compiler_params=pltpu.CompilerParams(vmem_limit_bytes=VMEM_LIMIT),
        name="bias_prep",
    )(rpb.reshape(-1), t5.reshape(-1), jnp.asarray(_dil_bucket_table()))


def _dil_kernel(q_ref, kp_ref, kc_ref, kn_ref, vp_ref, vc_ref, vn_ref, bias_ref, o_ref, lse_ref, *, dil):
    lane = lax.broadcasted_iota(jnp.int32, (DIL_BLOCK, LANES), 1)
    low_half = lane < HEAD_DIM
    hb = DIL_BLOCK // 2

    span = pl.program_id(1)
    n_spans = pl.num_programs(1)
    nbs = kc_ref.shape[2] // DIL_BLOCK

    def window(prev_ref, cur_ref, next_ref, rho, blk, cols):
        lo = blk * DIL_BLOCK
        first = prev_ref[0, rho, hb:, cols] if blk == 0 else cur_ref[0, rho, lo - hb:lo, cols]
        last = (next_ref[0, rho, :hb, cols] if blk == nbs - 1
                else cur_ref[0, rho, lo + DIL_BLOCK:lo + DIL_BLOCK + hb, cols])
        return jnp.concatenate([first, cur_ref[0, rho, lo:lo + DIL_BLOCK, cols], last], axis=0)

    def unit(rho, blk):
        variant = 1
        if blk == nbs - 1:
            variant = jnp.where(span == n_spans - 1, 2, variant)
        if blk == 0:
            variant = jnp.where(span == 0, 0, variant)
        scores = []
        for hp in range(DIL_HEADS_PER_GROUP // 2):
            cols = slice(hp * LANES, (hp + 1) * LANES)
            qp = q_ref[0, rho, blk * DIL_BLOCK:(blk + 1) * DIL_BLOCK, cols]
            zero = jnp.zeros_like(qp)
            q2 = jnp.concatenate([jnp.where(low_half, qp, zero), jnp.where(low_half, zero, qp)], axis=0)
            kw = window(kp_ref, kc_ref, kn_ref, rho, blk, cols)
            bias = jnp.concatenate([bias_ref[variant, 2 * hp], bias_ref[variant, 2 * hp + 1]], axis=0)
            scores.append(_dot_nt(q2, kw) + bias)
        probs = []
        for s in scores:
            m = jnp.max(s, axis=-1, keepdims=True)
            e = jnp.exp(s - m)
            probs.append((e.astype(jnp.bfloat16), m, jnp.sum(e, axis=-1, keepdims=True)))
        for hp, (p, m, l) in enumerate(probs):
            cols = slice(hp * LANES, (hp + 1) * LANES)
            o2 = _dot(p, window(vp_ref, vc_ref, vn_ref, rho, blk, cols)) / l
            lse2 = jnp.broadcast_to(m + jnp.log(l), (2 * DIL_BLOCK, LANES))
            o_pair = jnp.where(low_half, o2[:DIL_BLOCK], o2[DIL_BLOCK:])
            lse_pair = jnp.where(low_half, lse2[:DIL_BLOCK], lse2[DIL_BLOCK:])
            if dil == 1:
                rows = slice(blk * DIL_BLOCK, (blk + 1) * DIL_BLOCK)
                o_ref[hp, rows, :] = o_pair
                lse_ref[hp, rows, :] = lse_pair
            else:
                rows = pl.ds(blk * DIL_BLOCK * dil + rho, DIL_BLOCK, stride=dil)
                o_ref[hp, rows, :] = o_pair
                lse_ref[hp, rows, :] = lse_pair

    def class_body(rho, carry):
        for blk in range(nbs):
            unit(rho, blk)
        return carry

    if dil == 1:
        class_body(0, 0)
    else:
        lax.fori_loop(0, dil, class_body, 0, unroll=2 if nbs == 1 else 1)


def _dil_attn(qkv, bias_tab, group, dil, batch, seq):
    L = seq // dil
    nb = L // DIL_BLOCK
    assert nb >= 2
    tq = DIL_STEP_TOKENS
    span = tq // dil
    nbs = span // DIL_BLOCK
    n_spans = L // span

    def cur(col):
        return pl.BlockSpec((1, dil, span, WIDTH_G), lambda b, n: (b, 0, n, col))

    def edge(col, shift):
        return pl.BlockSpec((1, dil, DIL_BLOCK, WIDTH_G),
                            lambda b, n: (b, 0, jnp.clip(n * nbs + (shift if shift < 0 else nbs), 0, nb - 1), col))

    n_pairs = WIDTH_G // LANES
    out_spec = pl.BlockSpec((n_pairs, tq, LANES), lambda b, n: (0, b * n_spans + n, 0))
    return pl.pallas_call(
        functools.partial(_dil_kernel, dil=dil),
        grid=(batch, n_spans),
        in_specs=[cur(0), edge(1, -1), cur(1), edge(1, 1), edge(2, -1), cur(2), edge(2, 1),
                  pl.BlockSpec((None, 3, DIL_HEADS_PER_GROUP, DIL_BLOCK, 2 * DIL_BLOCK),
                               lambda b, n: (group, 0, 0, 0, 0))],
        out_specs=[out_spec, out_spec],
        out_shape=[jax.ShapeDtypeStruct((n_pairs, batch * seq, LANES), jnp.float32)] * 2,
        compiler_params=_cparams(2),
        name=f"dil_attn_d{dil}",
    )(qkv, qkv, qkv, qkv, qkv, qkv, qkv, bias_tab)


def _mix_kernel(x_ref, na_ref, o0_ref, l0_ref, o1_ref, l1_ref, o2_ref, l2_ref, qc_ref, gate_ref,
                mkv_ref, wa_ref, wb_ref, wc_ref, wo_ref, fg_ref, rw_ref, rb_ref,
                x1_ref, h_ref, idx_ref, wt_ref, seg_ref):
    T = x_ref.shape[0]
    ob = []
    for hp in range(WIDTH_G // LANES):
        l0, l1, l2 = l0_ref[hp], l1_ref[hp], l2_ref[hp]
        m = jnp.maximum(jnp.maximum(l0, l1), l2)
        e0, e1, e2 = jnp.exp(l0 - m), jnp.exp(l1 - m), jnp.exp(l2 - m)
        ob.append((e0 * o0_ref[hp] + e1 * o1_ref[hp] + e2 * o2_ref[hp]) / (e0 + e1 + e2))
    ob = jnp.concatenate(ob, axis=-1)

    mem_scale = MEM_HEAD_DIM ** -0.5
    scores = []
    for h in range(MEM_HEADS):
        cols = slice(h * MEM_HEAD_DIM, (h + 1) * MEM_HEAD_DIM)
        scores.append(_dot_nt(qc_ref[:, cols], mkv_ref[:, cols]) * mem_scale)
    probs = []
    for s in scores:
        e = jnp.exp(s - jnp.max(s, axis=-1, keepdims=True))
        probs.append((e.astype(jnp.bfloat16), jnp.sum(e, axis=-1, keepdims=True)))
    oc = []
    for h, (p, l) in enumerate(probs):
        oc.append(_dot(p, mkv_ref[:, WIDTH_C + h * MEM_HEAD_DIM:WIDTH_C + (h + 1) * MEM_HEAD_DIM]) / l)
    oc = jnp.concatenate(oc, axis=-1).astype(jnp.bfloat16)

    y_a = _dot(na_ref[...], wa_ref[...])
    y_b = _dot(ob.astype(jnp.bfloat16), wb_ref[...])
    y_c = _dot(oc, wc_ref[...])
    merged = (gate_ref[:, 0:D_MODEL].astype(jnp.float32) * y_a
              + gate_ref[:, D_MODEL:2 * D_MODEL].astype(jnp.float32) * y_b
              + gate_ref[:, 2 * D_MODEL:3 * D_MODEL].astype(jnp.float32) * y_c)
    x1 = x_ref[...] + _dot(merged.astype(jnp.bfloat16), wo_ref[...])
    x1_ref[...] = x1

    h = _rms(x1, fg_ref[...])
    h_ref[...] = h.astype(jnp.bfloat16)
    logits = lax.dot_general(rw_ref[...], h, (((1,), (1,)), ((), ())), preferred_element_type=jnp.float32,
                             precision=lax.Precision.HIGHEST) + rb_ref[...]
    expert = lax.broadcasted_iota(jnp.int32, (N_EXPERTS, T), 0)
    vals, idxs = [], []
    for _ in range(TOP_K):
        mx = jnp.max(logits, axis=0, keepdims=True)
        sel = jnp.min(jnp.where(logits == mx, expert, N_EXPERTS), axis=0, keepdims=True)
        vals.append(mx)
        idxs.append(sel)
        logits = jnp.where(expert == sel, -jnp.inf, logits)
    ev = [jnp.exp(v - vals[0]) for v in vals]
    den = ev[0] + ev[1] + ev[2] + ev[3]
    idx_ref[...] = jnp.concatenate(idxs, axis=0)
    wt_ref[...] = jnp.concatenate([e / den for e in ev], axis=0)
    member = sum((expert == sel).astype(jnp.float32) for sel in idxs)
    for s in range(T // MOE_TILE):
        cnt = jnp.sum(member[:, s * MOE_TILE:(s + 1) * MOE_TILE], axis=1, keepdims=True)
        seg = jnp.floor((cnt + (MOE_PIECE - 1)) * (1.0 / MOE_PIECE)) * MOE_PIECE
        seg_ref[s] = jnp.broadcast_to(seg, (N_EXPERTS, LANES))


def _mix_out(x2d, na_out, dil_outs, qc, gates, mkv, wa, wb, wc, wo, ffn_g, router_w, router_b, batch, seq):
    n = x2d.shape[0]
    T = PROJ_TILE
    tiles_per_batch = seq // T
    mem_len = mkv.shape[0] // batch
    row = lambda w: pl.BlockSpec((T, w), lambda i: (i, 0))
    full = lambda a: pl.BlockSpec(a.shape, lambda i: (0,) * a.ndim)
    in_specs = [row(D_MODEL), row(WIDTH_A)]
    args = [x2d, na_out]
    for o, l in dil_outs:
        pair_rows = pl.BlockSpec((WIDTH_G // LANES, T, LANES), lambda i: (0, i, 0))
        in_specs += [pair_rows, pair_rows]
        args += [o, l]
    in_specs += [row(WIDTH_C), row(3 * D_MODEL),
                 pl.BlockSpec((mem_len, 2 * WIDTH_C), lambda i: (i // tiles_per_batch, 0)),
                 full(wa), full(wb), full(wc), full(wo), full(ffn_g), full(router_w), full(router_b)]
    args += [qc, gates, mkv, wa, wb, wc, wo, ffn_g, router_w, router_b]
    return pl.pallas_call(
        _mix_kernel,
        grid=(n // T,),
        in_specs=in_specs,
        out_specs=[row(D_MODEL), row(D_MODEL), pl.BlockSpec((TOP_K, T), lambda i: (0, i)),
                   pl.BlockSpec((TOP_K, T), lambda i: (0, i)),
                   pl.BlockSpec((T // MOE_TILE, N_EXPERTS, LANES), lambda i: (i, 0, 0))],
        out_shape=[jax.ShapeDtypeStruct((n, D_MODEL), jnp.float32),
                   jax.ShapeDtypeStruct((n, D_MODEL), jnp.bfloat16),
                   jax.ShapeDtypeStruct((TOP_K, n), jnp.int32),
                   jax.ShapeDtypeStruct((TOP_K, n), jnp.float32),
                   jax.ShapeDtypeStruct((n // MOE_TILE, N_EXPERTS, LANES), jnp.float32)],
        compiler_params=_cparams(1),
        name="mix_out",
    )(*args)


def _route_kernel(idx_ref, segs_ref, lpos_ref, ptab_ref, be_ref, pe_ref, carry_ref, pstart_ref):
    i = pl.program_id(0)
    T = idx_ref.shape[1]
    n_blocks = be_ref.shape[1]
    expert = lax.broadcasted_iota(jnp.int32, (N_EXPERTS, T), 0)
    idx = idx_ref[...]
    onehots = [expert == idx[k:k + 1, :] for k in range(TOP_K)]
    member = sum(o.astype(jnp.float32) for o in onehots)
    seg_b = segs_ref[i]
    seg = seg_b[:, 0:1]
    r = lax.broadcasted_iota(jnp.int32, (N_EXPERTS, N_EXPERTS), 0)
    c = lax.broadcasted_iota(jnp.int32, (N_EXPERTS, N_EXPERTS), 1)
    lower_incl = (r >= c).astype(jnp.float32)

    def cumsum_experts(v):
        return jnp.dot(lower_incl, v, preferred_element_type=jnp.float32, precision=lax.Precision.HIGHEST)

    @pl.when(i == 0)
    def _():
        counts = jnp.sum(segs_ref[...], axis=0)
        padded = jnp.floor((counts + (MOE_BLOCK - 1)) * (1.0 / MOE_BLOCK)) * MOE_BLOCK
        pad_ends = cumsum_experts(padded)
        pstart_ref[...] = pad_ends - padded
        carry_ref[...] = jnp.zeros_like(carry_ref)
        pe_ref[...] = pad_ends[:, 0:1].astype(jnp.int32)
        blk_start = (lax.broadcasted_iota(jnp.int32, (N_EXPERTS, n_blocks), 1) * MOE_BLOCK).astype(jnp.float32)
        be = jnp.sum((pad_ends[:, 0:1] <= blk_start).astype(jnp.int32), axis=0, keepdims=True)
        be_ref[...] = jnp.minimum(be, N_EXPERTS - 1)

    local_end = cumsum_experts(seg_b)[:, 0:1]
    local_start = local_end - seg
    global_start = pstart_ref[:, 0:1] + carry_ref[:, 0:1]
    rr = lax.broadcasted_iota(jnp.int32, (T, T), 0)
    cc = lax.broadcasted_iota(jnp.int32, (T, T), 1)
    earlier = (rr < cc).astype(jnp.bfloat16)
    before = _dot(member.astype(jnp.bfloat16), earlier)
    base = local_start + before
    lpos = [jnp.sum(jnp.where(o, base, 0.0), axis=0, keepdims=True) for o in onehots]
    lpos_ref[...] = jnp.concatenate(lpos, axis=0).astype(jnp.int32)

    inv = 1.0 / MOE_PIECE
    piece = lax.broadcasted_iota(jnp.int32, (N_EXPERTS, MOE_PIECE_TAB), 1).astype(jnp.float32)
    first = local_start * inv
    inside = jnp.logical_and(piece >= first, piece < local_end * inv)
    dst = jnp.sum(jnp.where(inside, global_start * inv + piece - first, 0.0), axis=0, keepdims=True)
    n_pieces = local_end[N_EXPERTS - 1:N_EXPERTS, :] * inv
    spare = (n_blocks * MOE_BLOCK // MOE_PIECE
             + (i % 2) * (MOE_TILE_ROWS // MOE_PIECE)).astype(jnp.float32)
    dst = jnp.where(piece[0:1, :] >= n_pieces, spare + piece[0:1, :], dst)
    dst = jnp.where(piece[0:1, :] == MOE_PIECE_TAB - 1, n_pieces, dst)
    ptab_ref[...] = dst.astype(jnp.int32)
    carry_ref[...] += seg


def _moe_route(topk_idx, segs, n_blocks):
    n = topk_idx.shape[1]
    T = MOE_TILE
    acc = pltpu.VMEM((N_EXPERTS, LANES), jnp.float32)
    return pl.pallas_call(
        _route_kernel,
        grid=(n // T,),
        in_specs=[pl.BlockSpec((TOP_K, T), lambda i: (0, i)),
                  pl.BlockSpec(segs.shape, lambda i: (0, 0, 0))],
        out_specs=[pl.BlockSpec((TOP_K, T), lambda i: (0, i)),
                   pl.BlockSpec((None, 1, MOE_PIECE_TAB), lambda i: (i, 0, 0)),
                   pl.BlockSpec((1, n_blocks), lambda i: (0, 0)),
                   pl.BlockSpec((N_EXPERTS, 1), lambda i: (0, 0))],
        out_shape=[jax.ShapeDtypeStruct((TOP_K, n), jnp.int32),
                   jax.ShapeDtypeStruct((n // T, 1, MOE_PIECE_TAB), jnp.int32),
                   jax.ShapeDtypeStruct((1, n_blocks), jnp.int32),
                   jax.ShapeDtypeStruct((N_EXPERTS, 1), jnp.int32)],
        scratch_shapes=[acc, acc],
        compiler_params=_cparams(1),
        name="moe_route",
    )(topk_idx, segs)


def _piece(ref, p):
    return ref.at[pl.ds(pl.multiple_of(p * MOE_PIECE, MOE_PIECE), MOE_PIECE), :]


def _dispatch_kernel(pe_ref, ptab_ref, lpos_ref, h_ref, xs_hbm, zbuf, pbuf, sem, zsem):
    i = pl.program_id(0)
    T = h_ref.shape[0]
    half = h_ref.shape[1] // 2

    def zero_copy(e):
        end = pe_ref[e]
        start = pl.multiple_of(end - MOE_BLOCK, MOE_BLOCK)
        return pltpu.make_async_copy(zbuf, xs_hbm.at[pl.ds(start, MOE_BLOCK), :], zsem)

    def nonempty(e):
        return pe_ref[e] > (pe_ref[e - 1] if e else 0)

    @pl.when(i == 0)
    def _():
        zbuf[...] = jnp.zeros_like(zbuf)
        for e in range(N_EXPERTS):
            @pl.when(nonempty(e))
            def _():
                zero_copy(e).start()
        for e in range(N_EXPERTS):
            @pl.when(nonempty(e))
            def _():
                zero_copy(e).wait()

        def tail_copy(b):
            return pltpu.make_async_copy(
                zbuf, xs_hbm.at[pl.ds(pl.multiple_of(b * MOE_BLOCK, MOE_BLOCK), MOE_BLOCK), :], zsem)
        first_tail = pe_ref[N_EXPERTS - 1] // MOE_BLOCK
        n_blocks = xs_hbm.shape[0] // MOE_BLOCK
        lax.fori_loop(first_tail, n_blocks, lambda b, c: (tail_copy(b).start(), c)[1], 0)
        lax.fori_loop(first_tail, n_blocks, lambda b, c: (tail_copy(b).wait(), c)[1], 0)

    row = lax.broadcasted_iota(jnp.int32, (MOE_TILE_ROWS, T), 0)
    lpos = lpos_ref[...]
    perm = sum((row == lpos[k:k + 1, :]).astype(jnp.bfloat16) for k in range(TOP_K))
    lo = lax.bitcast_convert_type(_dot(perm, h_ref[:, :half]), jnp.uint32)
    hi = lax.bitcast_convert_type(_dot(perm, h_ref[:, half:]), jnp.uint32)
    slot = i % 2
    pbuf[slot] = hi | (lo >> 16)

    n_tile_pieces = MOE_TILE_ROWS // MOE_PIECE

    def piece_copy(s, p):
        return pltpu.make_async_copy(_piece(pbuf.at[s], p), _piece(xs_hbm, ptab_ref[p]), sem.at[s])

    def wait_all(s):
        lax.fori_loop(0, n_tile_pieces, lambda p, c: (piece_copy(s, 0).wait(), c)[1], 0, unroll=8)

    lax.fori_loop(0, n_tile_pieces, lambda p, c: (piece_copy(slot, p).start(), c)[1], 0, unroll=8)

    @pl.when(i > 0)
    def _():
        wait_all(1 - slot)

    @pl.when(i == pl.num_programs(0) - 1)
    def _():
        wait_all(slot)


def _moe_dispatch(h, lpos, ptab, pad_ends, n_blocks):
    n, width = h.shape
    T = MOE_TILE
    n_steps = n // T

    def kernel(pe_ref, ptab_ref, *rest):
        _dispatch_kernel(pe_ref, ptab_ref.at[0], *rest)

    grid_spec = pltpu.PrefetchScalarGridSpec(
        num_scalar_prefetch=1,
        grid=(n_steps,),
        in_specs=[pl.BlockSpec((None, 1, MOE_PIECE_TAB), lambda i, pe: (i, 0, 0), memory_space=pltpu.SMEM),
                  pl.BlockSpec((TOP_K, T), lambda i, pe: (0, i)),
                  pl.BlockSpec((T, width), lambda i, pe: (i, 0))],
        out_specs=pl.BlockSpec(memory_space=pl.ANY),
        scratch_shapes=[pltpu.VMEM((MOE_BLOCK, width // 2), jnp.uint32),
                        pltpu.VMEM((2, MOE_TILE_ROWS, width // 2), jnp.uint32),
                        pltpu.SemaphoreType.DMA((2,)),
                        pltpu.SemaphoreType.DMA(())],
    )
    return pl.pallas_call(
        kernel,
        grid_spec=grid_spec,
        out_shape=jax.ShapeDtypeStruct(((n_blocks + MOE_SPARE_BLOCKS) * MOE_BLOCK, width // 2), jnp.uint32),
        compiler_params=_cparams(1),
        name="moe_dispatch",
    )(pad_ends, ptab, lpos, h)


def _moe_kernel(be_ref, pe_ref, xs_ref, wgu_ref, bgu_ref, wdn_ref, bdn_ref, y_ref, wgu_bf, wdn_bf):
    i = pl.program_id(0)
    n_valid = pe_ref[N_EXPERTS - 1] // MOE_BLOCK
    new_expert = jnp.logical_or(i == 0, be_ref[i] != be_ref[jnp.maximum(i - 1, 0)])

    @pl.when(jnp.logical_and(i < n_valid, new_expert))
    def _():
        wgu_bf[...] = wgu_ref[0].astype(jnp.bfloat16)
        wdn_bf[...] = wdn_ref[0].astype(jnp.bfloat16)

    @pl.when(i < n_valid)
    def _():
        x = _unpack_bf16_pairs(xs_ref[...])
        gu = _dot(x, wgu_bf[...]) + bgu_ref[0]
        g = jnp.minimum(gu[:, :D_FF], SWIGLU_LIMIT)
        u = jnp.clip(gu[:, D_FF:], -SWIGLU_LIMIT, SWIGLU_LIMIT)
        a = (u + 1.0) * (g * _sigmoid(SWIGLU_ALPHA * g))
        y_ref[...] = _pack_bf16_pairs(_dot(a.astype(jnp.bfloat16), wdn_bf[...]) + bdn_ref[0])

    @pl.when(i >= n_valid)
    def _():
        y_ref[...] = jnp.zeros_like(y_ref)


def _moe_experts(xs, block_expert, pad_ends, wgu, bgu, wdn, bdn):
    n_blocks = block_expert.shape[0]

    def row_blk(i, be, pe):
        return (jnp.minimum(i, jnp.maximum(pe[N_EXPERTS - 1] // MOE_BLOCK - 1, 0)), 0)

    grid_spec = pltpu.PrefetchScalarGridSpec(
        num_scalar_prefetch=2,
        grid=(n_blocks,),
        in_specs=[pl.BlockSpec((MOE_BLOCK, D_MODEL // 2), row_blk),
                  pl.BlockSpec((1, D_MODEL, 2 * D_FF), lambda i, be, pe: (be[i], 0, 0)),
                  pl.BlockSpec((1, 1, 2 * D_FF), lambda i, be, pe: (be[i], 0, 0)),
                  pl.BlockSpec((1, D_FF, D_MODEL), lambda i, be, pe: (be[i], 0, 0)),
                  pl.BlockSpec((1, 1, D_MODEL), lambda i, be, pe: (be[i], 0, 0))],
        out_specs=pl.BlockSpec((MOE_BLOCK, D_MODEL // 2), lambda i, be, pe: (i, 0)),
        scratch_shapes=[pltpu.VMEM((D_MODEL, 2 * D_FF), jnp.bfloat16),
                        pltpu.VMEM((D_FF, D_MODEL), jnp.bfloat16)],
    )
    return pl.pallas_call(
        _moe_kernel,
        grid_spec=grid_spec,
        out_shape=jax.ShapeDtypeStruct((n_blocks * MOE_BLOCK, D_MODEL // 2), jnp.uint32),
        compiler_params=_cparams(1),
        name="moe_experts",
    )(block_expert, pad_ends, xs, wgu, bgu, wdn, bdn)


def _gather_pieces(src_hbm, ptab_ref, dst_ref, sem):
    spare = src_hbm.shape[0] // MOE_PIECE

    def body(p, carry):
        src = ptab_ref[p]
        src = jnp.where(src >= spare, 0, src)
        pltpu.make_async_copy(_piece(src_hbm, src), _piece(dst_ref, p), sem).start()
        return carry
    lax.fori_loop(0, MOE_TILE_ROWS // MOE_PIECE, body, 0, unroll=8)


def _wait_pieces(src_hbm, dst_ref, sem):
    def body(p, carry):
        pltpu.make_async_copy(_piece(src_hbm, 0), _piece(dst_ref, 0), sem).wait()
        return carry
    lax.fori_loop(0, MOE_TILE_ROWS // MOE_PIECE, body, 0, unroll=8)


def _combine_kernel(ptab_ref, ptab_next_ref, ys_hbm, lpos_ref, wt_ref, x1_ref, g_ref, out_ref, gbuf, sem):
    i = pl.program_id(0)
    n_steps = pl.num_programs(0)
    slot = i % 2
    T = x1_ref.shape[0]

    @pl.when(i == 0)
    def _():
        _gather_pieces(ys_hbm, ptab_ref, gbuf.at[0], sem.at[0])

    @pl.when(i + 1 < n_steps)
    def _():
        _gather_pieces(ys_hbm, ptab_next_ref, gbuf.at[1 - slot], sem.at[1 - slot])

    _wait_pieces(ys_hbm, gbuf.at[slot], sem.at[slot])
    rows = _unpack_bf16_pairs(gbuf[slot])
    col = lax.broadcasted_iota(jnp.int32, (T, MOE_TILE_ROWS), 1)
    lpos = lpos_ref[...]
    wt = wt_ref[...]
    cmb = jnp.zeros((T, MOE_TILE_ROWS), jnp.float32)
    for k in range(TOP_K):
        cmb = jnp.where(col == lpos[:, k:k + 1], wt[:, k:k + 1], cmb)
    hi = cmb.astype(jnp.bfloat16)
    lo = (cmb - hi.astype(jnp.float32)).astype(jnp.bfloat16)
    y2 = _dot(jnp.concatenate([hi, lo], axis=0), rows)
    y = x1_ref[...] + y2[:T] + y2[T:]
    out_ref[...] = _rms(y, g_ref[...])


def _moe_combine(ys, ptab, lpos_t, wts, x1, final_g):
    n = x1.shape[0]
    T = MOE_TILE
    n_steps = n // T
    smem_blk = lambda shift: pl.BlockSpec(
        (None, 1, MOE_PIECE_TAB), lambda i: (jnp.minimum(i + shift, n_steps - 1), 0, 0),
        memory_space=pltpu.SMEM)

    def kernel(ptab_ref, ptab_next_ref, *rest):
        _combine_kernel(ptab_ref.at[0], ptab_next_ref.at[0], *rest)

    return pl.pallas_call(
        kernel,
        grid=(n_steps,),
        in_specs=[smem_blk(0), smem_blk(1),
                  pl.BlockSpec(memory_space=pl.ANY),
                  pl.BlockSpec((T, TOP_K), lambda i: (i, 0)),
                  pl.BlockSpec((T, TOP_K), lambda i: (i, 0)),
                  pl.BlockSpec((T, D_MODEL), lambda i: (i, 0)),
                  pl.BlockSpec((1, D_MODEL), lambda i: (0, 0))],
        out_specs=pl.BlockSpec((T, D_MODEL), lambda i: (i, 0)),
        out_shape=jax.ShapeDtypeStruct((n, D_MODEL), jnp.float32),
        scratch_shapes=[pltpu.VMEM((2, MOE_TILE_ROWS, D_MODEL // 2), jnp.uint32),
                        pltpu.SemaphoreType.DMA((2,))],
        compiler_params=_cparams(1),
        name="moe_combine",
    )(ptab, ptab, ys, lpos_t, wts, x1, final_g)


def kernel(x, mem, attn_norm_g, mem_norm_g, w_in, w_mem_kv, na_rpb, t5_rel_bias, w_branch_a, w_branch_b,
           w_branch_c, w_out, ffn_norm_g, router_w, router_b, expert_w_gate_up, expert_b_gate_up,
           expert_w_down, expert_b_down, final_norm_g):
    B, S, D = x.shape
    depth = w_in.shape[0]
    bf = jnp.bfloat16
    x2d = x.reshape(B * S, D)
    mem2d = mem.reshape(B * mem.shape[1], D)
    assert depth == 1, "single-layer block (per-layer parameters carry a leading axis of 1)"
    l = 0
    na_bias, dil_bias = _bias_prep(na_rpb[l], t5_rel_bias)
    mkv = _mem_kv(mem2d, mem_norm_g[l][None], w_mem_kv[l].astype(bf))
    w_in_bf = w_in[l].astype(bf)
    na_qkv, d0, d1, d2, qc = _in_proj(x2d, attn_norm_g[l][None], w_in_bf, B, S)
    gates = _gate_proj(x2d, attn_norm_g[l][None], w_in_bf)
    na_out = _na_attn(na_qkv, na_bias, B, S)
    dil_outs = [_dil_attn(qkv, dil_bias, gi, dil, B, S)
                for gi, (qkv, (_, dil)) in enumerate(zip((d0, d1, d2), DIL_GROUPS))]
    x1, h, topk_idx, topk_w, segs = _mix_out(
        x2d, na_out, dil_outs, qc, gates, mkv, w_branch_a[l].astype(bf), w_branch_b[l].astype(bf),
        w_branch_c[l].astype(bf), w_out[l].astype(bf), ffn_norm_g[l][None], router_w[l].T,
        router_b[l][:, None], B, S)

    n = B * S
    max_rows = n * TOP_K + (n // MOE_TILE) * N_EXPERTS * (MOE_PIECE - 1)
    n_blocks = -(-max_rows // MOE_BLOCK) + N_EXPERTS
    lpos, ptab, block_expert, pad_ends = _moe_route(topk_idx, segs, n_blocks)
    block_expert = block_expert.reshape(n_blocks)
    pad_ends = pad_ends.reshape(N_EXPERTS)
    xs = _moe_dispatch(h, lpos, ptab, pad_ends, n_blocks)
    ys = _moe_experts(xs, block_expert, pad_ends, expert_w_gate_up[l], expert_b_gate_up[l][:, None, :],
                      expert_w_down[l], expert_b_down[l][:, None, :])
    out = _moe_combine(ys, ptab, lpos.T, topk_w.T, x1, final_norm_g[None])
    return out.reshape(B, S, D)
```

```python
import functools
import math

import numpy as np
import jax
import jax.numpy as jnp
from jax import lax
from jax.experimental import pallas as pl
from jax.experimental.pallas import tpu as pltpu

D_MODEL = 1024
GRID_W = 64
MEM_HEADS = 4
MEM_HEAD_DIM = 128
HEAD_DIM = 64
NA_HEADS = 8
NA_WIN_ROWS = 8
NA_WIN_COLS = 16
DIL_GROUPS = ((128, 1), (512, 4), (2048, 16))
DIL_HEADS_PER_GROUP = 4
DIL_BLOCK = 128
T5_BUCKETS = 32
T5_MAX_DIST = 1024
N_EXPERTS = 32
TOP_K = 4
D_FF = D_MODEL
SWIGLU_ALPHA = 1.702
SWIGLU_LIMIT = 7.0
RMS_EPS = 1e-6
NEG_INF = -1e30

WIDTH_A = NA_HEADS * HEAD_DIM
WIDTH_B = DIL_HEADS_PER_GROUP * len(DIL_GROUPS) * HEAD_DIM
WIDTH_G = DIL_HEADS_PER_GROUP * HEAD_DIM
WIDTH_C = MEM_HEADS * MEM_HEAD_DIM
OFF_QA, OFF_KA, OFF_VA = 0, WIDTH_A, 2 * WIDTH_A
OFF_QB = 3 * WIDTH_A
OFF_KB = OFF_QB + WIDTH_B
OFF_VB = OFF_KB + WIDTH_B
OFF_QC = OFF_VB + WIDTH_B
OFF_GATE = OFF_QC + WIDTH_C
IN_COLS = OFF_GATE + 3 * D_MODEL

LANES = 128
MXU_WIDTH = 256
VMEM_LIMIT = 56 * 1024 * 1024

IN_PROJ_TILE = 1024
PROJ_TILE = 512
NA_ROWS_PER_STEP = 16
DIL_STEP_TOKENS = 2048
MOE_BLOCK = 512
MOE_TILE = 256
MOE_PIECE = 8
MOE_TILE_ROWS = -(-(TOP_K * MOE_TILE + N_EXPERTS * (MOE_PIECE - 1)) // MXU_WIDTH) * MXU_WIDTH
MOE_PIECE_TAB = 256
assert MOE_TILE_ROWS // MOE_PIECE < MOE_PIECE_TAB
MOE_SPARE_BLOCKS = -(-2 * MOE_TILE_ROWS // MOE_BLOCK)


def _cparams(n_axes):
    return pltpu.CompilerParams(
        dimension_semantics=("arbitrary",) * n_axes, vmem_limit_bytes=VMEM_LIMIT)


def _rms(x, g):
    return x * lax.rsqrt(jnp.mean(x * x, axis=-1, keepdims=True) + RMS_EPS) * g


def _sigmoid(x):
    return 1.0 / (1.0 + jnp.exp(-x))


def _pack_bf16_pairs(x):
    w = x.shape[1] // 2
    bits = lax.bitcast_convert_type(x.astype(jnp.bfloat16).astype(jnp.float32), jnp.uint32)
    return (bits[:, w:] & jnp.uint32(0xFFFF0000)) | (bits[:, :w] >> 16)


def _unpack_bf16_pairs(p):
    lo = lax.bitcast_convert_type(p << 16, jnp.float32)
    hi = lax.bitcast_convert_type(p & jnp.uint32(0xFFFF0000), jnp.float32)
    return jnp.concatenate([lo, hi], axis=1).astype(jnp.bfloat16)


def _dot(a, b):
    return jnp.dot(a, b, preferred_element_type=jnp.float32)


def _dot_nt(a, b):
    return lax.dot_general(a, b, (((1,), (1,)), ((), ())), preferred_element_type=jnp.float32)


def _mem_kv_kernel(mem_ref, g_ref, w_ref, out_ref):
    h = _rms(mem_ref[...], g_ref[...]).astype(jnp.bfloat16)
    out_ref[...] = _dot(h, w_ref[...]).astype(jnp.bfloat16)


def _mem_kv(mem2d, g, w):
    rows = mem2d.shape[0]
    return pl.pallas_call(
        _mem_kv_kernel,
        out_shape=jax.ShapeDtypeStruct((rows, 2 * WIDTH_C), jnp.bfloat16),
        name="mem_kv",
        compiler_params=pltpu.CompilerParams(vmem_limit_bytes=VMEM_LIMIT),
    )(mem2d, g, w)


def _in_proj_kernel(x_ref, g_ref, w_ref, na_ref, d0_ref, d1_ref, d2_ref, qc_ref, acc_ref):
    T = x_ref.shape[0]
    h = _rms(x_ref[...], g_ref[...]).astype(jnp.bfloat16)
    q_scale = HEAD_DIM ** -0.5

    for c in range(3):
        acc = _dot(h, w_ref[:, c * WIDTH_A:(c + 1) * WIDTH_A])
        if c == 0:
            acc = acc * q_scale
        na_ref[:, c * WIDTH_A:(c + 1) * WIDTH_A] = acc.astype(jnp.bfloat16)

    for gi, ((_, dil), d_ref) in enumerate(zip(DIL_GROUPS, (d0_ref, d1_ref, d2_ref))):
        for kind, off in enumerate((OFF_QB, OFF_KB, OFF_VB)):
            lo = off + gi * WIDTH_G
            acc = _dot(h, w_ref[:, lo:lo + WIDTH_G])
            if kind == 0:
                acc = acc * q_scale
            cols = slice(kind * WIDTH_G, (kind + 1) * WIDTH_G)
            if dil == 1:
                d_ref[0, 0, :, cols] = acc.astype(jnp.bfloat16)
            else:
                for c in range(WIDTH_G // LANES):
                    acc_ref[c] = acc[:, c * LANES:(c + 1) * LANES]
                for rho in range(dil):
                    d_ref[0, rho, :, cols] = jnp.concatenate(
                        [acc_ref[c, pl.ds(rho, T // dil, stride=dil), :] for c in range(WIDTH_G // LANES)],
                        axis=-1).astype(jnp.bfloat16)

    qc_ref[...] = _dot(h, w_ref[:, OFF_QC:OFF_QC + WIDTH_C]).astype(jnp.bfloat16)


def _gate_proj_kernel(x_ref, g_ref, w_ref, gate_ref):
    h = _rms(x_ref[...], g_ref[...]).astype(jnp.bfloat16)
    chunk = 512
    first = OFF_GATE - IN_COLS // 2
    for c in range(3 * D_MODEL // chunk):
        lo = first + c * chunk
        gate_ref[:, c * chunk:(c + 1) * chunk] = _sigmoid(_dot(h, w_ref[:, lo:lo + chunk])).astype(jnp.bfloat16)


def _gate_proj(x2d, g, w_bf16):
    n = x2d.shape[0]
    T = IN_PROJ_TILE
    assert (IN_COLS // 2) % LANES == 0 and IN_COLS // 2 <= OFF_GATE
    return pl.pallas_call(
        _gate_proj_kernel,
        grid=(n // T,),
        in_specs=[pl.BlockSpec((T, D_MODEL), lambda i: (i, 0)),
                  pl.BlockSpec((1, D_MODEL), lambda i: (0, 0)),
                  pl.BlockSpec((D_MODEL, IN_COLS // 2), lambda i: (0, 1))],
        out_specs=pl.BlockSpec((T, 3 * D_MODEL), lambda i: (i, 0)),
        out_shape=jax.ShapeDtypeStruct((n, 3 * D_MODEL), jnp.bfloat16),
        compiler_params=_cparams(1),
        name="gate_proj",
    )(x2d, g, w_bf16)


def _in_proj(x2d, g, w_bf16, batch, seq):
    n = x2d.shape[0]
    T = IN_PROJ_TILE
    tiles_per_batch = seq // T
    out_shape = [jax.ShapeDtypeStruct((n, 3 * WIDTH_A), jnp.bfloat16)]
    out_specs = [pl.BlockSpec((T, 3 * WIDTH_A), lambda i: (i, 0))]
    for _, dil in DIL_GROUPS:
        out_shape.append(jax.ShapeDtypeStruct((batch, dil, seq // dil, 3 * WIDTH_G), jnp.bfloat16))
        out_specs.append(pl.BlockSpec((1, dil, T // dil, 3 * WIDTH_G),
                                      lambda i: (i // tiles_per_batch, 0, i % tiles_per_batch, 0)))
    out_shape += [jax.ShapeDtypeStruct((n, WIDTH_C), jnp.bfloat16)]
    out_specs += [pl.BlockSpec((T, WIDTH_C), lambda i: (i, 0))]
    return pl.pallas_call(
        _in_proj_kernel,
        grid=(n // T,),
        in_specs=[pl.BlockSpec((T, D_MODEL), lambda i: (i, 0)),
                  pl.BlockSpec((1, D_MODEL), lambda i: (0, 0)),
                  pl.BlockSpec((D_MODEL, OFF_GATE), lambda i: (0, 0))],
        out_specs=out_specs,
        out_shape=out_shape,
        scratch_shapes=[pltpu.VMEM((WIDTH_G // LANES, T, LANES), jnp.float32)],
        compiler_params=_cparams(1),
        name="in_proj",
    )(x2d, g, w_bf16)


def _na_kernel(q_ref, k_ref, v_ref, bias_ref, out_ref, *, rows):
    i = pl.program_id(1)
    lane = lax.broadcasted_iota(jnp.int32, (GRID_W, LANES), 1)
    low_half = lane < HEAD_DIM
    n_win = NA_WIN_ROWS * GRID_W

    def row_body(j, carry):
        qr = i * NA_ROWS_PER_STEP + j
        rs = jnp.clip(qr - NA_WIN_ROWS // 2, 0, rows - NA_WIN_ROWS)
        shift = rs - qr + NA_WIN_ROWS - 1
        q_off = pl.multiple_of(j * GRID_W, GRID_W)
        k_off = pl.multiple_of(rs * GRID_W, GRID_W)
        scores = []
        for hp in range(NA_HEADS // 2):
            cols = slice(hp * LANES, (hp + 1) * LANES)
            qp = q_ref[pl.ds(q_off, GRID_W), cols]
            kp = k_ref[pl.ds(k_off, n_win), cols]
            zero = jnp.zeros_like(qp)
            q2 = jnp.concatenate([jnp.where(low_half, qp, zero), jnp.where(low_half, zero, qp)], axis=0)
            bias = jnp.concatenate(
                [jnp.concatenate([bias_ref[shift + 2 * p, 2 * hp + hh] for p in range(NA_WIN_ROWS // 2)], axis=-1)
                 for hh in range(2)], axis=0)
            scores.append(_dot_nt(q2, kp) + bias)
        probs = []
        for s in scores:
            e = jnp.exp(s - jnp.max(s, axis=-1, keepdims=True))
            probs.append((e.astype(jnp.bfloat16), jnp.sum(e, axis=-1, keepdims=True)))
        for hp, (p, l) in enumerate(probs):
            cols = slice(hp * LANES, (hp + 1) * LANES)
            o2 = _dot(p, v_ref[pl.ds(k_off, n_win), cols]) / l
            out_ref[pl.ds(q_off, GRID_W), cols] = jnp.where(
                low_half, o2[:GRID_W], o2[GRID_W:]).astype(jnp.bfloat16)
        return carry

    lax.fori_loop(0, NA_ROWS_PER_STEP, row_body, 0, unroll=2)


def _na_attn(na_qkv, bias_tab, batch, seq):
    rows = seq // GRID_W
    steps = rows // NA_ROWS_PER_STEP
    tq = NA_ROWS_PER_STEP * GRID_W
    n = na_qkv.shape[0]
    return pl.pallas_call(
        functools.partial(_na_kernel, rows=rows),
        grid=(batch, steps),
        in_specs=[pl.BlockSpec((tq, WIDTH_A), lambda b, i: (b * steps + i, 0)),
                  pl.BlockSpec((seq, WIDTH_A), lambda b, i: (b, 1)),
                  pl.BlockSpec((seq, WIDTH_A), lambda b, i: (b, 2)),
                  pl.BlockSpec(bias_tab.shape, lambda b, i: (0, 0, 0, 0))],
        out_specs=pl.BlockSpec((tq, WIDTH_A), lambda b, i: (b * steps + i, 0)),
        out_shape=jax.ShapeDtypeStruct((n, WIDTH_A), jnp.bfloat16),
        compiler_params=_cparams(2),
        name="na_attn",
    )(na_qkv, na_qkv, na_qkv, bias_tab)


def _t5_bucket_np(rel):
    half = T5_BUCKETS // 2
    max_exact = half // 2
    ret = np.where(rel > 0, half, 0)
    n = np.abs(rel)
    nf = np.maximum(n, 1).astype(np.float32)
    large = max_exact + (np.log(nf / np.float32(max_exact)) / np.float32(math.log(T5_MAX_DIST / max_exact))
                         * np.float32(half - max_exact)).astype(np.int32)
    large = np.minimum(large, half - 1)
    return ret + np.where(n < max_exact, n, large)


def _dil_bucket_table():
    q = np.arange(DIL_BLOCK)[:, None]
    j = np.arange(2 * DIL_BLOCK)[None, :]
    tabs = []
    for window, dil in DIL_GROUPS:
        half = (window // 2) // dil
        assert half == DIL_BLOCK // 2
        rel = (j - half) - q
        tabs.append(np.where(np.abs(rel) <= half, _t5_bucket_np(rel * dil), -1))
    return np.stack(tabs).astype(np.int32)


def _bias_prep_kernel(rpb_ref, t5_ref, bucket_ref, na_ref, dil_ref):
    n_dc = 2 * NA_WIN_COLS - 1
    n_dr = 2 * NA_WIN_ROWS - 1
    qc = lax.broadcasted_iota(jnp.int32, (GRID_W, 2 * GRID_W), 0)
    lane = lax.broadcasted_iota(jnp.int32, (GRID_W, 2 * GRID_W), 1)
    second = lane >= GRID_W
    kc = jnp.where(second, lane - GRID_W, lane)
    diff = kc - qc
    cs = jnp.clip(qc - NA_WIN_COLS // 2, 0, GRID_W - NA_WIN_COLS)
    valid = jnp.logical_and(kc >= cs, kc < cs + NA_WIN_COLS)

    def na_tile(mh, carry):
        m = mh // NA_HEADS
        h = mh - m * NA_HEADS
        base = (h * n_dr + m) * n_dc
        acc = jnp.zeros((GRID_W, 2 * GRID_W), jnp.float32)
        for d in range(n_dc):
            v = jnp.where(second, rpb_ref[base + n_dc + d], rpb_ref[base + d])
            acc = jnp.where(diff == d - (NA_WIN_COLS - 1), v, acc)
        na_ref[m, h] = jnp.where(valid, acc, NEG_INF)
        return carry

    lax.fori_loop(0, (n_dr - 1) * NA_HEADS, na_tile, 0)

    j = lax.broadcasted_iota(jnp.int32, (DIL_BLOCK, 2 * DIL_BLOCK), 1)
    half = DIL_BLOCK // 2
    n_heads = DIL_HEADS_PER_GROUP * len(DIL_GROUPS)
    for gi in range(len(DIL_GROUPS)):
        bucket = bucket_ref[gi]
        in_win = bucket >= 0
        for h in range(DIL_HEADS_PER_GROUP):
            acc = jnp.zeros((DIL_BLOCK, 2 * DIL_BLOCK), jnp.float32)
            for b in range(T5_BUCKETS):
                acc = jnp.where(bucket == b, t5_ref[b * n_heads + gi * DIL_HEADS_PER_GROUP + h], acc)
            for v, ok in enumerate((jnp.logical_and(in_win, j >= half), in_win,
                                    jnp.logical_and(in_win, j < 2 * DIL_BLOCK - half))):
                dil_ref[gi, v, h] = jnp.where(ok, acc, NEG_INF)


def _bias_prep(rpb, t5):
    smem = pl.BlockSpec(memory_space=pltpu.SMEM)
    return pl.pallas_call(
        _bias_prep_kernel,
        in_specs=[smem, smem, pl.BlockSpec(memory_space=pltpu.VMEM)],
        out_shape=[jax.ShapeDtypeStruct((2 * NA_WIN_ROWS - 2, NA_HEADS, GRID_W, 2 * GRID_W), jnp.float32),
                   jax.ShapeDtypeStruct((len(DIL_GROUPS), 3, DIL_HEADS_PER_GROUP, DIL_BLOCK, 2 * DIL_BLOCK),
                                        jnp.float32)],
        compiler_params=pltpu.CompilerParams(vmem_limit_bytes=VMEM_LIMIT),
        name="bias_prep",
    )(rpb.reshape(-1), t5.reshape(-1), jnp.asarray(_dil_bucket_table()))


def _dil_kernel(q_ref, kp_ref, kc_ref, kn_ref, vp_ref, vc_ref, vn_ref, bias_ref, o_ref, lse_ref, *, dil):
    lane = lax.broadcasted_iota(jnp.int32, (DIL_BLOCK, LANES), 1)
    low_half = lane < HEAD_DIM
    hb = DIL_BLOCK // 2

    span = pl.program_id(1)
    n_spans = pl.num_programs(1)
    nbs = kc_ref.shape[2] // DIL_BLOCK

    def window(prev_ref, cur_ref, next_ref, rho, blk, cols):
        lo = blk * DIL_BLOCK
        first = prev_ref[0, rho, hb:, cols] if blk == 0 else cur_ref[0, rho, lo - hb:lo, cols]
        last = (next_ref[0, rho, :hb, cols] if blk == nbs - 1
                else cur_ref[0, rho, lo + DIL_BLOCK:lo + DIL_BLOCK + hb, cols])
        return jnp.concatenate([first, cur_ref[0, rho, lo:lo + DIL_BLOCK, cols], last], axis=0)

    def unit(rho, blk):
        variant = 1
        if blk == nbs - 1:
            variant = jnp.where(span == n_spans - 1, 2, variant)
        if blk == 0:
            variant = jnp.where(span == 0, 0, variant)
        scores = []
        for hp in range(DIL_HEADS_PER_GROUP // 2):
            cols = slice(hp * LANES, (hp + 1) * LANES)
            qp = q_ref[0, rho, blk * DIL_BLOCK:(blk + 1) * DIL_BLOCK, cols]
            zero = jnp.zeros_like(qp)
            q2 = jnp.concatenate([jnp.where(low_half, qp, zero), jnp.where(low_half, zero, qp)], axis=0)
            kw = window(kp_ref, kc_ref, kn_ref, rho, blk, cols)
            bias = jnp.concatenate([bias_ref[variant, 2 * hp], bias_ref[variant, 2 * hp + 1]], axis=0)
            scores.append(_dot_nt(q2, kw) + bias)
        probs = []
        for s in scores:
            m = jnp.max(s, axis=-1, keepdims=True)
            e = jnp.exp(s - m)
            probs.append((e.astype(jnp.bfloat16), m, jnp.sum(e, axis=-1, keepdims=True)))
        for hp, (p, m, l) in enumerate(probs):
            cols = slice(hp * LANES, (hp + 1) * LANES)
            o2 = _dot(p, window(vp_ref, vc_ref, vn_ref, rho, blk, cols)) / l
            lse2 = jnp.broadcast_to(m + jnp.log(l), (2 * DIL_BLOCK, LANES))
            o_pair = jnp.where(low_half, o2[:DIL_BLOCK], o2[DIL_BLOCK:])
            lse_pair = jnp.where(low_half, lse2[:DIL_BLOCK], lse2[DIL_BLOCK:])
            if dil == 1:
                rows = slice(blk * DIL_BLOCK, (blk + 1) * DIL_BLOCK)
                o_ref[hp, rows, :] = o_pair
                lse_ref[hp, rows, :] = lse_pair
            else:
                rows = pl.ds(blk * DIL_BLOCK * dil + rho, DIL_BLOCK, stride=dil)
                o_ref[hp, rows, :] = o_pair
                lse_ref[hp, rows, :] = lse_pair

    def class_body(rho, carry):
        for blk in range(nbs):
            unit(rho, blk)
        return carry

    if dil == 1:
        class_body(0, 0)
    else:
        lax.fori_loop(0, dil, class_body, 0, unroll=2 if nbs == 1 else 1)


def _dil_attn(qkv, bias_tab, group, dil, batch, seq):
    L = seq // dil
    nb = L // DIL_BLOCK
    assert nb >= 2
    tq = DIL_STEP_TOKENS
    span = tq // dil
    nbs = span // DIL_BLOCK
    n_spans = L // span

    def cur(col):
        return pl.BlockSpec((1, dil, span, WIDTH_G), lambda b, n: (b, 0, n, col))

    def edge(col, shift):
        return pl.BlockSpec((1, dil, DIL_BLOCK, WIDTH_G),
                            lambda b, n: (b, 0, jnp.clip(n * nbs + (shift if shift < 0 else nbs), 0, nb - 1), col))

    n_pairs = WIDTH_G // LANES
    out_spec = pl.BlockSpec((n_pairs, tq, LANES), lambda b, n: (0, b * n_spans + n, 0))
    return pl.pallas_call(
        functools.partial(_dil_kernel, dil=dil),
        grid=(batch, n_spans),
        in_specs=[cur(0), edge(1, -1), cur(1), edge(1, 1), edge(2, -1), cur(2), edge(2, 1),
                  pl.BlockSpec((None, 3, DIL_HEADS_PER_GROUP, DIL_BLOCK, 2 * DIL_BLOCK),
                               lambda b, n: (group, 0, 0, 0, 0))],
        out_specs=[out_spec, out_spec],
        out_shape=[jax.ShapeDtypeStruct((n_pairs, batch * seq, LANES), jnp.float32)] * 2,
        compiler_params=_cparams(2),
        name=f"dil_attn_d{dil}",
    )(qkv, qkv, qkv, qkv, qkv, qkv, qkv, bias_tab)


def _mix_kernel(x_ref, na_ref, o0_ref, l0_ref, o1_ref, l1_ref, o2_ref, l2_ref, qc_ref, gate_ref,
                mkv_ref, wa_ref, wb_ref, wc_ref, wo_ref, fg_ref, rw_ref, rb_ref,
                x1_ref, h_ref, idx_ref, wt_ref, seg_ref):
    T = x_ref.shape[0]
    ob = []
    for hp in range(WIDTH_G // LANES):
        l0, l1, l2 = l0_ref[hp], l1_ref[hp], l2_ref[hp]
        m = jnp.maximum(jnp.maximum(l0, l1), l2)
        e0, e1, e2 = jnp.exp(l0 - m), jnp.exp(l1 - m), jnp.exp(l2 - m)
        ob.append((e0 * o0_ref[hp] + e1 * o1_ref[hp] + e2 * o2_ref[hp]) / (e0 + e1 + e2))
    ob = jnp.concatenate(ob, axis=-1)

    mem_scale = MEM_HEAD_DIM ** -0.5
    scores = []
    for h in range(MEM_HEADS):
        cols = slice(h * MEM_HEAD_DIM, (h + 1) * MEM_HEAD_DIM)
        scores.append(_dot_nt(qc_ref[:, cols], mkv_ref[:, cols]) * mem_scale)
    probs = []
    for s in scores:
        e = jnp.exp(s - jnp.max(s, axis=-1, keepdims=True))
        probs.append((e.astype(jnp.bfloat16), jnp.sum(e, axis=-1, keepdims=True)))
    oc = []
    for h, (p, l) in enumerate(probs):
        oc.append(_dot(p, mkv_ref[:, WIDTH_C + h * MEM_HEAD_DIM:WIDTH_C + (h + 1) * MEM_HEAD_DIM]) / l)
    oc = jnp.concatenate(oc, axis=-1).astype(jnp.bfloat16)

    y_a = _dot(na_ref[...], wa_ref[...])
    y_b = _dot(ob.astype(jnp.bfloat16), wb_ref[...])
    y_c = _dot(oc, wc_ref[...])
    merged = (gate_ref[:, 0:D_MODEL].astype(jnp.float32) * y_a
              + gate_ref[:, D_MODEL:2 * D_MODEL].astype(jnp.float32) * y_b
              + gate_ref[:, 2 * D_MODEL:3 * D_MODEL].astype(jnp.float32) * y_c)
    x1 = x_ref[...] + _dot(merged.astype(jnp.bfloat16), wo_ref[...])
    x1_ref[...] = x1

    h = _rms(x1, fg_ref[...])
    h_ref[...] = h.astype(jnp.bfloat16)
    logits = lax.dot_general(rw_ref[...], h, (((1,), (1,)), ((), ())), preferred_element_type=jnp.float32,
                             precision=lax.Precision.HIGHEST) + rb_ref[...]
    expert = lax.broadcasted_iota(jnp.int32, (N_EXPERTS, T), 0)
    vals, idxs = [], []
    for _ in range(TOP_K):
        mx = jnp.max(logits, axis=0, keepdims=True)
        sel = jnp.min(jnp.where(logits == mx, expert, N_EXPERTS), axis=0, keepdims=True)
        vals.append(mx)
        idxs.append(sel)
        logits = jnp.where(expert == sel, -jnp.inf, logits)
    ev = [jnp.exp(v - vals[0]) for v in vals]
    den = ev[0] + ev[1] + ev[2] + ev[3]
    idx_ref[...] = jnp.concatenate(idxs, axis=0)
    wt_ref[...] = jnp.concatenate([e / den for e in ev], axis=0)
    member = sum((expert == sel).astype(jnp.float32) for sel in idxs)
    for s in range(T // MOE_TILE):
        cnt = jnp.sum(member[:, s * MOE_TILE:(s + 1) * MOE_TILE], axis=1, keepdims=True)
        seg = jnp.floor((cnt + (MOE_PIECE - 1)) * (1.0 / MOE_PIECE)) * MOE_PIECE
        seg_ref[s] = jnp.broadcast_to(seg, (N_EXPERTS, LANES))


def _mix_out(x2d, na_out, dil_outs, qc, gates, mkv, wa, wb, wc, wo, ffn_g, router_w, router_b, batch, seq):
    n = x2d.shape[0]
    T = PROJ_TILE
    tiles_per_batch = seq // T
    mem_len = mkv.shape[0] // batch
    row = lambda w: pl.BlockSpec((T, w), lambda i: (i, 0))
    full = lambda a: pl.BlockSpec(a.shape, lambda i: (0,) * a.ndim)
    in_specs = [row(D_MODEL), row(WIDTH_A)]
    args = [x2d, na_out]
    for o, l in dil_outs:
        pair_rows = pl.BlockSpec((WIDTH_G // LANES, T, LANES), lambda i: (0, i, 0))
        in_specs += [pair_rows, pair_rows]
        args += [o, l]
    in_specs += [row(WIDTH_C), row(3 * D_MODEL),
                 pl.BlockSpec((mem_len, 2 * WIDTH_C), lambda i: (i // tiles_per_batch, 0)),
                 full(wa), full(wb), full(wc), full(wo), full(ffn_g), full(router_w), full(router_b)]
    args += [qc, gates, mkv, wa, wb, wc, wo, ffn_g, router_w, router_b]
    return pl.pallas_call(
        _mix_kernel,
        grid=(n // T,),
        in_specs=in_specs,
        out_specs=[row(D_MODEL), row(D_MODEL), pl.BlockSpec((TOP_K, T), lambda i: (0, i)),
                   pl.BlockSpec((TOP_K, T), lambda i: (0, i)),
                   pl.BlockSpec((T // MOE_TILE, N_EXPERTS, LANES), lambda i: (i, 0, 0))],
        out_shape=[jax.ShapeDtypeStruct((n, D_MODEL), jnp.float32),
                   jax.ShapeDtypeStruct((n, D_MODEL), jnp.bfloat16),
                   jax.ShapeDtypeStruct((TOP_K, n), jnp.int32),
                   jax.ShapeDtypeStruct((TOP_K, n), jnp.float32),
                   jax.ShapeDtypeStruct((n // MOE_TILE, N_EXPERTS, LANES), jnp.float32)],
        compiler_params=_cparams(1),
        name="mix_out",
    )(*args)


def _route_kernel(idx_ref, segs_ref, lpos_ref, ptab_ref, be_ref, pe_ref, carry_ref, pstart_ref):
    i = pl.program_id(0)
    T = idx_ref.shape[1]
    n_blocks = be_ref.shape[1]
    expert = lax.broadcasted_iota(jnp.int32, (N_EXPERTS, T), 0)
    idx = idx_ref[...]
    onehots = [expert == idx[k:k + 1, :] for k in range(TOP_K)]
    member = sum(o.astype(jnp.float32) for o in onehots)
    seg_b = segs_ref[i]
    seg = seg_b[:, 0:1]
    r = lax.broadcasted_iota(jnp.int32, (N_EXPERTS, N_EXPERTS), 0)
    c = lax.broadcasted_iota(jnp.int32, (N_EXPERTS, N_EXPERTS), 1)
    lower_incl = (r >= c).astype(jnp.float32)

    def cumsum_experts(v):
        return jnp.dot(lower_incl, v, preferred_element_type=jnp.float32, precision=lax.Precision.HIGHEST)

    @pl.when(i == 0)
    def _():
        counts = jnp.sum(segs_ref[...], axis=0)
        padded = jnp.floor((counts + (MOE_BLOCK - 1)) * (1.0 / MOE_BLOCK)) * MOE_BLOCK
        pad_ends = cumsum_experts(padded)
        pstart_ref[...] = pad_ends - padded
        carry_ref[...] = jnp.zeros_like(carry_ref)
        pe_ref[...] = pad_ends[:, 0:1].astype(jnp.int32)
        blk_start = (lax.broadcasted_iota(jnp.int32, (N_EXPERTS, n_blocks), 1) * MOE_BLOCK).astype(jnp.float32)
        be = jnp.sum((pad_ends[:, 0:1] <= blk_start).astype(jnp.int32), axis=0, keepdims=True)
        be_ref[...] = jnp.minimum(be, N_EXPERTS - 1)

    local_end = cumsum_experts(seg_b)[:, 0:1]
    local_start = local_end - seg
    global_start = pstart_ref[:, 0:1] + carry_ref[:, 0:1]
    rr = lax.broadcasted_iota(jnp.int32, (T, T), 0)
    cc = lax.broadcasted_iota(jnp.int32, (T, T), 1)
    earlier = (rr < cc).astype(jnp.bfloat16)
    before = _dot(member.astype(jnp.bfloat16), earlier)
    base = local_start + before
    lpos = [jnp.sum(jnp.where(o, base, 0.0), axis=0, keepdims=True) for o in onehots]
    lpos_ref[...] = jnp.concatenate(lpos, axis=0).astype(jnp.int32)

    inv = 1.0 / MOE_PIECE
    piece = lax.broadcasted_iota(jnp.int32, (N_EXPERTS, MOE_PIECE_TAB), 1).astype(jnp.float32)
    first = local_start * inv
    inside = jnp.logical_and(piece >= first, piece < local_end * inv)
    dst = jnp.sum(jnp.where(inside, global_start * inv + piece - first, 0.0), axis=0, keepdims=True)
    n_pieces = local_end[N_EXPERTS - 1:N_EXPERTS, :] * inv
    spare = (n_blocks * MOE_BLOCK // MOE_PIECE
             + (i % 2) * (MOE_TILE_ROWS // MOE_PIECE)).astype(jnp.float32)
    dst = jnp.where(piece[0:1, :] >= n_pieces, spare + piece[0:1, :], dst)
    dst = jnp.where(piece[0:1, :] == MOE_PIECE_TAB - 1, n_pieces, dst)
    ptab_ref[...] = dst.astype(jnp.int32)
    carry_ref[...] += seg


def _moe_route(topk_idx, segs, n_blocks):
    n = topk_idx.shape[1]
    T = MOE_TILE
    acc = pltpu.VMEM((N_EXPERTS, LANES), jnp.float32)
    return pl.pallas_call(
        _route_kernel,
        grid=(n // T,),
        in_specs=[pl.BlockSpec((TOP_K, T), lambda i: (0, i)),
                  pl.BlockSpec(segs.shape, lambda i: (0, 0, 0))],
        out_specs=[pl.BlockSpec((TOP_K, T), lambda i: (0, i)),
                   pl.BlockSpec((None, 1, MOE_PIECE_TAB), lambda i: (i, 0, 0)),
                   pl.BlockSpec((1, n_blocks), lambda i: (0, 0)),
                   pl.BlockSpec((N_EXPERTS, 1), lambda i: (0, 0))],
        out_shape=[jax.ShapeDtypeStruct((TOP_K, n), jnp.int32),
                   jax.ShapeDtypeStruct((n // T, 1, MOE_PIECE_TAB), jnp.int32),
                   jax.ShapeDtypeStruct((1, n_blocks), jnp.int32),
                   jax.ShapeDtypeStruct((N_EXPERTS, 1), jnp.int32)],
        scratch_shapes=[acc, acc],
        compiler_params=_cparams(1),
        name="moe_route",
    )(topk_idx, segs)


def _piece(ref, p):
    return ref.at[pl.ds(pl.multiple_of(p * MOE_PIECE, MOE_PIECE), MOE_PIECE), :]


def _dispatch_kernel(pe_ref, ptab_ref, lpos_ref, h_ref, xs_hbm, zbuf, pbuf, sem, zsem):
    i = pl.program_id(0)
    T = h_ref.shape[0]
    half = h_ref.shape[1] // 2

    def zero_copy(e):
        end = pe_ref[e]
        start = pl.multiple_of(end - MOE_BLOCK, MOE_BLOCK)
        return pltpu.make_async_copy(zbuf, xs_hbm.at[pl.ds(start, MOE_BLOCK), :], zsem)

    def nonempty(e):
        return pe_ref[e] > (pe_ref[e - 1] if e else 0)

    @pl.when(i == 0)
    def _():
        zbuf[...] = jnp.zeros_like(zbuf)
        for e in range(N_EXPERTS):
            @pl.when(nonempty(e))
            def _():
                zero_copy(e).start()
        for e in range(N_EXPERTS):
            @pl.when(nonempty(e))
            def _():
                zero_copy(e).wait()

        def tail_copy(b):
            return pltpu.make_async_copy(
                zbuf, xs_hbm.at[pl.ds(pl.multiple_of(b * MOE_BLOCK, MOE_BLOCK), MOE_BLOCK), :], zsem)
        first_tail = pe_ref[N_EXPERTS - 1] // MOE_BLOCK
        n_blocks = xs_hbm.shape[0] // MOE_BLOCK
        lax.fori_loop(first_tail, n_blocks, lambda b, c: (tail_copy(b).start(), c)[1], 0)
        lax.fori_loop(first_tail, n_blocks, lambda b, c: (tail_copy(b).wait(), c)[1], 0)

    row = lax.broadcasted_iota(jnp.int32, (MOE_TILE_ROWS, T), 0)
    lpos = lpos_ref[...]
    perm = sum((row == lpos[k:k + 1, :]).astype(jnp.bfloat16) for k in range(TOP_K))
    lo = lax.bitcast_convert_type(_dot(perm, h_ref[:, :half]), jnp.uint32)
    hi = lax.bitcast_convert_type(_dot(perm, h_ref[:, half:]), jnp.uint32)
    slot = i % 2
    pbuf[slot] = hi | (lo >> 16)

    n_tile_pieces = MOE_TILE_ROWS // MOE_PIECE

    def piece_copy(s, p):
        return pltpu.make_async_copy(_piece(pbuf.at[s], p), _piece(xs_hbm, ptab_ref[p]), sem.at[s])

    def wait_all(s):
        lax.fori_loop(0, n_tile_pieces, lambda p, c: (piece_copy(s, 0).wait(), c)[1], 0, unroll=8)

    lax.fori_loop(0, n_tile_pieces, lambda p, c: (piece_copy(slot, p).start(), c)[1], 0, unroll=16)

    @pl.when(i > 0)
    def _():
        wait_all(1 - slot)

    @pl.when(i == pl.num_programs(0) - 1)
    def _():
        wait_all(slot)


def _moe_dispatch(h, lpos, ptab, pad_ends, n_blocks):
    n, width = h.shape
    T = MOE_TILE
    n_steps = n // T

    def kernel(pe_ref, ptab_ref, *rest):
        _dispatch_kernel(pe_ref, ptab_ref.at[0], *rest)

    grid_spec = pltpu.PrefetchScalarGridSpec(
        num_scalar_prefetch=1,
        grid=(n_steps,),
        in_specs=[pl.BlockSpec((None, 1, MOE_PIECE_TAB), lambda i, pe: (i, 0, 0), memory_space=pltpu.SMEM),
                  pl.BlockSpec((TOP_K, T), lambda i, pe: (0, i)),
                  pl.BlockSpec((T, width), lambda i, pe: (i, 0))],
        out_specs=pl.BlockSpec(memory_space=pl.ANY),
        scratch_shapes=[pltpu.VMEM((MOE_BLOCK, width // 2), jnp.uint32),
                        pltpu.VMEM((2, MOE_TILE_ROWS, width // 2), jnp.uint32),
                        pltpu.SemaphoreType.DMA((2,)),
                        pltpu.SemaphoreType.DMA(())],
    )
    return pl.pallas_call(
        kernel,
        grid_spec=grid_spec,
        out_shape=jax.ShapeDtypeStruct(((n_blocks + MOE_SPARE_BLOCKS) * MOE_BLOCK, width // 2), jnp.uint32),
        compiler_params=_cparams(1),
        name="moe_dispatch",
    )(pad_ends, ptab, lpos, h)


def _moe_kernel(be_ref, pe_ref, xs_ref, wgu_ref, bgu_ref, wdn_ref, bdn_ref, y_ref, wgu_bf, wdn_bf):
    i = pl.program_id(0)
    n_valid = pe_ref[N_EXPERTS - 1] // MOE_BLOCK
    new_expert = jnp.logical_or(i == 0, be_ref[i] != be_ref[jnp.maximum(i - 1, 0)])

    @pl.when(jnp.logical_and(i < n_valid, new_expert))
    def _():
        wgu_bf[...] = wgu_ref[0].astype(jnp.bfloat16)
        wdn_bf[...] = wdn_ref[0].astype(jnp.bfloat16)

    @pl.when(i < n_valid)
    def _():
        x = _unpack_bf16_pairs(xs_ref[...])
        gu = _dot(x, wgu_bf[...]) + bgu_ref[0]
        g = jnp.minimum(gu[:, :D_FF], SWIGLU_LIMIT)
        u = jnp.clip(gu[:, D_FF:], -SWIGLU_LIMIT, SWIGLU_LIMIT)
        a = (u + 1.0) * (g * _sigmoid(SWIGLU_ALPHA * g))
        y_ref[...] = _pack_bf16_pairs(_dot(a.astype(jnp.bfloat16), wdn_bf[...]) + bdn_ref[0])

    @pl.when(i >= n_valid)
    def _():
        y_ref[...] = jnp.zeros_like(y_ref)


def _moe_experts(xs, block_expert, pad_ends, wgu, bgu, wdn, bdn):
    n_blocks = block_expert.shape[0]

    def row_blk(i, be, pe):
        return (jnp.minimum(i, jnp.maximum(pe[N_EXPERTS - 1] // MOE_BLOCK - 1, 0)), 0)

    grid_spec = pltpu.PrefetchScalarGridSpec(
        num_scalar_prefetch=2,
        grid=(n_blocks,),
        in_specs=[pl.BlockSpec((MOE_BLOCK, D_MODEL // 2), row_blk),
                  pl.BlockSpec((1, D_MODEL, 2 * D_FF), lambda i, be, pe: (be[i], 0, 0)),
                  pl.BlockSpec((1, 1, 2 * D_FF), lambda i, be, pe: (be[i], 0, 0)),
                  pl.BlockSpec((1, D_FF, D_MODEL), lambda i, be, pe: (be[i], 0, 0)),
                  pl.BlockSpec((1, 1, D_MODEL), lambda i, be, pe: (be[i], 0, 0))],
        out_specs=pl.BlockSpec((MOE_BLOCK, D_MODEL // 2), lambda i, be, pe: (i, 0)),
        scratch_shapes=[pltpu.VMEM((D_MODEL, 2 * D_FF), jnp.bfloat16),
                        pltpu.VMEM((D_FF, D_MODEL), jnp.bfloat16)],
    )
    return pl.pallas_call(
        _moe_kernel,
        grid_spec=grid_spec,
        out_shape=jax.ShapeDtypeStruct((n_blocks * MOE_BLOCK, D_MODEL // 2), jnp.uint32),
        compiler_params=_cparams(1),
        name="moe_experts",
    )(block_expert, pad_ends, xs, wgu, bgu, wdn, bdn)


def _gather_pieces(src_hbm, ptab_ref, dst_ref, sem):
    spare = src_hbm.shape[0] // MOE_PIECE

    def body(p, carry):
        src = ptab_ref[p]
        src = jnp.where(src >= spare, 0, src)
        pltpu.make_async_copy(_piece(src_hbm, src), _piece(dst_ref, p), sem).start()
        return carry
    lax.fori_loop(0, MOE_TILE_ROWS // MOE_PIECE, body, 0, unroll=16)


def _wait_pieces(src_hbm, dst_ref, sem):
    def body(p, carry):
        pltpu.make_async_copy(_piece(src_hbm, 0), _piece(dst_ref, 0), sem).wait()
        return carry
    lax.fori_loop(0, MOE_TILE_ROWS // MOE_PIECE, body, 0, unroll=8)


def _combine_kernel(ptab_ref, ptab_next_ref, ys_hbm, lpos_ref, wt_ref, x1_ref, g_ref, out_ref, gbuf, sem):
    i = pl.program_id(0)
    n_steps = pl.num_programs(0)
    slot = i % 2
    T = x1_ref.shape[0]

    @pl.when(i == 0)
    def _():
        _gather_pieces(ys_hbm, ptab_ref, gbuf.at[0], sem.at[0])

    @pl.when(i + 1 < n_steps)
    def _():
        _gather_pieces(ys_hbm, ptab_next_ref, gbuf.at[1 - slot], sem.at[1 - slot])

    _wait_pieces(ys_hbm, gbuf.at[slot], sem.at[slot])
    rows = _unpack_bf16_pairs(gbuf[slot])
    col = lax.broadcasted_iota(jnp.int32, (T, MOE_TILE_ROWS), 1)
    lpos = lpos_ref[...]
    wt = wt_ref[...]
    cmb = jnp.zeros((T, MOE_TILE_ROWS), jnp.float32)
    for k in range(TOP_K):
        cmb = jnp.where(col == lpos[:, k:k + 1], wt[:, k:k + 1], cmb)
    hi = cmb.astype(jnp.bfloat16)
    lo = (cmb - hi.astype(jnp.float32)).astype(jnp.bfloat16)
    y2 = _dot(jnp.concatenate([hi, lo], axis=0), rows)
    y = x1_ref[...] + y2[:T] + y2[T:]
    out_ref[...] = _rms(y, g_ref[...])


def _moe_combine(ys, ptab, lpos_t, wts, x1, final_g):
    n = x1.shape[0]
    T = MOE_TILE
    n_steps = n // T
    smem_blk = lambda shift: pl.BlockSpec(
        (None, 1, MOE_PIECE_TAB), lambda i: (jnp.minimum(i + shift, n_steps - 1), 0, 0),
        memory_space=pltpu.SMEM)

    def kernel(ptab_ref, ptab_next_ref, *rest):
        _combine_kernel(ptab_ref.at[0], ptab_next_ref.at[0], *rest)

    return pl.pallas_call(
        kernel,
        grid=(n_steps,),
        in_specs=[smem_blk(0), smem_blk(1),
                  pl.BlockSpec(memory_space=pl.ANY),
                  pl.BlockSpec((T, TOP_K), lambda i: (i, 0)),
                  pl.BlockSpec((T, TOP_K), lambda i: (i, 0)),
                  pl.BlockSpec((T, D_MODEL), lambda i: (i, 0)),
                  pl.BlockSpec((1, D_MODEL), lambda i: (0, 0))],
        out_specs=pl.BlockSpec((T, D_MODEL), lambda i: (i, 0)),
        out_shape=jax.ShapeDtypeStruct((n, D_MODEL), jnp.float32),
        scratch_shapes=[pltpu.VMEM((2, MOE_TILE_ROWS, D_MODEL // 2), jnp.uint32),
                        pltpu.SemaphoreType.DMA((2,))],
        compiler_params=_cparams(1),
        name="moe_combine",
    )(ptab, ptab, ys, lpos_t, wts, x1, final_g)


def kernel(x, mem, attn_norm_g, mem_norm_g, w_in, w_mem_kv, na_rpb, t5_rel_bias, w_branch_a, w_branch_b,
           w_branch_c, w_out, ffn_norm_g, router_w, router_b, expert_w_gate_up, expert_b_gate_up,
           expert_w_down, expert_b_down, final_norm_g):
    B, S, D = x.shape
    depth = w_in.shape[0]
    bf = jnp.bfloat16
    x2d = x.reshape(B * S, D)
    mem2d = mem.reshape(B * mem.shape[1], D)
    assert depth == 1, "single-layer block (per-layer parameters carry a leading axis of 1)"
    l = 0
    na_bias, dil_bias = _bias_prep(na_rpb[l], t5_rel_bias)
    mkv = _mem_kv(mem2d, mem_norm_g[l][None], w_mem_kv[l].astype(bf))
    w_in_bf = w_in[l].astype(bf)
    na_qkv, d0, d1, d2, qc = _in_proj(x2d, attn_norm_g[l][None], w_in_bf, B, S)
    gates = _gate_proj(x2d, attn_norm_g[l][None], w_in_bf)
    na_out = _na_attn(na_qkv, na_bias, B, S)
    dil_outs = [_dil_attn(qkv, dil_bias, gi, dil, B, S)
                for gi, (qkv, (_, dil)) in enumerate(zip((d0, d1, d2), DIL_GROUPS))]
    x1, h, topk_idx, topk_w, segs = _mix_out(
        x2d, na_out, dil_outs, qc, gates, mkv, w_branch_a[l].astype(bf), w_branch_b[l].astype(bf),
        w_branch_c[l].astype(bf), w_out[l].astype(bf), ffn_norm_g[l][None], router_w[l].T,
        router_b[l][:, None], B, S)

    n = B * S
    max_rows = n * TOP_K + (n // MOE_TILE) * N_EXPERTS * (MOE_PIECE - 1)
    n_blocks = -(-max_rows // MOE_BLOCK) + N_EXPERTS
    lpos, ptab, block_expert, pad_ends = _moe_route(topk_idx, segs, n_blocks)
    block_expert = block_expert.reshape(n_blocks)
    pad_ends = pad_ends.reshape(N_EXPERTS)
    xs = _moe_dispatch(h, lpos, ptab, pad_ends, n_blocks)
    ys = _moe_experts(xs, block_expert, pad_ends, expert_w_gate_up[l], expert_b_gate_up[l][:, None, :],
                      expert_w_down[l], expert_b_down[l][:, None, :])
    out = _moe_combine(ys, ptab, lpos.T, topk_w.T, x1, final_norm_g[None])
    return out.reshape(B, S, D)
```

```python
import functools
import math

import numpy as np
import jax
import jax.numpy as jnp
from jax import lax
from jax.experimental import pallas as pl
from jax.experimental.pallas import tpu as pltpu

D_MODEL = 1024
GRID_W = 64
MEM_HEADS = 4
MEM_HEAD_DIM = 128
HEAD_DIM = 64
NA_HEADS = 8
NA_WIN_ROWS = 8
NA_WIN_COLS = 16
DIL_GROUPS = ((128, 1), (512, 4), (2048, 16))
DIL_HEADS_PER_GROUP = 4
DIL_BLOCK = 128
T5_BUCKETS = 32
T5_MAX_DIST = 1024
N_EXPERTS = 32
TOP_K = 4
D_FF = D_MODEL
SWIGLU_ALPHA = 1.702
SWIGLU_LIMIT = 7.0
RMS_EPS = 1e-6
NEG_INF = -1e30

WIDTH_A = NA_HEADS * HEAD_DIM
WIDTH_B = DIL_HEADS_PER_GROUP * len(DIL_GROUPS) * HEAD_DIM
WIDTH_G = DIL_HEADS_PER_GROUP * HEAD_DIM
WIDTH_C = MEM_HEADS * MEM_HEAD_DIM
OFF_QA, OFF_KA, OFF_VA = 0, WIDTH_A, 2 * WIDTH_A
OFF_QB = 3 * WIDTH_A
OFF_KB = OFF_QB + WIDTH_B
OFF_VB = OFF_KB + WIDTH_B
OFF_QC = OFF_VB + WIDTH_B
OFF_GATE = OFF_QC + WIDTH_C
IN_COLS = OFF_GATE + 3 * D_MODEL

LANES = 128
MXU_WIDTH = 256
VMEM_LIMIT = 56 * 1024 * 1024

IN_PROJ_TILE = 1024
PROJ_TILE = 512
NA_ROWS_PER_STEP = 16
DIL_STEP_TOKENS = 2048
MOE_BLOCK = 512
MOE_TILE = 256
MOE_PIECE = 8
MOE_TILE_ROWS = -(-(TOP_K * MOE_TILE + N_EXPERTS * (MOE_PIECE - 1)) // MXU_WIDTH) * MXU_WIDTH
MOE_PIECE_TAB = 256
assert MOE_TILE_ROWS // MOE_PIECE < MOE_PIECE_TAB
MOE_SPARE_BLOCKS = -(-2 * MOE_TILE_ROWS // MOE_BLOCK)


def _cparams(n_axes):
    return pltpu.CompilerParams(
        dimension_semantics=("arbitrary",) * n_axes, vmem_limit_bytes=VMEM_LIMIT)


def _rms(x, g):
    return x * lax.rsqrt(jnp.mean(x * x, axis=-1, keepdims=True) + RMS_EPS) * g


def _sigmoid(x):
    return 1.0 / (1.0 + jnp.exp(-x))


def _pack_bf16_pairs(x):
    w = x.shape[1] // 2
    bits = lax.bitcast_convert_type(x.astype(jnp.bfloat16).astype(jnp.float32), jnp.uint32)
    return (bits[:, w:] & jnp.uint32(0xFFFF0000)) | (bits[:, :w] >> 16)


def _unpack_bf16_pairs(p):
    lo = lax.bitcast_convert_type(p << 16, jnp.float32)
    hi = lax.bitcast_convert_type(p & jnp.uint32(0xFFFF0000), jnp.float32)
    return jnp.concatenate([lo, hi], axis=1).astype(jnp.bfloat16)


def _dot(a, b):
    return jnp.dot(a, b, preferred_element_type=jnp.float32)


def _dot_nt(a, b):
    return lax.dot_general(a, b, (((1,), (1,)), ((), ())), preferred_element_type=jnp.float32)


def _mem_kv_kernel(mem_ref, g_ref, w_ref, out_ref):
    h = _rms(mem_ref[...], g_ref[...]).astype(jnp.bfloat16)
    out_ref[...] = _dot(h, w_ref[...]).astype(jnp.bfloat16)


def _mem_kv(mem2d, g, w):
    rows = mem2d.shape[0]
    return pl.pallas_call(
        _mem_kv_kernel,
        out_shape=jax.ShapeDtypeStruct((rows, 2 * WIDTH_C), jnp.bfloat16),
        name="mem_kv",
        compiler_params=pltpu.CompilerParams(vmem_limit_bytes=VMEM_LIMIT),
    )(mem2d, g, w)


def _in_proj_kernel(x_ref, g_ref, w_ref, na_ref, d0_ref, d1_ref, d2_ref, qc_ref, acc_ref):
    T = x_ref.shape[0]
    h = _rms(x_ref[...], g_ref[...]).astype(jnp.bfloat16)
    q_scale = HEAD_DIM ** -0.5

    for c in range(3):
        acc = _dot(h, w_ref[:, c * WIDTH_A:(c + 1) * WIDTH_A])
        if c == 0:
            acc = acc * q_scale
        na_ref[:, c * WIDTH_A:(c + 1) * WIDTH_A] = acc.astype(jnp.bfloat16)

    for gi, ((_, dil), d_ref) in enumerate(zip(DIL_GROUPS, (d0_ref, d1_ref, d2_ref))):
        for kind, off in enumerate((OFF_QB, OFF_KB, OFF_VB)):
            lo = off + gi * WIDTH_G
            acc = _dot(h, w_ref[:, lo:lo + WIDTH_G])
            if kind == 0:
                acc = acc * q_scale
            cols = slice(kind * WIDTH_G, (kind + 1) * WIDTH_G)
            if dil == 1:
                d_ref[0, 0, :, cols] = acc.astype(jnp.bfloat16)
            else:
                for c in range(WIDTH_G // LANES):
                    acc_ref[c] = acc[:, c * LANES:(c + 1) * LANES]
                for rho in range(dil):
                    d_ref[0, rho, :, cols] = jnp.concatenate(
                        [acc_ref[c, pl.ds(rho, T // dil, stride=dil), :] for c in range(WIDTH_G // LANES)],
                        axis=-1).astype(jnp.bfloat16)

    qc_ref[...] = _dot(h, w_ref[:, OFF_QC:OFF_QC + WIDTH_C]).astype(jnp.bfloat16)


def _gate_proj_kernel(x_ref, g_ref, w_ref, gate_ref):
    h = _rms(x_ref[...], g_ref[...]).astype(jnp.bfloat16)
    chunk = 512
    first = OFF_GATE - IN_COLS // 2
    for c in range(3 * D_MODEL // chunk):
        lo = first + c * chunk
        gate_ref[:, c * chunk:(c + 1) * chunk] = _sigmoid(_dot(h, w_ref[:, lo:lo + chunk])).astype(jnp.bfloat16)


def _gate_proj(x2d, g, w_bf16):
    n = x2d.shape[0]
    T = IN_PROJ_TILE
    assert (IN_COLS // 2) % LANES == 0 and IN_COLS // 2 <= OFF_GATE
    return pl.pallas_call(
        _gate_proj_kernel,
        grid=(n // T,),
        in_specs=[pl.BlockSpec((T, D_MODEL), lambda i: (i, 0)),
                  pl.BlockSpec((1, D_MODEL), lambda i: (0, 0)),
                  pl.BlockSpec((D_MODEL, IN_COLS // 2), lambda i: (0, 1))],
        out_specs=pl.BlockSpec((T, 3 * D_MODEL), lambda i: (i, 0)),
        out_shape=jax.ShapeDtypeStruct((n, 3 * D_MODEL), jnp.bfloat16),
        compiler_params=_cparams(1),
        name="gate_proj",
    )(x2d, g, w_bf16)


def _in_proj(x2d, g, w_bf16, batch, seq):
    n = x2d.shape[0]
    T = IN_PROJ_TILE
    tiles_per_batch = seq // T
    out_shape = [jax.ShapeDtypeStruct((n, 3 * WIDTH_A), jnp.bfloat16)]
    out_specs = [pl.BlockSpec((T, 3 * WIDTH_A), lambda i: (i, 0))]
    for _, dil in DIL_GROUPS:
        out_shape.append(jax.ShapeDtypeStruct((batch, dil, seq // dil, 3 * WIDTH_G), jnp.bfloat16))
        out_specs.append(pl.BlockSpec((1, dil, T // dil, 3 * WIDTH_G),
                                      lambda i: (i // tiles_per_batch, 0, i % tiles_per_batch, 0)))
    out_shape += [jax.ShapeDtypeStruct((n, WIDTH_C), jnp.bfloat16)]
    out_specs += [pl.BlockSpec((T, WIDTH_C), lambda i: (i, 0))]
    return pl.pallas_call(
        _in_proj_kernel,
        grid=(n // T,),
        in_specs=[pl.BlockSpec((T, D_MODEL), lambda i: (i, 0)),
                  pl.BlockSpec((1, D_MODEL), lambda i: (0, 0)),
                  pl.BlockSpec((D_MODEL, OFF_GATE), lambda i: (0, 0))],
        out_specs=out_specs,
        out_shape=out_shape,
        scratch_shapes=[pltpu.VMEM((WIDTH_G // LANES, T, LANES), jnp.float32)],
        compiler_params=_cparams(1),
        name="in_proj",
    )(x2d, g, w_bf16)


def _na_kernel(q_ref, k_ref, v_ref, bias_ref, out_ref, *, rows):
    i = pl.program_id(1)
    lane = lax.broadcasted_iota(jnp.int32, (GRID_W, LANES), 1)
    low_half = lane < HEAD_DIM
    n_win = NA_WIN_ROWS * GRID_W

    def row_body(j, carry):
        qr = i * NA_ROWS_PER_STEP + j
        rs = jnp.clip(qr - NA_WIN_ROWS // 2, 0, rows - NA_WIN_ROWS)
        shift = rs - qr + NA_WIN_ROWS - 1
        q_off = pl.multiple_of(j * GRID_W, GRID_W)
        k_off = pl.multiple_of(rs * GRID_W, GRID_W)
        scores = []
        for hp in range(NA_HEADS // 2):
            cols = slice(hp * LANES, (hp + 1) * LANES)
            qp = q_ref[pl.ds(q_off, GRID_W), cols]
            kp = k_ref[pl.ds(k_off, n_win), cols]
            zero = jnp.zeros_like(qp)
            q2 = jnp.concatenate([jnp.where(low_half, qp, zero), jnp.where(low_half, zero, qp)], axis=0)
            bias = jnp.concatenate(
                [jnp.concatenate([bias_ref[shift + 2 * p, 2 * hp + hh] for p in range(NA_WIN_ROWS // 2)], axis=-1)
                 for hh in range(2)], axis=0)
            scores.append(_dot_nt(q2, kp) + bias)
        probs = []
        for s in scores:
            e = jnp.exp(s - jnp.max(s, axis=-1, keepdims=True))
            probs.append((e.astype(jnp.bfloat16), jnp.sum(e, axis=-1, keepdims=True)))
        for hp, (p, l) in enumerate(probs):
            cols = slice(hp * LANES, (hp + 1) * LANES)
            o2 = _dot(p, v_ref[pl.ds(k_off, n_win), cols]) / l
            out_ref[pl.ds(q_off, GRID_W), cols] = jnp.where(
                low_half, o2[:GRID_W], o2[GRID_W:]).astype(jnp.bfloat16)
        return carry

    lax.fori_loop(0, NA_ROWS_PER_STEP, row_body, 0, unroll=4)


def _na_attn(na_qkv, bias_tab, batch, seq):
    rows = seq // GRID_W
    steps = rows // NA_ROWS_PER_STEP
    tq = NA_ROWS_PER_STEP * GRID_W
    n = na_qkv.shape[0]
    return pl.pallas_call(
        functools.partial(_na_kernel, rows=rows),
        grid=(batch, steps),
        in_specs=[pl.BlockSpec((tq, WIDTH_A), lambda b, i: (b * steps + i, 0)),
                  pl.BlockSpec((seq, WIDTH_A), lambda b, i: (b, 1)),
                  pl.BlockSpec((seq, WIDTH_A), lambda b, i: (b, 2)),
                  pl.BlockSpec(bias_tab.shape, lambda b, i: (0, 0, 0, 0))],
        out_specs=pl.BlockSpec((tq, WIDTH_A), lambda b, i: (b * steps + i, 0)),
        out_shape=jax.ShapeDtypeStruct((n, WIDTH_A), jnp.bfloat16),
        compiler_params=_cparams(2),
        name="na_attn",
    )(na_qkv, na_qkv, na_qkv, bias_tab)


def _t5_bucket_np(rel):
    half = T5_BUCKETS // 2
    max_exact = half // 2
    ret = np.where(rel > 0, half, 0)
    n = np.abs(rel)
    nf = np.maximum(n, 1).astype(np.float32)
    large = max_exact + (np.log(nf / np.float32(max_exact)) / np.float32(math.log(T5_MAX_DIST / max_exact))
                         * np.float32(half - max_exact)).astype(np.int32)
    large = np.minimum(large, half - 1)
    return ret + np.where(n < max_exact, n, large)


def _dil_bucket_table():
    q = np.arange(DIL_BLOCK)[:, None]
    j = np.arange(2 * DIL_BLOCK)[None, :]
    tabs = []
    for window, dil in DIL_GROUPS:
        half = (window // 2) // dil
        assert half == DIL_BLOCK // 2
        rel = (j - half) - q
        tabs.append(np.where(np.abs(rel) <= half, _t5_bucket_np(rel * dil), -1))
    return np.stack(tabs).astype(np.int32)


def _bias_prep_kernel(rpb_ref, t5_ref, bucket_ref, na_ref, dil_ref):
    n_dc = 2 * NA_WIN_COLS - 1
    n_dr = 2 * NA_WIN_ROWS - 1
    qc = lax.broadcasted_iota(jnp.int32, (GRID_W, 2 * GRID_W), 0)
    lane = lax.broadcasted_iota(jnp.int32, (GRID_W, 2 * GRID_W), 1)
    second = lane >= GRID_W
    kc = jnp.where(second, lane - GRID_W, lane)
    diff = kc - qc
    cs = jnp.clip(qc - NA_WIN_COLS // 2, 0, GRID_W - NA_WIN_COLS)
    valid = jnp.logical_and(kc >= cs, kc < cs + NA_WIN_COLS)

    def na_tile(mh, carry):
        m = mh // NA_HEADS
        h = mh - m * NA_HEADS
        base = (h * n_dr + m) * n_dc
        acc = jnp.zeros((GRID_W, 2 * GRID_W), jnp.float32)
        for d in range(n_dc):
            v = jnp.where(second, rpb_ref[base + n_dc + d], rpb_ref[base + d])
            acc = jnp.where(diff == d - (NA_WIN_COLS - 1), v, acc)
        na_ref[m, h] = jnp.where(valid, acc, NEG_INF)
        return carry

    lax.fori_loop(0, (n_dr - 1) * NA_HEADS, na_tile, 0)

    j = lax.broadcasted_iota(jnp.int32, (DIL_BLOCK, 2 * DIL_BLOCK), 1)
    half = DIL_BLOCK // 2
    n_heads = DIL_HEADS_PER_GROUP * len(DIL_GROUPS)
    for gi in range(len(DIL_GROUPS)):
        bucket = bucket_ref[gi]
        in_win = bucket >= 0
        for h in range(DIL_HEADS_PER_GROUP):
            acc = jnp.zeros((DIL_BLOCK, 2 * DIL_BLOCK), jnp.float32)
            for b in range(T5_BUCKETS):
                acc = jnp.where(bucket == b, t5_ref[b * n_heads + gi * DIL_HEADS_PER_GROUP + h], acc)
            for v, ok in enumerate((jnp.logical_and(in_win, j >= half), in_win,
                                    jnp.logical_and(in_win, j < 2 * DIL_BLOCK - half))):
                dil_ref[gi, v, h] = jnp.where(ok, acc, NEG_INF)


def _bias_prep(rpb, t5):
    smem = pl.BlockSpec(memory_space=pltpu.SMEM)
    return pl.pallas_call(
        _bias_prep_kernel,
        in_specs=[smem, smem, pl.BlockSpec(memory_space=pltpu.VMEM)],
        out_shape=[jax.ShapeDtypeStruct((2 * NA_WIN_ROWS - 2, NA_HEADS, GRID_W, 2 * GRID_W), jnp.float32),
                   jax.ShapeDtypeStruct((len(DIL_GROUPS), 3, DIL_HEADS_PER_GROUP, DIL_BLOCK, 2 * DIL_BLOCK),
                                        jnp.float32)],
        compiler_params=pltpu.CompilerParams(vmem_limit_bytes=VMEM_LIMIT),
        name="bias_prep",
    )(rpb.reshape(-1), t5.reshape(-1), jnp.asarray(_dil_bucket_table()))


def _dil_kernel(q_ref, kp_ref, kc_ref, kn_ref, vp_ref, vc_ref, vn_ref, bias_ref, o_ref, lse_ref, *, dil):
    lane = lax.broadcasted_iota(jnp.int32, (DIL_BLOCK, LANES), 1)
    low_half = lane < HEAD_DIM
    hb = DIL_BLOCK // 2

    span = pl.program_id(1)
    n_spans = pl.num_programs(1)
    nbs = kc_ref.shape[2] // DIL_BLOCK

    def window(prev_ref, cur_ref, next_ref, rho, blk, cols):
        lo = blk * DIL_BLOCK
        first = prev_ref[0, rho, hb:, cols] if blk == 0 else cur_ref[0, rho, lo - hb:lo, cols]
        last = (next_ref[0, rho, :hb, cols] if blk == nbs - 1
                else cur_ref[0, rho, lo + DIL_BLOCK:lo + DIL_BLOCK + hb, cols])
        return jnp.concatenate([first, cur_ref[0, rho, lo:lo + DIL_BLOCK, cols], last], axis=0)

    def unit(rho, blk):
        variant = 1
        if blk == nbs - 1:
            variant = jnp.where(span == n_spans - 1, 2, variant)
        if blk == 0:
            variant = jnp.where(span == 0, 0, variant)
        scores = []
        for hp in range(DIL_HEADS_PER_GROUP // 2):
            cols = slice(hp * LANES, (hp + 1) * LANES)
            qp = q_ref[0, rho, blk * DIL_BLOCK:(blk + 1) * DIL_BLOCK, cols]
            zero = jnp.zeros_like(qp)
            q2 = jnp.concatenate([jnp.where(low_half, qp, zero), jnp.where(low_half, zero, qp)], axis=0)
            kw = window(kp_ref, kc_ref, kn_ref, rho, blk, cols)
            bias = jnp.concatenate([bias_ref[variant, 2 * hp], bias_ref[variant, 2 * hp + 1]], axis=0)
            scores.append(_dot_nt(q2, kw) + bias)
        probs = []
        for s in scores:
            m = jnp.max(s, axis=-1, keepdims=True)
            e = jnp.exp(s - m)
            probs.append((e.astype(jnp.bfloat16), m, jnp.sum(e, axis=-1, keepdims=True)))
        for hp, (p, m, l) in enumerate(probs):
            cols = slice(hp * LANES, (hp + 1) * LANES)
            o2 = _dot(p, window(vp_ref, vc_ref, vn_ref, rho, blk, cols)) / l
            lse2 = jnp.broadcast_to(m + jnp.log(l), (2 * DIL_BLOCK, LANES))
            o_pair = jnp.where(low_half, o2[:DIL_BLOCK], o2[DIL_BLOCK:])
            lse_pair = jnp.where(low_half, lse2[:DIL_BLOCK], lse2[DIL_BLOCK:])
            if dil == 1:
                rows = slice(blk * DIL_BLOCK, (blk + 1) * DIL_BLOCK)
                o_ref[hp, rows, :] = o_pair
                lse_ref[hp, rows, :] = lse_pair
            else:
                rows = pl.ds(blk * DIL_BLOCK * dil + rho, DIL_BLOCK, stride=dil)
                o_ref[hp, rows, :] = o_pair
                lse_ref[hp, rows, :] = lse_pair

    def class_body(rho, carry):
        for blk in range(nbs):
            unit(rho, blk)
        return carry

    if dil == 1:
        class_body(0, 0)
    else:
        lax.fori_loop(0, dil, class_body, 0, unroll=4 if nbs == 1 else 1)


def _dil_attn(qkv, bias_tab, group, dil, batch, seq):
    L = seq // dil
    nb = L // DIL_BLOCK
    assert nb >= 2
    tq = DIL_STEP_TOKENS
    span = tq // dil
    nbs = span // DIL_BLOCK
    n_spans = L // span

    def cur(col):
        return pl.BlockSpec((1, dil, span, WIDTH_G), lambda b, n: (b, 0, n, col))

    def edge(col, shift):
        return pl.BlockSpec((1, dil, DIL_BLOCK, WIDTH_G),
                            lambda b, n: (b, 0, jnp.clip(n * nbs + (shift if shift < 0 else nbs), 0, nb - 1), col))

    n_pairs = WIDTH_G // LANES
    out_spec = pl.BlockSpec((n_pairs, tq, LANES), lambda b, n: (0, b * n_spans + n, 0))
    return pl.pallas_call(
        functools.partial(_dil_kernel, dil=dil),
        grid=(batch, n_spans),
        in_specs=[cur(0), edge(1, -1), cur(1), edge(1, 1), edge(2, -1), cur(2), edge(2, 1),
                  pl.BlockSpec((None, 3, DIL_HEADS_PER_GROUP, DIL_BLOCK, 2 * DIL_BLOCK),
                               lambda b, n: (group, 0, 0, 0, 0))],
        out_specs=[out_spec, out_spec],
        out_shape=[jax.ShapeDtypeStruct((n_pairs, batch * seq, LANES), jnp.float32)] * 2,
        compiler_params=_cparams(2),
        name=f"dil_attn_d{dil}",
    )(qkv, qkv, qkv, qkv, qkv, qkv, qkv, bias_tab)


def _mix_kernel(x_ref, na_ref, o0_ref, l0_ref, o1_ref, l1_ref, o2_ref, l2_ref, qc_ref, gate_ref,
                mkv_ref, wa_ref, wb_ref, wc_ref, wo_ref, fg_ref, rw_ref, rb_ref,
                x1_ref, h_ref, idx_ref, wt_ref, seg_ref):
    T = x_ref.shape[0]
    ob = []
    for hp in range(WIDTH_G // LANES):
        l0, l1, l2 = l0_ref[hp], l1_ref[hp], l2_ref[hp]
        m = jnp.maximum(jnp.maximum(l0, l1), l2)
        e0, e1, e2 = jnp.exp(l0 - m), jnp.exp(l1 - m), jnp.exp(l2 - m)
        ob.append((e0 * o0_ref[hp] + e1 * o1_ref[hp] + e2 * o2_ref[hp]) / (e0 + e1 + e2))
    ob = jnp.concatenate(ob, axis=-1)

    mem_scale = MEM_HEAD_DIM ** -0.5
    scores = []
    for h in range(MEM_HEADS):
        cols = slice(h * MEM_HEAD_DIM, (h + 1) * MEM_HEAD_DIM)
        scores.append(_dot_nt(qc_ref[:, cols], mkv_ref[:, cols]) * mem_scale)
    probs = []
    for s in scores:
        e = jnp.exp(s - jnp.max(s, axis=-1, keepdims=True))
        probs.append((e.astype(jnp.bfloat16), jnp.sum(e, axis=-1, keepdims=True)))
    oc = []
    for h, (p, l) in enumerate(probs):
        oc.append(_dot(p, mkv_ref[:, WIDTH_C + h * MEM_HEAD_DIM:WIDTH_C + (h + 1) * MEM_HEAD_DIM]) / l)
    oc = jnp.concatenate(oc, axis=-1).astype(jnp.bfloat16)

    y_a = _dot(na_ref[...], wa_ref[...])
    y_b = _dot(ob.astype(jnp.bfloat16), wb_ref[...])
    y_c = _dot(oc, wc_ref[...])
    merged = (gate_ref[:, 0:D_MODEL].astype(jnp.float32) * y_a
              + gate_ref[:, D_MODEL:2 * D_MODEL].astype(jnp.float32) * y_b
              + gate_ref[:, 2 * D_MODEL:3 * D_MODEL].astype(jnp.float32) * y_c)
    x1 = x_ref[...] + _dot(merged.astype(jnp.bfloat16), wo_ref[...])
    x1_ref[...] = x1

    h = _rms(x1, fg_ref[...])
    h_ref[...] = h.astype(jnp.bfloat16)
    logits = lax.dot_general(rw_ref[...], h, (((1,), (1,)), ((), ())), preferred_element_type=jnp.float32,
                             precision=lax.Precision.HIGHEST) + rb_ref[...]
    expert = lax.broadcasted_iota(jnp.int32, (N_EXPERTS, T), 0)
    vals, idxs = [], []
    for _ in range(TOP_K):
        mx = jnp.max(logits, axis=0, keepdims=True)
        sel = jnp.min(jnp.where(logits == mx, expert, N_EXPERTS), axis=0, keepdims=True)
        vals.append(mx)
        idxs.append(sel)
        logits = jnp.where(expert == sel, -jnp.inf, logits)
    ev = [jnp.exp(v - vals[0]) for v in vals]
    den = ev[0] + ev[1] + ev[2] + ev[3]
    idx_ref[...] = jnp.concatenate(idxs, axis=0)
    wt_ref[...] = jnp.concatenate([e / den for e in ev], axis=0)
    member = sum((expert == sel).astype(jnp.float32) for sel in idxs)
    for s in range(T // MOE_TILE):
        cnt = jnp.sum(member[:, s * MOE_TILE:(s + 1) * MOE_TILE], axis=1, keepdims=True)
        seg = jnp.floor((cnt + (MOE_PIECE - 1)) * (1.0 / MOE_PIECE)) * MOE_PIECE
        seg_ref[s] = jnp.broadcast_to(seg, (N_EXPERTS, LANES))


def _mix_out(x2d, na_out, dil_outs, qc, gates, mkv, wa, wb, wc, wo, ffn_g, router_w, router_b, batch, seq):
    n = x2d.shape[0]
    T = PROJ_TILE
    tiles_per_batch = seq // T
    mem_len = mkv.shape[0] // batch
    row = lambda w: pl.BlockSpec((T, w), lambda i: (i, 0))
    full = lambda a: pl.BlockSpec(a.shape, lambda i: (0,) * a.ndim)
    in_specs = [row(D_MODEL), row(WIDTH_A)]
    args = [x2d, na_out]
    for o, l in dil_outs:
        pair_rows = pl.BlockSpec((WIDTH_G // LANES, T, LANES), lambda i: (0, i, 0))
        in_specs += [pair_rows, pair_rows]
        args += [o, l]
    in_specs += [row(WIDTH_C), row(3 * D_MODEL),
                 pl.BlockSpec((mem_len, 2 * WIDTH_C), lambda i: (i // tiles_per_batch, 0)),
                 full(wa), full(wb), full(wc), full(wo), full(ffn_g), full(router_w), full(router_b)]
    args += [qc, gates, mkv, wa, wb, wc, wo, ffn_g, router_w, router_b]
    return pl.pallas_call(
        _mix_kernel,
        grid=(n // T,),
        in_specs=in_specs,
        out_specs=[row(D_MODEL), row(D_MODEL), pl.BlockSpec((TOP_K, T), lambda i: (0, i)),
                   pl.BlockSpec((TOP_K, T), lambda i: (0, i)),
                   pl.BlockSpec((T // MOE_TILE, N_EXPERTS, LANES), lambda i: (i, 0, 0))],
        out_shape=[jax.ShapeDtypeStruct((n, D_MODEL), jnp.float32),
                   jax.ShapeDtypeStruct((n, D_MODEL), jnp.bfloat16),
                   jax.ShapeDtypeStruct((TOP_K, n), jnp.int32),
                   jax.ShapeDtypeStruct((TOP_K, n), jnp.float32),
                   jax.ShapeDtypeStruct((n // MOE_TILE, N_EXPERTS, LANES), jnp.float32)],
        compiler_params=_cparams(1),
        name="mix_out",
    )(*args)


def _route_kernel(idx_ref, segs_ref, lpos_ref, ptab_ref, be_ref, pe_ref, carry_ref, pstart_ref):
    i = pl.program_id(0)
    T = idx_ref.shape[1]
    n_blocks = be_ref.shape[1]
    expert = lax.broadcasted_iota(jnp.int32, (N_EXPERTS, T), 0)
    idx = idx_ref[...]
    onehots = [expert == idx[k:k + 1, :] for k in range(TOP_K)]
    member = sum(o.astype(jnp.float32) for o in onehots)
    seg_b = segs_ref[i]
    seg = seg_b[:, 0:1]
    r = lax.broadcasted_iota(jnp.int32, (N_EXPERTS, N_EXPERTS), 0)
    c = lax.broadcasted_iota(jnp.int32, (N_EXPERTS, N_EXPERTS), 1)
    lower_incl = (r >= c).astype(jnp.float32)

    def cumsum_experts(v):
        return jnp.dot(lower_incl, v, preferred_element_type=jnp.float32, precision=lax.Precision.HIGHEST)

    @pl.when(i == 0)
    def _():
        counts = jnp.sum(segs_ref[...], axis=0)
        padded = jnp.floor((counts + (MOE_BLOCK - 1)) * (1.0 / MOE_BLOCK)) * MOE_BLOCK
        pad_ends = cumsum_experts(padded)
        pstart_ref[...] = pad_ends - padded
        carry_ref[...] = jnp.zeros_like(carry_ref)
        pe_ref[...] = pad_ends[:, 0:1].astype(jnp.int32)
        blk_start = (lax.broadcasted_iota(jnp.int32, (N_EXPERTS, n_blocks), 1) * MOE_BLOCK).astype(jnp.float32)
        be = jnp.sum((pad_ends[:, 0:1] <= blk_start).astype(jnp.int32), axis=0, keepdims=True)
        be_ref[...] = jnp.minimum(be, N_EXPERTS - 1)

    local_end = cumsum_experts(seg_b)[:, 0:1]
    local_start = local_end - seg
    global_start = pstart_ref[:, 0:1] + carry_ref[:, 0:1]
    rr = lax.broadcasted_iota(jnp.int32, (T, T), 0)
    cc = lax.broadcasted_iota(jnp.int32, (T, T), 1)
    earlier = (rr < cc).astype(jnp.bfloat16)
    before = _dot(member.astype(jnp.bfloat16), earlier)
    base = local_start + before
    lpos = [jnp.sum(jnp.where(o, base, 0.0), axis=0, keepdims=True) for o in onehots]
    lpos_ref[...] = jnp.concatenate(lpos, axis=0).astype(jnp.int32)

    inv = 1.0 / MOE_PIECE
    piece = lax.broadcasted_iota(jnp.int32, (N_EXPERTS, MOE_PIECE_TAB), 1).astype(jnp.float32)
    first = local_start * inv
    inside = jnp.logical_and(piece >= first, piece < local_end * inv)
    dst = jnp.sum(jnp.where(inside, global_start * inv + piece - first, 0.0), axis=0, keepdims=True)
    n_pieces = local_end[N_EXPERTS - 1:N_EXPERTS, :] * inv
    spare = (n_blocks * MOE_BLOCK // MOE_PIECE
             + (i % 2) * (MOE_TILE_ROWS // MOE_PIECE)).astype(jnp.float32)
    dst = jnp.where(piece[0:1, :] >= n_pieces, spare + piece[0:1, :], dst)
    dst = jnp.where(piece[0:1, :] == MOE_PIECE_TAB - 1, n_pieces, dst)
    ptab_ref[...] = dst.astype(jnp.int32)
    carry_ref[...] += seg


def _moe_route(topk_idx, segs, n_blocks):
    n = topk_idx.shape[1]
    T = MOE_TILE
    acc = pltpu.VMEM((N_EXPERTS, LANES), jnp.float32)
    return pl.pallas_call(
        _route_kernel,
        grid=(n // T,),
        in_specs=[pl.BlockSpec((TOP_K, T), lambda i: (0, i)),
                  pl.BlockSpec(segs.shape, lambda i: (0, 0, 0))],
        out_specs=[pl.BlockSpec((TOP_K, T), lambda i: (0, i)),
                   pl.BlockSpec((None, 1, MOE_PIECE_TAB), lambda i: (i, 0, 0)),
                   pl.BlockSpec((1, n_blocks), lambda i: (0, 0)),
                   pl.BlockSpec((N_EXPERTS, 1), lambda i: (0, 0))],
        out_shape=[jax.ShapeDtypeStruct((TOP_K, n), jnp.int32),
                   jax.ShapeDtypeStruct((n // T, 1, MOE_PIECE_TAB), jnp.int32),
                   jax.ShapeDtypeStruct((1, n_blocks), jnp.int32),
                   jax.ShapeDtypeStruct((N_EXPERTS, 1), jnp.int32)],
        scratch_shapes=[acc, acc],
        compiler_params=_cparams(1),
        name="moe_route",
    )(topk_idx, segs)


def _piece(ref, p):
    return ref.at[pl.ds(pl.multiple_of(p * MOE_PIECE, MOE_PIECE), MOE_PIECE), :]


def _dispatch_kernel(pe_ref, ptab_ref, lpos_ref, h_ref, xs_hbm, zbuf, pbuf, sem, zsem):
    i = pl.program_id(0)
    T = h_ref.shape[0]
    half = h_ref.shape[1] // 2

    def zero_copy(e):
        end = pe_ref[e]
        start = pl.multiple_of(end - MOE_BLOCK, MOE_BLOCK)
        return pltpu.make_async_copy(zbuf, xs_hbm.at[pl.ds(start, MOE_BLOCK), :], zsem)

    def nonempty(e):
        return pe_ref[e] > (pe_ref[e - 1] if e else 0)

    @pl.when(i == 0)
    def _():
        zbuf[...] = jnp.zeros_like(zbuf)
        for e in range(N_EXPERTS):
            @pl.when(nonempty(e))
            def _():
                zero_copy(e).start()
        for e in range(N_EXPERTS):
            @pl.when(nonempty(e))
            def _():
                zero_copy(e).wait()

        def tail_copy(b):
            return pltpu.make_async_copy(
                zbuf, xs_hbm.at[pl.ds(pl.multiple_of(b * MOE_BLOCK, MOE_BLOCK), MOE_BLOCK), :], zsem)
        first_tail = pe_ref[N_EXPERTS - 1] // MOE_BLOCK
        n_blocks = xs_hbm.shape[0] // MOE_BLOCK
        lax.fori_loop(first_tail, n_blocks, lambda b, c: (tail_copy(b).start(), c)[1], 0)
        lax.fori_loop(first_tail, n_blocks, lambda b, c: (tail_copy(b).wait(), c)[1], 0)

    row = lax.broadcasted_iota(jnp.int32, (MOE_TILE_ROWS, T), 0)
    lpos = lpos_ref[...]
    perm = sum((row == lpos[k:k + 1, :]).astype(jnp.bfloat16) for k in range(TOP_K))
    lo = lax.bitcast_convert_type(_dot(perm, h_ref[:, :half]), jnp.uint32)
    hi = lax.bitcast_convert_type(_dot(perm, h_ref[:, half:]), jnp.uint32)
    slot = i % 2
    pbuf[slot] = hi | (lo >> 16)

    n_tile_pieces = MOE_TILE_ROWS // MOE_PIECE

    def piece_copy(s, p):
        return pltpu.make_async_copy(_piece(pbuf.at[s], p), _piece(xs_hbm, ptab_ref[p]), sem.at[s])

    def wait_all(s):
        lax.fori_loop(0, n_tile_pieces, lambda p, c: (piece_copy(s, 0).wait(), c)[1], 0, unroll=8)

    lax.fori_loop(0, n_tile_pieces, lambda p, c: (piece_copy(slot, p).start(), c)[1], 0, unroll=16)

    @pl.when(i > 0)
    def _():
        wait_all(1 - slot)

    @pl.when(i == pl.num_programs(0) - 1)
    def _():
        wait_all(slot)


def _moe_dispatch(h, lpos, ptab, pad_ends, n_blocks):
    n, width = h.shape
    T = MOE_TILE
    n_steps = n // T

    def kernel(pe_ref, ptab_ref, *rest):
        _dispatch_kernel(pe_ref, ptab_ref.at[0], *rest)

    grid_spec = pltpu.PrefetchScalarGridSpec(
        num_scalar_prefetch=1,
        grid=(n_steps,),
        in_specs=[pl.BlockSpec((None, 1, MOE_PIECE_TAB), lambda i, pe: (i, 0, 0), memory_space=pltpu.SMEM),
                  pl.BlockSpec((TOP_K, T), lambda i, pe: (0, i)),
                  pl.BlockSpec((T, width), lambda i, pe: (i, 0))],
        out_specs=pl.BlockSpec(memory_space=pl.ANY),
        scratch_shapes=[pltpu.VMEM((MOE_BLOCK, width // 2), jnp.uint32),
                        pltpu.VMEM((2, MOE_TILE_ROWS, width // 2), jnp.uint32),
                        pltpu.SemaphoreType.DMA((2,)),
                        pltpu.SemaphoreType.DMA(())],
    )
    return pl.pallas_call(
        kernel,
        grid_spec=grid_spec,
        out_shape=jax.ShapeDtypeStruct(((n_blocks + MOE_SPARE_BLOCKS) * MOE_BLOCK, width // 2), jnp.uint32),
        compiler_params=_cparams(1),
        name="moe_dispatch",
    )(pad_ends, ptab, lpos, h)


def _moe_kernel(be_ref, pe_ref, xs_ref, wgu_ref, bgu_ref, wdn_ref, bdn_ref, y_ref, wgu_bf, wdn_bf):
    i = pl.program_id(0)
    n_valid = pe_ref[N_EXPERTS - 1] // MOE_BLOCK
    new_expert = jnp.logical_or(i == 0, be_ref[i] != be_ref[jnp.maximum(i - 1, 0)])

    @pl.when(jnp.logical_and(i < n_valid, new_expert))
    def _():
        wgu_bf[...] = wgu_ref[0].astype(jnp.bfloat16)
        wdn_bf[...] = wdn_ref[0].astype(jnp.bfloat16)

    @pl.when(i < n_valid)
    def _():
        x = _unpack_bf16_pairs(xs_ref[...])
        gu = _dot(x, wgu_bf[...]) + bgu_ref[0]
        g = jnp.minimum(gu[:, :D_FF], SWIGLU_LIMIT)
        u = jnp.clip(gu[:, D_FF:], -SWIGLU_LIMIT, SWIGLU_LIMIT)
        a = (u + 1.0) * (g * _sigmoid(SWIGLU_ALPHA * g))
        y_ref[...] = _pack_bf16_pairs(_dot(a.astype(jnp.bfloat16), wdn_bf[...]) + bdn_ref[0])

    @pl.when(i >= n_valid)
    def _():
        y_ref[...] = jnp.zeros_like(y_ref)


def _moe_experts(xs, block_expert, pad_ends, wgu, bgu, wdn, bdn):
    n_blocks = block_expert.shape[0]

    def row_blk(i, be, pe):
        return (jnp.minimum(i, jnp.maximum(pe[N_EXPERTS - 1] // MOE_BLOCK - 1, 0)), 0)

    grid_spec = pltpu.PrefetchScalarGridSpec(
        num_scalar_prefetch=2,
        grid=(n_blocks,),
        in_specs=[pl.BlockSpec((MOE_BLOCK, D_MODEL // 2), row_blk),
                  pl.BlockSpec((1, D_MODEL, 2 * D_FF), lambda i, be, pe: (be[i], 0, 0)),
                  pl.BlockSpec((1, 1, 2 * D_FF), lambda i, be, pe: (be[i], 0, 0)),
                  pl.BlockSpec((1, D_FF, D_MODEL), lambda i, be, pe: (be[i], 0, 0)),
                  pl.BlockSpec((1, 1, D_MODEL), lambda i, be, pe: (be[i], 0, 0))],
        out_specs=pl.BlockSpec((MOE_BLOCK, D_MODEL // 2), lambda i, be, pe: (i, 0)),
        scratch_shapes=[pltpu.VMEM((D_MODEL, 2 * D_FF), jnp.bfloat16),
                        pltpu.VMEM((D_FF, D_MODEL), jnp.bfloat16)],
    )
    return pl.pallas_call(
        _moe_kernel,
        grid_spec=grid_spec,
        out_shape=jax.ShapeDtypeStruct((n_blocks * MOE_BLOCK, D_MODEL // 2), jnp.uint32),
        compiler_params=_cparams(1),
        name="moe_experts",
    )(block_expert, pad_ends, xs, wgu, bgu, wdn, bdn)


def _gather_pieces(src_hbm, ptab_ref, dst_ref, sem):
    spare = src_hbm.shape[0] // MOE_PIECE

    def body(p, carry):
        src = ptab_ref[p]
        src = jnp.where(src >= spare, 0, src)
        pltpu.make_async_copy(_piece(src_hbm, src), _piece(dst_ref, p), sem).start()
        return carry
    lax.fori_loop(0, MOE_TILE_ROWS // MOE_PIECE, body, 0, unroll=16)


def _wait_pieces(src_hbm, dst_ref, sem):
    def body(p, carry):
        pltpu.make_async_copy(_piece(src_hbm, 0), _piece(dst_ref, 0), sem).wait()
        return carry
    lax.fori_loop(0, MOE_TILE_ROWS // MOE_PIECE, body, 0, unroll=8)


def _combine_kernel(ptab_ref, ptab_next_ref, ys_hbm, lpos_ref, wt_ref, x1_ref, g_ref, out_ref, gbuf, sem):
    i = pl.program_id(0)
    n_steps = pl.num_programs(0)
    slot = i % 2
    T = x1_ref.shape[0]

    @pl.when(i == 0)
    def _():
        _gather_pieces(ys_hbm, ptab_ref, gbuf.at[0], sem.at[0])

    @pl.when(i + 1 < n_steps)
    def _():
        _gather_pieces(ys_hbm, ptab_next_ref, gbuf.at[1 - slot], sem.at[1 - slot])

    _wait_pieces(ys_hbm, gbuf.at[slot], sem.at[slot])
    rows = _unpack_bf16_pairs(gbuf[slot])
    col = lax.broadcasted_iota(jnp.int32, (T, MOE_TILE_ROWS), 1)
    lpos = lpos_ref[...]
    wt = wt_ref[...]
    cmb = jnp.zeros((T, MOE_TILE_ROWS), jnp.float32)
    for k in range(TOP_K):
        cmb = jnp.where(col == lpos[:, k:k + 1], wt[:, k:k + 1], cmb)
    hi = cmb.astype(jnp.bfloat16)
    lo = (cmb - hi.astype(jnp.float32)).astype(jnp.bfloat16)
    y2 = _dot(jnp.concatenate([hi, lo], axis=0), rows)
    y = x1_ref[...] + y2[:T] + y2[T:]
    out_ref[...] = _rms(y, g_ref[...])


def _moe_combine(ys, ptab, lpos_t, wts, x1, final_g):
    n = x1.shape[0]
    T = MOE_TILE
    n_steps = n // T
    smem_blk = lambda shift: pl.BlockSpec(
        (None, 1, MOE_PIECE_TAB), lambda i: (jnp.minimum(i + shift, n_steps - 1), 0, 0),
        memory_space=pltpu.SMEM)

    def kernel(ptab_ref, ptab_next_ref, *rest):
        _combine_kernel(ptab_ref.at[0], ptab_next_ref.at[0], *rest)

    return pl.pallas_call(
        kernel,
        grid=(n_steps,),
        in_specs=[smem_blk(0), smem_blk(1),
                  pl.BlockSpec(memory_space=pl.ANY),
                  pl.BlockSpec((T, TOP_K), lambda i: (i, 0)),
                  pl.BlockSpec((T, TOP_K), lambda i: (i, 0)),
                  pl.BlockSpec((T, D_MODEL), lambda i: (i, 0)),
                  pl.BlockSpec((1, D_MODEL), lambda i: (0, 0))],
        out_specs=pl.BlockSpec((T, D_MODEL), lambda i: (i, 0)),
        out_shape=jax.ShapeDtypeStruct((n, D_MODEL), jnp.float32),
        scratch_shapes=[pltpu.VMEM((2, MOE_TILE_ROWS, D_MODEL // 2), jnp.uint32),
                        pltpu.SemaphoreType.DMA((2,))],
        compiler_params=_cparams(1),
        name="moe_combine",
    )(ptab, ptab, ys, lpos_t, wts, x1, final_g)


def kernel(x, mem, attn_norm_g, mem_norm_g, w_in, w_mem_kv, na_rpb, t5_rel_bias, w_branch_a, w_branch_b,
           w_branch_c, w_out, ffn_norm_g, router_w, router_b, expert_w_gate_up, expert_b_gate_up,
           expert_w_down, expert_b_down, final_norm_g):
    B, S, D = x.shape
    depth = w_in.shape[0]
    bf = jnp.bfloat16
    x2d = x.reshape(B * S, D)
    mem2d = mem.reshape(B * mem.shape[1], D)
    assert depth == 1, "single-layer block (per-layer parameters carry a leading axis of 1)"
    l = 0
    na_bias, dil_bias = _bias_prep(na_rpb[l], t5_rel_bias)
    mkv = _mem_kv(mem2d, mem_norm_g[l][None], w_mem_kv[l].astype(bf))
    w_in_bf = w_in[l].astype(bf)
    na_qkv, d0, d1, d2, qc = _in_proj(x2d, attn_norm_g[l][None], w_in_bf, B, S)
    gates = _gate_proj(x2d, attn_norm_g[l][None], w_in_bf)
    na_out = _na_attn(na_qkv, na_bias, B, S)
    dil_outs = [_dil_attn(qkv, dil_bias, gi, dil, B, S)
                for gi, (qkv, (_, dil)) in enumerate(zip((d0, d1, d2), DIL_GROUPS))]
    x1, h, topk_idx, topk_w, segs = _mix_out(
        x2d, na_out, dil_outs, qc, gates, mkv, w_branch_a[l].astype(bf), w_branch_b[l].astype(bf),
        w_branch_c[l].astype(bf), w_out[l].astype(bf), ffn_norm_g[l][None], router_w[l].T,
        router_b[l][:, None], B, S)

    n = B * S
    max_rows = n * TOP_K + (n // MOE_TILE) * N_EXPERTS * (MOE_PIECE - 1)
    n_blocks = -(-max_rows // MOE_BLOCK) + N_EXPERTS
    lpos, ptab, block_expert, pad_ends = _moe_route(topk_idx, segs, n_blocks)
    block_expert = block_expert.reshape(n_blocks)
    pad_ends = pad_ends.reshape(N_EXPERTS)
    xs = _moe_dispatch(h, lpos, ptab, pad_ends, n_blocks)
    ys = _moe_experts(xs, block_expert, pad_ends, expert_w_gate_up[l], expert_b_gate_up[l][:, None, :],
                      expert_w_down[l], expert_b_down[l][:, None, :])
    out = _moe_combine(ys, ptab, lpos.T, topk_w.T, x1, final_norm_g[None])
    return out.reshape(B, S, D)
```

```python
import functools
import math

import numpy as np
import jax
import jax.numpy as jnp
from jax import lax
from jax.experimental import pallas as pl
from jax.experimental.pallas import tpu as pltpu

D_MODEL = 1024
GRID_W = 64
MEM_HEADS = 4
MEM_HEAD_DIM = 128
HEAD_DIM = 64
NA_HEADS = 8
NA_WIN_ROWS = 8
NA_WIN_COLS = 16
DIL_GROUPS = ((128, 1), (512, 4), (2048, 16))
DIL_HEADS_PER_GROUP = 4
DIL_BLOCK = 128
T5_BUCKETS = 32
T5_MAX_DIST = 1024
N_EXPERTS = 32
TOP_K = 4
D_FF = D_MODEL
SWIGLU_ALPHA = 1.702
SWIGLU_LIMIT = 7.0
RMS_EPS = 1e-6
NEG_INF = -1e30

WIDTH_A = NA_HEADS * HEAD_DIM
WIDTH_B = DIL_HEADS_PER_GROUP * len(DIL_GROUPS) * HEAD_DIM
WIDTH_G = DIL_HEADS_PER_GROUP * HEAD_DIM
WIDTH_C = MEM_HEADS * MEM_HEAD_DIM
OFF_QA, OFF_KA, OFF_VA = 0, WIDTH_A, 2 * WIDTH_A
OFF_QB = 3 * WIDTH_A
OFF_KB = OFF_QB + WIDTH_B
OFF_VB = OFF_KB + WIDTH_B
OFF_QC = OFF_VB + WIDTH_B
OFF_GATE = OFF_QC + WIDTH_C
IN_COLS = OFF_GATE + 3 * D_MODEL

LANES = 128
MXU_WIDTH = 256
VMEM_LIMIT = 56 * 1024 * 1024

IN_PROJ_TILE = 1024
PROJ_TILE = 512
NA_ROWS_PER_STEP = 16
DIL_STEP_TOKENS = 2048
MOE_BLOCK = 512
MOE_TILE = 256
MOE_PIECE = 8
MOE_TILE_ROWS = -(-(TOP_K * MOE_TILE + N_EXPERTS * (MOE_PIECE - 1)) // MXU_WIDTH) * MXU_WIDTH
MOE_PIECE_TAB = 256
assert MOE_TILE_ROWS // MOE_PIECE < MOE_PIECE_TAB
MOE_SPARE_BLOCKS = -(-2 * MOE_TILE_ROWS // MOE_BLOCK)


def _cparams(n_axes):
    return pltpu.CompilerParams(
        dimension_semantics=("arbitrary",) * n_axes, vmem_limit_bytes=VMEM_LIMIT)


def _rms(x, g):
    return x * lax.rsqrt(jnp.mean(x * x, axis=-1, keepdims=True) + RMS_EPS) * g


def _sigmoid(x):
    return 1.0 / (1.0 + jnp.exp(-x))


def _pack_bf16_pairs(x):
    w = x.shape[1] // 2
    bits = lax.bitcast_convert_type(x.astype(jnp.bfloat16).astype(jnp.float32), jnp.uint32)
    return (bits[:, w:] & jnp.uint32(0xFFFF0000)) | (bits[:, :w] >> 16)


def _unpack_bf16_pairs(p):
    lo = lax.bitcast_convert_type(p << 16, jnp.float32)
    hi = lax.bitcast_convert_type(p & jnp.uint32(0xFFFF0000), jnp.float32)
    return jnp.concatenate([lo, hi], axis=1).astype(jnp.bfloat16)


def _dot(a, b):
    return jnp.dot(a, b, preferred_element_type=jnp.float32)


def _dot_nt(a, b):
    return lax.dot_general(a, b, (((1,), (1,)), ((), ())), preferred_element_type=jnp.float32)


def _mem_kv_kernel(mem_ref, g_ref, w_ref, out_ref):
    h = _rms(mem_ref[...], g_ref[...]).astype(jnp.bfloat16)
    out_ref[...] = _dot(h, w_ref[...]).astype(jnp.bfloat16)


def _mem_kv(mem2d, g, w):
    rows = mem2d.shape[0]
    return pl.pallas_call(
        _mem_kv_kernel,
        out_shape=jax.ShapeDtypeStruct((rows, 2 * WIDTH_C), jnp.bfloat16),
        name="mem_kv",
        compiler_params=pltpu.CompilerParams(vmem_limit_bytes=VMEM_LIMIT),
    )(mem2d, g, w)


def _in_proj_kernel(x_ref, g_ref, w_ref, na_ref, d0_ref, d1_ref, d2_ref, qc_ref, acc_ref):
    T = x_ref.shape[0]
    h = _rms(x_ref[...], g_ref[...]).astype(jnp.bfloat16)
    q_scale = HEAD_DIM ** -0.5

    for c in range(3):
        acc = _dot(h, w_ref[:, c * WIDTH_A:(c + 1) * WIDTH_A])
        if c == 0:
            acc = acc * q_scale
        na_ref[:, c * WIDTH_A:(c + 1) * WIDTH_A] = acc.astype(jnp.bfloat16)

    for gi, ((_, dil), d_ref) in enumerate(zip(DIL_GROUPS, (d0_ref, d1_ref, d2_ref))):
        for kind, off in enumerate((OFF_QB, OFF_KB, OFF_VB)):
            lo = off + gi * WIDTH_G
            acc = _dot(h, w_ref[:, lo:lo + WIDTH_G])
            if kind == 0:
                acc = acc * q_scale
            cols = slice(kind * WIDTH_G, (kind + 1) * WIDTH_G)
            if dil == 1:
                d_ref[0, 0, :, cols] = acc.astype(jnp.bfloat16)
            else:
                for c in range(WIDTH_G // LANES):
                    acc_ref[c] = acc[:, c * LANES:(c + 1) * LANES]
                for rho in range(dil):
                    d_ref[0, rho, :, cols] = jnp.concatenate(
                        [acc_ref[c, pl.ds(rho, T // dil, stride=dil), :] for c in range(WIDTH_G // LANES)],
                        axis=-1).astype(jnp.bfloat16)

    qc_ref[...] = _dot(h, w_ref[:, OFF_QC:OFF_QC + WIDTH_C]).astype(jnp.bfloat16)


def _gate_proj_kernel(x_ref, g_ref, w_ref, gate_ref):
    h = _rms(x_ref[...], g_ref[...]).astype(jnp.bfloat16)
    chunk = 512
    first = OFF_GATE - IN_COLS // 2
    for c in range(3 * D_MODEL // chunk):
        lo = first + c * chunk
        gate_ref[:, c * chunk:(c + 1) * chunk] = _sigmoid(_dot(h, w_ref[:, lo:lo + chunk])).astype(jnp.bfloat16)


def _gate_proj(x2d, g, w_bf16):
    n = x2d.shape[0]
    T = IN_PROJ_TILE
    assert (IN_COLS // 2) % LANES == 0 and IN_COLS // 2 <= OFF_GATE
    return pl.pallas_call(
        _gate_proj_kernel,
        grid=(n // T,),
        in_specs=[pl.BlockSpec((T, D_MODEL), lambda i: (i, 0)),
                  pl.BlockSpec((1, D_MODEL), lambda i: (0, 0)),
                  pl.BlockSpec((D_MODEL, IN_COLS // 2), lambda i: (0, 1), pipeline_mode=pl.Buffered(1))],
        out_specs=pl.BlockSpec((T, 3 * D_MODEL), lambda i: (i, 0)),
        out_shape=jax.ShapeDtypeStruct((n, 3 * D_MODEL), jnp.bfloat16),
        compiler_params=_cparams(1),
        name="gate_proj",
    )(x2d, g, w_bf16)


def _in_proj(x2d, g, w_bf16, batch, seq):
    n = x2d.shape[0]
    T = IN_PROJ_TILE
    tiles_per_batch = seq // T
    out_shape = [jax.ShapeDtypeStruct((n, 3 * WIDTH_A), jnp.bfloat16)]
    out_specs = [pl.BlockSpec((T, 3 * WIDTH_A), lambda i: (i, 0))]
    for _, dil in DIL_GROUPS:
        out_shape.append(jax.ShapeDtypeStruct((batch, dil, seq // dil, 3 * WIDTH_G), jnp.bfloat16))
        out_specs.append(pl.BlockSpec((1, dil, T // dil, 3 * WIDTH_G),
                                      lambda i: (i // tiles_per_batch, 0, i % tiles_per_batch, 0)))
    out_shape += [jax.ShapeDtypeStruct((n, WIDTH_C), jnp.bfloat16)]
    out_specs += [pl.BlockSpec((T, WIDTH_C), lambda i: (i, 0))]
    return pl.pallas_call(
        _in_proj_kernel,
        grid=(n // T,),
        in_specs=[pl.BlockSpec((T, D_MODEL), lambda i: (i, 0)),
                  pl.BlockSpec((1, D_MODEL), lambda i: (0, 0)),
                  pl.BlockSpec((D_MODEL, OFF_GATE), lambda i: (0, 0), pipeline_mode=pl.Buffered(1))],
        out_specs=out_specs,
        out_shape=out_shape,
        scratch_shapes=[pltpu.VMEM((WIDTH_G // LANES, T, LANES), jnp.float32)],
        compiler_params=_cparams(1),
        name="in_proj",
    )(x2d, g, w_bf16)


def _na_kernel(q_ref, k_ref, v_ref, bias_ref, out_ref, *, rows):
    i = pl.program_id(1)
    lane = lax.broadcasted_iota(jnp.int32, (GRID_W, LANES), 1)
    low_half = lane < HEAD_DIM
    n_win = NA_WIN_ROWS * GRID_W

    def row_body(j, carry):
        qr = i * NA_ROWS_PER_STEP + j
        rs = jnp.clip(qr - NA_WIN_ROWS // 2, 0, rows - NA_WIN_ROWS)
        shift = rs - qr + NA_WIN_ROWS - 1
        q_off = pl.multiple_of(j * GRID_W, GRID_W)
        k_off = pl.multiple_of(rs * GRID_W, GRID_W)
        scores = []
        for hp in range(NA_HEADS // 2):
            cols = slice(hp * LANES, (hp + 1) * LANES)
            qp = q_ref[pl.ds(q_off, GRID_W), cols]
            kp = k_ref[pl.ds(k_off, n_win), cols]
            zero = jnp.zeros_like(qp)
            q2 = jnp.concatenate([jnp.where(low_half, qp, zero), jnp.where(low_half, zero, qp)], axis=0)
            bias = jnp.concatenate(
                [jnp.concatenate([bias_ref[shift + 2 * p, 2 * hp + hh] for p in range(NA_WIN_ROWS // 2)], axis=-1)
                 for hh in range(2)], axis=0)
            scores.append(_dot_nt(q2, kp) + bias)
        probs = []
        for s in scores:
            e = jnp.exp(s - jnp.max(s, axis=-1, keepdims=True))
            probs.append((e.astype(jnp.bfloat16), jnp.sum(e, axis=-1, keepdims=True)))
        for hp, (p, l) in enumerate(probs):
            cols = slice(hp * LANES, (hp + 1) * LANES)
            o2 = _dot(p, v_ref[pl.ds(k_off, n_win), cols]) / l
            out_ref[pl.ds(q_off, GRID_W), cols] = jnp.where(
                low_half, o2[:GRID_W], o2[GRID_W:]).astype(jnp.bfloat16)
        return carry

    lax.fori_loop(0, NA_ROWS_PER_STEP, row_body, 0, unroll=4)


def _na_attn(na_qkv, bias_tab, batch, seq):
    rows = seq // GRID_W
    steps = rows // NA_ROWS_PER_STEP
    tq = NA_ROWS_PER_STEP * GRID_W
    n = na_qkv.shape[0]
    return pl.pallas_call(
        functools.partial(_na_kernel, rows=rows),
        grid=(batch, steps),
        in_specs=[pl.BlockSpec((tq, WIDTH_A), lambda b, i: (b * steps + i, 0)),
                  pl.BlockSpec((seq, WIDTH_A), lambda b, i: (b, 1)),
                  pl.BlockSpec((seq, WIDTH_A), lambda b, i: (b, 2)),
                  pl.BlockSpec(bias_tab.shape, lambda b, i: (0, 0, 0, 0))],
        out_specs=pl.BlockSpec((tq, WIDTH_A), lambda b, i: (b * steps + i, 0)),
        out_shape=jax.ShapeDtypeStruct((n, WIDTH_A), jnp.bfloat16),
        compiler_params=_cparams(2),
        name="na_attn",
    )(na_qkv, na_qkv, na_qkv, bias_tab)


def _t5_bucket_np(rel):
    half = T5_BUCKETS // 2
    max_exact = half // 2
    ret = np.where(rel > 0, half, 0)
    n = np.abs(rel)
    nf = np.maximum(n, 1).astype(np.float32)
    large = max_exact + (np.log(nf / np.float32(max_exact)) / np.float32(math.log(T5_MAX_DIST / max_exact))
                         * np.float32(half - max_exact)).astype(np.int32)
    large = np.minimum(large, half - 1)
    return ret + np.where(n < max_exact, n, large)


def _dil_bucket_table():
    q = np.arange(DIL_BLOCK)[:, None]
    j = np.arange(2 * DIL_BLOCK)[None, :]
    tabs = []
    for window, dil in DIL_GROUPS:
        half = (window // 2) // dil
        assert half == DIL_BLOCK // 2
        rel = (j - half) - q
        tabs.append(np.where(np.abs(rel) <= half, _t5_bucket_np(rel * dil), -1))
    return np.stack(tabs).astype(np.int32)


def _bias_prep_kernel(rpb_ref, t5_ref, bucket_ref, na_ref, dil_ref):
    n_dc = 2 * NA_WIN_COLS - 1
    n_dr = 2 * NA_WIN_ROWS - 1
    qc = lax.broadcasted_iota(jnp.int32, (GRID_W, 2 * GRID_W), 0)
    lane = lax.broadcasted_iota(jnp.int32, (GRID_W, 2 * GRID_W), 1)
    second = lane >= GRID_W
    kc = jnp.where(second, lane - GRID_W, lane)
    diff = kc - qc
    cs = jnp.clip(qc - NA_WIN_COLS // 2, 0, GRID_W - NA_WIN_COLS)
    valid = jnp.logical_and(kc >= cs, kc < cs + NA_WIN_COLS)

    def na_tile(mh, carry):
        m = mh // NA_HEADS
        h = mh - m * NA_HEADS
        base = (h * n_dr + m) * n_dc
        acc = jnp.zeros((GRID_W, 2 * GRID_W), jnp.float32)
        for d in range(n_dc):
            v = jnp.where(second, rpb_ref[base + n_dc + d], rpb_ref[base + d])
            acc = jnp.where(diff == d - (NA_WIN_COLS - 1), v, acc)
        na_ref[m, h] = jnp.where(valid, acc, NEG_INF)
        return carry

    lax.fori_loop(0, (n_dr - 1) * NA_HEADS, na_tile, 0)

    j = lax.broadcasted_iota(jnp.int32, (DIL_BLOCK, 2 * DIL_BLOCK), 1)
    half = DIL_BLOCK // 2
    n_heads = DIL_HEADS_PER_GROUP * len(DIL_GROUPS)
    for gi in range(len(DIL_GROUPS)):
        bucket = bucket_ref[gi]
        in_win = bucket >= 0
        for h in range(DIL_HEADS_PER_GROUP):
            acc = jnp.zeros((DIL_BLOCK, 2 * DIL_BLOCK), jnp.float32)
            for b in range(T5_BUCKETS):
                acc = jnp.where(bucket == b, t5_ref[b * n_heads + gi * DIL_HEADS_PER_GROUP + h], acc)
            for v, ok in enumerate((jnp.logical_and(in_win, j >= half), in_win,
                                    jnp.logical_and(in_win, j < 2 * DIL_BLOCK - half))):
                dil_ref[gi, v, h] = jnp.where(ok, acc, NEG_INF)


def _bias_prep(rpb, t5):
    smem = pl.BlockSpec(memory_space=pltpu.SMEM)
    return pl.pallas_call(
        _bias_prep_kernel,
        in_specs=[smem, smem, pl.BlockSpec(memory_space=pltpu.VMEM)],
        out_shape=[jax.ShapeDtypeStruct((2 * NA_WIN_ROWS - 2, NA_HEADS, GRID_W, 2 * GRID_W), jnp.float32),
                   jax.ShapeDtypeStruct((len(DIL_GROUPS), 3, DIL_HEADS_PER_GROUP, DIL_BLOCK, 2 * DIL_BLOCK),
                                        jnp.float32)],
        compiler_params=pltpu.CompilerParams(vmem_limit_bytes=VMEM_LIMIT),
        name="bias_prep",
    )(rpb.reshape(-1), t5.reshape(-1), jnp.asarray(_dil_bucket_table()))


def _dil_kernel(q_ref, kp_ref, kc_ref, kn_ref, vp_ref, vc_ref, vn_ref, bias_ref, o_ref, lse_ref, *, dil):
    lane = lax.broadcasted_iota(jnp.int32, (DIL_BLOCK, LANES), 1)
    low_half = lane < HEAD_DIM
    hb = DIL_BLOCK // 2

    span = pl.program_id(1)
    n_spans = pl.num_programs(1)
    nbs = kc_ref.shape[2] // DIL_BLOCK

    def window(prev_ref, cur_ref, next_ref, rho, blk, cols):
        lo = blk * DIL_BLOCK
        first = prev_ref[0, rho, hb:, cols] if blk == 0 else cur_ref[0, rho, lo - hb:lo, cols]
        last = (next_ref[0, rho, :hb, cols] if blk == nbs - 1
                else cur_ref[0, rho, lo + DIL_BLOCK:lo + DIL_BLOCK + hb, cols])
        return jnp.concatenate([first, cur_ref[0, rho, lo:lo + DIL_BLOCK, cols], last], axis=0)

    def unit(rho, blk):
        variant = 1
        if blk == nbs - 1:
            variant = jnp.where(span == n_spans - 1, 2, variant)
        if blk == 0:
            variant = jnp.where(span == 0, 0, variant)
        scores = []
        for hp in range(DIL_HEADS_PER_GROUP // 2):
            cols = slice(hp * LANES, (hp + 1) * LANES)
            qp = q_ref[0, rho, blk * DIL_BLOCK:(blk + 1) * DIL_BLOCK, cols]
            zero = jnp.zeros_like(qp)
            q2 = jnp.concatenate([jnp.where(low_half, qp, zero), jnp.where(low_half, zero, qp)], axis=0)
            kw = window(kp_ref, kc_ref, kn_ref, rho, blk, cols)
            bias = jnp.concatenate([bias_ref[variant, 2 * hp], bias_ref[variant, 2 * hp + 1]], axis=0)
            scores.append(_dot_nt(q2, kw) + bias)
        probs = []
        for s in scores:
            m = jnp.max(s, axis=-1, keepdims=True)
            e = jnp.exp(s - m)
            probs.append((e.astype(jnp.bfloat16), m, jnp.sum(e, axis=-1, keepdims=True)))
        for hp, (p, m, l) in enumerate(probs):
            cols = slice(hp * LANES, (hp + 1) * LANES)
            o2 = _dot(p, window(vp_ref, vc_ref, vn_ref, rho, blk, cols)) / l
            lse2 = jnp.broadcast_to(m + jnp.log(l), (2 * DIL_BLOCK, LANES))
            o_pair = jnp.where(low_half, o2[:DIL_BLOCK], o2[DIL_BLOCK:])
            lse_pair = jnp.where(low_half, lse2[:DIL_BLOCK], lse2[DIL_BLOCK:])
            if dil == 1:
                rows = slice(blk * DIL_BLOCK, (blk + 1) * DIL_BLOCK)
                o_ref[hp, rows, :] = o_pair
                lse_ref[hp, rows, :] = lse_pair
            else:
                rows = pl.ds(blk * DIL_BLOCK * dil + rho, DIL_BLOCK, stride=dil)
                o_ref[hp, rows, :] = o_pair
                lse_ref[hp, rows, :] = lse_pair

    def class_body(rho, carry):
        for blk in range(nbs):
            unit(rho, blk)
        return carry

    if dil == 1:
        class_body(0, 0)
    else:
        lax.fori_loop(0, dil, class_body, 0, unroll=4 if nbs == 1 else 1)


def _dil_attn(qkv, bias_tab, group, dil, batch, seq):
    L = seq // dil
    nb = L // DIL_BLOCK
    assert nb >= 2
    tq = DIL_STEP_TOKENS
    span = tq // dil
    nbs = span // DIL_BLOCK
    n_spans = L // span

    def cur(col):
        return pl.BlockSpec((1, dil, span, WIDTH_G), lambda b, n: (b, 0, n, col))

    def edge(col, shift):
        return pl.BlockSpec((1, dil, DIL_BLOCK, WIDTH_G),
                            lambda b, n: (b, 0, jnp.clip(n * nbs + (shift if shift < 0 else nbs), 0, nb - 1), col))

    n_pairs = WIDTH_G // LANES
    out_spec = pl.BlockSpec((n_pairs, tq, LANES), lambda b, n: (0, b * n_spans + n, 0))
    return pl.pallas_call(
        functools.partial(_dil_kernel, dil=dil),
        grid=(batch, n_spans),
        in_specs=[cur(0), edge(1, -1), cur(1), edge(1, 1), edge(2, -1), cur(2), edge(2, 1),
                  pl.BlockSpec((None, 3, DIL_HEADS_PER_GROUP, DIL_BLOCK, 2 * DIL_BLOCK),
                               lambda b, n: (group, 0, 0, 0, 0))],
        out_specs=[out_spec, out_spec],
        out_shape=[jax.ShapeDtypeStruct((n_pairs, batch * seq, LANES), jnp.float32)] * 2,
        compiler_params=_cparams(2),
        name=f"dil_attn_d{dil}",
    )(qkv, qkv, qkv, qkv, qkv, qkv, qkv, bias_tab)


def _mix_kernel(x_ref, na_ref, o0_ref, l0_ref, o1_ref, l1_ref, o2_ref, l2_ref, qc_ref, gate_ref,
                mkv_ref, wa_ref, wb_ref, wc_ref, wo_ref, fg_ref, rw_ref, rb_ref,
                x1_ref, h_ref, idx_ref, wt_ref, seg_ref):
    T = x_ref.shape[0]
    ob = []
    for hp in range(WIDTH_G // LANES):
        l0, l1, l2 = l0_ref[hp], l1_ref[hp], l2_ref[hp]
        m = jnp.maximum(jnp.maximum(l0, l1), l2)
        e0, e1, e2 = jnp.exp(l0 - m), jnp.exp(l1 - m), jnp.exp(l2 - m)
        ob.append((e0 * o0_ref[hp] + e1 * o1_ref[hp] + e2 * o2_ref[hp]) / (e0 + e1 + e2))
    ob = jnp.concatenate(ob, axis=-1)

    mem_scale = MEM_HEAD_DIM ** -0.5
    scores = []
    for h in range(MEM_HEADS):
        cols = slice(h * MEM_HEAD_DIM, (h + 1) * MEM_HEAD_DIM)
        scores.append(_dot_nt(qc_ref[:, cols], mkv_ref[:, cols]) * mem_scale)
    probs = []
    for s in scores:
        e = jnp.exp(s - jnp.max(s, axis=-1, keepdims=True))
        probs.append((e.astype(jnp.bfloat16), jnp.sum(e, axis=-1, keepdims=True)))
    oc = []
    for h, (p, l) in enumerate(probs):
        oc.append(_dot(p, mkv_ref[:, WIDTH_C + h * MEM_HEAD_DIM:WIDTH_C + (h + 1) * MEM_HEAD_DIM]) / l)
    oc = jnp.concatenate(oc, axis=-1).astype(jnp.bfloat16)

    y_a = _dot(na_ref[...], wa_ref[...])
    y_b = _dot(ob.astype(jnp.bfloat16), wb_ref[...])
    y_c = _dot(oc, wc_ref[...])
    merged = (gate_ref[:, 0:D_MODEL].astype(jnp.float32) * y_a
              + gate_ref[:, D_MODEL:2 * D_MODEL].astype(jnp.float32) * y_b
              + gate_ref[:, 2 * D_MODEL:3 * D_MODEL].astype(jnp.float32) * y_c)
    x1 = x_ref[...] + _dot(merged.astype(jnp.bfloat16), wo_ref[...])
    x1_ref[...] = x1

    h = _rms(x1, fg_ref[...])
    h_ref[...] = h.astype(jnp.bfloat16)
    logits = lax.dot_general(rw_ref[...], h, (((1,), (1,)), ((), ())), preferred_element_type=jnp.float32,
                             precision=lax.Precision.HIGHEST) + rb_ref[...]
    expert = lax.broadcasted_iota(jnp.int32, (N_EXPERTS, T), 0)
    vals, idxs = [], []
    for _ in range(TOP_K):
        mx = jnp.max(logits, axis=0, keepdims=True)
        sel = jnp.min(jnp.where(logits == mx, expert, N_EXPERTS), axis=0, keepdims=True)
        vals.append(mx)
        idxs.append(sel)
        logits = jnp.where(expert == sel, -jnp.inf, logits)
    ev = [jnp.exp(v - vals[0]) for v in vals]
    den = ev[0] + ev[1] + ev[2] + ev[3]
    idx_ref[...] = jnp.concatenate(idxs, axis=0)
    wt_ref[...] = jnp.concatenate([e / den for e in ev], axis=0)
    member = sum((expert == sel).astype(jnp.float32) for sel in idxs)
    for s in range(T // MOE_TILE):
        cnt = jnp.sum(member[:, s * MOE_TILE:(s + 1) * MOE_TILE], axis=1, keepdims=True)
        seg = jnp.floor((cnt + (MOE_PIECE - 1)) * (1.0 / MOE_PIECE)) * MOE_PIECE
        seg_ref[s] = jnp.broadcast_to(seg, (N_EXPERTS, LANES))


def _mix_out(x2d, na_out, dil_outs, qc, gates, mkv, wa, wb, wc, wo, ffn_g, router_w, router_b, batch, seq):
    n = x2d.shape[0]
    T = PROJ_TILE
    tiles_per_batch = seq // T
    mem_len = mkv.shape[0] // batch
    row = lambda w: pl.BlockSpec((T, w), lambda i: (i, 0))
    full = lambda a: pl.BlockSpec(a.shape, lambda i: (0,) * a.ndim)
    in_specs = [row(D_MODEL), row(WIDTH_A)]
    args = [x2d, na_out]
    for o, l in dil_outs:
        pair_rows = pl.BlockSpec((WIDTH_G // LANES, T, LANES), lambda i: (0, i, 0))
        in_specs += [pair_rows, pair_rows]
        args += [o, l]
    in_specs += [row(WIDTH_C), row(3 * D_MODEL),
                 pl.BlockSpec((mem_len, 2 * WIDTH_C), lambda i: (i // tiles_per_batch, 0)),
                 full(wa), full(wb), full(wc), full(wo), full(ffn_g), full(router_w), full(router_b)]
    args += [qc, gates, mkv, wa, wb, wc, wo, ffn_g, router_w, router_b]
    return pl.pallas_call(
        _mix_kernel,
        grid=(n // T,),
        in_specs=in_specs,
        out_specs=[row(D_MODEL), row(D_MODEL), pl.BlockSpec((TOP_K, T), lambda i: (0, i)),
                   pl.BlockSpec((TOP_K, T), lambda i: (0, i)),
                   pl.BlockSpec((T // MOE_TILE, N_EXPERTS, LANES), lambda i: (i, 0, 0))],
        out_shape=[jax.ShapeDtypeStruct((n, D_MODEL), jnp.float32),
                   jax.ShapeDtypeStruct((n, D_MODEL), jnp.bfloat16),
                   jax.ShapeDtypeStruct((TOP_K, n), jnp.int32),
                   jax.ShapeDtypeStruct((TOP_K, n), jnp.float32),
                   jax.ShapeDtypeStruct((n // MOE_TILE, N_EXPERTS, LANES), jnp.float32)],
        compiler_params=_cparams(1),
        name="mix_out",
    )(*args)


def _route_kernel(idx_ref, segs_ref, lpos_ref, ptab_ref, be_ref, pe_ref, carry_ref, pstart_ref):
    i = pl.program_id(0)
    T = idx_ref.shape[1]
    n_blocks = be_ref.shape[1]
    expert = lax.broadcasted_iota(jnp.int32, (N_EXPERTS, T), 0)
    idx = idx_ref[...]
    onehots = [expert == idx[k:k + 1, :] for k in range(TOP_K)]
    member = sum(o.astype(jnp.float32) for o in onehots)
    seg_b = segs_ref[i]
    seg = seg_b[:, 0:1]
    r = lax.broadcasted_iota(jnp.int32, (N_EXPERTS, N_EXPERTS), 0)
    c = lax.broadcasted_iota(jnp.int32, (N_EXPERTS, N_EXPERTS), 1)
    lower_incl = (r >= c).astype(jnp.float32)

    def cumsum_experts(v):
        return jnp.dot(lower_incl, v, preferred_element_type=jnp.float32, precision=lax.Precision.HIGHEST)

    @pl.when(i == 0)
    def _():
        counts = jnp.sum(segs_ref[...], axis=0)
        padded = jnp.floor((counts + (MOE_BLOCK - 1)) * (1.0 / MOE_BLOCK)) * MOE_BLOCK
        pad_ends = cumsum_experts(padded)
        pstart_ref[...] = pad_ends - padded
        carry_ref[...] = jnp.zeros_like(carry_ref)
        pe_ref[...] = pad_ends[:, 0:1].astype(jnp.int32)
        blk_start = (lax.broadcasted_iota(jnp.int32, (N_EXPERTS, n_blocks), 1) * MOE_BLOCK).astype(jnp.float32)
        be = jnp.sum((pad_ends[:, 0:1] <= blk_start).astype(jnp.int32), axis=0, keepdims=True)
        be_ref[...] = jnp.minimum(be, N_EXPERTS - 1)

    local_end = cumsum_experts(seg_b)[:, 0:1]
    local_start = local_end - seg
    global_start = pstart_ref[:, 0:1] + carry_ref[:, 0:1]
    rr = lax.broadcasted_iota(jnp.int32, (T, T), 0)
    cc = lax.broadcasted_iota(jnp.int32, (T, T), 1)
    earlier = (rr < cc).astype(jnp.bfloat16)
    before = _dot(member.astype(jnp.bfloat16), earlier)
    base = local_start + before
    lpos = [jnp.sum(jnp.where(o, base, 0.0), axis=0, keepdims=True) for o in onehots]
    lpos_ref[...] = jnp.concatenate(lpos, axis=0).astype(jnp.int32)

    inv = 1.0 / MOE_PIECE
    piece = lax.broadcasted_iota(jnp.int32, (N_EXPERTS, MOE_PIECE_TAB), 1).astype(jnp.float32)
    first = local_start * inv
    inside = jnp.logical_and(piece >= first, piece < local_end * inv)
    dst = jnp.sum(jnp.where(inside, global_start * inv + piece - first, 0.0), axis=0, keepdims=True)
    n_pieces = local_end[N_EXPERTS - 1:N_EXPERTS, :] * inv
    spare = (n_blocks * MOE_BLOCK // MOE_PIECE
             + (i % 2) * (MOE_TILE_ROWS // MOE_PIECE)).astype(jnp.float32)
    dst = jnp.where(piece[0:1, :] >= n_pieces, spare + piece[0:1, :], dst)
    dst = jnp.where(piece[0:1, :] == MOE_PIECE_TAB - 1, n_pieces, dst)
    ptab_ref[...] = dst.astype(jnp.int32)
    carry_ref[...] += seg


def _moe_route(topk_idx, segs, n_blocks):
    n = topk_idx.shape[1]
    T = MOE_TILE
    acc = pltpu.VMEM((N_EXPERTS, LANES), jnp.float32)
    return pl.pallas_call(
        _route_kernel,
        grid=(n // T,),
        in_specs=[pl.BlockSpec((TOP_K, T), lambda i: (0, i)),
                  pl.BlockSpec(segs.shape, lambda i: (0, 0, 0))],
        out_specs=[pl.BlockSpec((TOP_K, T), lambda i: (0, i)),
                   pl.BlockSpec((None, 1, MOE_PIECE_TAB), lambda i: (i, 0, 0)),
                   pl.BlockSpec((1, n_blocks), lambda i: (0, 0)),
                   pl.BlockSpec((N_EXPERTS, 1), lambda i: (0, 0))],
        out_shape=[jax.ShapeDtypeStruct((TOP_K, n), jnp.int32),
                   jax.ShapeDtypeStruct((n // T, 1, MOE_PIECE_TAB), jnp.int32),
                   jax.ShapeDtypeStruct((1, n_blocks), jnp.int32),
                   jax.ShapeDtypeStruct((N_EXPERTS, 1), jnp.int32)],
        scratch_shapes=[acc, acc],
        compiler_params=_cparams(1),
        name="moe_route",
    )(topk_idx, segs)


def _piece(ref, p):
    return ref.at[pl.ds(pl.multiple_of(p * MOE_PIECE, MOE_PIECE), MOE_PIECE), :]


def _dispatch_kernel(pe_ref, ptab_ref, lpos_ref, h_ref, xs_hbm, zbuf, pbuf, sem, zsem):
    i = pl.program_id(0)
    T = h_ref.shape[0]
    half = h_ref.shape[1] // 2

    def zero_copy(e):
        end = pe_ref[e]
        start = pl.multiple_of(end - MOE_BLOCK, MOE_BLOCK)
        return pltpu.make_async_copy(zbuf, xs_hbm.at[pl.ds(start, MOE_BLOCK), :], zsem)

    def nonempty(e):
        return pe_ref[e] > (pe_ref[e - 1] if e else 0)

    @pl.when(i == 0)
    def _():
        zbuf[...] = jnp.zeros_like(zbuf)
        for e in range(N_EXPERTS):
            @pl.when(nonempty(e))
            def _():
                zero_copy(e).start()
        for e in range(N_EXPERTS):
            @pl.when(nonempty(e))
            def _():
                zero_copy(e).wait()

        def tail_copy(b):
            return pltpu.make_async_copy(
                zbuf, xs_hbm.at[pl.ds(pl.multiple_of(b * MOE_BLOCK, MOE_BLOCK), MOE_BLOCK), :], zsem)
        first_tail = pe_ref[N_EXPERTS - 1] // MOE_BLOCK
        n_blocks = xs_hbm.shape[0] // MOE_BLOCK
        lax.fori_loop(first_tail, n_blocks, lambda b, c: (tail_copy(b).start(), c)[1], 0)
        lax.fori_loop(first_tail, n_blocks, lambda b, c: (tail_copy(b).wait(), c)[1], 0)

    row = lax.broadcasted_iota(jnp.int32, (MOE_TILE_ROWS, T), 0)
    lpos = lpos_ref[...]
    perm = sum((row == lpos[k:k + 1, :]).astype(jnp.bfloat16) for k in range(TOP_K))
    lo = lax.bitcast_convert_type(_dot(perm, h_ref[:, :half]), jnp.uint32)
    hi = lax.bitcast_convert_type(_dot(perm, h_ref[:, half:]), jnp.uint32)
    slot = i % 2
    pbuf[slot] = hi | (lo >> 16)

    n_tile_pieces = MOE_TILE_ROWS // MOE_PIECE

    def piece_copy(s, p):
        return pltpu.make_async_copy(_piece(pbuf.at[s], p), _piece(xs_hbm, ptab_ref[p]), sem.at[s])

    def wait_all(s):
        lax.fori_loop(0, n_tile_pieces, lambda p, c: (piece_copy(s, 0).wait(), c)[1], 0, unroll=8)

    lax.fori_loop(0, n_tile_pieces, lambda p, c: (piece_copy(slot, p).start(), c)[1], 0, unroll=16)

    @pl.when(i > 0)
    def _():
        wait_all(1 - slot)

    @pl.when(i == pl.num_programs(0) - 1)
    def _():
        wait_all(slot)


def _moe_dispatch(h, lpos, ptab, pad_ends, n_blocks):
    n, width = h.shape
    T = MOE_TILE
    n_steps = n // T

    def kernel(pe_ref, ptab_ref, *rest):
        _dispatch_kernel(pe_ref, ptab_ref.at[0], *rest)

    grid_spec = pltpu.PrefetchScalarGridSpec(
        num_scalar_prefetch=1,
        grid=(n_steps,),
        in_specs=[pl.BlockSpec((None, 1, MOE_PIECE_TAB), lambda i, pe: (i, 0, 0), memory_space=pltpu.SMEM),
                  pl.BlockSpec((TOP_K, T), lambda i, pe: (0, i)),
                  pl.BlockSpec((T, width), lambda i, pe: (i, 0))],
        out_specs=pl.BlockSpec(memory_space=pl.ANY),
        scratch_shapes=[pltpu.VMEM((MOE_BLOCK, width // 2), jnp.uint32),
                        pltpu.VMEM((2, MOE_TILE_ROWS, width // 2), jnp.uint32),
                        pltpu.SemaphoreType.DMA((2,)),
                        pltpu.SemaphoreType.DMA(())],
    )
    return pl.pallas_call(
        kernel,
        grid_spec=grid_spec,
        out_shape=jax.ShapeDtypeStruct(((n_blocks + MOE_SPARE_BLOCKS) * MOE_BLOCK, width // 2), jnp.uint32),
        compiler_params=_cparams(1),
        name="moe_dispatch",
    )(pad_ends, ptab, lpos, h)


def _moe_kernel(be_ref, pe_ref, xs_ref, wgu_ref, bgu_ref, wdn_ref, bdn_ref, y_ref):
    del be_ref
    i = pl.program_id(0)
    n_valid = pe_ref[N_EXPERTS - 1] // MOE_BLOCK

    @pl.when(i < n_valid)
    def _():
        x = _unpack_bf16_pairs(xs_ref[...])
        gu = _dot(x, wgu_ref[0]) + bgu_ref[0]
        g = jnp.minimum(gu[:, :D_FF], SWIGLU_LIMIT)
        u = jnp.clip(gu[:, D_FF:], -SWIGLU_LIMIT, SWIGLU_LIMIT)
        a = (u + 1.0) * (g * _sigmoid(SWIGLU_ALPHA * g))
        y_ref[...] = _pack_bf16_pairs(_dot(a.astype(jnp.bfloat16), wdn_ref[0]) + bdn_ref[0])

    @pl.when(i >= n_valid)
    def _():
        y_ref[...] = jnp.zeros_like(y_ref)


def _moe_experts(xs, block_expert, pad_ends, wgu, bgu, wdn, bdn):
    n_blocks = block_expert.shape[0]

    def row_blk(i, be, pe):
        return (jnp.minimum(i, jnp.maximum(pe[N_EXPERTS - 1] // MOE_BLOCK - 1, 0)), 0)

    grid_spec = pltpu.PrefetchScalarGridSpec(
        num_scalar_prefetch=2,
        grid=(n_blocks,),
        in_specs=[pl.BlockSpec((MOE_BLOCK, D_MODEL // 2), row_blk),
                  pl.BlockSpec((1, D_MODEL, 2 * D_FF), lambda i, be, pe: (be[i], 0, 0)),
                  pl.BlockSpec((1, 1, 2 * D_FF), lambda i, be, pe: (be[i], 0, 0)),
                  pl.BlockSpec((1, D_FF, D_MODEL), lambda i, be, pe: (be[i], 0, 0)),
                  pl.BlockSpec((1, 1, D_MODEL), lambda i, be, pe: (be[i], 0, 0))],
        out_specs=pl.BlockSpec((MOE_BLOCK, D_MODEL // 2), lambda i, be, pe: (i, 0)),
    )
    return pl.pallas_call(
        _moe_kernel,
        grid_spec=grid_spec,
        out_shape=jax.ShapeDtypeStruct((n_blocks * MOE_BLOCK, D_MODEL // 2), jnp.uint32),
        compiler_params=_cparams(1),
        name="moe_experts",
    )(block_expert, pad_ends, xs, wgu, bgu, wdn, bdn)


def _gather_pieces(src_hbm, ptab_ref, dst_ref, sem):
    spare = src_hbm.shape[0] // MOE_PIECE

    def body(p, carry):
        src = ptab_ref[p]
        src = jnp.where(src >= spare, 0, src)
        pltpu.make_async_copy(_piece(src_hbm, src), _piece(dst_ref, p), sem).start()
        return carry
    lax.fori_loop(0, MOE_TILE_ROWS // MOE_PIECE, body, 0, unroll=16)


def _wait_pieces(src_hbm, dst_ref, sem):
    def body(p, carry):
        pltpu.make_async_copy(_piece(src_hbm, 0), _piece(dst_ref, 0), sem).wait()
        return carry
    lax.fori_loop(0, MOE_TILE_ROWS // MOE_PIECE, body, 0, unroll=8)


def _combine_kernel(ptab_ref, ptab_next_ref, ys_hbm, lpos_ref, wt_ref, x1_ref, g_ref, out_ref, gbuf, sem):
    i = pl.program_id(0)
    n_steps = pl.num_programs(0)
    slot = i % 2
    T = x1_ref.shape[0]

    @pl.when(i == 0)
    def _():
        _gather_pieces(ys_hbm, ptab_ref, gbuf.at[0], sem.at[0])

    @pl.when(i + 1 < n_steps)
    def _():
        _gather_pieces(ys_hbm, ptab_next_ref, gbuf.at[1 - slot], sem.at[1 - slot])

    _wait_pieces(ys_hbm, gbuf.at[slot], sem.at[slot])
    rows = _unpack_bf16_pairs(gbuf[slot])
    col = lax.broadcasted_iota(jnp.int32, (T, MOE_TILE_ROWS), 1)
    lpos = lpos_ref[...]
    wt = wt_ref[...]
    cmb = jnp.zeros((T, MOE_TILE_ROWS), jnp.float32)
    for k in range(TOP_K):
        cmb = jnp.where(col == lpos[:, k:k + 1], wt[:, k:k + 1], cmb)
    hi = cmb.astype(jnp.bfloat16)
    lo = (cmb - hi.astype(jnp.float32)).astype(jnp.bfloat16)
    y2 = _dot(jnp.concatenate([hi, lo], axis=0), rows)
    y = x1_ref[...] + y2[:T] + y2[T:]
    out_ref[...] = _rms(y, g_ref[...])


def _moe_combine(ys, ptab, lpos_t, wts, x1, final_g):
    n = x1.shape[0]
    T = MOE_TILE
    n_steps = n // T
    smem_blk = lambda shift: pl.BlockSpec(
        (None, 1, MOE_PIECE_TAB), lambda i: (jnp.minimum(i + shift, n_steps - 1), 0, 0),
        memory_space=pltpu.SMEM)

    def kernel(ptab_ref, ptab_next_ref, *rest):
        _combine_kernel(ptab_ref.at[0], ptab_next_ref.at[0], *rest)

    return pl.pallas_call(
        kernel,
        grid=(n_steps,),
        in_specs=[smem_blk(0), smem_blk(1),
                  pl.BlockSpec(memory_space=pl.ANY),
                  pl.BlockSpec((T, TOP_K), lambda i: (i, 0)),
                  pl.BlockSpec((T, TOP_K), lambda i: (i, 0)),
                  pl.BlockSpec((T, D_MODEL), lambda i: (i, 0)),
                  pl.BlockSpec((1, D_MODEL), lambda i: (0, 0))],
        out_specs=pl.BlockSpec((T, D_MODEL), lambda i: (i, 0)),
        out_shape=jax.ShapeDtypeStruct((n, D_MODEL), jnp.float32),
        scratch_shapes=[pltpu.VMEM((2, MOE_TILE_ROWS, D_MODEL // 2), jnp.uint32),
                        pltpu.SemaphoreType.DMA((2,))],
        compiler_params=_cparams(1),
        name="moe_combine",
    )(ptab, ptab, ys, lpos_t, wts, x1, final_g)


def kernel(x, mem, attn_norm_g, mem_norm_g, w_in, w_mem_kv, na_rpb, t5_rel_bias, w_branch_a, w_branch_b,
           w_branch_c, w_out, ffn_norm_g, router_w, router_b, expert_w_gate_up, expert_b_gate_up,
           expert_w_down, expert_b_down, final_norm_g):
    B, S, D = x.shape
    depth = w_in.shape[0]
    bf = jnp.bfloat16
    x2d = x.reshape(B * S, D)
    mem2d = mem.reshape(B * mem.shape[1], D)
    assert depth == 1, "single-layer block (per-layer parameters carry a leading axis of 1)"
    l = 0
    na_bias, dil_bias = _bias_prep(na_rpb[l], t5_rel_bias)
    mkv = _mem_kv(mem2d, mem_norm_g[l][None], w_mem_kv[l].astype(bf))
    na_qkv, d0, d1, d2, qc = _in_proj(x2d, attn_norm_g[l][None], w_in[l], B, S)
    gates = _gate_proj(x2d, attn_norm_g[l][None], w_in[l])
    na_out = _na_attn(na_qkv, na_bias, B, S)
    dil_outs = [_dil_attn(qkv, dil_bias, gi, dil, B, S)
                for gi, (qkv, (_, dil)) in enumerate(zip((d0, d1, d2), DIL_GROUPS))]
    x1, h, topk_idx, topk_w, segs = _mix_out(
        x2d, na_out, dil_outs, qc, gates, mkv, w_branch_a[l].astype(bf), w_branch_b[l].astype(bf),
        w_branch_c[l].astype(bf), w_out[l].astype(bf), ffn_norm_g[l][None], router_w[l].T,
        router_b[l][:, None], B, S)

    n = B * S
    max_rows = n * TOP_K + (n // MOE_TILE) * N_EXPERTS * (MOE_PIECE - 1)
    n_blocks = -(-max_rows // MOE_BLOCK) + N_EXPERTS
    lpos, ptab, block_expert, pad_ends = _moe_route(topk_idx, segs, n_blocks)
    block_expert = block_expert.reshape(n_blocks)
    pad_ends = pad_ends.reshape(N_EXPERTS)
    xs = _moe_dispatch(h, lpos, ptab, pad_ends, n_blocks)
    ys = _moe_experts(xs, block_expert, pad_ends, expert_w_gate_up[l], expert_b_gate_up[l][:, None, :],
                      expert_w_down[l], expert_b_down[l][:, None, :])
    out = _moe_combine(ys, ptab, lpos.T, topk_w.T, x1, final_norm_g[None])
    return out.reshape(B, S, D)
```

```python
import functools
import math

import numpy as np
import jax
import jax.numpy as jnp
from jax import lax
from jax.experimental import pallas as pl
from jax.experimental.pallas import tpu as pltpu

D_MODEL = 1024
GRID_W = 64
MEM_HEADS = 4
MEM_HEAD_DIM = 128
HEAD_DIM = 64
NA_HEADS = 8
NA_WIN_ROWS = 8
NA_WIN_COLS = 16
DIL_GROUPS = ((128, 1), (512, 4), (2048, 16))
DIL_HEADS_PER_GROUP = 4
DIL_BLOCK = 128
T5_BUCKETS = 32
T5_MAX_DIST = 1024
N_EXPERTS = 32
TOP_K = 4
D_FF = D_MODEL
SWIGLU_ALPHA = 1.702
SWIGLU_LIMIT = 7.0
RMS_EPS = 1e-6
NEG_INF = -1e30

WIDTH_A = NA_HEADS * HEAD_DIM
WIDTH_B = DIL_HEADS_PER_GROUP * len(DIL_GROUPS) * HEAD_DIM
WIDTH_G = DIL_HEADS_PER_GROUP * HEAD_DIM
WIDTH_C = MEM_HEADS * MEM_HEAD_DIM
OFF_QA, OFF_KA, OFF_VA = 0, WIDTH_A, 2 * WIDTH_A
OFF_QB = 3 * WIDTH_A
OFF_KB = OFF_QB + WIDTH_B
OFF_VB = OFF_KB + WIDTH_B
OFF_QC = OFF_VB + WIDTH_B
OFF_GATE = OFF_QC + WIDTH_C
IN_COLS = OFF_GATE + 3 * D_MODEL

LANES = 128
MXU_WIDTH = 256
VMEM_LIMIT = 56 * 1024 * 1024

IN_PROJ_TILE = 1024
PROJ_TILE = 512
NA_ROWS_PER_STEP = 16
DIL_STEP_TOKENS = 2048
MOE_BLOCK = 512
MOE_TILE = 256
MOE_PIECE = 8
MOE_TILE_ROWS = -(-(TOP_K * MOE_TILE + N_EXPERTS * (MOE_PIECE - 1)) // MXU_WIDTH) * MXU_WIDTH
MOE_PIECE_TAB = 256
assert MOE_TILE_ROWS // MOE_PIECE < MOE_PIECE_TAB
MOE_SPARE_BLOCKS = -(-2 * MOE_TILE_ROWS // MOE_BLOCK)


def _cparams(n_axes):
    return pltpu.CompilerParams(
        dimension_semantics=("arbitrary",) * n_axes, vmem_limit_bytes=VMEM_LIMIT)


def _rms(x, g):
    return x * lax.rsqrt(jnp.mean(x * x, axis=-1, keepdims=True) + RMS_EPS) * g


def _sigmoid(x):
    return 1.0 / (1.0 + jnp.exp(-x))


def _pack_bf16_pairs(x):
    w = x.shape[1] // 2
    bits = lax.bitcast_convert_type(x.astype(jnp.bfloat16).astype(jnp.float32), jnp.uint32)
    return (bits[:, w:] & jnp.uint32(0xFFFF0000)) | (bits[:, :w] >> 16)


def _unpack_bf16_pairs(p):
    lo = lax.bitcast_convert_type(p << 16, jnp.float32)
    hi = lax.bitcast_convert_type(p & jnp.uint32(0xFFFF0000), jnp.float32)
    return jnp.concatenate([lo, hi], axis=1).astype(jnp.bfloat16)


def _dot(a, b):
    return jnp.dot(a, b, preferred_element_type=jnp.float32)


def _dot_nt(a, b):
    return lax.dot_general(a, b, (((1,), (1,)), ((), ())), preferred_element_type=jnp.float32)


def _mem_kv_kernel(mem_ref, g_ref, w_ref, out_ref):
    h = _rms(mem_ref[...], g_ref[...]).astype(jnp.bfloat16)
    out_ref[...] = _dot(h, w_ref[...]).astype(jnp.bfloat16)


def _mem_kv(mem2d, g, w):
    rows = mem2d.shape[0]
    return pl.pallas_call(
        _mem_kv_kernel,
        out_shape=jax.ShapeDtypeStruct((rows, 2 * WIDTH_C), jnp.bfloat16),
        name="mem_kv",
        compiler_params=pltpu.CompilerParams(vmem_limit_bytes=VMEM_LIMIT),
    )(mem2d, g, w)


def _in_proj_kernel(x_ref, g_ref, w_ref, na_ref, d0_ref, d1_ref, d2_ref, qc_ref, acc_ref):
    T = x_ref.shape[0]
    h = _rms(x_ref[...], g_ref[...]).astype(jnp.bfloat16)
    q_scale = HEAD_DIM ** -0.5

    for c in range(3):
        acc = _dot(h, w_ref[:, c * WIDTH_A:(c + 1) * WIDTH_A])
        if c == 0:
            acc = acc * q_scale
        na_ref[:, c * WIDTH_A:(c + 1) * WIDTH_A] = acc.astype(jnp.bfloat16)

    for gi, ((_, dil), d_ref) in enumerate(zip(DIL_GROUPS, (d0_ref, d1_ref, d2_ref))):
        for kind, off in enumerate((OFF_QB, OFF_KB, OFF_VB)):
            lo = off + gi * WIDTH_G
            acc = _dot(h, w_ref[:, lo:lo + WIDTH_G])
            if kind == 0:
                acc = acc * q_scale
            cols = slice(kind * WIDTH_G, (kind + 1) * WIDTH_G)
            if dil == 1:
                d_ref[0, 0, :, cols] = acc.astype(jnp.bfloat16)
            else:
                for c in range(WIDTH_G // LANES):
                    acc_ref[c] = acc[:, c * LANES:(c + 1) * LANES]
                for rho in range(dil):
                    d_ref[0, rho, :, cols] = jnp.concatenate(
                        [acc_ref[c, pl.ds(rho, T // dil, stride=dil), :] for c in range(WIDTH_G // LANES)],
                        axis=-1).astype(jnp.bfloat16)

    qc_ref[...] = _dot(h, w_ref[:, OFF_QC:OFF_QC + WIDTH_C]).astype(jnp.bfloat16)


def _gate_proj_kernel(x_ref, g_ref, w_ref, gate_ref):
    h = _rms(x_ref[...], g_ref[...]).astype(jnp.bfloat16)
    chunk = 512
    first = OFF_GATE - IN_COLS // 2
    for c in range(3 * D_MODEL // chunk):
        lo = first + c * chunk
        gate_ref[:, c * chunk:(c + 1) * chunk] = _sigmoid(_dot(h, w_ref[:, lo:lo + chunk])).astype(jnp.bfloat16)


def _gate_proj(x2d, g, w_bf16):
    n = x2d.shape[0]
    T = IN_PROJ_TILE
    assert (IN_COLS // 2) % LANES == 0 and IN_COLS // 2 <= OFF_GATE
    return pl.pallas_call(
        _gate_proj_kernel,
        grid=(n // T,),
        in_specs=[pl.BlockSpec((T, D_MODEL), lambda i: (i, 0)),
                  pl.BlockSpec((1, D_MODEL), lambda i: (0, 0)),
                  pl.BlockSpec((D_MODEL, IN_COLS // 2), lambda i: (0, 1), pipeline_mode=pl.Buffered(1))],
        out_specs=pl.BlockSpec((T, 3 * D_MODEL), lambda i: (i, 0)),
        out_shape=jax.ShapeDtypeStruct((n, 3 * D_MODEL), jnp.bfloat16),
        compiler_params=_cparams(1),
        name="gate_proj",
    )(x2d, g, w_bf16)


def _in_proj(x2d, g, w_bf16, batch, seq):
    n = x2d.shape[0]
    T = IN_PROJ_TILE
    tiles_per_batch = seq // T
    out_shape = [jax.ShapeDtypeStruct((n, 3 * WIDTH_A), jnp.bfloat16)]
    out_specs = [pl.BlockSpec((T, 3 * WIDTH_A), lambda i: (i, 0))]
    for _, dil in DIL_GROUPS:
        out_shape.append(jax.ShapeDtypeStruct((batch, dil, seq // dil, 3 * WIDTH_G), jnp.bfloat16))
        out_specs.append(pl.BlockSpec((1, dil, T // dil, 3 * WIDTH_G),
                                      lambda i: (i // tiles_per_batch, 0, i % tiles_per_batch, 0)))
    out_shape += [jax.ShapeDtypeStruct((n, WIDTH_C), jnp.bfloat16)]
    out_specs += [pl.BlockSpec((T, WIDTH_C), lambda i: (i, 0))]
    return pl.pallas_call(
        _in_proj_kernel,
        grid=(n // T,),
        in_specs=[pl.BlockSpec((T, D_MODEL), lambda i: (i, 0)),
                  pl.BlockSpec((1, D_MODEL), lambda i: (0, 0)),
                  pl.BlockSpec((D_MODEL, OFF_GATE), lambda i: (0, 0), pipeline_mode=pl.Buffered(1))],
        out_specs=out_specs,
        out_shape=out_shape,
        scratch_shapes=[pltpu.VMEM((WIDTH_G // LANES, T, LANES), jnp.float32)],
        compiler_params=_cparams(1),
        name="in_proj",
    )(x2d, g, w_bf16)


def _na_kernel(q_ref, k_ref, v_ref, bias_ref, out_ref, *, rows):
    i = pl.program_id(1)
    lane = lax.broadcasted_iota(jnp.int32, (GRID_W, LANES), 1)
    low_half = lane < HEAD_DIM
    n_win = NA_WIN_ROWS * GRID_W

    def row_body(j, carry):
        qr = i * NA_ROWS_PER_STEP + j
        rs = jnp.clip(qr - NA_WIN_ROWS // 2, 0, rows - NA_WIN_ROWS)
        shift = rs - qr + NA_WIN_ROWS - 1
        q_off = pl.multiple_of(j * GRID_W, GRID_W)
        k_off = pl.multiple_of(rs * GRID_W, GRID_W)
        scores = []
        for hp in range(NA_HEADS // 2):
            cols = slice(hp * LANES, (hp + 1) * LANES)
            qp = q_ref[pl.ds(q_off, GRID_W), cols]
            kp = k_ref[pl.ds(k_off, n_win), cols]
            zero = jnp.zeros_like(qp)
            q2 = jnp.concatenate([jnp.where(low_half, qp, zero), jnp.where(low_half, zero, qp)], axis=0)
            bias = jnp.concatenate(
                [jnp.concatenate([bias_ref[shift + 2 * p, 2 * hp + hh] for p in range(NA_WIN_ROWS // 2)], axis=-1)
                 for hh in range(2)], axis=0)
            scores.append(_dot_nt(q2, kp) + bias)
        probs = []
        for s in scores:
            e = jnp.exp(s - jnp.max(s, axis=-1, keepdims=True))
            probs.append((e.astype(jnp.bfloat16), jnp.sum(e, axis=-1, keepdims=True)))
        for hp, (p, l) in enumerate(probs):
            cols = slice(hp * LANES, (hp + 1) * LANES)
            o2 = _dot(p, v_ref[pl.ds(k_off, n_win), cols]) / l
            out_ref[pl.ds(q_off, GRID_W), cols] = jnp.where(
                low_half, o2[:GRID_W], o2[GRID_W:]).astype(jnp.bfloat16)
        return carry

    lax.fori_loop(0, NA_ROWS_PER_STEP, row_body, 0, unroll=4)


def _na_attn(na_qkv, bias_tab, batch, seq):
    rows = seq // GRID_W
    steps = rows // NA_ROWS_PER_STEP
    tq = NA_ROWS_PER_STEP * GRID_W
    n = na_qkv.shape[0]
    return pl.pallas_call(
        functools.partial(_na_kernel, rows=rows),
        grid=(batch, steps),
        in_specs=[pl.BlockSpec((tq, WIDTH_A), lambda b, i: (b * steps + i, 0)),
                  pl.BlockSpec((seq, WIDTH_A), lambda b, i: (b, 1)),
                  pl.BlockSpec((seq, WIDTH_A), lambda b, i: (b, 2)),
                  pl.BlockSpec(bias_tab.shape, lambda b, i: (0, 0, 0, 0))],
        out_specs=pl.BlockSpec((tq, WIDTH_A), lambda b, i: (b * steps + i, 0)),
        out_shape=jax.ShapeDtypeStruct((n, WIDTH_A), jnp.bfloat16),
        compiler_params=_cparams(2),
        name="na_attn",
    )(na_qkv, na_qkv, na_qkv, bias_tab)


def _t5_bucket_np(rel):
    half = T5_BUCKETS // 2
    max_exact = half // 2
    ret = np.where(rel > 0, half, 0)
    n = np.abs(rel)
    nf = np.maximum(n, 1).astype(np.float32)
    large = max_exact + (np.log(nf / np.float32(max_exact)) / np.float32(math.log(T5_MAX_DIST / max_exact))
                         * np.float32(half - max_exact)).astype(np.int32)
    large = np.minimum(large, half - 1)
    return ret + np.where(n < max_exact, n, large)


def _dil_bucket_table():
    q = np.arange(DIL_BLOCK)[:, None]
    j = np.arange(2 * DIL_BLOCK)[None, :]
    tabs = []
    for window, dil in DIL_GROUPS:
        half = (window // 2) // dil
        assert half == DIL_BLOCK // 2
        rel = (j - half) - q
        tabs.append(np.where(np.abs(rel) <= half, _t5_bucket_np(rel * dil), -1))
    return np.stack(tabs).astype(np.int32)


def _bias_prep_kernel(rpb_ref, t5_ref, bucket_ref, na_ref, dil_ref):
    n_dc = 2 * NA_WIN_COLS - 1
    n_dr = 2 * NA_WIN_ROWS - 1
    qc = lax.broadcasted_iota(jnp.int32, (GRID_W, 2 * GRID_W), 0)
    lane = lax.broadcasted_iota(jnp.int32, (GRID_W, 2 * GRID_W), 1)
    second = lane >= GRID_W
    kc = jnp.where(second, lane - GRID_W, lane)
    diff = kc - qc
    cs = jnp.clip(qc - NA_WIN_COLS // 2, 0, GRID_W - NA_WIN_COLS)
    valid = jnp.logical_and(kc >= cs, kc < cs + NA_WIN_COLS)

    def na_tile(mh, carry):
        m = mh // NA_HEADS
        h = mh - m * NA_HEADS
        base = (h * n_dr + m) * n_dc
        acc = jnp.zeros((GRID_W, 2 * GRID_W), jnp.float32)
        for d in range(n_dc):
            v = jnp.where(second, rpb_ref[base + n_dc + d], rpb_ref[base + d])
            acc = jnp.where(diff == d - (NA_WIN_COLS - 1), v, acc)
        na_ref[m, h] = jnp.where(valid, acc, NEG_INF)
        return carry

    lax.fori_loop(0, (n_dr - 1) * NA_HEADS, na_tile, 0)

    j = lax.broadcasted_iota(jnp.int32, (DIL_BLOCK, 2 * DIL_BLOCK), 1)
    half = DIL_BLOCK // 2
    n_heads = DIL_HEADS_PER_GROUP * len(DIL_GROUPS)
    for gi in range(len(DIL_GROUPS)):
        bucket = bucket_ref[gi]
        in_win = bucket >= 0
        for h in range(DIL_HEADS_PER_GROUP):
            acc = jnp.zeros((DIL_BLOCK, 2 * DIL_BLOCK), jnp.float32)
            for b in range(T5_BUCKETS):
                acc = jnp.where(bucket == b, t5_ref[b * n_heads + gi * DIL_HEADS_PER_GROUP + h], acc)
            for v, ok in enumerate((jnp.logical_and(in_win, j >= half), in_win,
                                    jnp.logical_and(in_win, j < 2 * DIL_BLOCK - half))):
                dil_ref[gi, v, h] = jnp.where(ok, acc, NEG_INF)


def _bias_prep(rpb, t5):
    smem = pl.BlockSpec(memory_space=pltpu.SMEM)
    return pl.pallas_call(
        _bias_prep_kernel,
        in_specs=[smem, smem, pl.BlockSpec(memory_space=pltpu.VMEM)],
        out_shape=[jax.ShapeDtypeStruct((2 * NA_WIN_ROWS - 2, NA_HEADS, GRID_W, 2 * GRID_W), jnp.float32),
                   jax.ShapeDtypeStruct((len(DIL_GROUPS), 3, DIL_HEADS_PER_GROUP, DIL_BLOCK, 2 * DIL_BLOCK),
                                        jnp.float32)],
        compiler_params=pltpu.CompilerParams(vmem_limit_bytes=VMEM_LIMIT),
        name="bias_prep",
    )(rpb.reshape(-1), t5.reshape(-1), jnp.asarray(_dil_bucket_table()))


def _dil_kernel(q_ref, kp_ref, kc_ref, kn_ref, vp_ref, vc_ref, vn_ref, bias_ref, o_ref, lse_ref, *, dil):
    lane = lax.broadcasted_iota(jnp.int32, (DIL_BLOCK, LANES), 1)
    low_half = lane < HEAD_DIM
    hb = DIL_BLOCK // 2

    span = pl.program_id(1)
    n_spans = pl.num_programs(1)
    nbs = kc_ref.shape[2] // DIL_BLOCK

    def window(prev_ref, cur_ref, next_ref, rho, blk, cols):
        lo = blk * DIL_BLOCK
        first = prev_ref[0, rho, hb:, cols] if blk == 0 else cur_ref[0, rho, lo - hb:lo, cols]
        last = (next_ref[0, rho, :hb, cols] if blk == nbs - 1
                else cur_ref[0, rho, lo + DIL_BLOCK:lo + DIL_BLOCK + hb, cols])
        return jnp.concatenate([first, cur_ref[0, rho, lo:lo + DIL_BLOCK, cols], last], axis=0)

    def unit(rho, blk):
        variant = 1
        if blk == nbs - 1:
            variant = jnp.where(span == n_spans - 1, 2, variant)
        if blk == 0:
            variant = jnp.where(span == 0, 0, variant)
        scores = []
        for hp in range(DIL_HEADS_PER_GROUP // 2):
            cols = slice(hp * LANES, (hp + 1) * LANES)
            qp = q_ref[0, rho, blk * DIL_BLOCK:(blk + 1) * DIL_BLOCK, cols]
            zero = jnp.zeros_like(qp)
            q2 = jnp.concatenate([jnp.where(low_half, qp, zero), jnp.where(low_half, zero, qp)], axis=0)
            kw = window(kp_ref, kc_ref, kn_ref, rho, blk, cols)
            bias = jnp.concatenate([bias_ref[variant, 2 * hp], bias_ref[variant, 2 * hp + 1]], axis=0)
            scores.append(_dot_nt(q2, kw) + bias)
        probs = []
        for s in scores:
            m = jnp.max(s, axis=-1, keepdims=True)
            e = jnp.exp(s - m)
            probs.append((e.astype(jnp.bfloat16), m, jnp.sum(e, axis=-1, keepdims=True)))
        for hp, (p, m, l) in enumerate(probs):
            cols = slice(hp * LANES, (hp + 1) * LANES)
            o2 = _dot(p, window(vp_ref, vc_ref, vn_ref, rho, blk, cols)) / l
            lse2 = jnp.broadcast_to(m + jnp.log(l), (2 * DIL_BLOCK, LANES))
            o_pair = jnp.where(low_half, o2[:DIL_BLOCK], o2[DIL_BLOCK:])
            lse_pair = jnp.where(low_half, lse2[:DIL_BLOCK], lse2[DIL_BLOCK:])
            if dil == 1:
                rows = slice(blk * DIL_BLOCK, (blk + 1) * DIL_BLOCK)
                o_ref[hp, rows, :] = o_pair
                lse_ref[hp, rows, :] = lse_pair
            else:
                rows = pl.ds(blk * DIL_BLOCK * dil + rho, DIL_BLOCK, stride=dil)
                o_ref[hp, rows, :] = o_pair
                lse_ref[hp, rows, :] = lse_pair

    def class_body(rho, carry):
        for blk in range(nbs):
            unit(rho, blk)
        return carry

    if dil == 1:
        class_body(0, 0)
    else:
        lax.fori_loop(0, dil, class_body, 0, unroll=4 if nbs == 1 else 1)


def _dil_attn(qkv, bias_tab, group, dil, batch, seq):
    L = seq // dil
    nb = L // DIL_BLOCK
    assert nb >= 2
    tq = DIL_STEP_TOKENS
    span = tq // dil
    nbs = span // DIL_BLOCK
    n_spans = L // span

    def cur(col):
        return pl.BlockSpec((1, dil, span, WIDTH_G), lambda b, n: (b, 0, n, col))

    def edge(col, shift):
        return pl.BlockSpec((1, dil, DIL_BLOCK, WIDTH_G),
                            lambda b, n: (b, 0, jnp.clip(n * nbs + (shift if shift < 0 else nbs), 0, nb - 1), col))

    n_pairs = WIDTH_G // LANES
    out_spec = pl.BlockSpec((n_pairs, tq, LANES), lambda b, n: (0, b * n_spans + n, 0))
    return pl.pallas_call(
        functools.partial(_dil_kernel, dil=dil),
        grid=(batch, n_spans),
        in_specs=[cur(0), edge(1, -1), cur(1), edge(1, 1), edge(2, -1), cur(2), edge(2, 1),
                  pl.BlockSpec((None, 3, DIL_HEADS_PER_GROUP, DIL_BLOCK, 2 * DIL_BLOCK),
                               lambda b, n: (group, 0, 0, 0, 0))],
        out_specs=[out_spec, out_spec],
        out_shape=[jax.ShapeDtypeStruct((n_pairs, batch * seq, LANES), jnp.float32)] * 2,
        compiler_params=_cparams(2),
        name=f"dil_attn_d{dil}",
    )(qkv, qkv, qkv, qkv, qkv, qkv, qkv, bias_tab)


def _mix_kernel(x_ref, na_ref, o0_ref, l0_ref, o1_ref, l1_ref, o2_ref, l2_ref, qc_ref, gate_ref,
                mkv_ref, wa_ref, wb_ref, wc_ref, wo_ref, fg_ref, rw_ref, rb_ref,
                x1_ref, h_ref, idx_ref, wt_ref, seg_ref):
    T = x_ref.shape[0]
    ob = []
    for hp in range(WIDTH_G // LANES):
        l0, l1, l2 = l0_ref[hp], l1_ref[hp], l2_ref[hp]
        m = jnp.maximum(jnp.maximum(l0, l1), l2)
        e0, e1, e2 = jnp.exp(l0 - m), jnp.exp(l1 - m), jnp.exp(l2 - m)
        ob.append((e0 * o0_ref[hp] + e1 * o1_ref[hp] + e2 * o2_ref[hp]) / (e0 + e1 + e2))
    ob = jnp.concatenate(ob, axis=-1)

    mem_scale = MEM_HEAD_DIM ** -0.5
    scores = []
    for h in range(MEM_HEADS):
        cols = slice(h * MEM_HEAD_DIM, (h + 1) * MEM_HEAD_DIM)
        scores.append(_dot_nt(qc_ref[:, cols], mkv_ref[:, cols]) * mem_scale)
    probs = []
    for s in scores:
        e = jnp.exp(s - jnp.max(s, axis=-1, keepdims=True))
        probs.append((e.astype(jnp.bfloat16), jnp.sum(e, axis=-1, keepdims=True)))
    oc = []
    for h, (p, l) in enumerate(probs):
        oc.append(_dot(p, mkv_ref[:, WIDTH_C + h * MEM_HEAD_DIM:WIDTH_C + (h + 1) * MEM_HEAD_DIM]) / l)
    oc = jnp.concatenate(oc, axis=-1).astype(jnp.bfloat16)

    y_a = _dot(na_ref[...], wa_ref[...])
    y_b = _dot(ob.astype(jnp.bfloat16), wb_ref[...])
    y_c = _dot(oc, wc_ref[...])
    merged = (gate_ref[:, 0:D_MODEL].astype(jnp.float32) * y_a
              + gate_ref[:, D_MODEL:2 * D_MODEL].astype(jnp.float32) * y_b
              + gate_ref[:, 2 * D_MODEL:3 * D_MODEL].astype(jnp.float32) * y_c)
    x1 = x_ref[...] + _dot(merged.astype(jnp.bfloat16), wo_ref[...])
    x1_ref[...] = x1

    h = _rms(x1, fg_ref[...])
    h_ref[...] = h.astype(jnp.bfloat16)
    logits = lax.dot_general(rw_ref[...], h, (((1,), (1,)), ((), ())), preferred_element_type=jnp.float32,
                             precision=lax.Precision.HIGHEST) + rb_ref[...]
    expert = lax.broadcasted_iota(jnp.int32, (N_EXPERTS, T), 0)
    vals, idxs = [], []
    for _ in range(TOP_K):
        mx = jnp.max(logits, axis=0, keepdims=True)
        sel = jnp.min(jnp.where(logits == mx, expert, N_EXPERTS), axis=0, keepdims=True)
        vals.append(mx)
        idxs.append(sel)
        logits = jnp.where(expert == sel, -jnp.inf, logits)
    ev = [jnp.exp(v - vals[0]) for v in vals]
    den = ev[0] + ev[1] + ev[2] + ev[3]
    idx_ref[...] = jnp.concatenate(idxs, axis=0)
    wt_ref[...] = jnp.concatenate([e / den for e in ev], axis=0)
    member = sum((expert == sel).astype(jnp.float32) for sel in idxs)
    for s in range(T // MOE_TILE):
        cnt = jnp.sum(member[:, s * MOE_TILE:(s + 1) * MOE_TILE], axis=1, keepdims=True)
        seg = jnp.floor((cnt + (MOE_PIECE - 1)) * (1.0 / MOE_PIECE)) * MOE_PIECE
        seg_ref[s] = jnp.broadcast_to(seg, (N_EXPERTS, LANES))


def _mix_out(x2d, na_out, dil_outs, qc, gates, mkv, wa, wb, wc, wo, ffn_g, router_w, router_b, batch, seq):
    n = x2d.shape[0]
    T = PROJ_TILE
    tiles_per_batch = seq // T
    mem_len = mkv.shape[0] // batch
    row = lambda w: pl.BlockSpec((T, w), lambda i: (i, 0))
    full = lambda a: pl.BlockSpec(a.shape, lambda i: (0,) * a.ndim, pipeline_mode=pl.Buffered(1))
    in_specs = [row(D_MODEL), row(WIDTH_A)]
    args = [x2d, na_out]
    for o, l in dil_outs:
        pair_rows = pl.BlockSpec((WIDTH_G // LANES, T, LANES), lambda i: (0, i, 0))
        in_specs += [pair_rows, pair_rows]
        args += [o, l]
    in_specs += [row(WIDTH_C), row(3 * D_MODEL),
                 pl.BlockSpec((mem_len, 2 * WIDTH_C), lambda i: (i // tiles_per_batch, 0)),
                 full(wa), full(wb), full(wc), full(wo), full(ffn_g), full(router_w), full(router_b)]
    args += [qc, gates, mkv, wa, wb, wc, wo, ffn_g, router_w, router_b]
    return pl.pallas_call(
        _mix_kernel,
        grid=(n // T,),
        in_specs=in_specs,
        out_specs=[row(D_MODEL), row(D_MODEL), pl.BlockSpec((TOP_K, T), lambda i: (0, i)),
                   pl.BlockSpec((TOP_K, T), lambda i: (0, i)),
                   pl.BlockSpec((T // MOE_TILE, N_EXPERTS, LANES), lambda i: (i, 0, 0))],
        out_shape=[jax.ShapeDtypeStruct((n, D_MODEL), jnp.float32),
                   jax.ShapeDtypeStruct((n, D_MODEL), jnp.bfloat16),
                   jax.ShapeDtypeStruct((TOP_K, n), jnp.int32),
                   jax.ShapeDtypeStruct((TOP_K, n), jnp.float32),
                   jax.ShapeDtypeStruct((n // MOE_TILE, N_EXPERTS, LANES), jnp.float32)],
        compiler_params=_cparams(1),
        name="mix_out",
    )(*args)


def _route_kernel(idx_ref, segs_ref, lpos_ref, ptab_ref, be_ref, pe_ref, carry_ref, pstart_ref):
    i = pl.program_id(0)
    T = idx_ref.shape[1]
    n_blocks = be_ref.shape[1]
    expert = lax.broadcasted_iota(jnp.int32, (N_EXPERTS, T), 0)
    idx = idx_ref[...]
    onehots = [expert == idx[k:k + 1, :] for k in range(TOP_K)]
    member = sum(o.astype(jnp.float32) for o in onehots)
    seg_b = segs_ref[i]
    seg = seg_b[:, 0:1]
    r = lax.broadcasted_iota(jnp.int32, (N_EXPERTS, N_EXPERTS), 0)
    c = lax.broadcasted_iota(jnp.int32, (N_EXPERTS, N_EXPERTS), 1)
    lower_incl = (r >= c).astype(jnp.float32)

    def cumsum_experts(v):
        return jnp.dot(lower_incl, v, preferred_element_type=jnp.float32, precision=lax.Precision.HIGHEST)

    @pl.when(i == 0)
    def _():
        counts = jnp.sum(segs_ref[...], axis=0)
        padded = jnp.floor((counts + (MOE_BLOCK - 1)) * (1.0 / MOE_BLOCK)) * MOE_BLOCK
        pad_ends = cumsum_experts(padded)
        pstart_ref[...] = pad_ends - padded
        carry_ref[...] = jnp.zeros_like(carry_ref)
        pe_ref[...] = pad_ends[:, 0:1].astype(jnp.int32)
        blk_start = (lax.broadcasted_iota(jnp.int32, (N_EXPERTS, n_blocks), 1) * MOE_BLOCK).astype(jnp.float32)
        be = jnp.sum((pad_ends[:, 0:1] <= blk_start).astype(jnp.int32), axis=0, keepdims=True)
        be_ref[...] = jnp.minimum(be, N_EXPERTS - 1)

    local_end = cumsum_experts(seg_b)[:, 0:1]
    local_start = local_end - seg
    global_start = pstart_ref[:, 0:1] + carry_ref[:, 0:1]
    rr = lax.broadcasted_iota(jnp.int32, (T, T), 0)
    cc = lax.broadcasted_iota(jnp.int32, (T, T), 1)
    earlier = (rr < cc).astype(jnp.bfloat16)
    before = _dot(member.astype(jnp.bfloat16), earlier)
    base = local_start + before
    lpos = [jnp.sum(jnp.where(o, base, 0.0), axis=0, keepdims=True) for o in onehots]
    lpos_ref[...] = jnp.concatenate(lpos, axis=0).astype(jnp.int32)

    inv = 1.0 / MOE_PIECE
    piece = lax.broadcasted_iota(jnp.int32, (N_EXPERTS, MOE_PIECE_TAB), 1).astype(jnp.float32)
    first = local_start * inv
    inside = jnp.logical_and(piece >= first, piece < local_end * inv)
    dst = jnp.sum(jnp.where(inside, global_start * inv + piece - first, 0.0), axis=0, keepdims=True)
    n_pieces = local_end[N_EXPERTS - 1:N_EXPERTS, :] * inv
    spare = (n_blocks * MOE_BLOCK // MOE_PIECE
             + (i % 2) * (MOE_TILE_ROWS // MOE_PIECE)).astype(jnp.float32)
    dst = jnp.where(piece[0:1, :] >= n_pieces, spare + piece[0:1, :], dst)
    dst = jnp.where(piece[0:1, :] == MOE_PIECE_TAB - 1, n_pieces, dst)
    ptab_ref[...] = dst.astype(jnp.int32)
    carry_ref[...] += seg


def _moe_route(topk_idx, segs, n_blocks):
    n = topk_idx.shape[1]
    T = MOE_TILE
    acc = pltpu.VMEM((N_EXPERTS, LANES), jnp.float32)
    return pl.pallas_call(
        _route_kernel,
        grid=(n // T,),
        in_specs=[pl.BlockSpec((TOP_K, T), lambda i: (0, i)),
                  pl.BlockSpec(segs.shape, lambda i: (0, 0, 0))],
        out_specs=[pl.BlockSpec((TOP_K, T), lambda i: (0, i)),
                   pl.BlockSpec((None, 1, MOE_PIECE_TAB), lambda i: (i, 0, 0)),
                   pl.BlockSpec((1, n_blocks), lambda i: (0, 0)),
                   pl.BlockSpec((N_EXPERTS, 1), lambda i: (0, 0))],
        out_shape=[jax.ShapeDtypeStruct((TOP_K, n), jnp.int32),
                   jax.ShapeDtypeStruct((n // T, 1, MOE_PIECE_TAB), jnp.int32),
                   jax.ShapeDtypeStruct((1, n_blocks), jnp.int32),
                   jax.ShapeDtypeStruct((N_EXPERTS, 1), jnp.int32)],
        scratch_shapes=[acc, acc],
        compiler_params=_cparams(1),
        name="moe_route",
    )(topk_idx, segs)


def _piece(ref, p):
    return ref.at[pl.ds(pl.multiple_of(p * MOE_PIECE, MOE_PIECE), MOE_PIECE), :]


def _dispatch_kernel(pe_ref, ptab_ref, lpos_ref, h_ref, xs_hbm, zbuf, pbuf, sem, zsem):
    i = pl.program_id(0)
    T = h_ref.shape[0]
    half = h_ref.shape[1] // 2

    def zero_copy(e):
        end = pe_ref[e]
        start = pl.multiple_of(end - MOE_BLOCK, MOE_BLOCK)
        return pltpu.make_async_copy(zbuf, xs_hbm.at[pl.ds(start, MOE_BLOCK), :], zsem)

    def nonempty(e):
        return pe_ref[e] > (pe_ref[e - 1] if e else 0)

    @pl.when(i == 0)
    def _():
        zbuf[...] = jnp.zeros_like(zbuf)
        for e in range(N_EXPERTS):
            @pl.when(nonempty(e))
            def _():
                zero_copy(e).start()
        for e in range(N_EXPERTS):
            @pl.when(nonempty(e))
            def _():
                zero_copy(e).wait()

        def tail_copy(b):
            return pltpu.make_async_copy(
                zbuf, xs_hbm.at[pl.ds(pl.multiple_of(b * MOE_BLOCK, MOE_BLOCK), MOE_BLOCK), :], zsem)
        first_tail = pe_ref[N_EXPERTS - 1] // MOE_BLOCK
        n_blocks = xs_hbm.shape[0] // MOE_BLOCK
        lax.fori_loop(first_tail, n_blocks, lambda b, c: (tail_copy(b).start(), c)[1], 0)
        lax.fori_loop(first_tail, n_blocks, lambda b, c: (tail_copy(b).wait(), c)[1], 0)

    row = lax.broadcasted_iota(jnp.int32, (MOE_TILE_ROWS, T), 0)
    lpos = lpos_ref[...]
    perm = sum((row == lpos[k:k + 1, :]).astype(jnp.bfloat16) for k in range(TOP_K))
    lo = lax.bitcast_convert_type(_dot(perm, h_ref[:, :half]), jnp.uint32)
    hi = lax.bitcast_convert_type(_dot(perm, h_ref[:, half:]), jnp.uint32)
    slot = i % 2
    pbuf[slot] = hi | (lo >> 16)

    n_tile_pieces = MOE_TILE_ROWS // MOE_PIECE

    def piece_copy(s, p):
        return pltpu.make_async_copy(_piece(pbuf.at[s], p), _piece(xs_hbm, ptab_ref[p]), sem.at[s])

    def wait_all(s):
        lax.fori_loop(0, n_tile_pieces, lambda p, c: (piece_copy(s, 0).wait(), c)[1], 0, unroll=8)

    lax.fori_loop(0, n_tile_pieces, lambda p, c: (piece_copy(slot, p).start(), c)[1], 0, unroll=16)

    @pl.when(i > 0)
    def _():
        wait_all(1 - slot)

    @pl.when(i == pl.num_programs(0) - 1)
    def _():
        wait_all(slot)


def _moe_dispatch(h, lpos, ptab, pad_ends, n_blocks):
    n, width = h.shape
    T = MOE_TILE
    n_steps = n // T

    def kernel(pe_ref, ptab_ref, *rest):
        _dispatch_kernel(pe_ref, ptab_ref.at[0], *rest)

    grid_spec = pltpu.PrefetchScalarGridSpec(
        num_scalar_prefetch=1,
        grid=(n_steps,),
        in_specs=[pl.BlockSpec((None, 1, MOE_PIECE_TAB), lambda i, pe: (i, 0, 0), memory_space=pltpu.SMEM),
                  pl.BlockSpec((TOP_K, T), lambda i, pe: (0, i)),
                  pl.BlockSpec((T, width), lambda i, pe: (i, 0))],
        out_specs=pl.BlockSpec(memory_space=pl.ANY),
        scratch_shapes=[pltpu.VMEM((MOE_BLOCK, width // 2), jnp.uint32),
                        pltpu.VMEM((2, MOE_TILE_ROWS, width // 2), jnp.uint32),
                        pltpu.SemaphoreType.DMA((2,)),
                        pltpu.SemaphoreType.DMA(())],
    )
    return pl.pallas_call(
        kernel,
        grid_spec=grid_spec,
        out_shape=jax.ShapeDtypeStruct(((n_blocks + MOE_SPARE_BLOCKS) * MOE_BLOCK, width // 2), jnp.uint32),
        compiler_params=_cparams(1),
        name="moe_dispatch",
    )(pad_ends, ptab, lpos, h)


def _moe_kernel(be_ref, pe_ref, xs_ref, wgu_ref, bgu_ref, wdn_ref, bdn_ref, y_ref):
    del be_ref
    i = pl.program_id(0)
    n_valid = pe_ref[N_EXPERTS - 1] // MOE_BLOCK

    @pl.when(i < n_valid)
    def _():
        x = _unpack_bf16_pairs(xs_ref[...])
        gu = _dot(x, wgu_ref[0]) + bgu_ref[0]
        g = jnp.minimum(gu[:, :D_FF], SWIGLU_LIMIT)
        u = jnp.clip(gu[:, D_FF:], -SWIGLU_LIMIT, SWIGLU_LIMIT)
        a = (u + 1.0) * (g * _sigmoid(SWIGLU_ALPHA * g))
        y_ref[...] = _pack_bf16_pairs(_dot(a.astype(jnp.bfloat16), wdn_ref[0]) + bdn_ref[0])

    @pl.when(i >= n_valid)
    def _():
        y_ref[...] = jnp.zeros_like(y_ref)


def _moe_experts(xs, block_expert, pad_ends, wgu, bgu, wdn, bdn):
    n_blocks = block_expert.shape[0]

    def row_blk(i, be, pe):
        return (jnp.minimum(i, jnp.maximum(pe[N_EXPERTS - 1] // MOE_BLOCK - 1, 0)), 0)

    grid_spec = pltpu.PrefetchScalarGridSpec(
        num_scalar_prefetch=2,
        grid=(n_blocks,),
        in_specs=[pl.BlockSpec((MOE_BLOCK, D_MODEL // 2), row_blk),
                  pl.BlockSpec((1, D_MODEL, 2 * D_FF), lambda i, be, pe: (be[i], 0, 0)),
                  pl.BlockSpec((1, 1, 2 * D_FF), lambda i, be, pe: (be[i], 0, 0)),
                  pl.BlockSpec((1, D_FF, D_MODEL), lambda i, be, pe: (be[i], 0, 0)),
                  pl.BlockSpec((1, 1, D_MODEL), lambda i, be, pe: (be[i], 0, 0))],
        out_specs=pl.BlockSpec((MOE_BLOCK, D_MODEL // 2), lambda i, be, pe: (i, 0)),
    )
    return pl.pallas_call(
        _moe_kernel,
        grid_spec=grid_spec,
        out_shape=jax.ShapeDtypeStruct((n_blocks * MOE_BLOCK, D_MODEL // 2), jnp.uint32),
        compiler_params=_cparams(1),
        name="moe_experts",
    )(block_expert, pad_ends, xs, wgu, bgu, wdn, bdn)


def _gather_pieces(src_hbm, ptab_ref, dst_ref, sem):
    spare = src_hbm.shape[0] // MOE_PIECE

    def body(p, carry):
        src = ptab_ref[p]
        src = jnp.where(src >= spare, 0, src)
        pltpu.make_async_copy(_piece(src_hbm, src), _piece(dst_ref, p), sem).start()
        return carry
    lax.fori_loop(0, MOE_TILE_ROWS // MOE_PIECE, body, 0, unroll=16)


def _wait_pieces(src_hbm, dst_ref, sem):
    def body(p, carry):
        pltpu.make_async_copy(_piece(src_hbm, 0), _piece(dst_ref, 0), sem).wait()
        return carry
    lax.fori_loop(0, MOE_TILE_ROWS // MOE_PIECE, body, 0, unroll=8)


def _combine_kernel(ptab_ref, ptab_next_ref, ys_hbm, lpos_ref, wt_ref, x1_ref, g_ref, out_ref, gbuf, sem):
    i = pl.program_id(0)
    n_steps = pl.num_programs(0)
    slot = i % 2
    T = x1_ref.shape[0]

    @pl.when(i == 0)
    def _():
        _gather_pieces(ys_hbm, ptab_ref, gbuf.at[0], sem.at[0])

    @pl.when(i + 1 < n_steps)
    def _():
        _gather_pieces(ys_hbm, ptab_next_ref, gbuf.at[1 - slot], sem.at[1 - slot])

    _wait_pieces(ys_hbm, gbuf.at[slot], sem.at[slot])
    rows = _unpack_bf16_pairs(gbuf[slot])
    col = lax.broadcasted_iota(jnp.int32, (T, MOE_TILE_ROWS), 1)
    lpos = lpos_ref[...]
    wt = wt_ref[...]
    cmb = jnp.zeros((T, MOE_TILE_ROWS), jnp.float32)
    for k in range(TOP_K):
        cmb = jnp.where(col == lpos[:, k:k + 1], wt[:, k:k + 1], cmb)
    hi = cmb.astype(jnp.bfloat16)
    lo = (cmb - hi.astype(jnp.float32)).astype(jnp.bfloat16)
    y2 = _dot(jnp.concatenate([hi, lo], axis=0), rows)
    y = x1_ref[...] + y2[:T] + y2[T:]
    out_ref[...] = _rms(y, g_ref[...])


def _moe_combine(ys, ptab, lpos_t, wts, x1, final_g):
    n = x1.shape[0]
    T = MOE_TILE
    n_steps = n // T
    smem_blk = lambda shift: pl.BlockSpec(
        (None, 1, MOE_PIECE_TAB), lambda i: (jnp.minimum(i + shift, n_steps - 1), 0, 0),
        memory_space=pltpu.SMEM)

    def kernel(ptab_ref, ptab_next_ref, *rest):
        _combine_kernel(ptab_ref.at[0], ptab_next_ref.at[0], *rest)

    return pl.pallas_call(
        kernel,
        grid=(n_steps,),
        in_specs=[smem_blk(0), smem_blk(1),
                  pl.BlockSpec(memory_space=pl.ANY),
                  pl.BlockSpec((T, TOP_K), lambda i: (i, 0)),
                  pl.BlockSpec((T, TOP_K), lambda i: (i, 0)),
                  pl.BlockSpec((T, D_MODEL), lambda i: (i, 0)),
                  pl.BlockSpec((1, D_MODEL), lambda i: (0, 0))],
        out_specs=pl.BlockSpec((T, D_MODEL), lambda i: (i, 0)),
        out_shape=jax.ShapeDtypeStruct((n, D_MODEL), jnp.float32),
        scratch_shapes=[pltpu.VMEM((2, MOE_TILE_ROWS, D_MODEL // 2), jnp.uint32),
                        pltpu.SemaphoreType.DMA((2,))],
        compiler_params=_cparams(1),
        name="moe_combine",
    )(ptab, ptab, ys, lpos_t, wts, x1, final_g)


def kernel(x, mem, attn_norm_g, mem_norm_g, w_in, w_mem_kv, na_rpb, t5_rel_bias, w_branch_a, w_branch_b,
           w_branch_c, w_out, ffn_norm_g, router_w, router_b, expert_w_gate_up, expert_b_gate_up,
           expert_w_down, expert_b_down, final_norm_g):
    B, S, D = x.shape
    depth = w_in.shape[0]
    x2d = x.reshape(B * S, D)
    mem2d = mem.reshape(B * mem.shape[1], D)
    assert depth == 1, "single-layer block (per-layer parameters carry a leading axis of 1)"
    l = 0
    na_bias, dil_bias = _bias_prep(na_rpb[l], t5_rel_bias)
    mkv = _mem_kv(mem2d, mem_norm_g[l][None], w_mem_kv[l])
    na_qkv, d0, d1, d2, qc = _in_proj(x2d, attn_norm_g[l][None], w_in[l], B, S)
    gates = _gate_proj(x2d, attn_norm_g[l][None], w_in[l])
    na_out = _na_attn(na_qkv, na_bias, B, S)
    dil_outs = [_dil_attn(qkv, dil_bias, gi, dil, B, S)
                for gi, (qkv, (_, dil)) in enumerate(zip((d0, d1, d2), DIL_GROUPS))]
    x1, h, topk_idx, topk_w, segs = _mix_out(
        x2d, na_out, dil_outs, qc, gates, mkv, w_branch_a[l], w_branch_b[l], w_branch_c[l], w_out[l],
        ffn_norm_g[l][None], router_w[l].T, router_b[l][:, None], B, S)

    n = B * S
    max_rows = n * TOP_K + (n // MOE_TILE) * N_EXPERTS * (MOE_PIECE - 1)
    n_blocks = -(-max_rows // MOE_BLOCK) + N_EXPERTS
    lpos, ptab, block_expert, pad_ends = _moe_route(topk_idx, segs, n_blocks)
    block_expert = block_expert.reshape(n_blocks)
    pad_ends = pad_ends.reshape(N_EXPERTS)
    xs = _moe_dispatch(h, lpos, ptab, pad_ends, n_blocks)
    ys = _moe_experts(xs, block_expert, pad_ends, expert_w_gate_up[l], expert_b_gate_up[l][:, None, :],
                      expert_w_down[l], expert_b_down[l][:, None, :])
    out = _moe_combine(ys, ptab, lpos.T, topk_w.T, x1, final_norm_g[None])
    return out.reshape(B, S, D)
```

```python
import functools
import math

import numpy as np
import jax
import jax.numpy as jnp
from jax import lax
from jax.experimental import pallas as pl
from jax.experimental.pallas import tpu as pltpu

D_MODEL = 1024
GRID_W = 64
MEM_HEADS = 4
MEM_HEAD_DIM = 128
HEAD_DIM = 64
NA_HEADS = 8
NA_WIN_ROWS = 8
NA_WIN_COLS = 16
DIL_GROUPS = ((128, 1), (512, 4), (2048, 16))
DIL_HEADS_PER_GROUP = 4
DIL_BLOCK = 128
T5_BUCKETS = 32
T5_MAX_DIST = 1024
N_EXPERTS = 32
TOP_K = 4
D_FF = D_MODEL
SWIGLU_ALPHA = 1.702
SWIGLU_LIMIT = 7.0
RMS_EPS = 1e-6
NEG_INF = -1e30

WIDTH_A = NA_HEADS * HEAD_DIM
WIDTH_B = DIL_HEADS_PER_GROUP * len(DIL_GROUPS) * HEAD_DIM
WIDTH_G = DIL_HEADS_PER_GROUP * HEAD_DIM
WIDTH_C = MEM_HEADS * MEM_HEAD_DIM
OFF_QA, OFF_KA, OFF_VA = 0, WIDTH_A, 2 * WIDTH_A
OFF_QB = 3 * WIDTH_A
OFF_KB = OFF_QB + WIDTH_B
OFF_VB = OFF_KB + WIDTH_B
OFF_QC = OFF_VB + WIDTH_B
OFF_GATE = OFF_QC + WIDTH_C
IN_COLS = OFF_GATE + 3 * D_MODEL

LANES = 128
MXU_WIDTH = 256
VMEM_LIMIT = 56 * 1024 * 1024

IN_PROJ_TILE = 1024
PROJ_TILE = 512
NA_ROWS_PER_STEP = 16
DIL_STEP_TOKENS = 2048
MOE_BLOCK = 512
MOE_TILE = 256
MOE_PIECE = 8
MOE_TILE_ROWS = -(-(TOP_K * MOE_TILE + N_EXPERTS * (MOE_PIECE - 1)) // MXU_WIDTH) * MXU_WIDTH
MOE_PIECE_TAB = 256
assert MOE_TILE_ROWS // MOE_PIECE < MOE_PIECE_TAB
MOE_SPARE_BLOCKS = -(-2 * MOE_TILE_ROWS // MOE_BLOCK)


def _cparams(n_axes):
    return pltpu.CompilerParams(
        dimension_semantics=("arbitrary",) * n_axes, vmem_limit_bytes=VMEM_LIMIT)


def _rms(x, g):
    return x * lax.rsqrt(jnp.mean(x * x, axis=-1, keepdims=True) + RMS_EPS) * g


def _sigmoid(x):
    return 1.0 / (1.0 + jnp.exp(-x))


def _pack_bf16_pairs(x):
    w = x.shape[1] // 2
    bits = lax.bitcast_convert_type(x.astype(jnp.bfloat16).astype(jnp.float32), jnp.uint32)
    return (bits[:, w:] & jnp.uint32(0xFFFF0000)) | (bits[:, :w] >> 16)


def _unpack_bf16_pairs(p):
    lo = lax.bitcast_convert_type(p << 16, jnp.float32)
    hi = lax.bitcast_convert_type(p & jnp.uint32(0xFFFF0000), jnp.float32)
    return jnp.concatenate([lo, hi], axis=1).astype(jnp.bfloat16)


def _dot(a, b):
    return jnp.dot(a, b, preferred_element_type=jnp.float32)


def _dot_nt(a, b):
    return lax.dot_general(a, b, (((1,), (1,)), ((), ())), preferred_element_type=jnp.float32)


def _mem_kv_kernel(mem_ref, g_ref, w_ref, out_ref):
    h = _rms(mem_ref[...], g_ref[...]).astype(jnp.bfloat16)
    out_ref[...] = _dot(h, w_ref[...]).astype(jnp.bfloat16)


def _mem_kv(mem2d, g, w):
    rows = mem2d.shape[0]
    return pl.pallas_call(
        _mem_kv_kernel,
        out_shape=jax.ShapeDtypeStruct((rows, 2 * WIDTH_C), jnp.bfloat16),
        name="mem_kv",
        compiler_params=pltpu.CompilerParams(vmem_limit_bytes=VMEM_LIMIT),
    )(mem2d, g, w)


def _in_proj_kernel(x_ref, g_ref, w_ref, na_ref, d0_ref, d1_ref, d2_ref, qc_ref, acc_ref):
    T = x_ref.shape[0]
    h = _rms(x_ref[...], g_ref[...]).astype(jnp.bfloat16)
    q_scale = HEAD_DIM ** -0.5

    for c in range(3):
        acc = _dot(h, w_ref[:, c * WIDTH_A:(c + 1) * WIDTH_A])
        if c == 0:
            acc = acc * q_scale
        na_ref[:, c * WIDTH_A:(c + 1) * WIDTH_A] = acc.astype(jnp.bfloat16)

    for gi, ((_, dil), d_ref) in enumerate(zip(DIL_GROUPS, (d0_ref, d1_ref, d2_ref))):
        for kind, off in enumerate((OFF_QB, OFF_KB, OFF_VB)):
            lo = off + gi * WIDTH_G
            acc = _dot(h, w_ref[:, lo:lo + WIDTH_G])
            if kind == 0:
                acc = acc * q_scale
            cols = slice(kind * WIDTH_G, (kind + 1) * WIDTH_G)
            if dil == 1:
                d_ref[0, 0, :, cols] = acc.astype(jnp.bfloat16)
            else:
                for c in range(WIDTH_G // LANES):
                    acc_ref[c] = acc[:, c * LANES:(c + 1) * LANES]
                for rho in range(dil):
                    d_ref[0, rho, :, cols] = jnp.concatenate(
                        [acc_ref[c, pl.ds(rho, T // dil, stride=dil), :] for c in range(WIDTH_G // LANES)],
                        axis=-1).astype(jnp.bfloat16)

    qc_ref[...] = _dot(h, w_ref[:, OFF_QC:OFF_QC + WIDTH_C]).astype(jnp.bfloat16)


def _gate_proj_kernel(x_ref, g_ref, w_ref, gate_ref):
    h = _rms(x_ref[...], g_ref[...]).astype(jnp.bfloat16)
    chunk = 512
    first = OFF_GATE - IN_COLS // 2
    for c in range(3 * D_MODEL // chunk):
        lo = first + c * chunk
        gate_ref[:, c * chunk:(c + 1) * chunk] = _sigmoid(_dot(h, w_ref[:, lo:lo + chunk])).astype(jnp.bfloat16)


def _gate_proj(x2d, g, w_bf16):
    n = x2d.shape[0]
    T = IN_PROJ_TILE
    assert (IN_COLS // 2) % LANES == 0 and IN_COLS // 2 <= OFF_GATE
    return pl.pallas_call(
        _gate_proj_kernel,
        grid=(n // T,),
        in_specs=[pl.BlockSpec((T, D_MODEL), lambda i: (i, 0)),
                  pl.BlockSpec((1, D_MODEL), lambda i: (0, 0)),
                  pl.BlockSpec((D_MODEL, IN_COLS // 2), lambda i: (0, 1), pipeline_mode=pl.Buffered(1))],
        out_specs=pl.BlockSpec((T, 3 * D_MODEL), lambda i: (i, 0)),
        out_shape=jax.ShapeDtypeStruct((n, 3 * D_MODEL), jnp.bfloat16),
        compiler_params=_cparams(1),
        name="gate_proj",
    )(x2d, g, w_bf16)


def _in_proj(x2d, g, w_bf16, batch, seq):
    n = x2d.shape[0]
    T = IN_PROJ_TILE
    tiles_per_batch = seq // T
    out_shape = [jax.ShapeDtypeStruct((n, 3 * WIDTH_A), jnp.bfloat16)]
    out_specs = [pl.BlockSpec((T, 3 * WIDTH_A), lambda i: (i, 0))]
    for _, dil in DIL_GROUPS:
        out_shape.append(jax.ShapeDtypeStruct((batch, dil, seq // dil, 3 * WIDTH_G), jnp.bfloat16))
        out_specs.append(pl.BlockSpec((1, dil, T // dil, 3 * WIDTH_G),
                                      lambda i: (i // tiles_per_batch, 0, i % tiles_per_batch, 0)))
    out_shape += [jax.ShapeDtypeStruct((n, WIDTH_C), jnp.bfloat16)]
    out_specs += [pl.BlockSpec((T, WIDTH_C), lambda i: (i, 0))]
    return pl.pallas_call(
        _in_proj_kernel,
        grid=(n // T,),
        in_specs=[pl.BlockSpec((T, D_MODEL), lambda i: (i, 0)),
                  pl.BlockSpec((1, D_MODEL), lambda i: (0, 0)),
                  pl.BlockSpec((D_MODEL, OFF_GATE), lambda i: (0, 0), pipeline_mode=pl.Buffered(1))],
        out_specs=out_specs,
        out_shape=out_shape,
        scratch_shapes=[pltpu.VMEM((WIDTH_G // LANES, T, LANES), jnp.float32)],
        compiler_params=_cparams(1),
        name="in_proj",
    )(x2d, g, w_bf16)


def _na_kernel(q_ref, k_ref, v_ref, bias_ref, out_ref, *, rows):
    i = pl.program_id(1)
    lane = lax.broadcasted_iota(jnp.int32, (GRID_W, LANES), 1)
    low_half = lane < HEAD_DIM
    n_win = NA_WIN_ROWS * GRID_W

    def row_body(j, carry):
        qr = i * NA_ROWS_PER_STEP + j
        rs = jnp.clip(qr - NA_WIN_ROWS // 2, 0, rows - NA_WIN_ROWS)
        shift = rs - qr + NA_WIN_ROWS - 1
        q_off = pl.multiple_of(j * GRID_W, GRID_W)
        k_off = pl.multiple_of(rs * GRID_W, GRID_W)
        scores = []
        for hp in range(NA_HEADS // 2):
            cols = slice(hp * LANES, (hp + 1) * LANES)
            qp = q_ref[pl.ds(q_off, GRID_W), cols]
            kp = k_ref[pl.ds(k_off, n_win), cols]
            zero = jnp.zeros_like(qp)
            q2 = jnp.concatenate([jnp.where(low_half, qp, zero), jnp.where(low_half, zero, qp)], axis=0)
            bias = jnp.concatenate(
                [jnp.concatenate([bias_ref[shift + 2 * p, 2 * hp + hh] for p in range(NA_WIN_ROWS // 2)], axis=-1)
                 for hh in range(2)], axis=0)
            scores.append(_dot_nt(q2, kp) + bias)
        probs = []
        for s in scores:
            e = jnp.exp(s - jnp.max(s, axis=-1, keepdims=True))
            probs.append((e.astype(jnp.bfloat16), jnp.sum(e, axis=-1, keepdims=True)))
        for hp, (p, l) in enumerate(probs):
            cols = slice(hp * LANES, (hp + 1) * LANES)
            o2 = _dot(p, v_ref[pl.ds(k_off, n_win), cols]) / l
            out_ref[pl.ds(q_off, GRID_W), cols] = jnp.where(
                low_half, o2[:GRID_W], o2[GRID_W:]).astype(jnp.bfloat16)
        return carry

    lax.fori_loop(0, NA_ROWS_PER_STEP, row_body, 0, unroll=4)


def _na_attn(na_qkv, bias_tab, batch, seq):
    rows = seq // GRID_W
    steps = rows // NA_ROWS_PER_STEP
    tq = NA_ROWS_PER_STEP * GRID_W
    n = na_qkv.shape[0]
    return pl.pallas_call(
        functools.partial(_na_kernel, rows=rows),
        grid=(batch, steps),
        in_specs=[pl.BlockSpec((tq, WIDTH_A), lambda b, i: (b * steps + i, 0)),
                  pl.BlockSpec((seq, WIDTH_A), lambda b, i: (b, 1)),
                  pl.BlockSpec((seq, WIDTH_A), lambda b, i: (b, 2)),
                  pl.BlockSpec(bias_tab.shape, lambda b, i: (0, 0, 0, 0))],
        out_specs=pl.BlockSpec((tq, WIDTH_A), lambda b, i: (b * steps + i, 0)),
        out_shape=jax.ShapeDtypeStruct((n, WIDTH_A), jnp.bfloat16),
        compiler_params=_cparams(2),
        name="na_attn",
    )(na_qkv, na_qkv, na_qkv, bias_tab)


def _t5_bucket_np(rel):
    half = T5_BUCKETS // 2
    max_exact = half // 2
    ret = np.where(rel > 0, half, 0)
    n = np.abs(rel)
    nf = np.maximum(n, 1).astype(np.float32)
    large = max_exact + (np.log(nf / np.float32(max_exact)) / np.float32(math.log(T5_MAX_DIST / max_exact))
                         * np.float32(half - max_exact)).astype(np.int32)
    large = np.minimum(large, half - 1)
    return ret + np.where(n < max_exact, n, large)


def _dil_bucket_table():
    q = np.arange(DIL_BLOCK)[:, None]
    j = np.arange(2 * DIL_BLOCK)[None, :]
    tabs = []
    for window, dil in DIL_GROUPS:
        half = (window // 2) // dil
        assert half == DIL_BLOCK // 2
        rel = (j - half) - q
        tabs.append(np.where(np.abs(rel) <= half, _t5_bucket_np(rel * dil), -1))
    return np.stack(tabs).astype(np.int32)


def _bias_prep_kernel(rpb_ref, t5_ref, bucket_ref, na_ref, dil_ref):
    n_dc = 2 * NA_WIN_COLS - 1
    n_dr = 2 * NA_WIN_ROWS - 1
    qc = lax.broadcasted_iota(jnp.int32, (GRID_W, 2 * GRID_W), 0)
    lane = lax.broadcasted_iota(jnp.int32, (GRID_W, 2 * GRID_W), 1)
    second = lane >= GRID_W
    kc = jnp.where(second, lane - GRID_W, lane)
    diff = kc - qc
    cs = jnp.clip(qc - NA_WIN_COLS // 2, 0, GRID_W - NA_WIN_COLS)
    valid = jnp.logical_and(kc >= cs, kc < cs + NA_WIN_COLS)

    def na_tile(mh, carry):
        m = mh // NA_HEADS
        h = mh - m * NA_HEADS
        base = (h * n_dr + m) * n_dc
        acc = jnp.zeros((GRID_W, 2 * GRID_W), jnp.float32)
        for d in range(n_dc):
            v = jnp.where(second, rpb_ref[base + n_dc + d], rpb_ref[base + d])
            acc = jnp.where(diff == d - (NA_WIN_COLS - 1), v, acc)
        na_ref[m, h] = jnp.where(valid, acc, NEG_INF)
        return carry

    lax.fori_loop(0, (n_dr - 1) * NA_HEADS, na_tile, 0)

    j = lax.broadcasted_iota(jnp.int32, (DIL_BLOCK, 2 * DIL_BLOCK), 1)
    half = DIL_BLOCK // 2
    n_heads = DIL_HEADS_PER_GROUP * len(DIL_GROUPS)
    for gi in range(len(DIL_GROUPS)):
        bucket = bucket_ref[gi]
        in_win = bucket >= 0
        for h in range(DIL_HEADS_PER_GROUP):
            acc = jnp.zeros((DIL_BLOCK, 2 * DIL_BLOCK), jnp.float32)
            for b in range(T5_BUCKETS):
                acc = jnp.where(bucket == b, t5_ref[b * n_heads + gi * DIL_HEADS_PER_GROUP + h], acc)
            for v, ok in enumerate((jnp.logical_and(in_win, j >= half), in_win,
                                    jnp.logical_and(in_win, j < 2 * DIL_BLOCK - half))):
                dil_ref[gi, v, h] = jnp.where(ok, acc, NEG_INF)


def _bias_prep(rpb, t5):
    smem = pl.BlockSpec(memory_space=pltpu.SMEM)
    return pl.pallas_call(
        _bias_prep_kernel,
        in_specs=[smem, smem, pl.BlockSpec(memory_space=pltpu.VMEM)],
        out_shape=[jax.ShapeDtypeStruct((2 * NA_WIN_ROWS - 2, NA_HEADS, GRID_W, 2 * GRID_W), jnp.float32),
                   jax.ShapeDtypeStruct((len(DIL_GROUPS), 3, DIL_HEADS_PER_GROUP, DIL_BLOCK, 2 * DIL_BLOCK),
                                        jnp.float32)],
        compiler_params=pltpu.CompilerParams(vmem_limit_bytes=VMEM_LIMIT),
        name="bias_prep",
    )(rpb.reshape(-1), t5.reshape(-1), jnp.asarray(_dil_bucket_table()))


def _dil_kernel(q_ref, kp_ref, kc_ref, kn_ref, vp_ref, vc_ref, vn_ref, bias_ref, o_ref, lse_ref, *, dil):
    lane = lax.broadcasted_iota(jnp.int32, (DIL_BLOCK, LANES), 1)
    low_half = lane < HEAD_DIM
    hb = DIL_BLOCK // 2

    span = pl.program_id(1)
    n_spans = pl.num_programs(1)
    nbs = kc_ref.shape[2] // DIL_BLOCK

    def window(prev_ref, cur_ref, next_ref, rho, blk, cols):
        lo = blk * DIL_BLOCK
        first = prev_ref[0, rho, hb:, cols] if blk == 0 else cur_ref[0, rho, lo - hb:lo, cols]
        last = (next_ref[0, rho, :hb, cols] if blk == nbs - 1
                else cur_ref[0, rho, lo + DIL_BLOCK:lo + DIL_BLOCK + hb, cols])
        return jnp.concatenate([first, cur_ref[0, rho, lo:lo + DIL_BLOCK, cols], last], axis=0)

    def unit(rho, blk):
        variant = 1
        if blk == nbs - 1:
            variant = jnp.where(span == n_spans - 1, 2, variant)
        if blk == 0:
            variant = jnp.where(span == 0, 0, variant)
        scores = []
        for hp in range(DIL_HEADS_PER_GROUP // 2):
            cols = slice(hp * LANES, (hp + 1) * LANES)
            qp = q_ref[0, rho, blk * DIL_BLOCK:(blk + 1) * DIL_BLOCK, cols]
            zero = jnp.zeros_like(qp)
            q2 = jnp.concatenate([jnp.where(low_half, qp, zero), jnp.where(low_half, zero, qp)], axis=0)
            kw = window(kp_ref, kc_ref, kn_ref, rho, blk, cols)
            bias = jnp.concatenate([bias_ref[variant, 2 * hp], bias_ref[variant, 2 * hp + 1]], axis=0)
            scores.append(_dot_nt(q2, kw) + bias)
        probs = []
        for s in scores:
            m = jnp.max(s, axis=-1, keepdims=True)
            e = jnp.exp(s - m)
            probs.append((e.astype(jnp.bfloat16), m, jnp.sum(e, axis=-1, keepdims=True)))
        for hp, (p, m, l) in enumerate(probs):
            cols = slice(hp * LANES, (hp + 1) * LANES)
            o2 = _dot(p, window(vp_ref, vc_ref, vn_ref, rho, blk, cols)) / l
            lse2 = jnp.broadcast_to(m + jnp.log(l), (2 * DIL_BLOCK, LANES))
            o_pair = jnp.where(low_half, o2[:DIL_BLOCK], o2[DIL_BLOCK:])
            lse_pair = jnp.where(low_half, lse2[:DIL_BLOCK], lse2[DIL_BLOCK:])
            if dil == 1:
                rows = slice(blk * DIL_BLOCK, (blk + 1) * DIL_BLOCK)
                o_ref[hp, rows, :] = o_pair
                lse_ref[hp, rows, :] = lse_pair
            else:
                rows = pl.ds(blk * DIL_BLOCK * dil + rho, DIL_BLOCK, stride=dil)
                o_ref[hp, rows, :] = o_pair
                lse_ref[hp, rows, :] = lse_pair

    def class_body(rho, carry):
        for blk in range(nbs):
            unit(rho, blk)
        return carry

    if dil == 1:
        class_body(0, 0)
    else:
        lax.fori_loop(0, dil, class_body, 0, unroll=4 if nbs == 1 else 1)


def _dil_attn(qkv, bias_tab, group, dil, batch, seq):
    L = seq // dil
    nb = L // DIL_BLOCK
    assert nb >= 2
    tq = DIL_STEP_TOKENS
    span = tq // dil
    nbs = span // DIL_BLOCK
    n_spans = L // span

    def cur(col):
        return pl.BlockSpec((1, dil, span, WIDTH_G), lambda b, n: (b, 0, n, col))

    def edge(col, shift):
        return pl.BlockSpec((1, dil, DIL_BLOCK, WIDTH_G),
                            lambda b, n: (b, 0, jnp.clip(n * nbs + (shift if shift < 0 else nbs), 0, nb - 1), col))

    n_pairs = WIDTH_G // LANES
    out_spec = pl.BlockSpec((n_pairs, tq, LANES), lambda b, n: (0, b * n_spans + n, 0))
    return pl.pallas_call(
        functools.partial(_dil_kernel, dil=dil),
        grid=(batch, n_spans),
        in_specs=[cur(0), edge(1, -1), cur(1), edge(1, 1), edge(2, -1), cur(2), edge(2, 1),
                  pl.BlockSpec((None, 3, DIL_HEADS_PER_GROUP, DIL_BLOCK, 2 * DIL_BLOCK),
                               lambda b, n: (group, 0, 0, 0, 0))],
        out_specs=[out_spec, out_spec],
        out_shape=[jax.ShapeDtypeStruct((n_pairs, batch * seq, LANES), jnp.float32)] * 2,
        compiler_params=_cparams(2),
        name=f"dil_attn_d{dil}",
    )(qkv, qkv, qkv, qkv, qkv, qkv, qkv, bias_tab)


def _mix_kernel(x_ref, na_ref, o0_ref, l0_ref, o1_ref, l1_ref, o2_ref, l2_ref, qc_ref, gate_ref,
                mkv_ref, wa_ref, wb_ref, wc_ref, wo_ref, fg_ref, rw_ref, rb_ref,
                x1_ref, h_ref, idx_ref, wt_ref, seg_ref):
    T = x_ref.shape[0]
    ob = []
    for hp in range(WIDTH_G // LANES):
        l0, l1, l2 = l0_ref[hp], l1_ref[hp], l2_ref[hp]
        m = jnp.maximum(jnp.maximum(l0, l1), l2)
        e0, e1, e2 = jnp.exp(l0 - m), jnp.exp(l1 - m), jnp.exp(l2 - m)
        ob.append((e0 * o0_ref[hp] + e1 * o1_ref[hp] + e2 * o2_ref[hp]) / (e0 + e1 + e2))
    ob = jnp.concatenate(ob, axis=-1)

    mem_scale = MEM_HEAD_DIM ** -0.5
    scores = []
    for h in range(MEM_HEADS):
        cols = slice(h * MEM_HEAD_DIM, (h + 1) * MEM_HEAD_DIM)
        scores.append(_dot_nt(qc_ref[:, cols], mkv_ref[:, cols]) * mem_scale)
    probs = []
    for s in scores:
        e = jnp.exp(s - jnp.max(s, axis=-1, keepdims=True))
        probs.append((e.astype(jnp.bfloat16), jnp.sum(e, axis=-1, keepdims=True)))
    oc = []
    for h, (p, l) in enumerate(probs):
        oc.append(_dot(p, mkv_ref[:, WIDTH_C + h * MEM_HEAD_DIM:WIDTH_C + (h + 1) * MEM_HEAD_DIM]) / l)
    oc = jnp.concatenate(oc, axis=-1).astype(jnp.bfloat16)

    y_a = _dot(na_ref[...], wa_ref[...])
    y_b = _dot(ob.astype(jnp.bfloat16), wb_ref[...])
    y_c = _dot(oc, wc_ref[...])
    merged = (gate_ref[:, 0:D_MODEL].astype(jnp.float32) * y_a
              + gate_ref[:, D_MODEL:2 * D_MODEL].astype(jnp.float32) * y_b
              + gate_ref[:, 2 * D_MODEL:3 * D_MODEL].astype(jnp.float32) * y_c)
    x1 = x_ref[...] + _dot(merged.astype(jnp.bfloat16), wo_ref[...])
    x1_ref[...] = x1

    h = _rms(x1, fg_ref[...])
    h_ref[...] = h.astype(jnp.bfloat16)
    logits = lax.dot_general(rw_ref[...], h, (((1,), (1,)), ((), ())), preferred_element_type=jnp.float32,
                             precision=lax.Precision.HIGHEST) + rb_ref[...]
    expert = lax.broadcasted_iota(jnp.int32, (N_EXPERTS, T), 0)
    vals, idxs = [], []
    for _ in range(TOP_K):
        mx = jnp.max(logits, axis=0, keepdims=True)
        sel = jnp.min(jnp.where(logits == mx, expert, N_EXPERTS), axis=0, keepdims=True)
        vals.append(mx)
        idxs.append(sel)
        logits = jnp.where(expert == sel, -jnp.inf, logits)
    ev = [jnp.exp(v - vals[0]) for v in vals]
    den = ev[0] + ev[1] + ev[2] + ev[3]
    idx_ref[...] = jnp.concatenate(idxs, axis=0)
    wt_ref[...] = jnp.concatenate([e / den for e in ev], axis=0)
    member = sum((expert == sel).astype(jnp.float32) for sel in idxs)
    for s in range(T // MOE_TILE):
        cnt = jnp.sum(member[:, s * MOE_TILE:(s + 1) * MOE_TILE], axis=1, keepdims=True)
        seg = jnp.floor((cnt + (MOE_PIECE - 1)) * (1.0 / MOE_PIECE)) * MOE_PIECE
        seg_ref[s] = jnp.broadcast_to(seg, (N_EXPERTS, LANES))


def _mix_out(x2d, na_out, dil_outs, qc, gates, mkv, wa, wb, wc, wo, ffn_g, router_w, router_b, batch, seq):
    n = x2d.shape[0]
    T = PROJ_TILE
    tiles_per_batch = seq // T
    mem_len = mkv.shape[0] // batch
    row = lambda w: pl.BlockSpec((T, w), lambda i: (i, 0))
    full = lambda a: pl.BlockSpec(a.shape, lambda i: (0,) * a.ndim, pipeline_mode=pl.Buffered(1))
    in_specs = [row(D_MODEL), row(WIDTH_A)]
    args = [x2d, na_out]
    for o, l in dil_outs:
        pair_rows = pl.BlockSpec((WIDTH_G // LANES, T, LANES), lambda i: (0, i, 0))
        in_specs += [pair_rows, pair_rows]
        args += [o, l]
    in_specs += [row(WIDTH_C), row(3 * D_MODEL),
                 pl.BlockSpec((mem_len, 2 * WIDTH_C), lambda i: (i // tiles_per_batch, 0)),
                 full(wa), full(wb), full(wc), full(wo), full(ffn_g), full(router_w), full(router_b)]
    args += [qc, gates, mkv, wa, wb, wc, wo, ffn_g, router_w, router_b]
    return pl.pallas_call(
        _mix_kernel,
        grid=(n // T,),
        in_specs=in_specs,
        out_specs=[row(D_MODEL), row(D_MODEL), pl.BlockSpec((TOP_K, T), lambda i: (0, i)),
                   pl.BlockSpec((TOP_K, T), lambda i: (0, i)),
                   pl.BlockSpec((T // MOE_TILE, N_EXPERTS, LANES), lambda i: (i, 0, 0))],
        out_shape=[jax.ShapeDtypeStruct((n, D_MODEL), jnp.float32),
                   jax.ShapeDtypeStruct((n, D_MODEL), jnp.bfloat16),
                   jax.ShapeDtypeStruct((TOP_K, n), jnp.int32),
                   jax.ShapeDtypeStruct((TOP_K, n), jnp.float32),
                   jax.ShapeDtypeStruct((n // MOE_TILE, N_EXPERTS, LANES), jnp.float32)],
        compiler_params=_cparams(1),
        name="mix_out",
    )(*args)


def _route_kernel(idx_ref, segs_ref, lpos_ref, ptab_ref, be_ref, pe_ref, carry_ref, pstart_ref):
    i = pl.program_id(0)
    T = idx_ref.shape[1]
    n_blocks = be_ref.shape[1]
    expert = lax.broadcasted_iota(jnp.int32, (N_EXPERTS, T), 0)
    idx = idx_ref[...]
    onehots = [expert == idx[k:k + 1, :] for k in range(TOP_K)]
    member = sum(o.astype(jnp.float32) for o in onehots)
    seg_b = segs_ref[i]
    seg = seg_b[:, 0:1]
    r = lax.broadcasted_iota(jnp.int32, (N_EXPERTS, N_EXPERTS), 0)
    c = lax.broadcasted_iota(jnp.int32, (N_EXPERTS, N_EXPERTS), 1)
    lower_incl = (r >= c).astype(jnp.float32)

    def cumsum_experts(v):
        return jnp.dot(lower_incl, v, preferred_element_type=jnp.float32, precision=lax.Precision.HIGHEST)

    @pl.when(i == 0)
    def _():
        counts = jnp.sum(segs_ref[...], axis=0)
        padded = jnp.floor((counts + (MOE_BLOCK - 1)) * (1.0 / MOE_BLOCK)) * MOE_BLOCK
        pad_ends = cumsum_experts(padded)
        pstart_ref[...] = pad_ends - padded
        carry_ref[...] = jnp.zeros_like(carry_ref)
        pe_ref[...] = pad_ends[:, 0:1].astype(jnp.int32)
        blk_start = (lax.broadcasted_iota(jnp.int32, (N_EXPERTS, n_blocks), 1) * MOE_BLOCK).astype(jnp.float32)
        be = jnp.sum((pad_ends[:, 0:1] <= blk_start).astype(jnp.int32), axis=0, keepdims=True)
        be_ref[...] = jnp.minimum(be, N_EXPERTS - 1)

    local_end = cumsum_experts(seg_b)[:, 0:1]
    local_start = local_end - seg
    global_start = pstart_ref[:, 0:1] + carry_ref[:, 0:1]
    rr = lax.broadcasted_iota(jnp.int32, (T, T), 0)
    cc = lax.broadcasted_iota(jnp.int32, (T, T), 1)
    earlier = (rr < cc).astype(jnp.bfloat16)
    before = _dot(member.astype(jnp.bfloat16), earlier)
    base = local_start + before
    lpos = [jnp.sum(jnp.where(o, base, 0.0), axis=0, keepdims=True) for o in onehots]
    lpos_ref[...] = jnp.concatenate(lpos, axis=0).astype(jnp.int32)

    inv = 1.0 / MOE_PIECE
    piece = lax.broadcasted_iota(jnp.int32, (N_EXPERTS, MOE_PIECE_TAB), 1).astype(jnp.float32)
    first = local_start * inv
    inside = jnp.logical_and(piece >= first, piece < local_end * inv)
    dst = jnp.sum(jnp.where(inside, global_start * inv + piece - first, 0.0), axis=0, keepdims=True)
    n_pieces = local_end[N_EXPERTS - 1:N_EXPERTS, :] * inv
    spare = (n_blocks * MOE_BLOCK // MOE_PIECE
             + (i % 2) * (MOE_TILE_ROWS // MOE_PIECE)).astype(jnp.float32)
    dst = jnp.where(piece[0:1, :] >= n_pieces, spare + piece[0:1, :], dst)
    dst = jnp.where(piece[0:1, :] == MOE_PIECE_TAB - 1, n_pieces, dst)
    ptab_ref[...] = dst.astype(jnp.int32)
    carry_ref[...] += seg


def _moe_route(topk_idx, segs, n_blocks):
    n = topk_idx.shape[1]
    T = MOE_TILE
    acc = pltpu.VMEM((N_EXPERTS, LANES), jnp.float32)
    return pl.pallas_call(
        _route_kernel,
        grid=(n // T,),
        in_specs=[pl.BlockSpec((TOP_K, T), lambda i: (0, i)),
                  pl.BlockSpec(segs.shape, lambda i: (0, 0, 0))],
        out_specs=[pl.BlockSpec((TOP_K, T), lambda i: (0, i)),
                   pl.BlockSpec((None, 1, MOE_PIECE_TAB), lambda i: (i, 0, 0)),
                   pl.BlockSpec((1, n_blocks), lambda i: (0, 0)),
                   pl.BlockSpec((N_EXPERTS, 1), lambda i: (0, 0))],
        out_shape=[jax.ShapeDtypeStruct((TOP_K, n), jnp.int32),
                   jax.ShapeDtypeStruct((n // T, 1, MOE_PIECE_TAB), jnp.int32),
                   jax.ShapeDtypeStruct((1, n_blocks), jnp.int32),
                   jax.ShapeDtypeStruct((N_EXPERTS, 1), jnp.int32)],
        scratch_shapes=[acc, acc],
        compiler_params=_cparams(1),
        name="moe_route",
    )(topk_idx, segs)


def _piece(ref, p):
    start = p * MOE_PIECE if isinstance(p, int) else pl.multiple_of(p * MOE_PIECE, MOE_PIECE)
    return ref.at[pl.ds(start, MOE_PIECE), :]


def _dispatch_kernel(pe_ref, ptab_ref, lpos_ref, h_ref, xs_hbm, zbuf, pbuf, sem, zsem):
    i = pl.program_id(0)
    T = h_ref.shape[0]
    half = h_ref.shape[1] // 2

    def zero_copy(e):
        end = pe_ref[e]
        start = pl.multiple_of(end - MOE_BLOCK, MOE_BLOCK)
        return pltpu.make_async_copy(zbuf, xs_hbm.at[pl.ds(start, MOE_BLOCK), :], zsem)

    def nonempty(e):
        return pe_ref[e] > (pe_ref[e - 1] if e else 0)

    @pl.when(i == 0)
    def _():
        zbuf[...] = jnp.zeros_like(zbuf)
        for e in range(N_EXPERTS):
            @pl.when(nonempty(e))
            def _():
                zero_copy(e).start()
        for e in range(N_EXPERTS):
            @pl.when(nonempty(e))
            def _():
                zero_copy(e).wait()

        def tail_copy(b):
            return pltpu.make_async_copy(
                zbuf, xs_hbm.at[pl.ds(pl.multiple_of(b * MOE_BLOCK, MOE_BLOCK), MOE_BLOCK), :], zsem)
        first_tail = pe_ref[N_EXPERTS - 1] // MOE_BLOCK
        n_blocks = xs_hbm.shape[0] // MOE_BLOCK
        lax.fori_loop(first_tail, n_blocks, lambda b, c: (tail_copy(b).start(), c)[1], 0)
        lax.fori_loop(first_tail, n_blocks, lambda b, c: (tail_copy(b).wait(), c)[1], 0)

    row = lax.broadcasted_iota(jnp.int32, (MOE_TILE_ROWS, T), 0)
    lpos = lpos_ref[...]
    perm = sum((row == lpos[k:k + 1, :]).astype(jnp.bfloat16) for k in range(TOP_K))
    lo = lax.bitcast_convert_type(_dot(perm, h_ref[:, :half]), jnp.uint32)
    hi = lax.bitcast_convert_type(_dot(perm, h_ref[:, half:]), jnp.uint32)
    slot = i % 2
    pbuf[slot] = hi | (lo >> 16)

    n_tile_pieces = MOE_TILE_ROWS // MOE_PIECE

    def piece_copy(s, p):
        return pltpu.make_async_copy(_piece(pbuf.at[s], p), _piece(xs_hbm, ptab_ref[p]), sem.at[s])

    def wait_all(s):
        lax.fori_loop(0, n_tile_pieces, lambda p, c: (piece_copy(s, 0).wait(), c)[1], 0, unroll=8)

    for p in range(n_tile_pieces):
        piece_copy(slot, p).start()

    @pl.when(i > 0)
    def _():
        wait_all(1 - slot)

    @pl.when(i == pl.num_programs(0) - 1)
    def _():
        wait_all(slot)


def _moe_dispatch(h, lpos, ptab, pad_ends, n_blocks):
    n, width = h.shape
    T = MOE_TILE
    n_steps = n // T

    def kernel(pe_ref, ptab_ref, *rest):
        _dispatch_kernel(pe_ref, ptab_ref.at[0], *rest)

    grid_spec = pltpu.PrefetchScalarGridSpec(
        num_scalar_prefetch=1,
        grid=(n_steps,),
        in_specs=[pl.BlockSpec((None, 1, MOE_PIECE_TAB), lambda i, pe: (i, 0, 0), memory_space=pltpu.SMEM),
                  pl.BlockSpec((TOP_K, T), lambda i, pe: (0, i)),
                  pl.BlockSpec((T, width), lambda i, pe: (i, 0))],
        out_specs=pl.BlockSpec(memory_space=pl.ANY),
        scratch_shapes=[pltpu.VMEM((MOE_BLOCK, width // 2), jnp.uint32),
                        pltpu.VMEM((2, MOE_TILE_ROWS, width // 2), jnp.uint32),
                        pltpu.SemaphoreType.DMA((2,)),
                        pltpu.SemaphoreType.DMA(())],
    )
    return pl.pallas_call(
        kernel,
        grid_spec=grid_spec,
        out_shape=jax.ShapeDtypeStruct(((n_blocks + MOE_SPARE_BLOCKS) * MOE_BLOCK, width // 2), jnp.uint32),
        compiler_params=_cparams(1),
        name="moe_dispatch",
    )(pad_ends, ptab, lpos, h)


def _moe_kernel(be_ref, pe_ref, xs_ref, wgu_ref, bgu_ref, wdn_ref, bdn_ref, y_ref):
    del be_ref
    i = pl.program_id(0)
    n_valid = pe_ref[N_EXPERTS - 1] // MOE_BLOCK

    @pl.when(i < n_valid)
    def _():
        x = _unpack_bf16_pairs(xs_ref[...])
        gu = _dot(x, wgu_ref[0]) + bgu_ref[0]
        g = jnp.minimum(gu[:, :D_FF], SWIGLU_LIMIT)
        u = jnp.clip(gu[:, D_FF:], -SWIGLU_LIMIT, SWIGLU_LIMIT)
        a = (u + 1.0) * (g * _sigmoid(SWIGLU_ALPHA * g))
        y_ref[...] = _pack_bf16_pairs(_dot(a.astype(jnp.bfloat16), wdn_ref[0]) + bdn_ref[0])

    @pl.when(i >= n_valid)
    def _():
        y_ref[...] = jnp.zeros_like(y_ref)


def _moe_experts(xs, block_expert, pad_ends, wgu, bgu, wdn, bdn):
    n_blocks = block_expert.shape[0]

    def row_blk(i, be, pe):
        return (jnp.minimum(i, jnp.maximum(pe[N_EXPERTS - 1] // MOE_BLOCK - 1, 0)), 0)

    grid_spec = pltpu.PrefetchScalarGridSpec(
        num_scalar_prefetch=2,
        grid=(n_blocks,),
        in_specs=[pl.BlockSpec((MOE_BLOCK, D_MODEL // 2), row_blk),
                  pl.BlockSpec((1, D_MODEL, 2 * D_FF), lambda i, be, pe: (be[i], 0, 0)),
                  pl.BlockSpec((1, 1, 2 * D_FF), lambda i, be, pe: (be[i], 0, 0)),
                  pl.BlockSpec((1, D_FF, D_MODEL), lambda i, be, pe: (be[i], 0, 0)),
                  pl.BlockSpec((1, 1, D_MODEL), lambda i, be, pe: (be[i], 0, 0))],
        out_specs=pl.BlockSpec((MOE_BLOCK, D_MODEL // 2), lambda i, be, pe: (i, 0)),
    )
    return pl.pallas_call(
        _moe_kernel,
        grid_spec=grid_spec,
        out_shape=jax.ShapeDtypeStruct((n_blocks * MOE_BLOCK, D_MODEL // 2), jnp.uint32),
        compiler_params=_cparams(1),
        name="moe_experts",
    )(block_expert, pad_ends, xs, wgu, bgu, wdn, bdn)


def _gather_pieces(src_hbm, ptab_ref, dst_ref, sem):
    spare = src_hbm.shape[0] // MOE_PIECE

    for p in range(MOE_TILE_ROWS // MOE_PIECE):
        src = ptab_ref[p]
        src = jnp.where(src >= spare, 0, src)
        pltpu.make_async_copy(_piece(src_hbm, src), _piece(dst_ref, p), sem).start()


def _wait_pieces(src_hbm, dst_ref, sem):
    def body(p, carry):
        pltpu.make_async_copy(_piece(src_hbm, 0), _piece(dst_ref, 0), sem).wait()
        return carry
    lax.fori_loop(0, MOE_TILE_ROWS // MOE_PIECE, body, 0, unroll=8)


def _combine_kernel(ptab_ref, ptab_next_ref, ys_hbm, lpos_ref, wt_ref, x1_ref, g_ref, out_ref, gbuf, sem):
    i = pl.program_id(0)
    n_steps = pl.num_programs(0)
    slot = i % 2
    T = x1_ref.shape[0]

    @pl.when(i == 0)
    def _():
        _gather_pieces(ys_hbm, ptab_ref, gbuf.at[0], sem.at[0])

    @pl.when(i + 1 < n_steps)
    def _():
        _gather_pieces(ys_hbm, ptab_next_ref, gbuf.at[1 - slot], sem.at[1 - slot])

    _wait_pieces(ys_hbm, gbuf.at[slot], sem.at[slot])
    rows = _unpack_bf16_pairs(gbuf[slot])
    col = lax.broadcasted_iota(jnp.int32, (T, MOE_TILE_ROWS), 1)
    lpos = lpos_ref[...]
    wt = wt_ref[...]
    cmb = jnp.zeros((T, MOE_TILE_ROWS), jnp.float32)
    for k in range(TOP_K):
        cmb = jnp.where(col == lpos[:, k:k + 1], wt[:, k:k + 1], cmb)
    hi = cmb.astype(jnp.bfloat16)
    lo = (cmb - hi.astype(jnp.float32)).astype(jnp.bfloat16)
    y2 = _dot(jnp.concatenate([hi, lo], axis=0), rows)
    y = x1_ref[...] + y2[:T] + y2[T:]
    out_ref[...] = _rms(y, g_ref[...])


def _moe_combine(ys, ptab, lpos_t, wts, x1, final_g):
    n = x1.shape[0]
    T = MOE_TILE
    n_steps = n // T
    smem_blk = lambda shift: pl.BlockSpec(
        (None, 1, MOE_PIECE_TAB), lambda i: (jnp.minimum(i + shift, n_steps - 1), 0, 0),
        memory_space=pltpu.SMEM)

    def kernel(ptab_ref, ptab_next_ref, *rest):
        _combine_kernel(ptab_ref.at[0], ptab_next_ref.at[0], *rest)

    return pl.pallas_call(
        kernel,
        grid=(n_steps,),
        in_specs=[smem_blk(0), smem_blk(1),
                  pl.BlockSpec(memory_space=pl.ANY),
                  pl.BlockSpec((T, TOP_K), lambda i: (i, 0)),
                  pl.BlockSpec((T, TOP_K), lambda i: (i, 0)),
                  pl.BlockSpec((T, D_MODEL), lambda i: (i, 0)),
                  pl.BlockSpec((1, D_MODEL), lambda i: (0, 0))],
        out_specs=pl.BlockSpec((T, D_MODEL), lambda i: (i, 0)),
        out_shape=jax.ShapeDtypeStruct((n, D_MODEL), jnp.float32),
        scratch_shapes=[pltpu.VMEM((2, MOE_TILE_ROWS, D_MODEL // 2), jnp.uint32),
                        pltpu.SemaphoreType.DMA((2,))],
        compiler_params=_cparams(1),
        name="moe_combine",
    )(ptab, ptab, ys, lpos_t, wts, x1, final_g)


def kernel(x, mem, attn_norm_g, mem_norm_g, w_in, w_mem_kv, na_rpb, t5_rel_bias, w_branch_a, w_branch_b,
           w_branch_c, w_out, ffn_norm_g, router_w, router_b, expert_w_gate_up, expert_b_gate_up,
           expert_w_down, expert_b_down, final_norm_g):
    B, S, D = x.shape
    depth = w_in.shape[0]
    x2d = x.reshape(B * S, D)
    mem2d = mem.reshape(B * mem.shape[1], D)
    assert depth == 1, "single-layer block (per-layer parameters carry a leading axis of 1)"
    l = 0
    na_bias, dil_bias = _bias_prep(na_rpb[l], t5_rel_bias)
    mkv = _mem_kv(mem2d, mem_norm_g[l][None], w_mem_kv[l])
    na_qkv, d0, d1, d2, qc = _in_proj(x2d, attn_norm_g[l][None], w_in[l], B, S)
    gates = _gate_proj(x2d, attn_norm_g[l][None], w_in[l])
    na_out = _na_attn(na_qkv, na_bias, B, S)
    dil_outs = [_dil_attn(qkv, dil_bias, gi, dil, B, S)
                for gi, (qkv, (_, dil)) in enumerate(zip((d0, d1, d2), DIL_GROUPS))]
    x1, h, topk_idx, topk_w, segs = _mix_out(
        x2d, na_out, dil_outs, qc, gates, mkv, w_branch_a[l], w_branch_b[l], w_branch_c[l], w_out[l],
        ffn_norm_g[l][None], router_w[l].T, router_b[l][:, None], B, S)

    n = B * S
    max_rows = n * TOP_K + (n // MOE_TILE) * N_EXPERTS * (MOE_PIECE - 1)
    n_blocks = -(-max_rows // MOE_BLOCK) + N_EXPERTS
    lpos, ptab, block_expert, pad_ends = _moe_route(topk_idx, segs, n_blocks)
    block_expert = block_expert.reshape(n_blocks)
    pad_ends = pad_ends.reshape(N_EXPERTS)
    xs = _moe_dispatch(h, lpos, ptab, pad_ends, n_blocks)
    ys = _moe_experts(xs, block_expert, pad_ends, expert_w_gate_up[l], expert_b_gate_up[l][:, None, :],
                      expert_w_down[l], expert_b_down[l][:, None, :])
    out = _moe_combine(ys, ptab, lpos.T, topk_w.T, x1, final_norm_g[None])
    return out.reshape(B, S, D)
```

```python
import functools
import math

import numpy as np
import jax
import jax.numpy as jnp
from jax import lax
from jax.experimental import pallas as pl
from jax.experimental.pallas import tpu as pltpu

D_MODEL = 1024
GRID_W = 64
MEM_HEADS = 4
MEM_HEAD_DIM = 128
HEAD_DIM = 64
NA_HEADS = 8
NA_WIN_ROWS = 8
NA_WIN_COLS = 16
DIL_GROUPS = ((128, 1), (512, 4), (2048, 16))
DIL_HEADS_PER_GROUP = 4
DIL_BLOCK = 128
T5_BUCKETS = 32
T5_MAX_DIST = 1024
N_EXPERTS = 32
TOP_K = 4
D_FF = D_MODEL
SWIGLU_ALPHA = 1.702
SWIGLU_LIMIT = 7.0
RMS_EPS = 1e-6
NEG_INF = -1e30

WIDTH_A = NA_HEADS * HEAD_DIM
WIDTH_B = DIL_HEADS_PER_GROUP * len(DIL_GROUPS) * HEAD_DIM
WIDTH_G = DIL_HEADS_PER_GROUP * HEAD_DIM
WIDTH_C = MEM_HEADS * MEM_HEAD_DIM
OFF_QA, OFF_KA, OFF_VA = 0, WIDTH_A, 2 * WIDTH_A
OFF_QB = 3 * WIDTH_A
OFF_KB = OFF_QB + WIDTH_B
OFF_VB = OFF_KB + WIDTH_B
OFF_QC = OFF_VB + WIDTH_B
OFF_GATE = OFF_QC + WIDTH_C
IN_COLS = OFF_GATE + 3 * D_MODEL

LANES = 128
MXU_WIDTH = 256
VMEM_LIMIT = 56 * 1024 * 1024

IN_PROJ_TILE = 1024
PROJ_TILE = 512
NA_ROWS_PER_STEP = 16
DIL_STEP_TOKENS = 2048
MOE_BLOCK = 512
MOE_TILE = 256
MOE_PIECE = 8
MOE_TILE_ROWS = -(-(TOP_K * MOE_TILE + N_EXPERTS * (MOE_PIECE - 1)) // MXU_WIDTH) * MXU_WIDTH
MOE_PIECE_TAB = 256
assert MOE_TILE_ROWS // MOE_PIECE < MOE_PIECE_TAB
MOE_SPARE_BLOCKS = -(-2 * MOE_TILE_ROWS // MOE_BLOCK)


def _cparams(n_axes):
    return pltpu.CompilerParams(
        dimension_semantics=("arbitrary",) * n_axes, vmem_limit_bytes=VMEM_LIMIT)


def _rms(x, g):
    return x * lax.rsqrt(jnp.mean(x * x, axis=-1, keepdims=True) + RMS_EPS) * g


def _sigmoid(x):
    return 1.0 / (1.0 + jnp.exp(-x))


def _pack_bf16_pairs(x):
    w = x.shape[1] // 2
    bits = lax.bitcast_convert_type(x.astype(jnp.bfloat16).astype(jnp.float32), jnp.uint32)
    return (bits[:, w:] & jnp.uint32(0xFFFF0000)) | (bits[:, :w] >> 16)


def _unpack_bf16_pairs(p):
    lo = lax.bitcast_convert_type(p << 16, jnp.float32)
    hi = lax.bitcast_convert_type(p & jnp.uint32(0xFFFF0000), jnp.float32)
    return jnp.concatenate([lo, hi], axis=1).astype(jnp.bfloat16)


def _dot(a, b):
    return jnp.dot(a, b, preferred_element_type=jnp.float32)


def _dot_nt(a, b):
    return lax.dot_general(a, b, (((1,), (1,)), ((), ())), preferred_element_type=jnp.float32)


def _mem_kv_kernel(mem_ref, g_ref, w_ref, out_ref):
    h = _rms(mem_ref[...], g_ref[...]).astype(jnp.bfloat16)
    out_ref[...] = _dot(h, w_ref[...]).astype(jnp.bfloat16)


def _mem_kv(mem2d, g, w):
    rows = mem2d.shape[0]
    return pl.pallas_call(
        _mem_kv_kernel,
        out_shape=jax.ShapeDtypeStruct((rows, 2 * WIDTH_C), jnp.bfloat16),
        name="mem_kv",
        compiler_params=pltpu.CompilerParams(vmem_limit_bytes=VMEM_LIMIT),
    )(mem2d, g, w)


def _in_proj_kernel(x_ref, g_ref, w_ref, na_ref, d0_ref, d1_ref, d2_ref, qc_ref, acc_ref):
    T = x_ref.shape[0]
    h = _rms(x_ref[...], g_ref[...]).astype(jnp.bfloat16)
    q_scale = HEAD_DIM ** -0.5

    for c in range(3):
        acc = _dot(h, w_ref[:, c * WIDTH_A:(c + 1) * WIDTH_A])
        if c == 0:
            acc = acc * q_scale
        na_ref[:, c * WIDTH_A:(c + 1) * WIDTH_A] = acc.astype(jnp.bfloat16)

    for gi, ((_, dil), d_ref) in enumerate(zip(DIL_GROUPS, (d0_ref, d1_ref, d2_ref))):
        for kind, off in enumerate((OFF_QB, OFF_KB, OFF_VB)):
            lo = off + gi * WIDTH_G
            acc = _dot(h, w_ref[:, lo:lo + WIDTH_G])
            if kind == 0:
                acc = acc * q_scale
            cols = slice(kind * WIDTH_G, (kind + 1) * WIDTH_G)
            if dil == 1:
                d_ref[0, 0, :, cols] = acc.astype(jnp.bfloat16)
            else:
                for c in range(WIDTH_G // LANES):
                    acc_ref[c] = acc[:, c * LANES:(c + 1) * LANES]
                for rho in range(dil):
                    d_ref[0, rho, :, cols] = jnp.concatenate(
                        [acc_ref[c, pl.ds(rho, T // dil, stride=dil), :] for c in range(WIDTH_G // LANES)],
                        axis=-1).astype(jnp.bfloat16)

    qc_ref[...] = _dot(h, w_ref[:, OFF_QC:OFF_QC + WIDTH_C]).astype(jnp.bfloat16)


def _gate_proj_kernel(x_ref, g_ref, w_ref, gate_ref):
    h = _rms(x_ref[...], g_ref[...]).astype(jnp.bfloat16)
    chunk = 512
    first = OFF_GATE - IN_COLS // 2
    for c in range(3 * D_MODEL // chunk):
        lo = first + c * chunk
        gate_ref[:, c * chunk:(c + 1) * chunk] = _sigmoid(_dot(h, w_ref[:, lo:lo + chunk])).astype(jnp.bfloat16)


def _gate_proj(x2d, g, w_bf16):
    n = x2d.shape[0]
    T = IN_PROJ_TILE
    assert (IN_COLS // 2) % LANES == 0 and IN_COLS // 2 <= OFF_GATE
    return pl.pallas_call(
        _gate_proj_kernel,
        grid=(n // T,),
        in_specs=[pl.BlockSpec((T, D_MODEL), lambda i: (i, 0)),
                  pl.BlockSpec((1, D_MODEL), lambda i: (0, 0)),
                  pl.BlockSpec((D_MODEL, IN_COLS // 2), lambda i: (0, 1), pipeline_mode=pl.Buffered(1))],
        out_specs=pl.BlockSpec((T, 3 * D_MODEL), lambda i: (i, 0)),
        out_shape=jax.ShapeDtypeStruct((n, 3 * D_MODEL), jnp.bfloat16),
        compiler_params=_cparams(1),
        name="gate_proj",
    )(x2d, g, w_bf16)


def _in_proj(x2d, g, w_bf16, batch, seq):
    n = x2d.shape[0]
    T = IN_PROJ_TILE
    tiles_per_batch = seq // T
    out_shape = [jax.ShapeDtypeStruct((n, 3 * WIDTH_A), jnp.bfloat16)]
    out_specs = [pl.BlockSpec((T, 3 * WIDTH_A), lambda i: (i, 0))]
    for _, dil in DIL_GROUPS:
        out_shape.append(jax.ShapeDtypeStruct((batch, dil, seq // dil, 3 * WIDTH_G), jnp.bfloat16))
        out_specs.append(pl.BlockSpec((1, dil, T // dil, 3 * WIDTH_G),
                                      lambda i: (i // tiles_per_batch, 0, i % tiles_per_batch, 0)))
    out_shape += [jax.ShapeDtypeStruct((n, WIDTH_C), jnp.bfloat16)]
    out_specs += [pl.BlockSpec((T, WIDTH_C), lambda i: (i, 0))]
    return pl.pallas_call(
        _in_proj_kernel,
        grid=(n // T,),
        in_specs=[pl.BlockSpec((T, D_MODEL), lambda i: (i, 0)),
                  pl.BlockSpec((1, D_MODEL), lambda i: (0, 0)),
                  pl.BlockSpec((D_MODEL, OFF_GATE), lambda i: (0, 0), pipeline_mode=pl.Buffered(1))],
        out_specs=out_specs,
        out_shape=out_shape,
        scratch_shapes=[pltpu.VMEM((WIDTH_G // LANES, T, LANES), jnp.float32)],
        compiler_params=_cparams(1),
        name="in_proj",
    )(x2d, g, w_bf16)


def _na_kernel(q_ref, k_ref, v_ref, bias_ref, out_ref, *, rows):
    i = pl.program_id(1)
    lane = lax.broadcasted_iota(jnp.int32, (GRID_W, LANES), 1)
    low_half = lane < HEAD_DIM
    n_win = NA_WIN_ROWS * GRID_W

    def row_body(j, carry):
        qr = i * NA_ROWS_PER_STEP + j
        rs = jnp.clip(qr - NA_WIN_ROWS // 2, 0, rows - NA_WIN_ROWS)
        shift = rs - qr + NA_WIN_ROWS - 1
        q_off = pl.multiple_of(j * GRID_W, GRID_W)
        k_off = pl.multiple_of(rs * GRID_W, GRID_W)
        scores = []
        for hp in range(NA_HEADS // 2):
            cols = slice(hp * LANES, (hp + 1) * LANES)
            qp = q_ref[pl.ds(q_off, GRID_W), cols]
            kp = k_ref[pl.ds(k_off, n_win), cols]
            zero = jnp.zeros_like(qp)
            q2 = jnp.concatenate([jnp.where(low_half, qp, zero), jnp.where(low_half, zero, qp)], axis=0)
            bias = jnp.concatenate(
                [jnp.concatenate([bias_ref[shift + 2 * p, 2 * hp + hh] for p in range(NA_WIN_ROWS // 2)], axis=-1)
                 for hh in range(2)], axis=0)
            scores.append(_dot_nt(q2, kp) + bias)
        probs = []
        for s in scores:
            e = jnp.exp(s - jnp.max(s, axis=-1, keepdims=True))
            probs.append((e.astype(jnp.bfloat16), jnp.sum(e, axis=-1, keepdims=True)))
        for hp, (p, l) in enumerate(probs):
            cols = slice(hp * LANES, (hp + 1) * LANES)
            o2 = _dot(p, v_ref[pl.ds(k_off, n_win), cols]) / l
            out_ref[pl.ds(q_off, GRID_W), cols] = jnp.where(
                low_half, o2[:GRID_W], o2[GRID_W:]).astype(jnp.bfloat16)
        return carry

    lax.fori_loop(0, NA_ROWS_PER_STEP, row_body, 0, unroll=4)


def _na_attn(na_qkv, bias_tab, batch, seq):
    rows = seq // GRID_W
    steps = rows // NA_ROWS_PER_STEP
    tq = NA_ROWS_PER_STEP * GRID_W
    n = na_qkv.shape[0]
    return pl.pallas_call(
        functools.partial(_na_kernel, rows=rows),
        grid=(batch, steps),
        in_specs=[pl.BlockSpec((tq, WIDTH_A), lambda b, i: (b * steps + i, 0)),
                  pl.BlockSpec((seq, WIDTH_A), lambda b, i: (b, 1)),
                  pl.BlockSpec((seq, WIDTH_A), lambda b, i: (b, 2)),
                  pl.BlockSpec(bias_tab.shape, lambda b, i: (0, 0, 0, 0))],
        out_specs=pl.BlockSpec((tq, WIDTH_A), lambda b, i: (b * steps + i, 0)),
        out_shape=jax.ShapeDtypeStruct((n, WIDTH_A), jnp.bfloat16),
        compiler_params=_cparams(2),
        name="na_attn",
    )(na_qkv, na_qkv, na_qkv, bias_tab)


def _t5_bucket_np(rel):
    half = T5_BUCKETS // 2
    max_exact = half // 2
    ret = np.where(rel > 0, half, 0)
    n = np.abs(rel)
    nf = np.maximum(n, 1).astype(np.float32)
    large = max_exact + (np.log(nf / np.float32(max_exact)) / np.float32(math.log(T5_MAX_DIST / max_exact))
                         * np.float32(half - max_exact)).astype(np.int32)
    large = np.minimum(large, half - 1)
    return ret + np.where(n < max_exact, n, large)


def _dil_bucket_table():
    q = np.arange(DIL_BLOCK)[:, None]
    j = np.arange(2 * DIL_BLOCK)[None, :]
    tabs = []
    for window, dil in DIL_GROUPS:
        half = (window // 2) // dil
        assert half == DIL_BLOCK // 2
        rel = (j - half) - q
        tabs.append(np.where(np.abs(rel) <= half, _t5_bucket_np(rel * dil), -1))
    return np.stack(tabs).astype(np.int32)


def _bias_prep_kernel(rpb_ref, t5_ref, bucket_ref, na_ref, dil_ref):
    n_dc = 2 * NA_WIN_COLS - 1
    n_dr = 2 * NA_WIN_ROWS - 1
    qc = lax.broadcasted_iota(jnp.int32, (GRID_W, 2 * GRID_W), 0)
    lane = lax.broadcasted_iota(jnp.int32, (GRID_W, 2 * GRID_W), 1)
    second = lane >= GRID_W
    kc = jnp.where(second, lane - GRID_W, lane)
    diff = kc - qc
    cs = jnp.clip(qc - NA_WIN_COLS // 2, 0, GRID_W - NA_WIN_COLS)
    valid = jnp.logical_and(kc >= cs, kc < cs + NA_WIN_COLS)

    def na_tile(mh, carry):
        m = mh // NA_HEADS
        h = mh - m * NA_HEADS
        base = (h * n_dr + m) * n_dc
        acc = jnp.zeros((GRID_W, 2 * GRID_W), jnp.float32)
        for d in range(n_dc):
            v = jnp.where(second, rpb_ref[base + n_dc + d], rpb_ref[base + d])
            acc = jnp.where(diff == d - (NA_WIN_COLS - 1), v, acc)
        na_ref[m, h] = jnp.where(valid, acc, NEG_INF)
        return carry

    lax.fori_loop(0, (n_dr - 1) * NA_HEADS, na_tile, 0)

    j = lax.broadcasted_iota(jnp.int32, (DIL_BLOCK, 2 * DIL_BLOCK), 1)
    half = DIL_BLOCK // 2
    n_heads = DIL_HEADS_PER_GROUP * len(DIL_GROUPS)
    for gi in range(len(DIL_GROUPS)):
        bucket = bucket_ref[gi]
        in_win = bucket >= 0
        for h in range(DIL_HEADS_PER_GROUP):
            acc = jnp.zeros((DIL_BLOCK, 2 * DIL_BLOCK), jnp.float32)
            for b in range(T5_BUCKETS):
                acc = jnp.where(bucket == b, t5_ref[b * n_heads + gi * DIL_HEADS_PER_GROUP + h], acc)
            for v, ok in enumerate((jnp.logical_and(in_win, j >= half), in_win,
                                    jnp.logical_and(in_win, j < 2 * DIL_BLOCK - half))):
                dil_ref[gi, v, h] = jnp.where(ok, acc, NEG_INF)


def _bias_prep(rpb, t5):
    smem = pl.BlockSpec(memory_space=pltpu.SMEM)
    return pl.pallas_call(
        _bias_prep_kernel,
        in_specs=[smem, smem, pl.BlockSpec(memory_space=pltpu.VMEM)],
        out_shape=[jax.ShapeDtypeStruct((2 * NA_WIN_ROWS - 2, NA_HEADS, GRID_W, 2 * GRID_W), jnp.float32),
                   jax.ShapeDtypeStruct((len(DIL_GROUPS), 3, DIL_HEADS_PER_GROUP, DIL_BLOCK, 2 * DIL_BLOCK),
                                        jnp.float32)],
        compiler_params=pltpu.CompilerParams(vmem_limit_bytes=VMEM_LIMIT),
        name="bias_prep",
    )(rpb.reshape(-1), t5.reshape(-1), jnp.asarray(_dil_bucket_table()))


def _dil_kernel(q_ref, kp_ref, kc_ref, kn_ref, vp_ref, vc_ref, vn_ref, bias_ref, o_ref, lse_ref, *, dil):
    lane = lax.broadcasted_iota(jnp.int32, (DIL_BLOCK, LANES), 1)
    low_half = lane < HEAD_DIM
    hb = DIL_BLOCK // 2

    span = pl.program_id(1)
    n_spans = pl.num_programs(1)
    nbs = kc_ref.shape[2] // DIL_BLOCK

    def window(prev_ref, cur_ref, next_ref, rho, blk, cols):
        lo = blk * DIL_BLOCK
        first = prev_ref[0, rho, hb:, cols] if blk == 0 else cur_ref[0, rho, lo - hb:lo, cols]
        last = (next_ref[0, rho, :hb, cols] if blk == nbs - 1
                else cur_ref[0, rho, lo + DIL_BLOCK:lo + DIL_BLOCK + hb, cols])
        return jnp.concatenate([first, cur_ref[0, rho, lo:lo + DIL_BLOCK, cols], last], axis=0)

    def unit(rho, blk):
        variant = 1
        if blk == nbs - 1:
            variant = jnp.where(span == n_spans - 1, 2, variant)
        if blk == 0:
            variant = jnp.where(span == 0, 0, variant)
        scores = []
        for hp in range(DIL_HEADS_PER_GROUP // 2):
            cols = slice(hp * LANES, (hp + 1) * LANES)
            qp = q_ref[0, rho, blk * DIL_BLOCK:(blk + 1) * DIL_BLOCK, cols]
            zero = jnp.zeros_like(qp)
            q2 = jnp.concatenate([jnp.where(low_half, qp, zero), jnp.where(low_half, zero, qp)], axis=0)
            kw = window(kp_ref, kc_ref, kn_ref, rho, blk, cols)
            bias = jnp.concatenate([bias_ref[variant, 2 * hp], bias_ref[variant, 2 * hp + 1]], axis=0)
            scores.append(_dot_nt(q2, kw) + bias)
        probs = []
        for s in scores:
            m = jnp.max(s, axis=-1, keepdims=True)
            e = jnp.exp(s - m)
            probs.append((e.astype(jnp.bfloat16), m, jnp.sum(e, axis=-1, keepdims=True)))
        for hp, (p, m, l) in enumerate(probs):
            cols = slice(hp * LANES, (hp + 1) * LANES)
            o2 = _dot(p, window(vp_ref, vc_ref, vn_ref, rho, blk, cols)) / l
            lse2 = jnp.broadcast_to(m + jnp.log(l), (2 * DIL_BLOCK, LANES))
            o_pair = jnp.where(low_half, o2[:DIL_BLOCK], o2[DIL_BLOCK:])
            lse_pair = jnp.where(low_half, lse2[:DIL_BLOCK], lse2[DIL_BLOCK:])
            if dil == 1:
                rows = slice(blk * DIL_BLOCK, (blk + 1) * DIL_BLOCK)
                o_ref[hp, rows, :] = o_pair
                lse_ref[hp, rows, :] = lse_pair
            else:
                rows = pl.ds(blk * DIL_BLOCK * dil + rho, DIL_BLOCK, stride=dil)
                o_ref[hp, rows, :] = o_pair
                lse_ref[hp, rows, :] = lse_pair

    def class_body(rho, carry):
        for blk in range(nbs):
            unit(rho, blk)
        return carry

    if dil == 1:
        class_body(0, 0)
    else:
        lax.fori_loop(0, dil, class_body, 0, unroll=4 if nbs == 1 else 2)


def _dil_attn(qkv, bias_tab, group, dil, batch, seq):
    L = seq // dil
    nb = L // DIL_BLOCK
    assert nb >= 2
    tq = DIL_STEP_TOKENS
    span = tq // dil
    nbs = span // DIL_BLOCK
    n_spans = L // span

    def cur(col):
        return pl.BlockSpec((1, dil, span, WIDTH_G), lambda b, n: (b, 0, n, col))

    def edge(col, shift):
        return pl.BlockSpec((1, dil, DIL_BLOCK, WIDTH_G),
                            lambda b, n: (b, 0, jnp.clip(n * nbs + (shift if shift < 0 else nbs), 0, nb - 1), col))

    n_pairs = WIDTH_G // LANES
    out_spec = pl.BlockSpec((n_pairs, tq, LANES), lambda b, n: (0, b * n_spans + n, 0))
    return pl.pallas_call(
        functools.partial(_dil_kernel, dil=dil),
        grid=(batch, n_spans),
        in_specs=[cur(0), edge(1, -1), cur(1), edge(1, 1), edge(2, -1), cur(2), edge(2, 1),
                  pl.BlockSpec((None, 3, DIL_HEADS_PER_GROUP, DIL_BLOCK, 2 * DIL_BLOCK),
                               lambda b, n: (group, 0, 0, 0, 0))],
        out_specs=[out_spec, out_spec],
        out_shape=[jax.ShapeDtypeStruct((n_pairs, batch * seq, LANES), jnp.float32)] * 2,
        compiler_params=_cparams(2),
        name=f"dil_attn_d{dil}",
    )(qkv, qkv, qkv, qkv, qkv, qkv, qkv, bias_tab)


def _mix_kernel(x_ref, na_ref, o0_ref, l0_ref, o1_ref, l1_ref, o2_ref, l2_ref, qc_ref, gate_ref,
                mkv_ref, wa_ref, wb_ref, wc_ref, wo_ref, fg_ref, rw_ref, rb_ref,
                x1_ref, h_ref, idx_ref, wt_ref, seg_ref):
    T = x_ref.shape[0]
    ob = []
    for hp in range(WIDTH_G // LANES):
        l0, l1, l2 = l0_ref[hp], l1_ref[hp], l2_ref[hp]
        m = jnp.maximum(jnp.maximum(l0, l1), l2)
        e0, e1, e2 = jnp.exp(l0 - m), jnp.exp(l1 - m), jnp.exp(l2 - m)
        ob.append((e0 * o0_ref[hp] + e1 * o1_ref[hp] + e2 * o2_ref[hp]) / (e0 + e1 + e2))
    ob = jnp.concatenate(ob, axis=-1)

    mem_scale = MEM_HEAD_DIM ** -0.5
    scores = []
    for h in range(MEM_HEADS):
        cols = slice(h * MEM_HEAD_DIM, (h + 1) * MEM_HEAD_DIM)
        scores.append(_dot_nt(qc_ref[:, cols], mkv_ref[:, cols]) * mem_scale)
    probs = []
    for s in scores:
        e = jnp.exp(s - jnp.max(s, axis=-1, keepdims=True))
        probs.append((e.astype(jnp.bfloat16), jnp.sum(e, axis=-1, keepdims=True)))
    oc = []
    for h, (p, l) in enumerate(probs):
        oc.append(_dot(p, mkv_ref[:, WIDTH_C + h * MEM_HEAD_DIM:WIDTH_C + (h + 1) * MEM_HEAD_DIM]) / l)
    oc = jnp.concatenate(oc, axis=-1).astype(jnp.bfloat16)

    y_a = _dot(na_ref[...], wa_ref[...])
    y_b = _dot(ob.astype(jnp.bfloat16), wb_ref[...])
    y_c = _dot(oc, wc_ref[...])
    merged = (gate_ref[:, 0:D_MODEL].astype(jnp.float32) * y_a
              + gate_ref[:, D_MODEL:2 * D_MODEL].astype(jnp.float32) * y_b
              + gate_ref[:, 2 * D_MODEL:3 * D_MODEL].astype(jnp.float32) * y_c)
    x1 = x_ref[...] + _dot(merged.astype(jnp.bfloat16), wo_ref[...])
    x1_ref[...] = x1

    h = _rms(x1, fg_ref[...])
    h_ref[...] = h.astype(jnp.bfloat16)
    logits = lax.dot_general(rw_ref[...], h, (((1,), (1,)), ((), ())), preferred_element_type=jnp.float32,
                             precision=lax.Precision.HIGHEST) + rb_ref[...]
    expert = lax.broadcasted_iota(jnp.int32, (N_EXPERTS, T), 0)
    vals, idxs = [], []
    for _ in range(TOP_K):
        mx = jnp.max(logits, axis=0, keepdims=True)
        sel = jnp.min(jnp.where(logits == mx, expert, N_EXPERTS), axis=0, keepdims=True)
        vals.append(mx)
        idxs.append(sel)
        logits = jnp.where(expert == sel, -jnp.inf, logits)
    ev = [jnp.exp(v - vals[0]) for v in vals]
    den = ev[0] + ev[1] + ev[2] + ev[3]
    idx_ref[...] = jnp.concatenate(idxs, axis=0)
    wt_ref[...] = jnp.concatenate([e / den for e in ev], axis=0)
    member = sum((expert == sel).astype(jnp.float32) for sel in idxs)
    for s in range(T // MOE_TILE):
        cnt = jnp.sum(member[:, s * MOE_TILE:(s + 1) * MOE_TILE], axis=1, keepdims=True)
        seg = jnp.floor((cnt + (MOE_PIECE - 1)) * (1.0 / MOE_PIECE)) * MOE_PIECE
        seg_ref[s] = jnp.broadcast_to(seg, (N_EXPERTS, LANES))


def _mix_out(x2d, na_out, dil_outs, qc, gates, mkv, wa, wb, wc, wo, ffn_g, router_w, router_b, batch, seq):
    n = x2d.shape[0]
    T = PROJ_TILE
    tiles_per_batch = seq // T
    mem_len = mkv.shape[0] // batch
    row = lambda w: pl.BlockSpec((T, w), lambda i: (i, 0))
    full = lambda a: pl.BlockSpec(a.shape, lambda i: (0,) * a.ndim, pipeline_mode=pl.Buffered(1))
    in_specs = [row(D_MODEL), row(WIDTH_A)]
    args = [x2d, na_out]
    for o, l in dil_outs:
        pair_rows = pl.BlockSpec((WIDTH_G // LANES, T, LANES), lambda i: (0, i, 0))
        in_specs += [pair_rows, pair_rows]
        args += [o, l]
    in_specs += [row(WIDTH_C), row(3 * D_MODEL),
                 pl.BlockSpec((mem_len, 2 * WIDTH_C), lambda i: (i // tiles_per_batch, 0)),
                 full(wa), full(wb), full(wc), full(wo), full(ffn_g), full(router_w), full(router_b)]
    args += [qc, gates, mkv, wa, wb, wc, wo, ffn_g, router_w, router_b]
    return pl.pallas_call(
        _mix_kernel,
        grid=(n // T,),
        in_specs=in_specs,
        out_specs=[row(D_MODEL), row(D_MODEL), pl.BlockSpec((TOP_K, T), lambda i: (0, i)),
                   pl.BlockSpec((TOP_K, T), lambda i: (0, i)),
                   pl.BlockSpec((T // MOE_TILE, N_EXPERTS, LANES), lambda i: (i, 0, 0))],
        out_shape=[jax.ShapeDtypeStruct((n, D_MODEL), jnp.float32),
                   jax.ShapeDtypeStruct((n, D_MODEL), jnp.bfloat16),
                   jax.ShapeDtypeStruct((TOP_K, n), jnp.int32),
                   jax.ShapeDtypeStruct((TOP_K, n), jnp.float32),
                   jax.ShapeDtypeStruct((n // MOE_TILE, N_EXPERTS, LANES), jnp.float32)],
        compiler_params=_cparams(1),
        name="mix_out",
    )(*args)


def _route_kernel(idx_ref, segs_ref, lpos_ref, ptab_ref, be_ref, pe_ref, carry_ref, pstart_ref):
    i = pl.program_id(0)
    T = idx_ref.shape[1]
    n_blocks = be_ref.shape[1]
    expert = lax.broadcasted_iota(jnp.int32, (N_EXPERTS, T), 0)
    idx = idx_ref[...]
    onehots = [expert == idx[k:k + 1, :] for k in range(TOP_K)]
    member = sum(o.astype(jnp.float32) for o in onehots)
    seg_b = segs_ref[i]
    seg = seg_b[:, 0:1]
    r = lax.broadcasted_iota(jnp.int32, (N_EXPERTS, N_EXPERTS), 0)
    c = lax.broadcasted_iota(jnp.int32, (N_EXPERTS, N_EXPERTS), 1)
    lower_incl = (r >= c).astype(jnp.float32)

    def cumsum_experts(v):
        return jnp.dot(lower_incl, v, preferred_element_type=jnp.float32, precision=lax.Precision.HIGHEST)

    @pl.when(i == 0)
    def _():
        counts = jnp.sum(segs_ref[...], axis=0)
        padded = jnp.floor((counts + (MOE_BLOCK - 1)) * (1.0 / MOE_BLOCK)) * MOE_BLOCK
        pad_ends = cumsum_experts(padded)
        pstart_ref[...] = pad_ends - padded
        carry_ref[...] = jnp.zeros_like(carry_ref)
        pe_ref[...] = pad_ends[:, 0:1].astype(jnp.int32)
        blk_start = (lax.broadcasted_iota(jnp.int32, (N_EXPERTS, n_blocks), 1) * MOE_BLOCK).astype(jnp.float32)
        be = jnp.sum((pad_ends[:, 0:1] <= blk_start).astype(jnp.int32), axis=0, keepdims=True)
        be_ref[...] = jnp.minimum(be, N_EXPERTS - 1)

    local_end = cumsum_experts(seg_b)[:, 0:1]
    local_start = local_end - seg
    global_start = pstart_ref[:, 0:1] + carry_ref[:, 0:1]
    rr = lax.broadcasted_iota(jnp.int32, (T, T), 0)
    cc = lax.broadcasted_iota(jnp.int32, (T, T), 1)
    earlier = (rr < cc).astype(jnp.bfloat16)
    before = _dot(member.astype(jnp.bfloat16), earlier)
    base = local_start + before
    lpos = [jnp.sum(jnp.where(o, base, 0.0), axis=0, keepdims=True) for o in onehots]
    lpos_ref[...] = jnp.concatenate(lpos, axis=0).astype(jnp.int32)

    inv = 1.0 / MOE_PIECE
    piece = lax.broadcasted_iota(jnp.int32, (N_EXPERTS, MOE_PIECE_TAB), 1).astype(jnp.float32)
    first = local_start * inv
    inside = jnp.logical_and(piece >= first, piece < local_end * inv)
    dst = jnp.sum(jnp.where(inside, global_start * inv + piece - first, 0.0), axis=0, keepdims=True)
    n_pieces = local_end[N_EXPERTS - 1:N_EXPERTS, :] * inv
    spare = (n_blocks * MOE_BLOCK // MOE_PIECE
             + (i % 2) * (MOE_TILE_ROWS // MOE_PIECE)).astype(jnp.float32)
    dst = jnp.where(piece[0:1, :] >= n_pieces, spare + piece[0:1, :], dst)
    dst = jnp.where(piece[0:1, :] == MOE_PIECE_TAB - 1, n_pieces, dst)
    ptab_ref[...] = dst.astype(jnp.int32)
    carry_ref[...] += seg


def _moe_route(topk_idx, segs, n_blocks):
    n = topk_idx.shape[1]
    T = MOE_TILE
    acc = pltpu.VMEM((N_EXPERTS, LANES), jnp.float32)
    return pl.pallas_call(
        _route_kernel,
        grid=(n // T,),
        in_specs=[pl.BlockSpec((TOP_K, T), lambda i: (0, i)),
                  pl.BlockSpec(segs.shape, lambda i: (0, 0, 0))],
        out_specs=[pl.BlockSpec((TOP_K, T), lambda i: (0, i)),
                   pl.BlockSpec((None, 1, MOE_PIECE_TAB), lambda i: (i, 0, 0)),
                   pl.BlockSpec((1, n_blocks), lambda i: (0, 0)),
                   pl.BlockSpec((N_EXPERTS, 1), lambda i: (0, 0))],
        out_shape=[jax.ShapeDtypeStruct((TOP_K, n), jnp.int32),
                   jax.ShapeDtypeStruct((n // T, 1, MOE_PIECE_TAB), jnp.int32),
                   jax.ShapeDtypeStruct((1, n_blocks), jnp.int32),
                   jax.ShapeDtypeStruct((N_EXPERTS, 1), jnp.int32)],
        scratch_shapes=[acc, acc],
        compiler_params=_cparams(1),
        name="moe_route",
    )(topk_idx, segs)


def _piece(ref, p):
    start = p * MOE_PIECE if isinstance(p, int) else pl.multiple_of(p * MOE_PIECE, MOE_PIECE)
    return ref.at[pl.ds(start, MOE_PIECE), :]


def _dispatch_kernel(pe_ref, ptab_ref, lpos_ref, h_ref, xs_hbm, zbuf, pbuf, sem, zsem):
    i = pl.program_id(0)
    T = h_ref.shape[0]
    half = h_ref.shape[1] // 2

    def zero_copy(e):
        end = pe_ref[e]
        start = pl.multiple_of(end - MOE_BLOCK, MOE_BLOCK)
        return pltpu.make_async_copy(zbuf, xs_hbm.at[pl.ds(start, MOE_BLOCK), :], zsem)

    def nonempty(e):
        return pe_ref[e] > (pe_ref[e - 1] if e else 0)

    @pl.when(i == 0)
    def _():
        zbuf[...] = jnp.zeros_like(zbuf)
        for e in range(N_EXPERTS):
            @pl.when(nonempty(e))
            def _():
                zero_copy(e).start()
        for e in range(N_EXPERTS):
            @pl.when(nonempty(e))
            def _():
                zero_copy(e).wait()

        def tail_copy(b):
            return pltpu.make_async_copy(
                zbuf, xs_hbm.at[pl.ds(pl.multiple_of(b * MOE_BLOCK, MOE_BLOCK), MOE_BLOCK), :], zsem)
        first_tail = pe_ref[N_EXPERTS - 1] // MOE_BLOCK
        n_blocks = xs_hbm.shape[0] // MOE_BLOCK
        lax.fori_loop(first_tail, n_blocks, lambda b, c: (tail_copy(b).start(), c)[1], 0)
        lax.fori_loop(first_tail, n_blocks, lambda b, c: (tail_copy(b).wait(), c)[1], 0)

    row = lax.broadcasted_iota(jnp.int32, (MOE_TILE_ROWS, T), 0)
    lpos = lpos_ref[...]
    perm = sum((row == lpos[k:k + 1, :]).astype(jnp.bfloat16) for k in range(TOP_K))
    lo = lax.bitcast_convert_type(_dot(perm, h_ref[:, :half]), jnp.uint32)
    hi = lax.bitcast_convert_type(_dot(perm, h_ref[:, half:]), jnp.uint32)
    slot = i % 2
    pbuf[slot] = hi | (lo >> 16)

    n_tile_pieces = MOE_TILE_ROWS // MOE_PIECE

    def piece_copy(s, p):
        return pltpu.make_async_copy(_piece(pbuf.at[s], p), _piece(xs_hbm, ptab_ref[p]), sem.at[s])

    def wait_all(s):
        for _ in range(n_tile_pieces):
            piece_copy(s, 0).wait()

    for p in range(n_tile_pieces):
        piece_copy(slot, p).start()

    @pl.when(i > 0)
    def _():
        wait_all(1 - slot)

    @pl.when(i == pl.num_programs(0) - 1)
    def _():
        wait_all(slot)


def _moe_dispatch(h, lpos, ptab, pad_ends, n_blocks):
    n, width = h.shape
    T = MOE_TILE
    n_steps = n // T

    def kernel(pe_ref, ptab_ref, *rest):
        _dispatch_kernel(pe_ref, ptab_ref.at[0], *rest)

    grid_spec = pltpu.PrefetchScalarGridSpec(
        num_scalar_prefetch=1,
        grid=(n_steps,),
        in_specs=[pl.BlockSpec((None, 1, MOE_PIECE_TAB), lambda i, pe: (i, 0, 0), memory_space=pltpu.SMEM),
                  pl.BlockSpec((TOP_K, T), lambda i, pe: (0, i)),
                  pl.BlockSpec((T, width), lambda i, pe: (i, 0))],
        out_specs=pl.BlockSpec(memory_space=pl.ANY),
        scratch_shapes=[pltpu.VMEM((MOE_BLOCK, width // 2), jnp.uint32),
                        pltpu.VMEM((2, MOE_TILE_ROWS, width // 2), jnp.uint32),
                        pltpu.SemaphoreType.DMA((2,)),
                        pltpu.SemaphoreType.DMA(())],
    )
    return pl.pallas_call(
        kernel,
        grid_spec=grid_spec,
        out_shape=jax.ShapeDtypeStruct(((n_blocks + MOE_SPARE_BLOCKS) * MOE_BLOCK, width // 2), jnp.uint32),
        compiler_params=_cparams(1),
        name="moe_dispatch",
    )(pad_ends, ptab, lpos, h)


def _moe_kernel(be_ref, pe_ref, xs_ref, wgu_ref, bgu_ref, wdn_ref, bdn_ref, y_ref):
    del be_ref
    i = pl.program_id(0)
    n_valid = pe_ref[N_EXPERTS - 1] // MOE_BLOCK

    @pl.when(i < n_valid)
    def _():
        x = _unpack_bf16_pairs(xs_ref[...])
        gu = _dot(x, wgu_ref[0]) + bgu_ref[0]
        g = jnp.minimum(gu[:, :D_FF], SWIGLU_LIMIT)
        u = jnp.clip(gu[:, D_FF:], -SWIGLU_LIMIT, SWIGLU_LIMIT)
        a = (u + 1.0) * (g * _sigmoid(SWIGLU_ALPHA * g))
        y_ref[...] = _pack_bf16_pairs(_dot(a.astype(jnp.bfloat16), wdn_ref[0]) + bdn_ref[0])

    @pl.when(i >= n_valid)
    def _():
        y_ref[...] = jnp.zeros_like(y_ref)


def _moe_experts(xs, block_expert, pad_ends, wgu, bgu, wdn, bdn):
    n_blocks = block_expert.shape[0]

    def row_blk(i, be, pe):
        return (jnp.minimum(i, jnp.maximum(pe[N_EXPERTS - 1] // MOE_BLOCK - 1, 0)), 0)

    grid_spec = pltpu.PrefetchScalarGridSpec(
        num_scalar_prefetch=2,
        grid=(n_blocks,),
        in_specs=[pl.BlockSpec((MOE_BLOCK, D_MODEL // 2), row_blk),
                  pl.BlockSpec((1, D_MODEL, 2 * D_FF), lambda i, be, pe: (be[i], 0, 0)),
                  pl.BlockSpec((1, 1, 2 * D_FF), lambda i, be, pe: (be[i], 0, 0)),
                  pl.BlockSpec((1, D_FF, D_MODEL), lambda i, be, pe: (be[i], 0, 0)),
                  pl.BlockSpec((1, 1, D_MODEL), lambda i, be, pe: (be[i], 0, 0))],
        out_specs=pl.BlockSpec((MOE_BLOCK, D_MODEL // 2), lambda i, be, pe: (i, 0)),
    )
    return pl.pallas_call(
        _moe_kernel,
        grid_spec=grid_spec,
        out_shape=jax.ShapeDtypeStruct((n_blocks * MOE_BLOCK, D_MODEL // 2), jnp.uint32),
        compiler_params=_cparams(1),
        name="moe_experts",
    )(block_expert, pad_ends, xs, wgu, bgu, wdn, bdn)


def _gather_pieces(src_hbm, ptab_ref, dst_ref, sem):
    spare = src_hbm.shape[0] // MOE_PIECE

    for p in range(MOE_TILE_ROWS // MOE_PIECE):
        src = ptab_ref[p]
        src = jnp.where(src >= spare, 0, src)
        pltpu.make_async_copy(_piece(src_hbm, src), _piece(dst_ref, p), sem).start()


def _wait_pieces(src_hbm, dst_ref, sem):
    for _ in range(MOE_TILE_ROWS // MOE_PIECE):
        pltpu.make_async_copy(_piece(src_hbm, 0), _piece(dst_ref, 0), sem).wait()


def _combine_kernel(ptab_ref, ptab_next_ref, ys_hbm, lpos_ref, wt_ref, x1_ref, g_ref, out_ref, gbuf, sem):
    i = pl.program_id(0)
    n_steps = pl.num_programs(0)
    slot = i % 2
    T = x1_ref.shape[0]

    @pl.when(i == 0)
    def _():
        _gather_pieces(ys_hbm, ptab_ref, gbuf.at[0], sem.at[0])

    @pl.when(i + 1 < n_steps)
    def _():
        _gather_pieces(ys_hbm, ptab_next_ref, gbuf.at[1 - slot], sem.at[1 - slot])

    _wait_pieces(ys_hbm, gbuf.at[slot], sem.at[slot])
    rows = _unpack_bf16_pairs(gbuf[slot])
    col = lax.broadcasted_iota(jnp.int32, (T, MOE_TILE_ROWS), 1)
    lpos = lpos_ref[...]
    wt = wt_ref[...]
    cmb = jnp.zeros((T, MOE_TILE_ROWS), jnp.float32)
    for k in range(TOP_K):
        cmb = jnp.where(col == lpos[:, k:k + 1], wt[:, k:k + 1], cmb)
    hi = cmb.astype(jnp.bfloat16)
    lo = (cmb - hi.astype(jnp.float32)).astype(jnp.bfloat16)
    y2 = _dot(jnp.concatenate([hi, lo], axis=0), rows)
    y = x1_ref[...] + y2[:T] + y2[T:]
    out_ref[...] = _rms(y, g_ref[...])


def _moe_combine(ys, ptab, lpos_t, wts, x1, final_g):
    n = x1.shape[0]
    T = MOE_TILE
    n_steps = n // T
    smem_blk = lambda shift: pl.BlockSpec(
        (None, 1, MOE_PIECE_TAB), lambda i: (jnp.minimum(i + shift, n_steps - 1), 0, 0),
        memory_space=pltpu.SMEM)

    def kernel(ptab_ref, ptab_next_ref, *rest):
        _combine_kernel(ptab_ref.at[0], ptab_next_ref.at[0], *rest)

    return pl.pallas_call(
        kernel,
        grid=(n_steps,),
        in_specs=[smem_blk(0), smem_blk(1),
                  pl.BlockSpec(memory_space=pl.ANY),
                  pl.BlockSpec((T, TOP_K), lambda i: (i, 0)),
                  pl.BlockSpec((T, TOP_K), lambda i: (i, 0)),
                  pl.BlockSpec((T, D_MODEL), lambda i: (i, 0)),
                  pl.BlockSpec((1, D_MODEL), lambda i: (0, 0))],
        out_specs=pl.BlockSpec((T, D_MODEL), lambda i: (i, 0)),
        out_shape=jax.ShapeDtypeStruct((n, D_MODEL), jnp.float32),
        scratch_shapes=[pltpu.VMEM((2, MOE_TILE_ROWS, D_MODEL // 2), jnp.uint32),
                        pltpu.SemaphoreType.DMA((2,))],
        compiler_params=_cparams(1),
        name="moe_combine",
    )(ptab, ptab, ys, lpos_t, wts, x1, final_g)


def kernel(x, mem, attn_norm_g, mem_norm_g, w_in, w_mem_kv, na_rpb, t5_rel_bias, w_branch_a, w_branch_b,
           w_branch_c, w_out, ffn_norm_g, router_w, router_b, expert_w_gate_up, expert_b_gate_up,
           expert_w_down, expert_b_down, final_norm_g):
    B, S, D = x.shape
    depth = w_in.shape[0]
    x2d = x.reshape(B * S, D)
    mem2d = mem.reshape(B * mem.shape[1], D)
    assert depth == 1, "single-layer block (per-layer parameters carry a leading axis of 1)"
    l = 0
    na_bias, dil_bias = _bias_prep(na_rpb[l], t5_rel_bias)
    mkv = _mem_kv(mem2d, mem_norm_g[l][None], w_mem_kv[l])
    na_qkv, d0, d1, d2, qc = _in_proj(x2d, attn_norm_g[l][None], w_in[l], B, S)
    gates = _gate_proj(x2d, attn_norm_g[l][None], w_in[l])
    na_out = _na_attn(na_qkv, na_bias, B, S)
    dil_outs = [_dil_attn(qkv, dil_bias, gi, dil, B, S)
                for gi, (qkv, (_, dil)) in enumerate(zip((d0, d1, d2), DIL_GROUPS))]
    x1, h, topk_idx, topk_w, segs = _mix_out(
        x2d, na_out, dil_outs, qc, gates, mkv, w_branch_a[l], w_branch_b[l], w_branch_c[l], w_out[l],
        ffn_norm_g[l][None], router_w[l].T, router_b[l][:, None], B, S)

    n = B * S
    max_rows = n * TOP_K + (n // MOE_TILE) * N_EXPERTS * (MOE_PIECE - 1)
    n_blocks = -(-max_rows // MOE_BLOCK) + N_EXPERTS
    lpos, ptab, block_expert, pad_ends = _moe_route(topk_idx, segs, n_blocks)
    block_expert = block_expert.reshape(n_blocks)
    pad_ends = pad_ends.reshape(N_EXPERTS)
    xs = _moe_dispatch(h, lpos, ptab, pad_ends, n_blocks)
    ys = _moe_experts(xs, block_expert, pad_ends, expert_w_gate_up[l], expert_b_gate_up[l][:, None, :],
                      expert_w_down[l], expert_b_down[l][:, None, :])
    out = _moe_combine(ys, ptab, lpos.T, topk_w.T, x1, final_norm_g[None])
    return out.reshape(B, S, D)
```

```python
import functools
import math

import numpy as np
import jax
import jax.numpy as jnp
from jax import lax
from jax.experimental import pallas as pl
from jax.experimental.pallas import tpu as pltpu

D_MODEL = 1024
GRID_W = 64
MEM_HEADS = 4
MEM_HEAD_DIM = 128
HEAD_DIM = 64
NA_HEADS = 8
NA_WIN_ROWS = 8
NA_WIN_COLS = 16
DIL_GROUPS = ((128, 1), (512, 4), (2048, 16))
DIL_HEADS_PER_GROUP = 4
DIL_BLOCK = 128
T5_BUCKETS = 32
T5_MAX_DIST = 1024
N_EXPERTS = 32
TOP_K = 4
D_FF = D_MODEL
SWIGLU_ALPHA = 1.702
SWIGLU_LIMIT = 7.0
RMS_EPS = 1e-6
NEG_INF = -1e30

WIDTH_A = NA_HEADS * HEAD_DIM
WIDTH_B = DIL_HEADS_PER_GROUP * len(DIL_GROUPS) * HEAD_DIM
WIDTH_G = DIL_HEADS_PER_GROUP * HEAD_DIM
WIDTH_C = MEM_HEADS * MEM_HEAD_DIM
OFF_QA, OFF_KA, OFF_VA = 0, WIDTH_A, 2 * WIDTH_A
OFF_QB = 3 * WIDTH_A
OFF_KB = OFF_QB + WIDTH_B
OFF_VB = OFF_KB + WIDTH_B
OFF_QC = OFF_VB + WIDTH_B
OFF_GATE = OFF_QC + WIDTH_C
IN_COLS = OFF_GATE + 3 * D_MODEL

LANES = 128
MXU_WIDTH = 256
VMEM_LIMIT = 56 * 1024 * 1024

IN_PROJ_TILE = 1024
PROJ_TILE = 512
NA_ROWS_PER_STEP = 16
DIL_STEP_TOKENS = 2048
MOE_BLOCK = 512
MOE_TILE = 256
MOE_PIECE = 8
MOE_TILE_ROWS = -(-(TOP_K * MOE_TILE + N_EXPERTS * (MOE_PIECE - 1)) // MXU_WIDTH) * MXU_WIDTH
MOE_PIECE_TAB = 256
assert MOE_TILE_ROWS // MOE_PIECE < MOE_PIECE_TAB
MOE_SPARE_BLOCKS = -(-2 * MOE_TILE_ROWS // MOE_BLOCK)


def _cparams(n_axes):
    return pltpu.CompilerParams(
        dimension_semantics=("arbitrary",) * n_axes, vmem_limit_bytes=VMEM_LIMIT)


def _rms(x, g):
    return x * lax.rsqrt(jnp.mean(x * x, axis=-1, keepdims=True) + RMS_EPS) * g


def _sigmoid(x):
    return 1.0 / (1.0 + jnp.exp(-x))


def _pack_bf16_pairs(x):
    w = x.shape[1] // 2
    bits = lax.bitcast_convert_type(x.astype(jnp.bfloat16).astype(jnp.float32), jnp.uint32)
    return (bits[:, w:] & jnp.uint32(0xFFFF0000)) | (bits[:, :w] >> 16)


def _unpack_bf16_pairs(p):
    lo = lax.bitcast_convert_type(p << 16, jnp.float32)
    hi = lax.bitcast_convert_type(p & jnp.uint32(0xFFFF0000), jnp.float32)
    return jnp.concatenate([lo, hi], axis=1).astype(jnp.bfloat16)


def _dot(a, b):
    return jnp.dot(a, b, preferred_element_type=jnp.float32)


def _dot_nt(a, b):
    return lax.dot_general(a, b, (((1,), (1,)), ((), ())), preferred_element_type=jnp.float32)


def _mem_kv_kernel(mem_ref, g_ref, w_ref, out_ref):
    h = _rms(mem_ref[...], g_ref[...]).astype(jnp.bfloat16)
    out_ref[...] = _dot(h, w_ref[...]).astype(jnp.bfloat16)


def _mem_kv(mem2d, g, w):
    rows = mem2d.shape[0]
    return pl.pallas_call(
        _mem_kv_kernel,
        out_shape=jax.ShapeDtypeStruct((rows, 2 * WIDTH_C), jnp.bfloat16),
        name="mem_kv",
        compiler_params=pltpu.CompilerParams(vmem_limit_bytes=VMEM_LIMIT),
    )(mem2d, g, w)


def _in_proj_kernel(x_ref, g_ref, w_ref, na_ref, d0_ref, d1_ref, d2_ref, qc_ref, acc_ref):
    T = x_ref.shape[0]
    h = _rms(x_ref[...], g_ref[...]).astype(jnp.bfloat16)
    q_scale = HEAD_DIM ** -0.5

    for c in range(3):
        acc = _dot(h, w_ref[:, c * WIDTH_A:(c + 1) * WIDTH_A])
        if c == 0:
            acc = acc * q_scale
        na_ref[:, c * WIDTH_A:(c + 1) * WIDTH_A] = acc.astype(jnp.bfloat16)

    for gi, ((_, dil), d_ref) in enumerate(zip(DIL_GROUPS, (d0_ref, d1_ref, d2_ref))):
        for kind, off in enumerate((OFF_QB, OFF_KB, OFF_VB)):
            lo = off + gi * WIDTH_G
            acc = _dot(h, w_ref[:, lo:lo + WIDTH_G])
            if kind == 0:
                acc = acc * q_scale
            cols = slice(kind * WIDTH_G, (kind + 1) * WIDTH_G)
            if dil == 1:
                d_ref[0, 0, :, cols] = acc.astype(jnp.bfloat16)
            else:
                for c in range(WIDTH_G // LANES):
                    acc_ref[c] = acc[:, c * LANES:(c + 1) * LANES]
                for rho in range(dil):
                    d_ref[0, rho, :, cols] = jnp.concatenate(
                        [acc_ref[c, pl.ds(rho, T // dil, stride=dil), :] for c in range(WIDTH_G // LANES)],
                        axis=-1).astype(jnp.bfloat16)

    qc_ref[...] = _dot(h, w_ref[:, OFF_QC:OFF_QC + WIDTH_C]).astype(jnp.bfloat16)


def _gate_proj_kernel(x_ref, g_ref, w_ref, gate_ref):
    h = _rms(x_ref[...], g_ref[...]).astype(jnp.bfloat16)
    chunk = 512
    first = OFF_GATE - IN_COLS // 2
    for c in range(3 * D_MODEL // chunk):
        lo = first + c * chunk
        gate_ref[:, c * chunk:(c + 1) * chunk] = _sigmoid(_dot(h, w_ref[:, lo:lo + chunk])).astype(jnp.bfloat16)


def _gate_proj(x2d, g, w_bf16):
    n = x2d.shape[0]
    T = IN_PROJ_TILE
    assert (IN_COLS // 2) % LANES == 0 and IN_COLS // 2 <= OFF_GATE
    return pl.pallas_call(
        _gate_proj_kernel,
        grid=(n // T,),
        in_specs=[pl.BlockSpec((T, D_MODEL), lambda i: (i, 0)),
                  pl.BlockSpec((1, D_MODEL), lambda i: (0, 0)),
                  pl.BlockSpec((D_MODEL, IN_COLS // 2), lambda i: (0, 1), pipeline_mode=pl.Buffered(1))],
        out_specs=pl.BlockSpec((T, 3 * D_MODEL), lambda i: (i, 0)),
        out_shape=jax.ShapeDtypeStruct((n, 3 * D_MODEL), jnp.bfloat16),
        compiler_params=_cparams(1),
        name="gate_proj",
    )(x2d, g, w_bf16)


def _in_proj(x2d, g, w_bf16, batch, seq):
    n = x2d.shape[0]
    T = IN_PROJ_TILE
    tiles_per_batch = seq // T
    out_shape = [jax.ShapeDtypeStruct((n, 3 * WIDTH_A), jnp.bfloat16)]
    out_specs = [pl.BlockSpec((T, 3 * WIDTH_A), lambda i: (i, 0))]
    for _, dil in DIL_GROUPS:
        out_shape.append(jax.ShapeDtypeStruct((batch, dil, seq // dil, 3 * WIDTH_G), jnp.bfloat16))
        out_specs.append(pl.BlockSpec((1, dil, T // dil, 3 * WIDTH_G),
                                      lambda i: (i // tiles_per_batch, 0, i % tiles_per_batch, 0)))
    out_shape += [jax.ShapeDtypeStruct((n, WIDTH_C), jnp.bfloat16)]
    out_specs += [pl.BlockSpec((T, WIDTH_C), lambda i: (i, 0))]
    return pl.pallas_call(
        _in_proj_kernel,
        grid=(n // T,),
        in_specs=[pl.BlockSpec((T, D_MODEL), lambda i: (i, 0)),
                  pl.BlockSpec((1, D_MODEL), lambda i: (0, 0)),
                  pl.BlockSpec((D_MODEL, OFF_GATE), lambda i: (0, 0), pipeline_mode=pl.Buffered(1))],
        out_specs=out_specs,
        out_shape=out_shape,
        scratch_shapes=[pltpu.VMEM((WIDTH_G // LANES, T, LANES), jnp.float32)],
        compiler_params=_cparams(1),
        name="in_proj",
    )(x2d, g, w_bf16)


def _na_kernel(q_ref, k_ref, v_ref, bias_ref, out_ref, *, rows):
    i = pl.program_id(1)
    lane = lax.broadcasted_iota(jnp.int32, (GRID_W, LANES), 1)
    low_half = lane < HEAD_DIM
    n_win = NA_WIN_ROWS * GRID_W

    def row_body(j, carry):
        qr = i * NA_ROWS_PER_STEP + j
        rs = jnp.clip(qr - NA_WIN_ROWS // 2, 0, rows - NA_WIN_ROWS)
        shift = rs - qr + NA_WIN_ROWS - 1
        q_off = pl.multiple_of(j * GRID_W, GRID_W)
        k_off = pl.multiple_of(rs * GRID_W, GRID_W)
        scores = []
        for hp in range(NA_HEADS // 2):
            cols = slice(hp * LANES, (hp + 1) * LANES)
            qp = q_ref[pl.ds(q_off, GRID_W), cols]
            kp = k_ref[pl.ds(k_off, n_win), cols]
            zero = jnp.zeros_like(qp)
            q2 = jnp.concatenate([jnp.where(low_half, qp, zero), jnp.where(low_half, zero, qp)], axis=0)
            bias = jnp.concatenate(
                [jnp.concatenate([bias_ref[shift + 2 * p, 2 * hp + hh] for p in range(NA_WIN_ROWS // 2)], axis=-1)
                 for hh in range(2)], axis=0)
            scores.append(_dot_nt(q2, kp) + bias)
        probs = []
        for s in scores:
            e = jnp.exp(s - jnp.max(s, axis=-1, keepdims=True))
            probs.append((e.astype(jnp.bfloat16), jnp.sum(e, axis=-1, keepdims=True)))
        for hp, (p, l) in enumerate(probs):
            cols = slice(hp * LANES, (hp + 1) * LANES)
            o2 = _dot(p, v_ref[pl.ds(k_off, n_win), cols]) / l
            out_ref[pl.ds(q_off, GRID_W), cols] = jnp.where(
                low_half, o2[:GRID_W], o2[GRID_W:]).astype(jnp.bfloat16)
        return carry

    lax.fori_loop(0, NA_ROWS_PER_STEP, row_body, 0, unroll=4)


def _na_attn(na_qkv, bias_tab, batch, seq):
    rows = seq // GRID_W
    steps = rows // NA_ROWS_PER_STEP
    tq = NA_ROWS_PER_STEP * GRID_W
    n = na_qkv.shape[0]
    return pl.pallas_call(
        functools.partial(_na_kernel, rows=rows),
        grid=(batch, steps),
        in_specs=[pl.BlockSpec((tq, WIDTH_A), lambda b, i: (b * steps + i, 0)),
                  pl.BlockSpec((seq, WIDTH_A), lambda b, i: (b, 1)),
                  pl.BlockSpec((seq, WIDTH_A), lambda b, i: (b, 2)),
                  pl.BlockSpec(bias_tab.shape, lambda b, i: (0, 0, 0, 0))],
        out_specs=pl.BlockSpec((tq, WIDTH_A), lambda b, i: (b * steps + i, 0)),
        out_shape=jax.ShapeDtypeStruct((n, WIDTH_A), jnp.bfloat16),
        compiler_params=_cparams(2),
        name="na_attn",
    )(na_qkv, na_qkv, na_qkv, bias_tab)


def _t5_bucket_np(rel):
    half = T5_BUCKETS // 2
    max_exact = half // 2
    ret = np.where(rel > 0, half, 0)
    n = np.abs(rel)
    nf = np.maximum(n, 1).astype(np.float32)
    large = max_exact + (np.log(nf / np.float32(max_exact)) / np.float32(math.log(T5_MAX_DIST / max_exact))
                         * np.float32(half - max_exact)).astype(np.int32)
    large = np.minimum(large, half - 1)
    return ret + np.where(n < max_exact, n, large)


def _dil_bucket_table():
    q = np.arange(DIL_BLOCK)[:, None]
    j = np.arange(2 * DIL_BLOCK)[None, :]
    tabs = []
    for window, dil in DIL_GROUPS:
        half = (window // 2) // dil
        assert half == DIL_BLOCK // 2
        rel = (j - half) - q
        tabs.append(np.where(np.abs(rel) <= half, _t5_bucket_np(rel * dil), -1))
    return np.stack(tabs).astype(np.int32)


def _bias_prep_kernel(rpb_ref, t5_ref, bucket_ref, na_ref, dil_ref):
    n_dc = 2 * NA_WIN_COLS - 1
    n_dr = 2 * NA_WIN_ROWS - 1
    qc = lax.broadcasted_iota(jnp.int32, (GRID_W, 2 * GRID_W), 0)
    lane = lax.broadcasted_iota(jnp.int32, (GRID_W, 2 * GRID_W), 1)
    second = lane >= GRID_W
    kc = jnp.where(second, lane - GRID_W, lane)
    diff = kc - qc
    cs = jnp.clip(qc - NA_WIN_COLS // 2, 0, GRID_W - NA_WIN_COLS)
    valid = jnp.logical_and(kc >= cs, kc < cs + NA_WIN_COLS)

    def na_tile(mh, carry):
        m = mh // NA_HEADS
        h = mh - m * NA_HEADS
        base = (h * n_dr + m) * n_dc
        acc = jnp.zeros((GRID_W, 2 * GRID_W), jnp.float32)
        for d in range(n_dc):
            v = jnp.where(second, rpb_ref[base + n_dc + d], rpb_ref[base + d])
            acc = jnp.where(diff == d - (NA_WIN_COLS - 1), v, acc)
        na_ref[m, h] = jnp.where(valid, acc, NEG_INF)
        return carry

    lax.fori_loop(0, (n_dr - 1) * NA_HEADS, na_tile, 0)

    j = lax.broadcasted_iota(jnp.int32, (DIL_BLOCK, 2 * DIL_BLOCK), 1)
    half = DIL_BLOCK // 2
    n_heads = DIL_HEADS_PER_GROUP * len(DIL_GROUPS)
    for gi in range(len(DIL_GROUPS)):
        bucket = bucket_ref[gi]
        in_win = bucket >= 0
        for h in range(DIL_HEADS_PER_GROUP):
            acc = jnp.zeros((DIL_BLOCK, 2 * DIL_BLOCK), jnp.float32)
            for b in range(T5_BUCKETS):
                acc = jnp.where(bucket == b, t5_ref[b * n_heads + gi * DIL_HEADS_PER_GROUP + h], acc)
            for v, ok in enumerate((jnp.logical_and(in_win, j >= half), in_win,
                                    jnp.logical_and(in_win, j < 2 * DIL_BLOCK - half))):
                dil_ref[gi, v, h] = jnp.where(ok, acc, NEG_INF)


def _bias_prep(rpb, t5):
    smem = pl.BlockSpec(memory_space=pltpu.SMEM)
    return pl.pallas_call(
        _bias_prep_kernel,
        in_specs=[smem, smem, pl.BlockSpec(memory_space=pltpu.VMEM)],
        out_shape=[jax.ShapeDtypeStruct((2 * NA_WIN_ROWS - 2, NA_HEADS, GRID_W, 2 * GRID_W), jnp.float32),
                   jax.ShapeDtypeStruct((len(DIL_GROUPS), 3, DIL_HEADS_PER_GROUP, DIL_BLOCK, 2 * DIL_BLOCK),
                                        jnp.float32)],
        compiler_params=pltpu.CompilerParams(vmem_limit_bytes=VMEM_LIMIT),
        name="bias_prep",
    )(rpb.reshape(-1), t5.reshape(-1), jnp.asarray(_dil_bucket_table()))


def _dil_kernel(q_ref, kp_ref, kc_ref, kn_ref, vp_ref, vc_ref, vn_ref, bias_ref, o_ref, lse_ref, *, dil):
    lane = lax.broadcasted_iota(jnp.int32, (DIL_BLOCK, LANES), 1)
    low_half = lane < HEAD_DIM
    hb = DIL_BLOCK // 2

    span = pl.program_id(1)
    n_spans = pl.num_programs(1)
    nbs = kc_ref.shape[2] // DIL_BLOCK

    def window(prev_ref, cur_ref, next_ref, rho, blk, cols):
        lo = blk * DIL_BLOCK
        first = prev_ref[0, rho, hb:, cols] if blk == 0 else cur_ref[0, rho, lo - hb:lo, cols]
        last = (next_ref[0, rho, :hb, cols] if blk == nbs - 1
                else cur_ref[0, rho, lo + DIL_BLOCK:lo + DIL_BLOCK + hb, cols])
        return jnp.concatenate([first, cur_ref[0, rho, lo:lo + DIL_BLOCK, cols], last], axis=0)

    def unit(rho, blk):
        variant = 1
        if blk == nbs - 1:
            variant = jnp.where(span == n_spans - 1, 2, variant)
        if blk == 0:
            variant = jnp.where(span == 0, 0, variant)
        scores = []
        for hp in range(DIL_HEADS_PER_GROUP // 2):
            cols = slice(hp * LANES, (hp + 1) * LANES)
            qp = q_ref[0, rho, blk * DIL_BLOCK:(blk + 1) * DIL_BLOCK, cols]
            zero = jnp.zeros_like(qp)
            q2 = jnp.concatenate([jnp.where(low_half, qp, zero), jnp.where(low_half, zero, qp)], axis=0)
            kw = window(kp_ref, kc_ref, kn_ref, rho, blk, cols)
            bias = jnp.concatenate([bias_ref[variant, 2 * hp], bias_ref[variant, 2 * hp + 1]], axis=0)
            scores.append(_dot_nt(q2, kw) + bias)
        probs = []
        for s in scores:
            m = jnp.max(s, axis=-1, keepdims=True)
            e = jnp.exp(s - m)
            probs.append((e.astype(jnp.bfloat16), m, jnp.sum(e, axis=-1, keepdims=True)))
        for hp, (p, m, l) in enumerate(probs):
            cols = slice(hp * LANES, (hp + 1) * LANES)
            o2 = _dot(p, window(vp_ref, vc_ref, vn_ref, rho, blk, cols)) / l
            lse2 = jnp.broadcast_to(m + jnp.log(l), (2 * DIL_BLOCK, LANES))
            o_pair = jnp.where(low_half, o2[:DIL_BLOCK], o2[DIL_BLOCK:])
            lse_pair = jnp.where(low_half, lse2[:DIL_BLOCK], lse2[DIL_BLOCK:])
            if dil == 1:
                rows = slice(blk * DIL_BLOCK, (blk + 1) * DIL_BLOCK)
                o_ref[hp, rows, :] = o_pair
                lse_ref[hp, rows, :] = lse_pair
            else:
                rows = pl.ds(blk * DIL_BLOCK * dil + rho, DIL_BLOCK, stride=dil)
                o_ref[hp, rows, :] = o_pair
                lse_ref[hp, rows, :] = lse_pair

    def class_body(rho, carry):
        for blk in range(nbs):
            unit(rho, blk)
        return carry

    if dil == 1:
        class_body(0, 0)
    else:
        lax.fori_loop(0, dil, class_body, 0, unroll=4 if nbs == 1 else 2)


def _dil_attn(qkv, bias_tab, group, dil, batch, seq):
    L = seq // dil
    nb = L // DIL_BLOCK
    assert nb >= 2
    tq = DIL_STEP_TOKENS
    span = tq // dil
    nbs = span // DIL_BLOCK
    n_spans = L // span

    def cur(col):
        return pl.BlockSpec((1, dil, span, WIDTH_G), lambda b, n: (b, 0, n, col))

    def edge(col, shift):
        return pl.BlockSpec((1, dil, DIL_BLOCK, WIDTH_G),
                            lambda b, n: (b, 0, jnp.clip(n * nbs + (shift if shift < 0 else nbs), 0, nb - 1), col))

    n_pairs = WIDTH_G // LANES
    out_spec = pl.BlockSpec((n_pairs, tq, LANES), lambda b, n: (0, b * n_spans + n, 0))
    return pl.pallas_call(
        functools.partial(_dil_kernel, dil=dil),
        grid=(batch, n_spans),
        in_specs=[cur(0), edge(1, -1), cur(1), edge(1, 1), edge(2, -1), cur(2), edge(2, 1),
                  pl.BlockSpec((None, 3, DIL_HEADS_PER_GROUP, DIL_BLOCK, 2 * DIL_BLOCK),
                               lambda b, n: (group, 0, 0, 0, 0))],
        out_specs=[out_spec, out_spec],
        out_shape=[jax.ShapeDtypeStruct((n_pairs, batch * seq, LANES), jnp.float32)] * 2,
        compiler_params=_cparams(2),
        name=f"dil_attn_d{dil}",
    )(qkv, qkv, qkv, qkv, qkv, qkv, qkv, bias_tab)


def _mix_kernel(x_ref, na_ref, o0_ref, l0_ref, o1_ref, l1_ref, o2_ref, l2_ref, qc_ref, gate_ref,
                mkv_ref, wa_ref, wb_ref, wc_ref, wo_ref, fg_ref, rw_ref, rb_ref,
                x1_ref, h_ref, idx_ref, wt_ref, seg_ref):
    T = x_ref.shape[0]
    ob = []
    for hp in range(WIDTH_G // LANES):
        l0, l1, l2 = l0_ref[hp], l1_ref[hp], l2_ref[hp]
        m = jnp.maximum(jnp.maximum(l0, l1), l2)
        e0, e1, e2 = jnp.exp(l0 - m), jnp.exp(l1 - m), jnp.exp(l2 - m)
        ob.append((e0 * o0_ref[hp] + e1 * o1_ref[hp] + e2 * o2_ref[hp]) / (e0 + e1 + e2))
    ob = jnp.concatenate(ob, axis=-1)

    mem_scale = MEM_HEAD_DIM ** -0.5
    scores = []
    for h in range(MEM_HEADS):
        cols = slice(h * MEM_HEAD_DIM, (h + 1) * MEM_HEAD_DIM)
        scores.append(_dot_nt(qc_ref[:, cols], mkv_ref[:, cols]) * mem_scale)
    probs = []
    for s in scores:
        e = jnp.exp(s - jnp.max(s, axis=-1, keepdims=True))
        probs.append((e.astype(jnp.bfloat16), jnp.sum(e, axis=-1, keepdims=True)))
    oc = []
    for h, (p, l) in enumerate(probs):
        oc.append(_dot(p, mkv_ref[:, WIDTH_C + h * MEM_HEAD_DIM:WIDTH_C + (h + 1) * MEM_HEAD_DIM]) / l)
    oc = jnp.concatenate(oc, axis=-1).astype(jnp.bfloat16)

    y_a = _dot(na_ref[...], wa_ref[...])
    y_b = _dot(ob.astype(jnp.bfloat16), wb_ref[...])
    y_c = _dot(oc, wc_ref[...])
    merged = (gate_ref[:, 0:D_MODEL].astype(jnp.float32) * y_a
              + gate_ref[:, D_MODEL:2 * D_MODEL].astype(jnp.float32) * y_b
              + gate_ref[:, 2 * D_MODEL:3 * D_MODEL].astype(jnp.float32) * y_c)
    x1 = x_ref[...] + _dot(merged.astype(jnp.bfloat16), wo_ref[...])
    x1_ref[...] = x1

    h = _rms(x1, fg_ref[...])
    h_ref[...] = h.astype(jnp.bfloat16)
    logits = lax.dot_general(rw_ref[...], h, (((1,), (1,)), ((), ())), preferred_element_type=jnp.float32,
                             precision=lax.Precision.HIGHEST) + rb_ref[...]
    expert = lax.broadcasted_iota(jnp.int32, (N_EXPERTS, T), 0)
    vals, idxs = [], []
    for _ in range(TOP_K):
        mx = jnp.max(logits, axis=0, keepdims=True)
        sel = jnp.min(jnp.where(logits == mx, expert, N_EXPERTS), axis=0, keepdims=True)
        vals.append(mx)
        idxs.append(sel)
        logits = jnp.where(expert == sel, -jnp.inf, logits)
    ev = [jnp.exp(v - vals[0]) for v in vals]
    den = ev[0] + ev[1] + ev[2] + ev[3]
    idx_ref[...] = jnp.concatenate(idxs, axis=0)
    wt_ref[...] = jnp.concatenate([e / den for e in ev], axis=0)
    member = sum((expert == sel).astype(jnp.float32) for sel in idxs)
    for s in range(T // MOE_TILE):
        cnt = jnp.sum(member[:, s * MOE_TILE:(s + 1) * MOE_TILE], axis=1, keepdims=True)
        seg = jnp.floor((cnt + (MOE_PIECE - 1)) * (1.0 / MOE_PIECE)) * MOE_PIECE
        seg_ref[s] = jnp.broadcast_to(seg, (N_EXPERTS, LANES))


def _mix_out(x2d, na_out, dil_outs, qc, gates, mkv, wa, wb, wc, wo, ffn_g, router_w, router_b, batch, seq):
    n = x2d.shape[0]
    T = PROJ_TILE
    tiles_per_batch = seq // T
    mem_len = mkv.shape[0] // batch
    row = lambda w: pl.BlockSpec((T, w), lambda i: (i, 0))
    full = lambda a: pl.BlockSpec(a.shape, lambda i: (0,) * a.ndim, pipeline_mode=pl.Buffered(1))
    in_specs = [row(D_MODEL), row(WIDTH_A)]
    args = [x2d, na_out]
    for o, l in dil_outs:
        pair_rows = pl.BlockSpec((WIDTH_G // LANES, T, LANES), lambda i: (0, i, 0))
        in_specs += [pair_rows, pair_rows]
        args += [o, l]
    in_specs += [row(WIDTH_C), row(3 * D_MODEL),
                 pl.BlockSpec((mem_len, 2 * WIDTH_C), lambda i: (i // tiles_per_batch, 0)),
                 full(wa), full(wb), full(wc), full(wo), full(ffn_g), full(router_w), full(router_b)]
    args += [qc, gates, mkv, wa, wb, wc, wo, ffn_g, router_w, router_b]
    return pl.pallas_call(
        _mix_kernel,
        grid=(n // T,),
        in_specs=in_specs,
        out_specs=[row(D_MODEL), row(D_MODEL), pl.BlockSpec((TOP_K, T), lambda i: (0, i)),
                   pl.BlockSpec((TOP_K, T), lambda i: (0, i)),
                   pl.BlockSpec((T // MOE_TILE, N_EXPERTS, LANES), lambda i: (i, 0, 0))],
        out_shape=[jax.ShapeDtypeStruct((n, D_MODEL), jnp.float32),
                   jax.ShapeDtypeStruct((n, D_MODEL), jnp.bfloat16),
                   jax.ShapeDtypeStruct((TOP_K, n), jnp.int32),
                   jax.ShapeDtypeStruct((TOP_K, n), jnp.float32),
                   jax.ShapeDtypeStruct((n // MOE_TILE, N_EXPERTS, LANES), jnp.float32)],
        compiler_params=_cparams(1),
        name="mix_out",
    )(*args)


def _route_kernel(idx_ref, segs_ref, lpos_ref, ptab_ref, be_ref, pe_ref, carry_ref, pstart_ref):
    i = pl.program_id(0)
    T = idx_ref.shape[1]
    n_blocks = be_ref.shape[1]
    expert = lax.broadcasted_iota(jnp.int32, (N_EXPERTS, T), 0)
    idx = idx_ref[...]
    onehots = [expert == idx[k:k + 1, :] for k in range(TOP_K)]
    member = sum(o.astype(jnp.float32) for o in onehots)
    seg_b = segs_ref[i]
    seg = seg_b[:, 0:1]
    r = lax.broadcasted_iota(jnp.int32, (N_EXPERTS, N_EXPERTS), 0)
    c = lax.broadcasted_iota(jnp.int32, (N_EXPERTS, N_EXPERTS), 1)
    lower_incl = (r >= c).astype(jnp.float32)

    def cumsum_experts(v):
        return jnp.dot(lower_incl, v, preferred_element_type=jnp.float32, precision=lax.Precision.HIGHEST)

    @pl.when(i == 0)
    def _():
        counts = jnp.sum(segs_ref[...], axis=0)
        padded = jnp.floor((counts + (MOE_BLOCK - 1)) * (1.0 / MOE_BLOCK)) * MOE_BLOCK
        pad_ends = cumsum_experts(padded)
        pstart_ref[...] = pad_ends - padded
        carry_ref[...] = jnp.zeros_like(carry_ref)
        pe_ref[...] = pad_ends[:, 0:1].astype(jnp.int32)
        blk_start = (lax.broadcasted_iota(jnp.int32, (N_EXPERTS, n_blocks), 1) * MOE_BLOCK).astype(jnp.float32)
        be = jnp.sum((pad_ends[:, 0:1] <= blk_start).astype(jnp.int32), axis=0, keepdims=True)
        be_ref[...] = jnp.minimum(be, N_EXPERTS - 1)

    local_end = cumsum_experts(seg_b)[:, 0:1]
    local_start = local_end - seg
    global_start = pstart_ref[:, 0:1] + carry_ref[:, 0:1]
    rr = lax.broadcasted_iota(jnp.int32, (T, T), 0)
    cc = lax.broadcasted_iota(jnp.int32, (T, T), 1)
    earlier = (rr < cc).astype(jnp.bfloat16)
    before = _dot(member.astype(jnp.bfloat16), earlier)
    base = local_start + before
    lpos = [jnp.sum(jnp.where(o, base, 0.0), axis=0, keepdims=True) for o in onehots]
    lpos_ref[...] = jnp.concatenate(lpos, axis=0).astype(jnp.int32)

    inv = 1.0 / MOE_PIECE
    piece = lax.broadcasted_iota(jnp.int32, (N_EXPERTS, MOE_PIECE_TAB), 1).astype(jnp.float32)
    first = local_start * inv
    inside = jnp.logical_and(piece >= first, piece < local_end * inv)
    dst = jnp.sum(jnp.where(inside, global_start * inv + piece - first, 0.0), axis=0, keepdims=True)
    n_pieces = local_end[N_EXPERTS - 1:N_EXPERTS, :] * inv
    spare = (n_blocks * MOE_BLOCK // MOE_PIECE
             + (i % 2) * (MOE_TILE_ROWS // MOE_PIECE)).astype(jnp.float32)
    dst = jnp.where(piece[0:1, :] >= n_pieces, spare + piece[0:1, :], dst)
    dst = jnp.where(piece[0:1, :] == MOE_PIECE_TAB - 1, n_pieces, dst)
    ptab_ref[...] = dst.astype(jnp.int32)
    carry_ref[...] += seg


def _moe_route(topk_idx, segs, n_blocks):
    n = topk_idx.shape[1]
    T = MOE_TILE
    acc = pltpu.VMEM((N_EXPERTS, LANES), jnp.float32)
    return pl.pallas_call(
        _route_kernel,
        grid=(n // T,),
        in_specs=[pl.BlockSpec((TOP_K, T), lambda i: (0, i)),
                  pl.BlockSpec(segs.shape, lambda i: (0, 0, 0))],
        out_specs=[pl.BlockSpec((TOP_K, T), lambda i: (0, i)),
                   pl.BlockSpec((None, 1, MOE_PIECE_TAB), lambda i: (i, 0, 0)),
                   pl.BlockSpec((1, n_blocks), lambda i: (0, 0)),
                   pl.BlockSpec((N_EXPERTS, 1), lambda i: (0, 0))],
        out_shape=[jax.ShapeDtypeStruct((TOP_K, n), jnp.int32),
                   jax.ShapeDtypeStruct((n // T, 1, MOE_PIECE_TAB), jnp.int32),
                   jax.ShapeDtypeStruct((1, n_blocks), jnp.int32),
                   jax.ShapeDtypeStruct((N_EXPERTS, 1), jnp.int32)],
        scratch_shapes=[acc, acc],
        compiler_params=_cparams(1),
        name="moe_route",
    )(topk_idx, segs)


def _piece(ref, p):
    start = p * MOE_PIECE if isinstance(p, int) else pl.multiple_of(p * MOE_PIECE, MOE_PIECE)
    return ref.at[pl.ds(start, MOE_PIECE), :]


def _dispatch_kernel(pe_ref, ptab_ref, lpos_ref, h_ref, xs_hbm, zbuf, pbuf, sem, zsem, tsem):
    i = pl.program_id(0)
    T = h_ref.shape[0]
    half = h_ref.shape[1] // 2

    def zero_copy(e):
        end = pe_ref[e]
        start = pl.multiple_of(end - MOE_BLOCK, MOE_BLOCK)
        return pltpu.make_async_copy(zbuf, xs_hbm.at[pl.ds(start, MOE_BLOCK), :], zsem)

    def nonempty(e):
        return pe_ref[e] > (pe_ref[e - 1] if e else 0)

    def block_zero(b, s):
        return pltpu.make_async_copy(
            zbuf, xs_hbm.at[pl.ds(pl.multiple_of(b * MOE_BLOCK, MOE_BLOCK), MOE_BLOCK), :], s)

    first_tail = pe_ref[N_EXPERTS - 1] // MOE_BLOCK
    n_blocks = xs_hbm.shape[0] // MOE_BLOCK
    first_spare = n_blocks - MOE_SPARE_BLOCKS

    @pl.when(i == 0)
    def _():
        zbuf[...] = jnp.zeros_like(zbuf)
        for e in range(N_EXPERTS):
            @pl.when(nonempty(e))
            def _():
                zero_copy(e).start()
        for e in range(N_EXPERTS):
            @pl.when(nonempty(e))
            def _():
                zero_copy(e).wait()

        lax.fori_loop(first_spare, n_blocks, lambda b, c: (block_zero(b, zsem).start(), c)[1], 0)
        lax.fori_loop(first_spare, n_blocks, lambda b, c: (block_zero(b, zsem).wait(), c)[1], 0)
        lax.fori_loop(first_tail, first_spare, lambda b, c: (block_zero(b, tsem).start(), c)[1], 0)

    @pl.when(i == pl.num_programs(0) - 1)
    def _():
        lax.fori_loop(first_tail, first_spare, lambda b, c: (block_zero(b, tsem).wait(), c)[1], 0)

    row = lax.broadcasted_iota(jnp.int32, (MOE_TILE_ROWS, T), 0)
    lpos = lpos_ref[...]
    perm = sum((row == lpos[k:k + 1, :]).astype(jnp.bfloat16) for k in range(TOP_K))
    lo = lax.bitcast_convert_type(_dot(perm, h_ref[:, :half]), jnp.uint32)
    hi = lax.bitcast_convert_type(_dot(perm, h_ref[:, half:]), jnp.uint32)
    slot = i % 2
    pbuf[slot] = hi | (lo >> 16)

    n_tile_pieces = MOE_TILE_ROWS // MOE_PIECE

    def piece_copy(s, p):
        return pltpu.make_async_copy(_piece(pbuf.at[s], p), _piece(xs_hbm, ptab_ref[p]), sem.at[s])

    def wait_all(s):
        for _ in range(n_tile_pieces):
            piece_copy(s, 0).wait()

    for p in range(n_tile_pieces):
        piece_copy(slot, p).start()

    @pl.when(i > 0)
    def _():
        wait_all(1 - slot)

    @pl.when(i == pl.num_programs(0) - 1)
    def _():
        wait_all(slot)


def _moe_dispatch(h, lpos, ptab, pad_ends, n_blocks):
    n, width = h.shape
    T = MOE_TILE
    n_steps = n // T

    def kernel(pe_ref, ptab_ref, *rest):
        _dispatch_kernel(pe_ref, ptab_ref.at[0], *rest)

    grid_spec = pltpu.PrefetchScalarGridSpec(
        num_scalar_prefetch=1,
        grid=(n_steps,),
        in_specs=[pl.BlockSpec((None, 1, MOE_PIECE_TAB), lambda i, pe: (i, 0, 0), memory_space=pltpu.SMEM),
                  pl.BlockSpec((TOP_K, T), lambda i, pe: (0, i)),
                  pl.BlockSpec((T, width), lambda i, pe: (i, 0))],
        out_specs=pl.BlockSpec(memory_space=pl.ANY),
        scratch_shapes=[pltpu.VMEM((MOE_BLOCK, width // 2), jnp.uint32),
                        pltpu.VMEM((2, MOE_TILE_ROWS, width // 2), jnp.uint32),
                        pltpu.SemaphoreType.DMA((2,)),
                        pltpu.SemaphoreType.DMA(()),
                        pltpu.SemaphoreType.DMA(())],
    )
    return pl.pallas_call(
        kernel,
        grid_spec=grid_spec,
        out_shape=jax.ShapeDtypeStruct(((n_blocks + MOE_SPARE_BLOCKS) * MOE_BLOCK, width // 2), jnp.uint32),
        compiler_params=_cparams(1),
        name="moe_dispatch",
    )(pad_ends, ptab, lpos, h)


def _moe_kernel(be_ref, pe_ref, xs_ref, wgu_ref, bgu_ref, wdn_ref, bdn_ref, y_ref):
    del be_ref
    i = pl.program_id(0)
    n_valid = pe_ref[N_EXPERTS - 1] // MOE_BLOCK

    @pl.when(i < n_valid)
    def _():
        x = _unpack_bf16_pairs(xs_ref[...])
        gu = _dot(x, wgu_ref[0]) + bgu_ref[0]
        g = jnp.minimum(gu[:, :D_FF], SWIGLU_LIMIT)
        u = jnp.clip(gu[:, D_FF:], -SWIGLU_LIMIT, SWIGLU_LIMIT)
        a = (u + 1.0) * (g * _sigmoid(SWIGLU_ALPHA * g))
        y_ref[...] = _pack_bf16_pairs(_dot(a.astype(jnp.bfloat16), wdn_ref[0]) + bdn_ref[0])

    @pl.when(i >= n_valid)
    def _():
        y_ref[...] = jnp.zeros_like(y_ref)


def _moe_experts(xs, block_expert, pad_ends, wgu, bgu, wdn, bdn):
    n_blocks = block_expert.shape[0]

    def row_blk(i, be, pe):
        return (jnp.minimum(i, jnp.maximum(pe[N_EXPERTS - 1] // MOE_BLOCK - 1, 0)), 0)

    grid_spec = pltpu.PrefetchScalarGridSpec(
        num_scalar_prefetch=2,
        grid=(n_blocks,),
        in_specs=[pl.BlockSpec((MOE_BLOCK, D_MODEL // 2), row_blk),
                  pl.BlockSpec((1, D_MODEL, 2 * D_FF), lambda i, be, pe: (be[i], 0, 0)),
                  pl.BlockSpec((1, 1, 2 * D_FF), lambda i, be, pe: (be[i], 0, 0)),
                  pl.BlockSpec((1, D_FF, D_MODEL), lambda i, be, pe: (be[i], 0, 0)),
                  pl.BlockSpec((1, 1, D_MODEL), lambda i, be, pe: (be[i], 0, 0))],
        out_specs=pl.BlockSpec((MOE_BLOCK, D_MODEL // 2), lambda i, be, pe: (i, 0)),
    )
    return pl.pallas_call(
        _moe_kernel,
        grid_spec=grid_spec,
        out_shape=jax.ShapeDtypeStruct((n_blocks * MOE_BLOCK, D_MODEL // 2), jnp.uint32),
        compiler_params=_cparams(1),
        name="moe_experts",
    )(block_expert, pad_ends, xs, wgu, bgu, wdn, bdn)


def _gather_pieces(src_hbm, ptab_ref, dst_ref, sem):
    spare = src_hbm.shape[0] // MOE_PIECE

    for p in range(MOE_TILE_ROWS // MOE_PIECE):
        src = ptab_ref[p]
        src = jnp.where(src >= spare, 0, src)
        pltpu.make_async_copy(_piece(src_hbm, src), _piece(dst_ref, p), sem).start()


def _wait_pieces(src_hbm, dst_ref, sem):
    for _ in range(MOE_TILE_ROWS // MOE_PIECE):
        pltpu.make_async_copy(_piece(src_hbm, 0), _piece(dst_ref, 0), sem).wait()


def _combine_kernel(ptab_ref, ptab_next_ref, ys_hbm, lpos_ref, wt_ref, x1_ref, g_ref, out_ref, gbuf, sem):
    i = pl.program_id(0)
    n_steps = pl.num_programs(0)
    slot = i % 2
    T = x1_ref.shape[0]

    @pl.when(i == 0)
    def _():
        _gather_pieces(ys_hbm, ptab_ref, gbuf.at[0], sem.at[0])

    @pl.when(i + 1 < n_steps)
    def _():
        _gather_pieces(ys_hbm, ptab_next_ref, gbuf.at[1 - slot], sem.at[1 - slot])

    _wait_pieces(ys_hbm, gbuf.at[slot], sem.at[slot])
    rows = _unpack_bf16_pairs(gbuf[slot])
    col = lax.broadcasted_iota(jnp.int32, (T, MOE_TILE_ROWS), 1)
    lpos = lpos_ref[...]
    wt = wt_ref[...]
    cmb = jnp.zeros((T, MOE_TILE_ROWS), jnp.float32)
    for k in range(TOP_K):
        cmb = jnp.where(col == lpos[:, k:k + 1], wt[:, k:k + 1], cmb)
    hi = cmb.astype(jnp.bfloat16)
    lo = (cmb - hi.astype(jnp.float32)).astype(jnp.bfloat16)
    y2 = _dot(jnp.concatenate([hi, lo], axis=0), rows)
    y = x1_ref[...] + y2[:T] + y2[T:]
    out_ref[...] = _rms(y, g_ref[...])


def _moe_combine(ys, ptab, lpos_t, wts, x1, final_g):
    n = x1.shape[0]
    T = MOE_TILE
    n_steps = n // T
    smem_blk = lambda shift: pl.BlockSpec(
        (None, 1, MOE_PIECE_TAB), lambda i: (jnp.minimum(i + shift, n_steps - 1), 0, 0),
        memory_space=pltpu.SMEM)

    def kernel(ptab_ref, ptab_next_ref, *rest):
        _combine_kernel(ptab_ref.at[0], ptab_next_ref.at[0], *rest)

    return pl.pallas_call(
        kernel,
        grid=(n_steps,),
        in_specs=[smem_blk(0), smem_blk(1),
                  pl.BlockSpec(memory_space=pl.ANY),
                  pl.BlockSpec((T, TOP_K), lambda i: (i, 0)),
                  pl.BlockSpec((T, TOP_K), lambda i: (i, 0)),
                  pl.BlockSpec((T, D_MODEL), lambda i: (i, 0)),
                  pl.BlockSpec((1, D_MODEL), lambda i: (0, 0))],
        out_specs=pl.BlockSpec((T, D_MODEL), lambda i: (i, 0)),
        out_shape=jax.ShapeDtypeStruct((n, D_MODEL), jnp.float32),
        scratch_shapes=[pltpu.VMEM((2, MOE_TILE_ROWS, D_MODEL // 2), jnp.uint32),
                        pltpu.SemaphoreType.DMA((2,))],
        compiler_params=_cparams(1),
        name="moe_combine",
    )(ptab, ptab, ys, lpos_t, wts, x1, final_g)


def kernel(x, mem, attn_norm_g, mem_norm_g, w_in, w_mem_kv, na_rpb, t5_rel_bias, w_branch_a, w_branch_b,
           w_branch_c, w_out, ffn_norm_g, router_w, router_b, expert_w_gate_up, expert_b_gate_up,
           expert_w_down, expert_b_down, final_norm_g):
    B, S, D = x.shape
    depth = w_in.shape[0]
    x2d = x.reshape(B * S, D)
    mem2d = mem.reshape(B * mem.shape[1], D)
    assert depth == 1, "single-layer block (per-layer parameters carry a leading axis of 1)"
    l = 0
    na_bias, dil_bias = _bias_prep(na_rpb[l], t5_rel_bias)
    mkv = _mem_kv(mem2d, mem_norm_g[l][None], w_mem_kv[l])
    na_qkv, d0, d1, d2, qc = _in_proj(x2d, attn_norm_g[l][None], w_in[l], B, S)
    gates = _gate_proj(x2d, attn_norm_g[l][None], w_in[l])
    na_out = _na_attn(na_qkv, na_bias, B, S)
    dil_outs = [_dil_attn(qkv, dil_bias, gi, dil, B, S)
                for gi, (qkv, (_, dil)) in enumerate(zip((d0, d1, d2), DIL_GROUPS))]
    x1, h, topk_idx, topk_w, segs = _mix_out(
        x2d, na_out, dil_outs, qc, gates, mkv, w_branch_a[l], w_branch_b[l], w_branch_c[l], w_out[l],
        ffn_norm_g[l][None], router_w[l].T, router_b[l][:, None], B, S)

    n = B * S
    max_rows = n * TOP_K + (n // MOE_TILE) * N_EXPERTS * (MOE_PIECE - 1)
    n_blocks = -(-max_rows // MOE_BLOCK) + N_EXPERTS
    lpos, ptab, block_expert, pad_ends = _moe_route(topk_idx, segs, n_blocks)
    block_expert = block_expert.reshape(n_blocks)
    pad_ends = pad_ends.reshape(N_EXPERTS)
    xs = _moe_dispatch(h, lpos, ptab, pad_ends, n_blocks)
    ys = _moe_experts(xs, block_expert, pad_ends, expert_w_gate_up[l], expert_b_gate_up[l][:, None, :],
                      expert_w_down[l], expert_b_down[l][:, None, :])
    out = _moe_combine(ys, ptab, lpos.T, topk_w.T, x1, final_norm_g[None])
    return out.reshape(B, S, D)
```

```python
import functools
import math

import numpy as np
import jax
import jax.numpy as jnp
from jax import lax
from jax.experimental import pallas as pl
from jax.experimental.pallas import tpu as pltpu

D_MODEL = 1024
GRID_W = 64
MEM_HEADS = 4
MEM_HEAD_DIM = 128
HEAD_DIM = 64
NA_HEADS = 8
NA_WIN_ROWS = 8
NA_WIN_COLS = 16
DIL_GROUPS = ((128, 1), (512, 4), (2048, 16))
DIL_HEADS_PER_GROUP = 4
DIL_BLOCK = 128
T5_BUCKETS = 32
T5_MAX_DIST = 1024
N_EXPERTS = 32
TOP_K = 4
D_FF = D_MODEL
SWIGLU_ALPHA = 1.702
SWIGLU_LIMIT = 7.0
RMS_EPS = 1e-6
NEG_INF = -1e30

WIDTH_A = NA_HEADS * HEAD_DIM
WIDTH_B = DIL_HEADS_PER_GROUP * len(DIL_GROUPS) * HEAD_DIM
WIDTH_G = DIL_HEADS_PER_GROUP * HEAD_DIM
WIDTH_C = MEM_HEADS * MEM_HEAD_DIM
OFF_QA, OFF_KA, OFF_VA = 0, WIDTH_A, 2 * WIDTH_A
OFF_QB = 3 * WIDTH_A
OFF_KB = OFF_QB + WIDTH_B
OFF_VB = OFF_KB + WIDTH_B
OFF_QC = OFF_VB + WIDTH_B
OFF_GATE = OFF_QC + WIDTH_C
IN_COLS = OFF_GATE + 3 * D_MODEL

LANES = 128
MXU_WIDTH = 256
VMEM_LIMIT = 56 * 1024 * 1024

IN_PROJ_TILE = 1024
PROJ_TILE = 512
NA_ROWS_PER_STEP = 16
DIL_STEP_TOKENS = 2048
MOE_BLOCK = 512
MOE_TILE = 256
MOE_PIECE = 8
MOE_TILE_ROWS = -(-(TOP_K * MOE_TILE + N_EXPERTS * (MOE_PIECE - 1)) // MXU_WIDTH) * MXU_WIDTH
MOE_PIECE_TAB = 256
assert MOE_TILE_ROWS // MOE_PIECE < MOE_PIECE_TAB
MOE_SPARE_BLOCKS = -(-2 * MOE_TILE_ROWS // MOE_BLOCK)


def _cparams(n_axes):
    return pltpu.CompilerParams(
        dimension_semantics=("arbitrary",) * n_axes, vmem_limit_bytes=VMEM_LIMIT)


def _rms(x, g):
    return x * lax.rsqrt(jnp.mean(x * x, axis=-1, keepdims=True) + RMS_EPS) * g


def _sigmoid(x):
    return 1.0 / (1.0 + jnp.exp(-x))


def _pack_bf16_pairs(x):
    w = x.shape[1] // 2
    bits = lax.bitcast_convert_type(x.astype(jnp.bfloat16).astype(jnp.float32), jnp.uint32)
    return (bits[:, w:] & jnp.uint32(0xFFFF0000)) | (bits[:, :w] >> 16)


def _unpack_bf16_pairs(p):
    lo = lax.bitcast_convert_type(p << 16, jnp.float32)
    hi = lax.bitcast_convert_type(p & jnp.uint32(0xFFFF0000), jnp.float32)
    return jnp.concatenate([lo, hi], axis=1).astype(jnp.bfloat16)


def _dot(a, b):
    return jnp.dot(a, b, preferred_element_type=jnp.float32)


def _dot_nt(a, b):
    return lax.dot_general(a, b, (((1,), (1,)), ((), ())), preferred_element_type=jnp.float32)


def _mem_kv_kernel(mem_ref, g_ref, w_ref, out_ref):
    h = _rms(mem_ref[...], g_ref[...]).astype(jnp.bfloat16)
    out_ref[...] = _dot(h, w_ref[...]).astype(jnp.bfloat16)


def _mem_kv(mem2d, g, w):
    rows = mem2d.shape[0]
    return pl.pallas_call(
        _mem_kv_kernel,
        out_shape=jax.ShapeDtypeStruct((rows, 2 * WIDTH_C), jnp.bfloat16),
        name="mem_kv",
        compiler_params=pltpu.CompilerParams(vmem_limit_bytes=VMEM_LIMIT),
    )(mem2d, g, w)


def _in_proj_kernel(x_ref, g_ref, w_ref, na_ref, d0_ref, d1_ref, d2_ref, qc_ref, acc_ref):
    T = x_ref.shape[0]
    h = _rms(x_ref[...], g_ref[...]).astype(jnp.bfloat16)
    q_scale = HEAD_DIM ** -0.5

    for c in range(3):
        acc = _dot(h, w_ref[:, c * WIDTH_A:(c + 1) * WIDTH_A])
        if c == 0:
            acc = acc * q_scale
        na_ref[:, c * WIDTH_A:(c + 1) * WIDTH_A] = acc.astype(jnp.bfloat16)

    for gi, ((_, dil), d_ref) in enumerate(zip(DIL_GROUPS, (d0_ref, d1_ref, d2_ref))):
        for kind, off in enumerate((OFF_QB, OFF_KB, OFF_VB)):
            lo = off + gi * WIDTH_G
            acc = _dot(h, w_ref[:, lo:lo + WIDTH_G])
            if kind == 0:
                acc = acc * q_scale
            cols = slice(kind * WIDTH_G, (kind + 1) * WIDTH_G)
            if dil == 1:
                d_ref[0, 0, :, cols] = acc.astype(jnp.bfloat16)
            else:
                for c in range(WIDTH_G // LANES):
                    acc_ref[c] = acc[:, c * LANES:(c + 1) * LANES]
                for rho in range(dil):
                    d_ref[0, rho, :, cols] = jnp.concatenate(
                        [acc_ref[c, pl.ds(rho, T // dil, stride=dil), :] for c in range(WIDTH_G // LANES)],
                        axis=-1).astype(jnp.bfloat16)

    qc_ref[...] = _dot(h, w_ref[:, OFF_QC:OFF_QC + WIDTH_C]).astype(jnp.bfloat16)


def _in_proj(x2d, g, w_bf16, batch, seq):
    n = x2d.shape[0]
    T = IN_PROJ_TILE
    tiles_per_batch = seq // T
    out_shape = [jax.ShapeDtypeStruct((n, 3 * WIDTH_A), jnp.bfloat16)]
    out_specs = [pl.BlockSpec((T, 3 * WIDTH_A), lambda i: (i, 0))]
    for _, dil in DIL_GROUPS:
        out_shape.append(jax.ShapeDtypeStruct((batch, dil, seq // dil, 3 * WIDTH_G), jnp.bfloat16))
        out_specs.append(pl.BlockSpec((1, dil, T // dil, 3 * WIDTH_G),
                                      lambda i: (i // tiles_per_batch, 0, i % tiles_per_batch, 0)))
    out_shape += [jax.ShapeDtypeStruct((n, WIDTH_C), jnp.bfloat16)]
    out_specs += [pl.BlockSpec((T, WIDTH_C), lambda i: (i, 0))]
    return pl.pallas_call(
        _in_proj_kernel,
        grid=(n // T,),
        in_specs=[pl.BlockSpec((T, D_MODEL), lambda i: (i, 0)),
                  pl.BlockSpec((1, D_MODEL), lambda i: (0, 0)),
                  pl.BlockSpec((D_MODEL, OFF_GATE), lambda i: (0, 0), pipeline_mode=pl.Buffered(1))],
        out_specs=out_specs,
        out_shape=out_shape,
        scratch_shapes=[pltpu.VMEM((WIDTH_G // LANES, T, LANES), jnp.float32)],
        compiler_params=_cparams(1),
        name="in_proj",
    )(x2d, g, w_bf16)


def _na_kernel(q_ref, k_ref, v_ref, bias_ref, out_ref, *, rows):
    i = pl.program_id(1)
    lane = lax.broadcasted_iota(jnp.int32, (GRID_W, LANES), 1)
    low_half = lane < HEAD_DIM
    n_win = NA_WIN_ROWS * GRID_W

    def row_body(j, carry):
        qr = i * NA_ROWS_PER_STEP + j
        rs = jnp.clip(qr - NA_WIN_ROWS // 2, 0, rows - NA_WIN_ROWS)
        shift = rs - qr + NA_WIN_ROWS - 1
        q_off = pl.multiple_of(j * GRID_W, GRID_W)
        k_off = pl.multiple_of(rs * GRID_W, GRID_W)
        scores = []
        for hp in range(NA_HEADS // 2):
            cols = slice(hp * LANES, (hp + 1) * LANES)
            qp = q_ref[pl.ds(q_off, GRID_W), cols]
            kp = k_ref[pl.ds(k_off, n_win), cols]
            zero = jnp.zeros_like(qp)
            q2 = jnp.concatenate([jnp.where(low_half, qp, zero), jnp.where(low_half, zero, qp)], axis=0)
            bias = jnp.concatenate(
                [jnp.concatenate([bias_ref[shift + 2 * p, 2 * hp + hh] for p in range(NA_WIN_ROWS // 2)], axis=-1)
                 for hh in range(2)], axis=0)
            scores.append(_dot_nt(q2, kp) + bias)
        probs = []
        for s in scores:
            e = jnp.exp(s - jnp.max(s, axis=-1, keepdims=True))
            probs.append((e.astype(jnp.bfloat16), jnp.sum(e, axis=-1, keepdims=True)))
        for hp, (p, l) in enumerate(probs):
            cols = slice(hp * LANES, (hp + 1) * LANES)
            o2 = _dot(p, v_ref[pl.ds(k_off, n_win), cols]) / l
            out_ref[pl.ds(q_off, GRID_W), cols] = jnp.where(
                low_half, o2[:GRID_W], o2[GRID_W:]).astype(jnp.bfloat16)
        return carry

    lax.fori_loop(0, NA_ROWS_PER_STEP, row_body, 0, unroll=4)


def _na_attn(na_qkv, bias_tab, batch, seq):
    rows = seq // GRID_W
    steps = rows // NA_ROWS_PER_STEP
    tq = NA_ROWS_PER_STEP * GRID_W
    n = na_qkv.shape[0]
    return pl.pallas_call(
        functools.partial(_na_kernel, rows=rows),
        grid=(batch, steps),
        in_specs=[pl.BlockSpec((tq, WIDTH_A), lambda b, i: (b * steps + i, 0)),
                  pl.BlockSpec((seq, WIDTH_A), lambda b, i: (b, 1)),
                  pl.BlockSpec((seq, WIDTH_A), lambda b, i: (b, 2)),
                  pl.BlockSpec(bias_tab.shape, lambda b, i: (0, 0, 0, 0))],
        out_specs=pl.BlockSpec((tq, WIDTH_A), lambda b, i: (b * steps + i, 0)),
        out_shape=jax.ShapeDtypeStruct((n, WIDTH_A), jnp.bfloat16),
        compiler_params=_cparams(2),
        name="na_attn",
    )(na_qkv, na_qkv, na_qkv, bias_tab)


def _t5_bucket_np(rel):
    half = T5_BUCKETS // 2
    max_exact = half // 2
    ret = np.where(rel > 0, half, 0)
    n = np.abs(rel)
    nf = np.maximum(n, 1).astype(np.float32)
    large = max_exact + (np.log(nf / np.float32(max_exact)) / np.float32(math.log(T5_MAX_DIST / max_exact))
                         * np.float32(half - max_exact)).astype(np.int32)
    large = np.minimum(large, half - 1)
    return ret + np.where(n < max_exact, n, large)


def _dil_bucket_table():
    q = np.arange(DIL_BLOCK)[:, None]
    j = np.arange(2 * DIL_BLOCK)[None, :]
    tabs = []
    for window, dil in DIL_GROUPS:
        half = (window // 2) // dil
        assert half == DIL_BLOCK // 2
        rel = (j - half) - q
        tabs.append(np.where(np.abs(rel) <= half, _t5_bucket_np(rel * dil), -1))
    return np.stack(tabs).astype(np.int32)


def _bias_prep_kernel(rpb_ref, t5_ref, bucket_ref, na_ref, dil_ref):
    n_dc = 2 * NA_WIN_COLS - 1
    n_dr = 2 * NA_WIN_ROWS - 1
    qc = lax.broadcasted_iota(jnp.int32, (GRID_W, 2 * GRID_W), 0)
    lane = lax.broadcasted_iota(jnp.int32, (GRID_W, 2 * GRID_W), 1)
    second = lane >= GRID_W
    kc = jnp.where(second, lane - GRID_W, lane)
    diff = kc - qc
    cs = jnp.clip(qc - NA_WIN_COLS // 2, 0, GRID_W - NA_WIN_COLS)
    valid = jnp.logical_and(kc >= cs, kc < cs + NA_WIN_COLS)

    def na_tile(mh, carry):
        m = mh // NA_HEADS
        h = mh - m * NA_HEADS
        base = (h * n_dr + m) * n_dc
        acc = jnp.zeros((GRID_W, 2 * GRID_W), jnp.float32)
        for d in range(n_dc):
            v = jnp.where(second, rpb_ref[base + n_dc + d], rpb_ref[base + d])
            acc = jnp.where(diff == d - (NA_WIN_COLS - 1), v, acc)
        na_ref[m, h] = jnp.where(valid, acc, NEG_INF)
        return carry

    lax.fori_loop(0, (n_dr - 1) * NA_HEADS, na_tile, 0)

    j = lax.broadcasted_iota(jnp.int32, (DIL_BLOCK, 2 * DIL_BLOCK), 1)
    half = DIL_BLOCK // 2
    n_heads = DIL_HEADS_PER_GROUP * len(DIL_GROUPS)
    for gi in range(len(DIL_GROUPS)):
        bucket = bucket_ref[gi]
        in_win = bucket >= 0
        for h in range(DIL_HEADS_PER_GROUP):
            acc = jnp.zeros((DIL_BLOCK, 2 * DIL_BLOCK), jnp.float32)
            for b in range(T5_BUCKETS):
                acc = jnp.where(bucket == b, t5_ref[b * n_heads + gi * DIL_HEADS_PER_GROUP + h], acc)
            for v, ok in enumerate((jnp.logical_and(in_win, j >= half), in_win,
                                    jnp.logical_and(in_win, j < 2 * DIL_BLOCK - half))):
                dil_ref[gi, v, h] = jnp.where(ok, acc, NEG_INF)


def _bias_prep(rpb, t5):
    smem = pl.BlockSpec(memory_space=pltpu.SMEM)
    return pl.pallas_call(
        _bias_prep_kernel,
        in_specs=[smem, smem, pl.BlockSpec(memory_space=pltpu.VMEM)],
        out_shape=[jax.ShapeDtypeStruct((2 * NA_WIN_ROWS - 2, NA_HEADS, GRID_W, 2 * GRID_W), jnp.float32),
                   jax.ShapeDtypeStruct((len(DIL_GROUPS), 3, DIL_HEADS_PER_GROUP, DIL_BLOCK, 2 * DIL_BLOCK),
                                        jnp.float32)],
        compiler_params=pltpu.CompilerParams(vmem_limit_bytes=VMEM_LIMIT),
        name="bias_prep",
    )(rpb.reshape(-1), t5.reshape(-1), jnp.asarray(_dil_bucket_table()))


def _dil_kernel(q_ref, kp_ref, kc_ref, kn_ref, vp_ref, vc_ref, vn_ref, bias_ref, o_ref, lse_ref, *, dil):
    lane = lax.broadcasted_iota(jnp.int32, (DIL_BLOCK, LANES), 1)
    low_half = lane < HEAD_DIM
    hb = DIL_BLOCK // 2

    span = pl.program_id(1)
    n_spans = pl.num_programs(1)
    nbs = kc_ref.shape[2] // DIL_BLOCK

    def window(prev_ref, cur_ref, next_ref, rho, blk, cols):
        lo = blk * DIL_BLOCK
        first = prev_ref[0, rho, hb:, cols] if blk == 0 else cur_ref[0, rho, lo - hb:lo, cols]
        last = (next_ref[0, rho, :hb, cols] if blk == nbs - 1
                else cur_ref[0, rho, lo + DIL_BLOCK:lo + DIL_BLOCK + hb, cols])
        return jnp.concatenate([first, cur_ref[0, rho, lo:lo + DIL_BLOCK, cols], last], axis=0)

    def unit(rho, blk):
        variant = 1
        if blk == nbs - 1:
            variant = jnp.where(span == n_spans - 1, 2, variant)
        if blk == 0:
            variant = jnp.where(span == 0, 0, variant)
        scores = []
        for hp in range(DIL_HEADS_PER_GROUP // 2):
            cols = slice(hp * LANES, (hp + 1) * LANES)
            qp = q_ref[0, rho, blk * DIL_BLOCK:(blk + 1) * DIL_BLOCK, cols]
            zero = jnp.zeros_like(qp)
            q2 = jnp.concatenate([jnp.where(low_half, qp, zero), jnp.where(low_half, zero, qp)], axis=0)
            kw = window(kp_ref, kc_ref, kn_ref, rho, blk, cols)
            bias = jnp.concatenate([bias_ref[variant, 2 * hp], bias_ref[variant, 2 * hp + 1]], axis=0)
            scores.append(_dot_nt(q2, kw) + bias)
        probs = []
        for s in scores:
            m = jnp.max(s, axis=-1, keepdims=True)
            e = jnp.exp(s - m)
            probs.append((e.astype(jnp.bfloat16), m, jnp.sum(e, axis=-1, keepdims=True)))
        for hp, (p, m, l) in enumerate(probs):
            cols = slice(hp * LANES, (hp + 1) * LANES)
            o2 = _dot(p, window(vp_ref, vc_ref, vn_ref, rho, blk, cols)) / l
            lse2 = jnp.broadcast_to(m + jnp.log(l), (2 * DIL_BLOCK, LANES))
            o_pair = jnp.where(low_half, o2[:DIL_BLOCK], o2[DIL_BLOCK:])
            lse_pair = jnp.where(low_half, lse2[:DIL_BLOCK], lse2[DIL_BLOCK:])
            if dil == 1:
                rows = slice(blk * DIL_BLOCK, (blk + 1) * DIL_BLOCK)
                o_ref[hp, rows, :] = o_pair
                lse_ref[hp, rows, :] = lse_pair
            else:
                rows = pl.ds(blk * DIL_BLOCK * dil + rho, DIL_BLOCK, stride=dil)
                o_ref[hp, rows, :] = o_pair
                lse_ref[hp, rows, :] = lse_pair

    def class_body(rho, carry):
        for blk in range(nbs):
            unit(rho, blk)
        return carry

    if dil == 1:
        class_body(0, 0)
    else:
        lax.fori_loop(0, dil, class_body, 0, unroll=4 if nbs == 1 else 2)


def _dil_attn(qkv, bias_tab, group, dil, batch, seq):
    L = seq // dil
    nb = L // DIL_BLOCK
    assert nb >= 2
    tq = DIL_STEP_TOKENS
    span = tq // dil
    nbs = span // DIL_BLOCK
    n_spans = L // span

    def cur(col):
        return pl.BlockSpec((1, dil, span, WIDTH_G), lambda b, n: (b, 0, n, col))

    def edge(col, shift):
        return pl.BlockSpec((1, dil, DIL_BLOCK, WIDTH_G),
                            lambda b, n: (b, 0, jnp.clip(n * nbs + (shift if shift < 0 else nbs), 0, nb - 1), col))

    n_pairs = WIDTH_G // LANES
    out_spec = pl.BlockSpec((n_pairs, tq, LANES), lambda b, n: (0, b * n_spans + n, 0))
    return pl.pallas_call(
        functools.partial(_dil_kernel, dil=dil),
        grid=(batch, n_spans),
        in_specs=[cur(0), edge(1, -1), cur(1), edge(1, 1), edge(2, -1), cur(2), edge(2, 1),
                  pl.BlockSpec((None, 3, DIL_HEADS_PER_GROUP, DIL_BLOCK, 2 * DIL_BLOCK),
                               lambda b, n: (group, 0, 0, 0, 0))],
        out_specs=[out_spec, out_spec],
        out_shape=[jax.ShapeDtypeStruct((n_pairs, batch * seq, LANES), jnp.float32)] * 2,
        compiler_params=_cparams(2),
        name=f"dil_attn_d{dil}",
    )(qkv, qkv, qkv, qkv, qkv, qkv, qkv, bias_tab)


def _mix_kernel(x_ref, na_ref, o0_ref, l0_ref, o1_ref, l1_ref, o2_ref, l2_ref, qc_ref, ag_ref, wg_ref,
                mkv_ref, wa_ref, wb_ref, wc_ref, wo_ref, fg_ref, rw_ref, rb_ref,
                x1_ref, h_ref, idx_ref, wt_ref, seg_ref):
    T = x_ref.shape[0]
    hx = _rms(x_ref[...], ag_ref[...]).astype(jnp.bfloat16)
    gate_first = OFF_GATE - IN_COLS // 2

    def gate(b):
        lo = gate_first + b * D_MODEL
        return _sigmoid(_dot(hx, wg_ref[:, lo:lo + D_MODEL])).astype(jnp.bfloat16)

    gate_a = gate(0)
    ob = []
    for hp in range(WIDTH_G // LANES):
        l0, l1, l2 = l0_ref[hp], l1_ref[hp], l2_ref[hp]
        m = jnp.maximum(jnp.maximum(l0, l1), l2)
        e0, e1, e2 = jnp.exp(l0 - m), jnp.exp(l1 - m), jnp.exp(l2 - m)
        ob.append((e0 * o0_ref[hp] + e1 * o1_ref[hp] + e2 * o2_ref[hp]) / (e0 + e1 + e2))
    ob = jnp.concatenate(ob, axis=-1)

    mem_scale = MEM_HEAD_DIM ** -0.5
    scores = []
    for h in range(MEM_HEADS):
        cols = slice(h * MEM_HEAD_DIM, (h + 1) * MEM_HEAD_DIM)
        scores.append(_dot_nt(qc_ref[:, cols], mkv_ref[:, cols]) * mem_scale)
    gate_b = gate(1)
    probs = []
    for s in scores:
        e = jnp.exp(s - jnp.max(s, axis=-1, keepdims=True))
        probs.append((e.astype(jnp.bfloat16), jnp.sum(e, axis=-1, keepdims=True)))
    oc = []
    for h, (p, l) in enumerate(probs):
        oc.append(_dot(p, mkv_ref[:, WIDTH_C + h * MEM_HEAD_DIM:WIDTH_C + (h + 1) * MEM_HEAD_DIM]) / l)
    oc = jnp.concatenate(oc, axis=-1).astype(jnp.bfloat16)
    gate_c = gate(2)

    y_a = _dot(na_ref[...], wa_ref[...])
    y_b = _dot(ob.astype(jnp.bfloat16), wb_ref[...])
    y_c = _dot(oc, wc_ref[...])
    merged = (gate_a.astype(jnp.float32) * y_a + gate_b.astype(jnp.float32) * y_b
              + gate_c.astype(jnp.float32) * y_c)
    x1 = x_ref[...] + _dot(merged.astype(jnp.bfloat16), wo_ref[...])
    x1_ref[...] = x1

    h = _rms(x1, fg_ref[...])
    h_ref[...] = h.astype(jnp.bfloat16)
    logits = lax.dot_general(rw_ref[...], h, (((1,), (1,)), ((), ())), preferred_element_type=jnp.float32,
                             precision=lax.Precision.HIGHEST) + rb_ref[...]
    expert = lax.broadcasted_iota(jnp.int32, (N_EXPERTS, T), 0)
    vals, idxs = [], []
    for _ in range(TOP_K):
        mx = jnp.max(logits, axis=0, keepdims=True)
        sel = jnp.min(jnp.where(logits == mx, expert, N_EXPERTS), axis=0, keepdims=True)
        vals.append(mx)
        idxs.append(sel)
        logits = jnp.where(expert == sel, -jnp.inf, logits)
    ev = [jnp.exp(v - vals[0]) for v in vals]
    den = ev[0] + ev[1] + ev[2] + ev[3]
    idx_ref[...] = jnp.concatenate(idxs, axis=0)
    wt_ref[...] = jnp.concatenate([e / den for e in ev], axis=0)
    member = sum((expert == sel).astype(jnp.float32) for sel in idxs)
    for s in range(T // MOE_TILE):
        cnt = jnp.sum(member[:, s * MOE_TILE:(s + 1) * MOE_TILE], axis=1, keepdims=True)
        seg = jnp.floor((cnt + (MOE_PIECE - 1)) * (1.0 / MOE_PIECE)) * MOE_PIECE
        seg_ref[s] = jnp.broadcast_to(seg, (N_EXPERTS, LANES))


def _mix_out(x2d, na_out, dil_outs, qc, attn_g, w_in, mkv, wa, wb, wc, wo, ffn_g, router_w, router_b, batch, seq):
    n = x2d.shape[0]
    T = PROJ_TILE
    tiles_per_batch = seq // T
    mem_len = mkv.shape[0] // batch
    row = lambda w: pl.BlockSpec((T, w), lambda i: (i, 0))
    full = lambda a: pl.BlockSpec(a.shape, lambda i: (0,) * a.ndim, pipeline_mode=pl.Buffered(1))
    in_specs = [row(D_MODEL), row(WIDTH_A)]
    args = [x2d, na_out]
    for o, l in dil_outs:
        pair_rows = pl.BlockSpec((WIDTH_G // LANES, T, LANES), lambda i: (0, i, 0))
        in_specs += [pair_rows, pair_rows]
        args += [o, l]
    assert (IN_COLS // 2) % LANES == 0 and IN_COLS // 2 <= OFF_GATE
    in_specs += [row(WIDTH_C), full(attn_g),
                 pl.BlockSpec((D_MODEL, IN_COLS // 2), lambda i: (0, 1), pipeline_mode=pl.Buffered(1)),
                 pl.BlockSpec((mem_len, 2 * WIDTH_C), lambda i: (i // tiles_per_batch, 0)),
                 full(wa), full(wb), full(wc), full(wo), full(ffn_g), full(router_w), full(router_b)]
    args += [qc, attn_g, w_in, mkv, wa, wb, wc, wo, ffn_g, router_w, router_b]
    return pl.pallas_call(
        _mix_kernel,
        grid=(n // T,),
        in_specs=in_specs,
        out_specs=[row(D_MODEL), row(D_MODEL), pl.BlockSpec((TOP_K, T), lambda i: (0, i)),
                   pl.BlockSpec((TOP_K, T), lambda i: (0, i)),
                   pl.BlockSpec((T // MOE_TILE, N_EXPERTS, LANES), lambda i: (i, 0, 0))],
        out_shape=[jax.ShapeDtypeStruct((n, D_MODEL), jnp.float32),
                   jax.ShapeDtypeStruct((n, D_MODEL), jnp.bfloat16),
                   jax.ShapeDtypeStruct((TOP_K, n), jnp.int32),
                   jax.ShapeDtypeStruct((TOP_K, n), jnp.float32),
                   jax.ShapeDtypeStruct((n // MOE_TILE, N_EXPERTS, LANES), jnp.float32)],
        compiler_params=_cparams(1),
        name="mix_out",
    )(*args)


def _route_kernel(idx_ref, segs_ref, lpos_ref, ptab_ref, be_ref, pe_ref, carry_ref, pstart_ref):
    i = pl.program_id(0)
    T = idx_ref.shape[1]
    n_blocks = be_ref.shape[1]
    expert = lax.broadcasted_iota(jnp.int32, (N_EXPERTS, T), 0)
    idx = idx_ref[...]
    onehots = [expert == idx[k:k + 1, :] for k in range(TOP_K)]
    member = sum(o.astype(jnp.float32) for o in onehots)
    seg_b = segs_ref[i]
    seg = seg_b[:, 0:1]
    r = lax.broadcasted_iota(jnp.int32, (N_EXPERTS, N_EXPERTS), 0)
    c = lax.broadcasted_iota(jnp.int32, (N_EXPERTS, N_EXPERTS), 1)
    lower_incl = (r >= c).astype(jnp.float32)

    def cumsum_experts(v):
        return jnp.dot(lower_incl, v, preferred_element_type=jnp.float32, precision=lax.Precision.HIGHEST)

    @pl.when(i == 0)
    def _():
        counts = jnp.sum(segs_ref[...], axis=0)
        padded = jnp.floor((counts + (MOE_BLOCK - 1)) * (1.0 / MOE_BLOCK)) * MOE_BLOCK
        pad_ends = cumsum_experts(padded)
        pstart_ref[...] = pad_ends - padded
        carry_ref[...] = jnp.zeros_like(carry_ref)
        pe_ref[...] = pad_ends[:, 0:1].astype(jnp.int32)
        blk_start = (lax.broadcasted_iota(jnp.int32, (N_EXPERTS, n_blocks), 1) * MOE_BLOCK).astype(jnp.float32)
        be = jnp.sum((pad_ends[:, 0:1] <= blk_start).astype(jnp.int32), axis=0, keepdims=True)
        be_ref[...] = jnp.minimum(be, N_EXPERTS - 1)

    local_end = cumsum_experts(seg_b)[:, 0:1]
    local_start = local_end - seg
    global_start = pstart_ref[:, 0:1] + carry_ref[:, 0:1]
    rr = lax.broadcasted_iota(jnp.int32, (T, T), 0)
    cc = lax.broadcasted_iota(jnp.int32, (T, T), 1)
    earlier = (rr < cc).astype(jnp.bfloat16)
    before = _dot(member.astype(jnp.bfloat16), earlier)
    base = local_start + before
    lpos = [jnp.sum(jnp.where(o, base, 0.0), axis=0, keepdims=True) for o in onehots]
    lpos_ref[...] = jnp.concatenate(lpos, axis=0).astype(jnp.int32)

    inv = 1.0 / MOE_PIECE
    piece = lax.broadcasted_iota(jnp.int32, (N_EXPERTS, MOE_PIECE_TAB), 1).astype(jnp.float32)
    first = local_start * inv
    inside = jnp.logical_and(piece >= first, piece < local_end * inv)
    dst = jnp.sum(jnp.where(inside, global_start * inv + piece - first, 0.0), axis=0, keepdims=True)
    n_pieces = local_end[N_EXPERTS - 1:N_EXPERTS, :] * inv
    spare = (n_blocks * MOE_BLOCK // MOE_PIECE
             + (i % 2) * (MOE_TILE_ROWS // MOE_PIECE)).astype(jnp.float32)
    dst = jnp.where(piece[0:1, :] >= n_pieces, spare + piece[0:1, :], dst)
    dst = jnp.where(piece[0:1, :] == MOE_PIECE_TAB - 1, n_pieces, dst)
    ptab_ref[...] = dst.astype(jnp.int32)
    carry_ref[...] += seg


def _moe_route(topk_idx, segs, n_blocks):
    n = topk_idx.shape[1]
    T = MOE_TILE
    acc = pltpu.VMEM((N_EXPERTS, LANES), jnp.float32)
    return pl.pallas_call(
        _route_kernel,
        grid=(n // T,),
        in_specs=[pl.BlockSpec((TOP_K, T), lambda i: (0, i)),
                  pl.BlockSpec(segs.shape, lambda i: (0, 0, 0))],
        out_specs=[pl.BlockSpec((TOP_K, T), lambda i: (0, i)),
                   pl.BlockSpec((None, 1, MOE_PIECE_TAB), lambda i: (i, 0, 0)),
                   pl.BlockSpec((1, n_blocks), lambda i: (0, 0)),
                   pl.BlockSpec((N_EXPERTS, 1), lambda i: (0, 0))],
        out_shape=[jax.ShapeDtypeStruct((TOP_K, n), jnp.int32),
                   jax.ShapeDtypeStruct((n // T, 1, MOE_PIECE_TAB), jnp.int32),
                   jax.ShapeDtypeStruct((1, n_blocks), jnp.int32),
                   jax.ShapeDtypeStruct((N_EXPERTS, 1), jnp.int32)],
        scratch_shapes=[acc, acc],
        compiler_params=_cparams(1),
        name="moe_route",
    )(topk_idx, segs)


def _piece(ref, p):
    start = p * MOE_PIECE if isinstance(p, int) else pl.multiple_of(p * MOE_PIECE, MOE_PIECE)
    return ref.at[pl.ds(start, MOE_PIECE), :]


def _dispatch_kernel(pe_ref, ptab_ref, lpos_ref, h_ref, xs_hbm, zbuf, pbuf, sem, zsem):
    i = pl.program_id(0)
    T = h_ref.shape[0]
    half = h_ref.shape[1] // 2

    def zero_copy(e):
        end = pe_ref[e]
        start = pl.multiple_of(end - MOE_BLOCK, MOE_BLOCK)
        return pltpu.make_async_copy(zbuf, xs_hbm.at[pl.ds(start, MOE_BLOCK), :], zsem)

    def nonempty(e):
        return pe_ref[e] > (pe_ref[e - 1] if e else 0)

    @pl.when(i == 0)
    def _():
        zbuf[...] = jnp.zeros_like(zbuf)
        for e in range(N_EXPERTS):
            @pl.when(nonempty(e))
            def _():
                zero_copy(e).start()
        for e in range(N_EXPERTS):
            @pl.when(nonempty(e))
            def _():
                zero_copy(e).wait()

        def tail_copy(b):
            return pltpu.make_async_copy(
                zbuf, xs_hbm.at[pl.ds(pl.multiple_of(b * MOE_BLOCK, MOE_BLOCK), MOE_BLOCK), :], zsem)
        first_tail = pe_ref[N_EXPERTS - 1] // MOE_BLOCK
        n_blocks = xs_hbm.shape[0] // MOE_BLOCK
        lax.fori_loop(first_tail, n_blocks, lambda b, c: (tail_copy(b).start(), c)[1], 0)
        lax.fori_loop(first_tail, n_blocks, lambda b, c: (tail_copy(b).wait(), c)[1], 0)

    row = lax.broadcasted_iota(jnp.int32, (MOE_TILE_ROWS, T), 0)
    lpos = lpos_ref[...]
    perm = sum((row == lpos[k:k + 1, :]).astype(jnp.bfloat16) for k in range(TOP_K))
    lo = lax.bitcast_convert_type(_dot(perm, h_ref[:, :half]), jnp.uint32)
    hi = lax.bitcast_convert_type(_dot(perm, h_ref[:, half:]), jnp.uint32)
    slot = i % 2
    pbuf[slot] = hi | (lo >> 16)

    n_tile_pieces = MOE_TILE_ROWS // MOE_PIECE

    def piece_copy(s, p):
        return pltpu.make_async_copy(_piece(pbuf.at[s], p), _piece(xs_hbm, ptab_ref[p]), sem.at[s])

    def wait_all(s):
        for _ in range(n_tile_pieces):
            piece_copy(s, 0).wait()

    for p in range(n_tile_pieces):
        piece_copy(slot, p).start()

    @pl.when(i > 0)
    def _():
        wait_all(1 - slot)

    @pl.when(i == pl.num_programs(0) - 1)
    def _():
        wait_all(slot)


def _moe_dispatch(h, lpos, ptab, pad_ends, n_blocks):
    n, width = h.shape
    T = MOE_TILE
    n_steps = n // T

    def kernel(pe_ref, ptab_ref, *rest):
        _dispatch_kernel(pe_ref, ptab_ref.at[0], *rest)

    grid_spec = pltpu.PrefetchScalarGridSpec(
        num_scalar_prefetch=1,
        grid=(n_steps,),
        in_specs=[pl.BlockSpec((None, 1, MOE_PIECE_TAB), lambda i, pe: (i, 0, 0), memory_space=pltpu.SMEM),
                  pl.BlockSpec((TOP_K, T), lambda i, pe: (0, i)),
                  pl.BlockSpec((T, width), lambda i, pe: (i, 0))],
        out_specs=pl.BlockSpec(memory_space=pl.ANY),
        scratch_shapes=[pltpu.VMEM((MOE_BLOCK, width // 2), jnp.uint32),
                        pltpu.VMEM((2, MOE_TILE_ROWS, width // 2), jnp.uint32),
                        pltpu.SemaphoreType.DMA((2,)),
                        pltpu.SemaphoreType.DMA(())],
    )
    return pl.pallas_call(
        kernel,
        grid_spec=grid_spec,
        out_shape=jax.ShapeDtypeStruct(((n_blocks + MOE_SPARE_BLOCKS) * MOE_BLOCK, width // 2), jnp.uint32),
        compiler_params=_cparams(1),
        name="moe_dispatch",
    )(pad_ends, ptab, lpos, h)


def _moe_kernel(be_ref, pe_ref, xs_ref, wgu_ref, bgu_ref, wdn_ref, bdn_ref, y_ref):
    del be_ref
    i = pl.program_id(0)
    n_valid = pe_ref[N_EXPERTS - 1] // MOE_BLOCK

    @pl.when(i < n_valid)
    def _():
        x = _unpack_bf16_pairs(xs_ref[...])
        gu = _dot(x, wgu_ref[0]) + bgu_ref[0]
        g = jnp.minimum(gu[:, :D_FF], SWIGLU_LIMIT)
        u = jnp.clip(gu[:, D_FF:], -SWIGLU_LIMIT, SWIGLU_LIMIT)
        a = (u + 1.0) * (g * _sigmoid(SWIGLU_ALPHA * g))
        y_ref[...] = _pack_bf16_pairs(_dot(a.astype(jnp.bfloat16), wdn_ref[0]) + bdn_ref[0])

    @pl.when(i >= n_valid)
    def _():
        y_ref[...] = jnp.zeros_like(y_ref)


def _moe_experts(xs, block_expert, pad_ends, wgu, bgu, wdn, bdn):
    n_blocks = block_expert.shape[0]

    def row_blk(i, be, pe):
        return (jnp.minimum(i, jnp.maximum(pe[N_EXPERTS - 1] // MOE_BLOCK - 1, 0)), 0)

    grid_spec = pltpu.PrefetchScalarGridSpec(
        num_scalar_prefetch=2,
        grid=(n_blocks,),
        in_specs=[pl.BlockSpec((MOE_BLOCK, D_MODEL // 2), row_blk),
                  pl.BlockSpec((1, D_MODEL, 2 * D_FF), lambda i, be, pe: (be[i], 0, 0)),
                  pl.BlockSpec((1, 1, 2 * D_FF), lambda i, be, pe: (be[i], 0, 0)),
                  pl.BlockSpec((1, D_FF, D_MODEL), lambda i, be, pe: (be[i], 0, 0)),
                  pl.BlockSpec((1, 1, D_MODEL), lambda i, be, pe: (be[i], 0, 0))],
        out_specs=pl.BlockSpec((MOE_BLOCK, D_MODEL // 2), lambda i, be, pe: (i, 0)),
    )
    return pl.pallas_call(
        _moe_kernel,
        grid_spec=grid_spec,
        out_shape=jax.ShapeDtypeStruct((n_blocks * MOE_BLOCK, D_MODEL // 2), jnp.uint32),
        compiler_params=_cparams(1),
        name="moe_experts",
    )(block_expert, pad_ends, xs, wgu, bgu, wdn, bdn)


def _gather_pieces(src_hbm, ptab_ref, dst_ref, sem):
    spare = src_hbm.shape[0] // MOE_PIECE

    for p in range(MOE_TILE_ROWS // MOE_PIECE):
        src = ptab_ref[p]
        src = jnp.where(src >= spare, 0, src)
        pltpu.make_async_copy(_piece(src_hbm, src), _piece(dst_ref, p), sem).start()


def _wait_pieces(src_hbm, dst_ref, sem):
    for _ in range(MOE_TILE_ROWS // MOE_PIECE):
        pltpu.make_async_copy(_piece(src_hbm, 0), _piece(dst_ref, 0), sem).wait()


def _combine_kernel(ptab_ref, ptab_next_ref, ys_hbm, lpos_ref, wt_ref, x1_ref, g_ref, out_ref, gbuf, sem):
    i = pl.program_id(0)
    n_steps = pl.num_programs(0)
    slot = i % 2
    T = x1_ref.shape[0]

    @pl.when(i == 0)
    def _():
        _gather_pieces(ys_hbm, ptab_ref, gbuf.at[0], sem.at[0])

    @pl.when(i + 1 < n_steps)
    def _():
        _gather_pieces(ys_hbm, ptab_next_ref, gbuf.at[1 - slot], sem.at[1 - slot])

    _wait_pieces(ys_hbm, gbuf.at[slot], sem.at[slot])
    rows = _unpack_bf16_pairs(gbuf[slot])
    col = lax.broadcasted_iota(jnp.int32, (T, MOE_TILE_ROWS), 1)
    lpos = lpos_ref[...]
    wt = wt_ref[...]
    cmb = jnp.zeros((T, MOE_TILE_ROWS), jnp.float32)
    for k in range(TOP_K):
        cmb = jnp.where(col == lpos[:, k:k + 1], wt[:, k:k + 1], cmb)
    hi = cmb.astype(jnp.bfloat16)
    lo = (cmb - hi.astype(jnp.float32)).astype(jnp.bfloat16)
    y2 = _dot(jnp.concatenate([hi, lo], axis=0), rows)
    y = x1_ref[...] + y2[:T] + y2[T:]
    out_ref[...] = _rms(y, g_ref[...])


def _moe_combine(ys, ptab, lpos_t, wts, x1, final_g):
    n = x1.shape[0]
    T = MOE_TILE
    n_steps = n // T
    smem_blk = lambda shift: pl.BlockSpec(
        (None, 1, MOE_PIECE_TAB), lambda i: (jnp.minimum(i + shift, n_steps - 1), 0, 0),
        memory_space=pltpu.SMEM)

    def kernel(ptab_ref, ptab_next_ref, *rest):
        _combine_kernel(ptab_ref.at[0], ptab_next_ref.at[0], *rest)

    return pl.pallas_call(
        kernel,
        grid=(n_steps,),
        in_specs=[smem_blk(0), smem_blk(1),
                  pl.BlockSpec(memory_space=pl.ANY),
                  pl.BlockSpec((T, TOP_K), lambda i: (i, 0)),
                  pl.BlockSpec((T, TOP_K), lambda i: (i, 0)),
                  pl.BlockSpec((T, D_MODEL), lambda i: (i, 0)),
                  pl.BlockSpec((1, D_MODEL), lambda i: (0, 0))],
        out_specs=pl.BlockSpec((T, D_MODEL), lambda i: (i, 0)),
        out_shape=jax.ShapeDtypeStruct((n, D_MODEL), jnp.float32),
        scratch_shapes=[pltpu.VMEM((2, MOE_TILE_ROWS, D_MODEL // 2), jnp.uint32),
                        pltpu.SemaphoreType.DMA((2,))],
        compiler_params=_cparams(1),
        name="moe_combine",
    )(ptab, ptab, ys, lpos_t, wts, x1, final_g)


def kernel(x, mem, attn_norm_g, mem_norm_g, w_in, w_mem_kv, na_rpb, t5_rel_bias, w_branch_a, w_branch_b,
           w_branch_c, w_out, ffn_norm_g, router_w, router_b, expert_w_gate_up, expert_b_gate_up,
           expert_w_down, expert_b_down, final_norm_g):
    B, S, D = x.shape
    depth = w_in.shape[0]
    x2d = x.reshape(B * S, D)
    mem2d = mem.reshape(B * mem.shape[1], D)
    assert depth == 1, "single-layer block (per-layer parameters carry a leading axis of 1)"
    l = 0
    na_bias, dil_bias = _bias_prep(na_rpb[l], t5_rel_bias)
    mkv = _mem_kv(mem2d, mem_norm_g[l][None], w_mem_kv[l])
    na_qkv, d0, d1, d2, qc = _in_proj(x2d, attn_norm_g[l][None], w_in[l], B, S)
    na_out = _na_attn(na_qkv, na_bias, B, S)
    dil_outs = [_dil_attn(qkv, dil_bias, gi, dil, B, S)
                for gi, (qkv, (_, dil)) in enumerate(zip((d0, d1, d2), DIL_GROUPS))]
    x1, h, topk_idx, topk_w, segs = _mix_out(
        x2d, na_out, dil_outs, qc, attn_norm_g[l][None], w_in[l], mkv, w_branch_a[l], w_branch_b[l],
        w_branch_c[l], w_out[l], ffn_norm_g[l][None], router_w[l].T, router_b[l][:, None], B, S)

    n = B * S
    max_rows = n * TOP_K + (n // MOE_TILE) * N_EXPERTS * (MOE_PIECE - 1)
    n_blocks = -(-max_rows // MOE_BLOCK) + N_EXPERTS
    lpos, ptab, block_expert, pad_ends = _moe_route(topk_idx, segs, n_blocks)
    block_expert = block_expert.reshape(n_blocks)
    pad_ends = pad_ends.reshape(N_EXPERTS)
    xs = _moe_dispatch(h, lpos, ptab, pad_ends, n_blocks)
    ys = _moe_experts(xs, block_expert, pad_ends, expert_w_gate_up[l], expert_b_gate_up[l][:, None, :],
                      expert_w_down[l], expert_b_down[l][:, None, :])
    out = _moe_combine(ys, ptab, lpos.T, topk_w.T, x1, final_norm_g[None])
    return out.reshape(B, S, D)
```

```python
import functools
import math

import numpy as np
import jax
import jax.numpy as jnp
from jax import lax
from jax.experimental import pallas as pl
from jax.experimental.pallas import tpu as pltpu

D_MODEL = 1024
GRID_W = 64
MEM_HEADS = 4
MEM_HEAD_DIM = 128
HEAD_DIM = 64
NA_HEADS = 8
NA_WIN_ROWS = 8
NA_WIN_COLS = 16
DIL_GROUPS = ((128, 1), (512, 4), (2048, 16))
DIL_HEADS_PER_GROUP = 4
DIL_BLOCK = 128
T5_BUCKETS = 32
T5_MAX_DIST = 1024
N_EXPERTS = 32
TOP_K = 4
D_FF = D_MODEL
SWIGLU_ALPHA = 1.702
SWIGLU_LIMIT = 7.0
RMS_EPS = 1e-6
NEG_INF = -1e30

WIDTH_A = NA_HEADS * HEAD_DIM
WIDTH_B = DIL_HEADS_PER_GROUP * len(DIL_GROUPS) * HEAD_DIM
WIDTH_G = DIL_HEADS_PER_GROUP * HEAD_DIM
WIDTH_C = MEM_HEADS * MEM_HEAD_DIM
OFF_QA, OFF_KA, OFF_VA = 0, WIDTH_A, 2 * WIDTH_A
OFF_QB = 3 * WIDTH_A
OFF_KB = OFF_QB + WIDTH_B
OFF_VB = OFF_KB + WIDTH_B
OFF_QC = OFF_VB + WIDTH_B
OFF_GATE = OFF_QC + WIDTH_C
IN_COLS = OFF_GATE + 3 * D_MODEL

LANES = 128
MXU_WIDTH = 256
VMEM_LIMIT = 56 * 1024 * 1024

IN_PROJ_TILE = 1024
PROJ_TILE = 512
NA_ROWS_PER_STEP = 16
DIL_STEP_TOKENS = 2048
MOE_BLOCK = 512
MOE_TILE = 256
MOE_PIECE = 8
MOE_TILE_ROWS = -(-(TOP_K * MOE_TILE + N_EXPERTS * (MOE_PIECE - 1)) // MXU_WIDTH) * MXU_WIDTH
MOE_PIECE_TAB = 256
assert MOE_TILE_ROWS // MOE_PIECE < MOE_PIECE_TAB
MOE_SPARE_BLOCKS = -(-2 * MOE_TILE_ROWS // MOE_BLOCK)


def _cparams(n_axes):
    return pltpu.CompilerParams(
        dimension_semantics=("arbitrary",) * n_axes, vmem_limit_bytes=VMEM_LIMIT)


def _rms(x, g):
    return x * lax.rsqrt(jnp.mean(x * x, axis=-1, keepdims=True) + RMS_EPS) * g


def _sigmoid(x):
    return 1.0 / (1.0 + jnp.exp(-x))


def _pack_bf16_pairs(x):
    w = x.shape[1] // 2
    bits = lax.bitcast_convert_type(x.astype(jnp.bfloat16).astype(jnp.float32), jnp.uint32)
    return (bits[:, w:] & jnp.uint32(0xFFFF0000)) | (bits[:, :w] >> 16)


def _unpack_bf16_pairs(p):
    lo = lax.bitcast_convert_type(p << 16, jnp.float32)
    hi = lax.bitcast_convert_type(p & jnp.uint32(0xFFFF0000), jnp.float32)
    return jnp.concatenate([lo, hi], axis=1).astype(jnp.bfloat16)


def _dot(a, b):
    return jnp.dot(a, b, preferred_element_type=jnp.float32)


def _dot_nt(a, b):
    return lax.dot_general(a, b, (((1,), (1,)), ((), ())), preferred_element_type=jnp.float32)


def _mem_kv_kernel(mem_ref, g_ref, w_ref, out_ref):
    h = _rms(mem_ref[...], g_ref[...]).astype(jnp.bfloat16)
    out_ref[...] = _dot(h, w_ref[...]).astype(jnp.bfloat16)


def _mem_kv(mem2d, g, w):
    rows = mem2d.shape[0]
    return pl.pallas_call(
        _mem_kv_kernel,
        out_shape=jax.ShapeDtypeStruct((rows, 2 * WIDTH_C), jnp.bfloat16),
        name="mem_kv",
        compiler_params=pltpu.CompilerParams(vmem_limit_bytes=VMEM_LIMIT),
    )(mem2d, g, w)


def _in_proj_kernel(x_ref, g_ref, w_ref, na_ref, d0_ref, d1_ref, d2_ref, qc_ref, acc_ref):
    T = x_ref.shape[0]
    h = _rms(x_ref[...], g_ref[...]).astype(jnp.bfloat16)
    q_scale = HEAD_DIM ** -0.5

    for c in range(3):
        acc = _dot(h, w_ref[:, c * WIDTH_A:(c + 1) * WIDTH_A])
        if c == 0:
            acc = acc * q_scale
        na_ref[:, c * WIDTH_A:(c + 1) * WIDTH_A] = acc.astype(jnp.bfloat16)

    for gi, ((_, dil), d_ref) in enumerate(zip(DIL_GROUPS, (d0_ref, d1_ref, d2_ref))):
        for kind, off in enumerate((OFF_QB, OFF_KB, OFF_VB)):
            lo = off + gi * WIDTH_G
            acc = _dot(h, w_ref[:, lo:lo + WIDTH_G])
            if kind == 0:
                acc = acc * q_scale
            cols = slice(kind * WIDTH_G, (kind + 1) * WIDTH_G)
            if dil == 1:
                d_ref[0, 0, :, cols] = acc.astype(jnp.bfloat16)
            else:
                for c in range(WIDTH_G // LANES):
                    acc_ref[c] = acc[:, c * LANES:(c + 1) * LANES]
                for rho in range(dil):
                    d_ref[0, rho, :, cols] = jnp.concatenate(
                        [acc_ref[c, pl.ds(rho, T // dil, stride=dil), :] for c in range(WIDTH_G // LANES)],
                        axis=-1).astype(jnp.bfloat16)

    qc_ref[...] = _dot(h, w_ref[:, OFF_QC:OFF_QC + WIDTH_C]).astype(jnp.bfloat16)


def _in_proj(x2d, g, w_bf16, batch, seq):
    n = x2d.shape[0]
    T = IN_PROJ_TILE
    tiles_per_batch = seq // T
    out_shape = [jax.ShapeDtypeStruct((n, 3 * WIDTH_A), jnp.bfloat16)]
    out_specs = [pl.BlockSpec((T, 3 * WIDTH_A), lambda i: (i, 0))]
    for _, dil in DIL_GROUPS:
        out_shape.append(jax.ShapeDtypeStruct((batch, dil, seq // dil, 3 * WIDTH_G), jnp.bfloat16))
        out_specs.append(pl.BlockSpec((1, dil, T // dil, 3 * WIDTH_G),
                                      lambda i: (i // tiles_per_batch, 0, i % tiles_per_batch, 0)))
    out_shape += [jax.ShapeDtypeStruct((n, WIDTH_C), jnp.bfloat16)]
    out_specs += [pl.BlockSpec((T, WIDTH_C), lambda i: (i, 0))]
    return pl.pallas_call(
        _in_proj_kernel,
        grid=(n // T,),
        in_specs=[pl.BlockSpec((T, D_MODEL), lambda i: (i, 0)),
                  pl.BlockSpec((1, D_MODEL), lambda i: (0, 0)),
                  pl.BlockSpec((D_MODEL, OFF_GATE), lambda i: (0, 0), pipeline_mode=pl.Buffered(1))],
        out_specs=out_specs,
        out_shape=out_shape,
        scratch_shapes=[pltpu.VMEM((WIDTH_G // LANES, T, LANES), jnp.float32)],
        compiler_params=_cparams(1),
        name="in_proj",
    )(x2d, g, w_bf16)


def _na_kernel(q_ref, k_ref, v_ref, bias_ref, out_ref, *, rows):
    i = pl.program_id(1)
    lane = lax.broadcasted_iota(jnp.int32, (GRID_W, LANES), 1)
    low_half = lane < HEAD_DIM
    n_win = NA_WIN_ROWS * GRID_W

    def row_body(j, carry):
        qr = i * NA_ROWS_PER_STEP + j
        rs = jnp.clip(qr - NA_WIN_ROWS // 2, 0, rows - NA_WIN_ROWS)
        shift = rs - qr + NA_WIN_ROWS - 1
        q_off = pl.multiple_of(j * GRID_W, GRID_W)
        k_off = pl.multiple_of(rs * GRID_W, GRID_W)
        scores = []
        for hp in range(NA_HEADS // 2):
            cols = slice(hp * LANES, (hp + 1) * LANES)
            qp = q_ref[pl.ds(q_off, GRID_W), cols]
            kp = k_ref[pl.ds(k_off, n_win), cols]
            zero = jnp.zeros_like(qp)
            q2 = jnp.concatenate([jnp.where(low_half, qp, zero), jnp.where(low_half, zero, qp)], axis=0)
            bias = jnp.concatenate(
                [jnp.concatenate([bias_ref[shift + 2 * p, 2 * hp + hh] for p in range(NA_WIN_ROWS // 2)], axis=-1)
                 for hh in range(2)], axis=0)
            scores.append(_dot_nt(q2, kp) + bias)
        probs = []
        for s in scores:
            e = jnp.exp(s - jnp.max(s, axis=-1, keepdims=True))
            probs.append((e.astype(jnp.bfloat16), jnp.sum(e, axis=-1, keepdims=True)))
        for hp, (p, l) in enumerate(probs):
            cols = slice(hp * LANES, (hp + 1) * LANES)
            o2 = _dot(p, v_ref[pl.ds(k_off, n_win), cols]) / l
            out_ref[pl.ds(q_off, GRID_W), cols] = jnp.where(
                low_half, o2[:GRID_W], o2[GRID_W:]).astype(jnp.bfloat16)
        return carry

    lax.fori_loop(0, NA_ROWS_PER_STEP, row_body, 0, unroll=4)


def _na_attn(na_qkv, bias_tab, batch, seq):
    rows = seq // GRID_W
    steps = rows // NA_ROWS_PER_STEP
    tq = NA_ROWS_PER_STEP * GRID_W
    n = na_qkv.shape[0]
    return pl.pallas_call(
        functools.partial(_na_kernel, rows=rows),
        grid=(batch, steps),
        in_specs=[pl.BlockSpec((tq, WIDTH_A), lambda b, i: (b * steps + i, 0)),
                  pl.BlockSpec((seq, WIDTH_A), lambda b, i: (b, 1)),
                  pl.BlockSpec((seq, WIDTH_A), lambda b, i: (b, 2)),
                  pl.BlockSpec(bias_tab.shape, lambda b, i: (0, 0, 0, 0))],
        out_specs=pl.BlockSpec((tq, WIDTH_A), lambda b, i: (b * steps + i, 0)),
        out_shape=jax.ShapeDtypeStruct((n, WIDTH_A), jnp.bfloat16),
        compiler_params=_cparams(2),
        name="na_attn",
    )(na_qkv, na_qkv, na_qkv, bias_tab)


def _t5_bucket_np(rel):
    half = T5_BUCKETS // 2
    max_exact = half // 2
    ret = np.where(rel > 0, half, 0)
    n = np.abs(rel)
    nf = np.maximum(n, 1).astype(np.float32)
    large = max_exact + (np.log(nf / np.float32(max_exact)) / np.float32(math.log(T5_MAX_DIST / max_exact))
                         * np.float32(half - max_exact)).astype(np.int32)
    large = np.minimum(large, half - 1)
    return ret + np.where(n < max_exact, n, large)


def _dil_bucket_table():
    q = np.arange(DIL_BLOCK)[:, None]
    j = np.arange(2 * DIL_BLOCK)[None, :]
    tabs = []
    for window, dil in DIL_GROUPS:
        half = (window // 2) // dil
        assert half == DIL_BLOCK // 2
        rel = (j - half) - q
        tabs.append(np.where(np.abs(rel) <= half, _t5_bucket_np(rel * dil), -1))
    return np.stack(tabs).astype(np.int32)


def _bias_prep_kernel(rpb_ref, t5_ref, bucket_ref, na_ref, dil_ref):
    n_dc = 2 * NA_WIN_COLS - 1
    n_dr = 2 * NA_WIN_ROWS - 1
    qc = lax.broadcasted_iota(jnp.int32, (GRID_W, 2 * GRID_W), 0)
    lane = lax.broadcasted_iota(jnp.int32, (GRID_W, 2 * GRID_W), 1)
    second = lane >= GRID_W
    kc = jnp.where(second, lane - GRID_W, lane)
    diff = kc - qc
    cs = jnp.clip(qc - NA_WIN_COLS // 2, 0, GRID_W - NA_WIN_COLS)
    valid = jnp.logical_and(kc >= cs, kc < cs + NA_WIN_COLS)

    def na_tile(mh, carry):
        m = mh // NA_HEADS
        h = mh - m * NA_HEADS
        base = (h * n_dr + m) * n_dc
        acc = jnp.zeros((GRID_W, 2 * GRID_W), jnp.float32)
        for d in range(n_dc):
            v = jnp.where(second, rpb_ref[base + n_dc + d], rpb_ref[base + d])
            acc = jnp.where(diff == d - (NA_WIN_COLS - 1), v, acc)
        na_ref[m, h] = jnp.where(valid, acc, NEG_INF)
        return carry

    lax.fori_loop(0, (n_dr - 1) * NA_HEADS, na_tile, 0)

    j = lax.broadcasted_iota(jnp.int32, (DIL_BLOCK, 2 * DIL_BLOCK), 1)
    half = DIL_BLOCK // 2
    n_heads = DIL_HEADS_PER_GROUP * len(DIL_GROUPS)
    for gi in range(len(DIL_GROUPS)):
        bucket = bucket_ref[gi]
        in_win = bucket >= 0
        for h in range(DIL_HEADS_PER_GROUP):
            acc = jnp.zeros((DIL_BLOCK, 2 * DIL_BLOCK), jnp.float32)
            for b in range(T5_BUCKETS):
                acc = jnp.where(bucket == b, t5_ref[b * n_heads + gi * DIL_HEADS_PER_GROUP + h], acc)
            for v, ok in enumerate((jnp.logical_and(in_win, j >= half), in_win,
                                    jnp.logical_and(in_win, j < 2 * DIL_BLOCK - half))):
                dil_ref[gi, v, h] = jnp.where(ok, acc, NEG_INF)


def _bias_prep(rpb, t5):
    smem = pl.BlockSpec(memory_space=pltpu.SMEM)
    return pl.pallas_call(
        _bias_prep_kernel,
        in_specs=[smem, smem, pl.BlockSpec(memory_space=pltpu.VMEM)],
        out_shape=[jax.ShapeDtypeStruct((2 * NA_WIN_ROWS - 2, NA_HEADS, GRID_W, 2 * GRID_W), jnp.float32),
                   jax.ShapeDtypeStruct((len(DIL_GROUPS), 3, DIL_HEADS_PER_GROUP, DIL_BLOCK, 2 * DIL_BLOCK),
                                        jnp.float32)],
        compiler_params=pltpu.CompilerParams(vmem_limit_bytes=VMEM_LIMIT),
        name="bias_prep",
    )(rpb.reshape(-1), t5.reshape(-1), jnp.asarray(_dil_bucket_table()))


def _dil_kernel(q_ref, kp_ref, kc_ref, kn_ref, vp_ref, vc_ref, vn_ref, bias_ref, o_ref, lse_ref, *, dil):
    lane = lax.broadcasted_iota(jnp.int32, (DIL_BLOCK, LANES), 1)
    low_half = lane < HEAD_DIM
    hb = DIL_BLOCK // 2

    span = pl.program_id(1)
    n_spans = pl.num_programs(1)
    nbs = kc_ref.shape[2] // DIL_BLOCK

    def window(prev_ref, cur_ref, next_ref, rho, blk, cols):
        lo = blk * DIL_BLOCK
        first = prev_ref[0, rho, hb:, cols] if blk == 0 else cur_ref[0, rho, lo - hb:lo, cols]
        last = (next_ref[0, rho, :hb, cols] if blk == nbs - 1
                else cur_ref[0, rho, lo + DIL_BLOCK:lo + DIL_BLOCK + hb, cols])
        return jnp.concatenate([first, cur_ref[0, rho, lo:lo + DIL_BLOCK, cols], last], axis=0)

    def unit(rho, blk):
        variant = 1
        if blk == nbs - 1:
            variant = jnp.where(span == n_spans - 1, 2, variant)
        if blk == 0:
            variant = jnp.where(span == 0, 0, variant)
        scores = []
        for hp in range(DIL_HEADS_PER_GROUP // 2):
            cols = slice(hp * LANES, (hp + 1) * LANES)
            qp = q_ref[0, rho, blk * DIL_BLOCK:(blk + 1) * DIL_BLOCK, cols]
            zero = jnp.zeros_like(qp)
            q2 = jnp.concatenate([jnp.where(low_half, qp, zero), jnp.where(low_half, zero, qp)], axis=0)
            kw = window(kp_ref, kc_ref, kn_ref, rho, blk, cols)
            bias = jnp.concatenate([bias_ref[variant, 2 * hp], bias_ref[variant, 2 * hp + 1]], axis=0)
            scores.append(_dot_nt(q2, kw) + bias)
        probs = []
        for s in scores:
            m = jnp.max(s, axis=-1, keepdims=True)
            e = jnp.exp(s - m)
            probs.append((e.astype(jnp.bfloat16), m, jnp.sum(e, axis=-1, keepdims=True)))
        for hp, (p, m, l) in enumerate(probs):
            cols = slice(hp * LANES, (hp + 1) * LANES)
            o2 = _dot(p, window(vp_ref, vc_ref, vn_ref, rho, blk, cols)) / l
            lse2 = jnp.broadcast_to(m + jnp.log(l), (2 * DIL_BLOCK, LANES))
            o_pair = jnp.where(low_half, o2[:DIL_BLOCK], o2[DIL_BLOCK:])
            lse_pair = jnp.where(low_half, lse2[:DIL_BLOCK], lse2[DIL_BLOCK:])
            if dil == 1:
                rows = slice(blk * DIL_BLOCK, (blk + 1) * DIL_BLOCK)
                o_ref[hp, rows, :] = o_pair
                lse_ref[hp, rows, :] = lse_pair
            else:
                rows = pl.ds(blk * DIL_BLOCK * dil + rho, DIL_BLOCK, stride=dil)
                o_ref[hp, rows, :] = o_pair
                lse_ref[hp, rows, :] = lse_pair

    def class_body(rho, carry):
        for blk in range(nbs):
            unit(rho, blk)
        return carry

    if dil == 1:
        class_body(0, 0)
    else:
        lax.fori_loop(0, dil, class_body, 0, unroll=4 if nbs == 1 else 2)


def _dil_attn(qkv, bias_tab, group, dil, batch, seq):
    L = seq // dil
    nb = L // DIL_BLOCK
    assert nb >= 2
    tq = DIL_STEP_TOKENS
    span = tq // dil
    nbs = span // DIL_BLOCK
    n_spans = L // span

    def cur(col):
        return pl.BlockSpec((1, dil, span, WIDTH_G), lambda b, n: (b, 0, n, col))

    def edge(col, shift):
        return pl.BlockSpec((1, dil, DIL_BLOCK, WIDTH_G),
                            lambda b, n: (b, 0, jnp.clip(n * nbs + (shift if shift < 0 else nbs), 0, nb - 1), col))

    n_pairs = WIDTH_G // LANES
    out_spec = pl.BlockSpec((n_pairs, tq, LANES), lambda b, n: (0, b * n_spans + n, 0))
    return pl.pallas_call(
        functools.partial(_dil_kernel, dil=dil),
        grid=(batch, n_spans),
        in_specs=[cur(0), edge(1, -1), cur(1), edge(1, 1), edge(2, -1), cur(2), edge(2, 1),
                  pl.BlockSpec((None, 3, DIL_HEADS_PER_GROUP, DIL_BLOCK, 2 * DIL_BLOCK),
                               lambda b, n: (group, 0, 0, 0, 0))],
        out_specs=[out_spec, out_spec],
        out_shape=[jax.ShapeDtypeStruct((n_pairs, batch * seq, LANES), jnp.float32)] * 2,
        compiler_params=_cparams(2),
        name=f"dil_attn_d{dil}",
    )(qkv, qkv, qkv, qkv, qkv, qkv, qkv, bias_tab)


def _mix_kernel(x_ref, na_ref, o0_ref, l0_ref, o1_ref, l1_ref, o2_ref, l2_ref, qc_ref, ag_ref, wg_ref,
                mkv_ref, wa_ref, wb_ref, wc_ref, wo_ref, fg_ref, rw_ref, rb_ref,
                x1_ref, h_ref, idx_ref, wt_ref, seg_ref):
    T = x_ref.shape[0]
    hx = _rms(x_ref[...], ag_ref[...]).astype(jnp.bfloat16)
    gate_first = OFF_GATE - IN_COLS // 2

    def gate(b):
        lo = gate_first + b * D_MODEL
        return _sigmoid(_dot(hx, wg_ref[:, lo:lo + D_MODEL])).astype(jnp.bfloat16)

    gate_a = gate(0)
    gate_b = gate(1)
    ob = []
    for hp in range(WIDTH_G // LANES):
        l0, l1, l2 = l0_ref[hp], l1_ref[hp], l2_ref[hp]
        m = jnp.maximum(jnp.maximum(l0, l1), l2)
        e0, e1, e2 = jnp.exp(l0 - m), jnp.exp(l1 - m), jnp.exp(l2 - m)
        ob.append((e0 * o0_ref[hp] + e1 * o1_ref[hp] + e2 * o2_ref[hp]) / (e0 + e1 + e2))
    ob = jnp.concatenate(ob, axis=-1)

    mem_scale = MEM_HEAD_DIM ** -0.5
    scores = []
    for h in range(MEM_HEADS):
        cols = slice(h * MEM_HEAD_DIM, (h + 1) * MEM_HEAD_DIM)
        scores.append(_dot_nt(qc_ref[:, cols], mkv_ref[:, cols]) * mem_scale)
    gate_c = gate(2)
    probs = []
    for s in scores:
        e = jnp.exp(s - jnp.max(s, axis=-1, keepdims=True))
        probs.append((e.astype(jnp.bfloat16), jnp.sum(e, axis=-1, keepdims=True)))
    oc = []
    for h, (p, l) in enumerate(probs):
        oc.append(_dot(p, mkv_ref[:, WIDTH_C + h * MEM_HEAD_DIM:WIDTH_C + (h + 1) * MEM_HEAD_DIM]) / l)
    oc = jnp.concatenate(oc, axis=-1).astype(jnp.bfloat16)

    y_a = _dot(na_ref[...], wa_ref[...])
    y_b = _dot(ob.astype(jnp.bfloat16), wb_ref[...])
    y_c = _dot(oc, wc_ref[...])
    merged = (gate_a.astype(jnp.float32) * y_a + gate_b.astype(jnp.float32) * y_b
              + gate_c.astype(jnp.float32) * y_c)
    x1 = x_ref[...] + _dot(merged.astype(jnp.bfloat16), wo_ref[...])
    x1_ref[...] = x1

    h = _rms(x1, fg_ref[...])
    h_ref[...] = h.astype(jnp.bfloat16)
    logits = lax.dot_general(rw_ref[...], h, (((1,), (1,)), ((), ())), preferred_element_type=jnp.float32,
                             precision=lax.Precision.HIGHEST) + rb_ref[...]
    expert = lax.broadcasted_iota(jnp.int32, (N_EXPERTS, T), 0)
    vals, idxs = [], []
    for _ in range(TOP_K):
        mx = jnp.max(logits, axis=0, keepdims=True)
        sel = jnp.min(jnp.where(logits == mx, expert, N_EXPERTS), axis=0, keepdims=True)
        vals.append(mx)
        idxs.append(sel)
        logits = jnp.where(expert == sel, -jnp.inf, logits)
    ev = [jnp.exp(v - vals[0]) for v in vals]
    den = ev[0] + ev[1] + ev[2] + ev[3]
    idx_ref[...] = jnp.concatenate(idxs, axis=0)
    wt_ref[...] = jnp.concatenate([e / den for e in ev], axis=0)
    member = sum((expert == sel).astype(jnp.float32) for sel in idxs)
    for s in range(T // MOE_TILE):
        cnt = jnp.sum(member[:, s * MOE_TILE:(s + 1) * MOE_TILE], axis=1, keepdims=True)
        seg = jnp.floor((cnt + (MOE_PIECE - 1)) * (1.0 / MOE_PIECE)) * MOE_PIECE
        seg_ref[s] = jnp.broadcast_to(seg, (N_EXPERTS, LANES))


def _mix_out(x2d, na_out, dil_outs, qc, attn_g, w_in, mkv, wa, wb, wc, wo, ffn_g, router_w, router_b, batch, seq):
    n = x2d.shape[0]
    T = PROJ_TILE
    tiles_per_batch = seq // T
    mem_len = mkv.shape[0] // batch
    row = lambda w: pl.BlockSpec((T, w), lambda i: (i, 0))
    full = lambda a: pl.BlockSpec(a.shape, lambda i: (0,) * a.ndim, pipeline_mode=pl.Buffered(1))
    in_specs = [row(D_MODEL), row(WIDTH_A)]
    args = [x2d, na_out]
    for o, l in dil_outs:
        pair_rows = pl.BlockSpec((WIDTH_G // LANES, T, LANES), lambda i: (0, i, 0))
        in_specs += [pair_rows, pair_rows]
        args += [o, l]
    assert (IN_COLS // 2) % LANES == 0 and IN_COLS // 2 <= OFF_GATE
    in_specs += [row(WIDTH_C), full(attn_g),
                 pl.BlockSpec((D_MODEL, IN_COLS // 2), lambda i: (0, 1), pipeline_mode=pl.Buffered(1)),
                 pl.BlockSpec((mem_len, 2 * WIDTH_C), lambda i: (i // tiles_per_batch, 0)),
                 full(wa), full(wb), full(wc), full(wo), full(ffn_g), full(router_w), full(router_b)]
    args += [qc, attn_g, w_in, mkv, wa, wb, wc, wo, ffn_g, router_w, router_b]
    return pl.pallas_call(
        _mix_kernel,
        grid=(n // T,),
        in_specs=in_specs,
        out_specs=[row(D_MODEL), row(D_MODEL), pl.BlockSpec((TOP_K, T), lambda i: (0, i)),
                   pl.BlockSpec((TOP_K, T), lambda i: (0, i)),
                   pl.BlockSpec((T // MOE_TILE, N_EXPERTS, LANES), lambda i: (i, 0, 0))],
        out_shape=[jax.ShapeDtypeStruct((n, D_MODEL), jnp.float32),
                   jax.ShapeDtypeStruct((n, D_MODEL), jnp.bfloat16),
                   jax.ShapeDtypeStruct((TOP_K, n), jnp.int32),
                   jax.ShapeDtypeStruct((TOP_K, n), jnp.float32),
                   jax.ShapeDtypeStruct((n // MOE_TILE, N_EXPERTS, LANES), jnp.float32)],
        compiler_params=_cparams(1),
        name="mix_out",
    )(*args)


def _route_kernel(idx_ref, segs_ref, lpos_ref, ptab_ref, be_ref, pe_ref, carry_ref, pstart_ref):
    i = pl.program_id(0)
    T = idx_ref.shape[1]
    n_blocks = be_ref.shape[1]
    expert = lax.broadcasted_iota(jnp.int32, (N_EXPERTS, T), 0)
    idx = idx_ref[...]
    onehots = [expert == idx[k:k + 1, :] for k in range(TOP_K)]
    member = sum(o.astype(jnp.float32) for o in onehots)
    seg_b = segs_ref[i]
    seg = seg_b[:, 0:1]
    r = lax.broadcasted_iota(jnp.int32, (N_EXPERTS, N_EXPERTS), 0)
    c = lax.broadcasted_iota(jnp.int32, (N_EXPERTS, N_EXPERTS), 1)
    lower_incl = (r >= c).astype(jnp.float32)

    def cumsum_experts(v):
        return jnp.dot(lower_incl, v, preferred_element_type=jnp.float32, precision=lax.Precision.HIGHEST)

    @pl.when(i == 0)
    def _():
        counts = jnp.sum(segs_ref[...], axis=0)
        padded = jnp.floor((counts + (MOE_BLOCK - 1)) * (1.0 / MOE_BLOCK)) * MOE_BLOCK
        pad_ends = cumsum_experts(padded)
        pstart_ref[...] = pad_ends - padded
        carry_ref[...] = jnp.zeros_like(carry_ref)
        pe_ref[...] = pad_ends[:, 0:1].astype(jnp.int32)
        blk_start = (lax.broadcasted_iota(jnp.int32, (N_EXPERTS, n_blocks), 1) * MOE_BLOCK).astype(jnp.float32)
        be = jnp.sum((pad_ends[:, 0:1] <= blk_start).astype(jnp.int32), axis=0, keepdims=True)
        be_ref[...] = jnp.minimum(be, N_EXPERTS - 1)

    local_end = cumsum_experts(seg_b)[:, 0:1]
    local_start = local_end - seg
    global_start = pstart_ref[:, 0:1] + carry_ref[:, 0:1]
    rr = lax.broadcasted_iota(jnp.int32, (T, T), 0)
    cc = lax.broadcasted_iota(jnp.int32, (T, T), 1)
    earlier = (rr < cc).astype(jnp.bfloat16)
    before = _dot(member.astype(jnp.bfloat16), earlier)
    base = local_start + before
    lpos = [jnp.sum(jnp.where(o, base, 0.0), axis=0, keepdims=True) for o in onehots]
    lpos_ref[...] = jnp.concatenate(lpos, axis=0).astype(jnp.int32)

    inv = 1.0 / MOE_PIECE
    piece = lax.broadcasted_iota(jnp.int32, (N_EXPERTS, MOE_PIECE_TAB), 1).astype(jnp.float32)
    first = local_start * inv
    inside = jnp.logical_and(piece >= first, piece < local_end * inv)
    dst = jnp.sum(jnp.where(inside, global_start * inv + piece - first, 0.0), axis=0, keepdims=True)
    n_pieces = local_end[N_EXPERTS - 1:N_EXPERTS, :] * inv
    spare = (n_blocks * MOE_BLOCK // MOE_PIECE
             + (i % 2) * (MOE_TILE_ROWS // MOE_PIECE)).astype(jnp.float32)
    dst = jnp.where(piece[0:1, :] >= n_pieces, spare + piece[0:1, :], dst)
    dst = jnp.where(piece[0:1, :] == MOE_PIECE_TAB - 1, n_pieces, dst)
    ptab_ref[...] = dst.astype(jnp.int32)
    carry_ref[...] += seg


def _moe_route(topk_idx, segs, n_blocks):
    n = topk_idx.shape[1]
    T = MOE_TILE
    acc = pltpu.VMEM((N_EXPERTS, LANES), jnp.float32)
    return pl.pallas_call(
        _route_kernel,
        grid=(n // T,),
        in_specs=[pl.BlockSpec((TOP_K, T), lambda i: (0, i)),
                  pl.BlockSpec(segs.shape, lambda i: (0, 0, 0))],
        out_specs=[pl.BlockSpec((TOP_K, T), lambda i: (0, i)),
                   pl.BlockSpec((None, 1, MOE_PIECE_TAB), lambda i: (i, 0, 0)),
                   pl.BlockSpec((1, n_blocks), lambda i: (0, 0)),
                   pl.BlockSpec((N_EXPERTS, 1), lambda i: (0, 0))],
        out_shape=[jax.ShapeDtypeStruct((TOP_K, n), jnp.int32),
                   jax.ShapeDtypeStruct((n // T, 1, MOE_PIECE_TAB), jnp.int32),
                   jax.ShapeDtypeStruct((1, n_blocks), jnp.int32),
                   jax.ShapeDtypeStruct((N_EXPERTS, 1), jnp.int32)],
        scratch_shapes=[acc, acc],
        compiler_params=_cparams(1),
        name="moe_route",
    )(topk_idx, segs)


def _piece(ref, p):
    start = p * MOE_PIECE if isinstance(p, int) else pl.multiple_of(p * MOE_PIECE, MOE_PIECE)
    return ref.at[pl.ds(start, MOE_PIECE), :]


def _dispatch_kernel(pe_ref, ptab_ref, lpos_ref, h_ref, xs_hbm, zbuf, pbuf, sem, zsem):
    i = pl.program_id(0)
    T = h_ref.shape[0]
    half = h_ref.shape[1] // 2

    def zero_copy(e):
        end = pe_ref[e]
        start = pl.multiple_of(end - MOE_BLOCK, MOE_BLOCK)
        return pltpu.make_async_copy(zbuf, xs_hbm.at[pl.ds(start, MOE_BLOCK), :], zsem)

    def nonempty(e):
        return pe_ref[e] > (pe_ref[e - 1] if e else 0)

    @pl.when(i == 0)
    def _():
        zbuf[...] = jnp.zeros_like(zbuf)
        for e in range(N_EXPERTS):
            @pl.when(nonempty(e))
            def _():
                zero_copy(e).start()
        for e in range(N_EXPERTS):
            @pl.when(nonempty(e))
            def _():
                zero_copy(e).wait()

        def tail_copy(b):
            return pltpu.make_async_copy(
                zbuf, xs_hbm.at[pl.ds(pl.multiple_of(b * MOE_BLOCK, MOE_BLOCK), MOE_BLOCK), :], zsem)
        first_tail = pe_ref[N_EXPERTS - 1] // MOE_BLOCK
        n_blocks = xs_hbm.shape[0] // MOE_BLOCK
        lax.fori_loop(first_tail, n_blocks, lambda b, c: (tail_copy(b).start(), c)[1], 0)
        lax.fori_loop(first_tail, n_blocks, lambda b, c: (tail_copy(b).wait(), c)[1], 0)

    row = lax.broadcasted_iota(jnp.int32, (MOE_TILE_ROWS, T), 0)
    lpos = lpos_ref[...]
    perm = sum((row == lpos[k:k + 1, :]).astype(jnp.bfloat16) for k in range(TOP_K))
    lo = lax.bitcast_convert_type(_dot(perm, h_ref[:, :half]), jnp.uint32)
    hi = lax.bitcast_convert_type(_dot(perm, h_ref[:, half:]), jnp.uint32)
    slot = i % 2
    pbuf[slot] = hi | (lo >> 16)

    n_tile_pieces = MOE_TILE_ROWS // MOE_PIECE

    def piece_copy(s, p):
        return pltpu.make_async_copy(_piece(pbuf.at[s], p), _piece(xs_hbm, ptab_ref[p]), sem.at[s])

    def wait_all(s):
        for _ in range(n_tile_pieces):
            piece_copy(s, 0).wait()

    for p in range(n_tile_pieces):
        piece_copy(slot, p).start()

    @pl.when(i > 0)
    def _():
        wait_all(1 - slot)

    @pl.when(i == pl.num_programs(0) - 1)
    def _():
        wait_all(slot)


def _moe_dispatch(h, lpos, ptab, pad_ends, n_blocks):
    n, width = h.shape
    T = MOE_TILE
    n_steps = n // T

    def kernel(pe_ref, ptab_ref, *rest):
        _dispatch_kernel(pe_ref, ptab_ref.at[0], *rest)

    grid_spec = pltpu.PrefetchScalarGridSpec(
        num_scalar_prefetch=1,
        grid=(n_steps,),
        in_specs=[pl.BlockSpec((None, 1, MOE_PIECE_TAB), lambda i, pe: (i, 0, 0), memory_space=pltpu.SMEM),
                  pl.BlockSpec((TOP_K, T), lambda i, pe: (0, i)),
                  pl.BlockSpec((T, width), lambda i, pe: (i, 0))],
        out_specs=pl.BlockSpec(memory_space=pl.ANY),
        scratch_shapes=[pltpu.VMEM((MOE_BLOCK, width // 2), jnp.uint32),
                        pltpu.VMEM((2, MOE_TILE_ROWS, width // 2), jnp.uint32),
                        pltpu.SemaphoreType.DMA((2,)),
                        pltpu.SemaphoreType.DMA(())],
    )
    return pl.pallas_call(
        kernel,
        grid_spec=grid_spec,
        out_shape=jax.ShapeDtypeStruct(((n_blocks + MOE_SPARE_BLOCKS) * MOE_BLOCK, width // 2), jnp.uint32),
        compiler_params=_cparams(1),
        name="moe_dispatch",
    )(pad_ends, ptab, lpos, h)


def _moe_kernel(be_ref, pe_ref, xs_ref, wgu_ref, bgu_ref, wdn_ref, bdn_ref, y_ref):
    del be_ref
    i = pl.program_id(0)
    n_valid = pe_ref[N_EXPERTS - 1] // MOE_BLOCK

    @pl.when(i < n_valid)
    def _():
        x = _unpack_bf16_pairs(xs_ref[...])
        gu = _dot(x, wgu_ref[0]) + bgu_ref[0]
        g = jnp.minimum(gu[:, :D_FF], SWIGLU_LIMIT)
        u = jnp.clip(gu[:, D_FF:], -SWIGLU_LIMIT, SWIGLU_LIMIT)
        a = (u + 1.0) * (g * _sigmoid(SWIGLU_ALPHA * g))
        y_ref[...] = _pack_bf16_pairs(_dot(a.astype(jnp.bfloat16), wdn_ref[0]) + bdn_ref[0])

    @pl.when(i >= n_valid)
    def _():
        y_ref[...] = jnp.zeros_like(y_ref)


def _moe_experts(xs, block_expert, pad_ends, wgu, bgu, wdn, bdn):
    n_blocks = block_expert.shape[0]

    def row_blk(i, be, pe):
        return (jnp.minimum(i, jnp.maximum(pe[N_EXPERTS - 1] // MOE_BLOCK - 1, 0)), 0)

    grid_spec = pltpu.PrefetchScalarGridSpec(
        num_scalar_prefetch=2,
        grid=(n_blocks,),
        in_specs=[pl.BlockSpec((MOE_BLOCK, D_MODEL // 2), row_blk),
                  pl.BlockSpec((1, D_MODEL, 2 * D_FF), lambda i, be, pe: (be[i], 0, 0)),
                  pl.BlockSpec((1, 1, 2 * D_FF), lambda i, be, pe: (be[i], 0, 0)),
                  pl.BlockSpec((1, D_FF, D_MODEL), lambda i, be, pe: (be[i], 0, 0)),
                  pl.BlockSpec((1, 1, D_MODEL), lambda i, be, pe: (be[i], 0, 0))],
        out_specs=pl.BlockSpec((MOE_BLOCK, D_MODEL // 2), lambda i, be, pe: (i, 0)),
    )
    return pl.pallas_call(
        _moe_kernel,
        grid_spec=grid_spec,
        out_shape=jax.ShapeDtypeStruct((n_blocks * MOE_BLOCK, D_MODEL // 2), jnp.uint32),
        compiler_params=_cparams(1),
        name="moe_experts",
    )(block_expert, pad_ends, xs, wgu, bgu, wdn, bdn)


def _gather_pieces(src_hbm, ptab_ref, dst_ref, sem):
    spare = src_hbm.shape[0] // MOE_PIECE

    for p in range(MOE_TILE_ROWS // MOE_PIECE):
        src = ptab_ref[p]
        src = jnp.where(src >= spare, 0, src)
        pltpu.make_async_copy(_piece(src_hbm, src), _piece(dst_ref, p), sem).start()


def _wait_pieces(src_hbm, dst_ref, sem):
    for _ in range(MOE_TILE_ROWS // MOE_PIECE):
        pltpu.make_async_copy(_piece(src_hbm, 0), _piece(dst_ref, 0), sem).wait()


def _combine_kernel(ptab_ref, ptab_next_ref, ys_hbm, lpos_ref, wt_ref, x1_ref, g_ref, out_ref, gbuf, sem):
    i = pl.program_id(0)
    n_steps = pl.num_programs(0)
    slot = i % 2
    T = x1_ref.shape[0]

    @pl.when(i == 0)
    def _():
        _gather_pieces(ys_hbm, ptab_ref, gbuf.at[0], sem.at[0])

    @pl.when(i + 1 < n_steps)
    def _():
        _gather_pieces(ys_hbm, ptab_next_ref, gbuf.at[1 - slot], sem.at[1 - slot])

    _wait_pieces(ys_hbm, gbuf.at[slot], sem.at[slot])
    rows = _unpack_bf16_pairs(gbuf[slot])
    col = lax.broadcasted_iota(jnp.int32, (T, MOE_TILE_ROWS), 1)
    lpos = lpos_ref[...]
    wt = wt_ref[...]
    cmb = jnp.zeros((T, MOE_TILE_ROWS), jnp.float32)
    for k in range(TOP_K):
        cmb = jnp.where(col == lpos[:, k:k + 1], wt[:, k:k + 1], cmb)
    hi = cmb.astype(jnp.bfloat16)
    lo = (cmb - hi.astype(jnp.float32)).astype(jnp.bfloat16)
    y2 = _dot(jnp.concatenate([hi, lo], axis=0), rows)
    y = x1_ref[...] + y2[:T] + y2[T:]
    out_ref[...] = _rms(y, g_ref[...])


def _moe_combine(ys, ptab, lpos_t, wts, x1, final_g):
    n = x1.shape[0]
    T = MOE_TILE
    n_steps = n // T
    smem_blk = lambda shift: pl.BlockSpec(
        (None, 1, MOE_PIECE_TAB), lambda i: (jnp.minimum(i + shift, n_steps - 1), 0, 0),
        memory_space=pltpu.SMEM)

    def kernel(ptab_ref, ptab_next_ref, *rest):
        _combine_kernel(ptab_ref.at[0], ptab_next_ref.at[0], *rest)

    return pl.pallas_call(
        kernel,
        grid=(n_steps,),
        in_specs=[smem_blk(0), smem_blk(1),
                  pl.BlockSpec(memory_space=pl.ANY),
                  pl.BlockSpec((T, TOP_K), lambda i: (i, 0)),
                  pl.BlockSpec((T, TOP_K), lambda i: (i, 0)),
                  pl.BlockSpec((T, D_MODEL), lambda i: (i, 0)),
                  pl.BlockSpec((1, D_MODEL), lambda i: (0, 0))],
        out_specs=pl.BlockSpec((T, D_MODEL), lambda i: (i, 0)),
        out_shape=jax.ShapeDtypeStruct((n, D_MODEL), jnp.float32),
        scratch_shapes=[pltpu.VMEM((2, MOE_TILE_ROWS, D_MODEL // 2), jnp.uint32),
                        pltpu.SemaphoreType.DMA((2,))],
        compiler_params=_cparams(1),
        name="moe_combine",
    )(ptab, ptab, ys, lpos_t, wts, x1, final_g)


def kernel(x, mem, attn_norm_g, mem_norm_g, w_in, w_mem_kv, na_rpb, t5_rel_bias, w_branch_a, w_branch_b,
           w_branch_c, w_out, ffn_norm_g, router_w, router_b, expert_w_gate_up, expert_b_gate_up,
           expert_w_down, expert_b_down, final_norm_g):
    B, S, D = x.shape
    depth = w_in.shape[0]
    x2d = x.reshape(B * S, D)
    mem2d = mem.reshape(B * mem.shape[1], D)
    assert depth == 1, "single-layer block (per-layer parameters carry a leading axis of 1)"
    l = 0
    na_bias, dil_bias = _bias_prep(na_rpb[l], t5_rel_bias)
    mkv = _mem_kv(mem2d, mem_norm_g[l][None], w_mem_kv[l])
    na_qkv, d0, d1, d2, qc = _in_proj(x2d, attn_norm_g[l][None], w_in[l], B, S)
    na_out = _na_attn(na_qkv, na_bias, B, S)
    dil_outs = [_dil_attn(qkv, dil_bias, gi, dil, B, S)
                for gi, (qkv, (_, dil)) in enumerate(zip((d0, d1, d2), DIL_GROUPS))]
    x1, h, topk_idx, topk_w, segs = _mix_out(
        x2d, na_out, dil_outs, qc, attn_norm_g[l][None], w_in[l], mkv, w_branch_a[l], w_branch_b[l],
        w_branch_c[l], w_out[l], ffn_norm_g[l][None], router_w[l].T, router_b[l][:, None], B, S)

    n = B * S
    max_rows = n * TOP_K + (n // MOE_TILE) * N_EXPERTS * (MOE_PIECE - 1)
    n_blocks = -(-max_rows // MOE_BLOCK) + N_EXPERTS
    lpos, ptab, block_expert, pad_ends = _moe_route(topk_idx, segs, n_blocks)
    block_expert = block_expert.reshape(n_blocks)
    pad_ends = pad_ends.reshape(N_EXPERTS)
    xs = _moe_dispatch(h, lpos, ptab, pad_ends, n_blocks)
    ys = _moe_experts(xs, block_expert, pad_ends, expert_w_gate_up[l], expert_b_gate_up[l][:, None, :],
                      expert_w_down[l], expert_b_down[l][:, None, :])
    out = _moe_combine(ys, ptab, lpos.T, topk_w.T, x1, final_norm_g[None])
    return out.reshape(B, S, D)
```

```python
import functools
import math

import numpy as np
import jax
import jax.numpy as jnp
from jax import lax
from jax.experimental import pallas as pl
from jax.experimental.pallas import tpu as pltpu

D_MODEL = 1024
GRID_W = 64
MEM_HEADS = 4
MEM_HEAD_DIM = 128
HEAD_DIM = 64
NA_HEADS = 8
NA_WIN_ROWS = 8
NA_WIN_COLS = 16
DIL_GROUPS = ((128, 1), (512, 4), (2048, 16))
DIL_HEADS_PER_GROUP = 4
DIL_BLOCK = 128
T5_BUCKETS = 32
T5_MAX_DIST = 1024
N_EXPERTS = 32
TOP_K = 4
D_FF = D_MODEL
SWIGLU_ALPHA = 1.702
SWIGLU_LIMIT = 7.0
RMS_EPS = 1e-6
NEG_INF = -1e30

WIDTH_A = NA_HEADS * HEAD_DIM
WIDTH_B = DIL_HEADS_PER_GROUP * len(DIL_GROUPS) * HEAD_DIM
WIDTH_G = DIL_HEADS_PER_GROUP * HEAD_DIM
WIDTH_C = MEM_HEADS * MEM_HEAD_DIM
OFF_QA, OFF_KA, OFF_VA = 0, WIDTH_A, 2 * WIDTH_A
OFF_QB = 3 * WIDTH_A
OFF_KB = OFF_QB + WIDTH_B
OFF_VB = OFF_KB + WIDTH_B
OFF_QC = OFF_VB + WIDTH_B
OFF_GATE = OFF_QC + WIDTH_C
IN_COLS = OFF_GATE + 3 * D_MODEL

LANES = 128
MXU_WIDTH = 256
VMEM_LIMIT = 56 * 1024 * 1024

IN_PROJ_TILE = 1024
PROJ_TILE = 512
NA_ROWS_PER_STEP = 16
DIL_STEP_TOKENS = 2048
MOE_BLOCK = 512
MOE_TILE = 256
MOE_PIECE = 8
MOE_TILE_ROWS = -(-(TOP_K * MOE_TILE + N_EXPERTS * (MOE_PIECE - 1)) // MXU_WIDTH) * MXU_WIDTH
MOE_PIECE_TAB = 256
assert MOE_TILE_ROWS // MOE_PIECE < MOE_PIECE_TAB
MOE_SPARE_BLOCKS = -(-2 * MOE_TILE_ROWS // MOE_BLOCK)


def _cparams(n_axes):
    return pltpu.CompilerParams(
        dimension_semantics=("arbitrary",) * n_axes, vmem_limit_bytes=VMEM_LIMIT)


def _rms(x, g):
    return x * lax.rsqrt(jnp.mean(x * x, axis=-1, keepdims=True) + RMS_EPS) * g


def _sigmoid(x):
    return 1.0 / (1.0 + jnp.exp(-x))


def _pack_bf16_pairs(x):
    w = x.shape[1] // 2
    bits = lax.bitcast_convert_type(x.astype(jnp.bfloat16).astype(jnp.float32), jnp.uint32)
    return (bits[:, w:] & jnp.uint32(0xFFFF0000)) | (bits[:, :w] >> 16)


def _unpack_bf16_pairs(p):
    lo = lax.bitcast_convert_type(p << 16, jnp.float32)
    hi = lax.bitcast_convert_type(p & jnp.uint32(0xFFFF0000), jnp.float32)
    return jnp.concatenate([lo, hi], axis=1).astype(jnp.bfloat16)


def _dot(a, b):
    return jnp.dot(a, b, preferred_element_type=jnp.float32)


def _dot_nt(a, b):
    return lax.dot_general(a, b, (((1,), (1,)), ((), ())), preferred_element_type=jnp.float32)


def _mem_kv_kernel(mem_ref, g_ref, w_ref, out_ref):
    h = _rms(mem_ref[...], g_ref[...]).astype(jnp.bfloat16)
    out_ref[...] = _dot(h, w_ref[...]).astype(jnp.bfloat16)


def _mem_kv(mem2d, g, w):
    rows = mem2d.shape[0]
    return pl.pallas_call(
        _mem_kv_kernel,
        out_shape=jax.ShapeDtypeStruct((rows, 2 * WIDTH_C), jnp.bfloat16),
        name="mem_kv",
        compiler_params=pltpu.CompilerParams(vmem_limit_bytes=VMEM_LIMIT),
    )(mem2d, g, w)


def _in_proj_kernel(x_ref, g_ref, w_ref, na_ref, d0_ref, d1_ref, d2_ref, qc_ref, acc_ref):
    T = x_ref.shape[0]
    h = _rms(x_ref[...], g_ref[...]).astype(jnp.bfloat16)
    q_scale = HEAD_DIM ** -0.5

    for c in range(3):
        acc = _dot(h, w_ref[:, c * WIDTH_A:(c + 1) * WIDTH_A])
        if c == 0:
            acc = acc * q_scale
        na_ref[:, c * WIDTH_A:(c + 1) * WIDTH_A] = acc.astype(jnp.bfloat16)

    for gi, ((_, dil), d_ref) in enumerate(zip(DIL_GROUPS, (d0_ref, d1_ref, d2_ref))):
        for kind, off in enumerate((OFF_QB, OFF_KB, OFF_VB)):
            lo = off + gi * WIDTH_G
            acc = _dot(h, w_ref[:, lo:lo + WIDTH_G])
            if kind == 0:
                acc = acc * q_scale
            cols = slice(kind * WIDTH_G, (kind + 1) * WIDTH_G)
            if dil == 1:
                d_ref[0, 0, :, cols] = acc.astype(jnp.bfloat16)
            else:
                for c in range(WIDTH_G // LANES):
                    acc_ref[c] = acc[:, c * LANES:(c + 1) * LANES]
                for rho in range(dil):
                    d_ref[0, rho, :, cols] = jnp.concatenate(
                        [acc_ref[c, pl.ds(rho, T // dil, stride=dil), :] for c in range(WIDTH_G // LANES)],
                        axis=-1).astype(jnp.bfloat16)

    qc_ref[...] = _dot(h, w_ref[:, OFF_QC:OFF_QC + WIDTH_C]).astype(jnp.bfloat16)


def _in_proj(x2d, g, w_bf16, batch, seq):
    n = x2d.shape[0]
    T = IN_PROJ_TILE
    tiles_per_batch = seq // T
    out_shape = [jax.ShapeDtypeStruct((n, 3 * WIDTH_A), jnp.bfloat16)]
    out_specs = [pl.BlockSpec((T, 3 * WIDTH_A), lambda i: (i, 0))]
    for _, dil in DIL_GROUPS:
        out_shape.append(jax.ShapeDtypeStruct((batch, dil, seq // dil, 3 * WIDTH_G), jnp.bfloat16))
        out_specs.append(pl.BlockSpec((1, dil, T // dil, 3 * WIDTH_G),
                                      lambda i: (i // tiles_per_batch, 0, i % tiles_per_batch, 0)))
    out_shape += [jax.ShapeDtypeStruct((n, WIDTH_C), jnp.bfloat16)]
    out_specs += [pl.BlockSpec((T, WIDTH_C), lambda i: (i, 0))]
    return pl.pallas_call(
        _in_proj_kernel,
        grid=(n // T,),
        in_specs=[pl.BlockSpec((T, D_MODEL), lambda i: (i, 0)),
                  pl.BlockSpec((1, D_MODEL), lambda i: (0, 0)),
                  pl.BlockSpec((D_MODEL, OFF_GATE), lambda i: (0, 0), pipeline_mode=pl.Buffered(1))],
        out_specs=out_specs,
        out_shape=out_shape,
        scratch_shapes=[pltpu.VMEM((WIDTH_G // LANES, T, LANES), jnp.float32)],
        compiler_params=_cparams(1),
        name="in_proj",
    )(x2d, g, w_bf16)


def _na_kernel(q_ref, k_ref, v_ref, bias_ref, out_ref, *, rows):
    i = pl.program_id(1)
    lane = lax.broadcasted_iota(jnp.int32, (GRID_W, LANES), 1)
    low_half = lane < HEAD_DIM
    n_win = NA_WIN_ROWS * GRID_W

    def row_body(j, carry):
        qr = i * NA_ROWS_PER_STEP + j
        rs = jnp.clip(qr - NA_WIN_ROWS // 2, 0, rows - NA_WIN_ROWS)
        shift = rs - qr + NA_WIN_ROWS - 1
        q_off = pl.multiple_of(j * GRID_W, GRID_W)
        k_off = pl.multiple_of(rs * GRID_W, GRID_W)
        scores = []
        for hp in range(NA_HEADS // 2):
            cols = slice(hp * LANES, (hp + 1) * LANES)
            qp = q_ref[pl.ds(q_off, GRID_W), cols]
            kp = k_ref[pl.ds(k_off, n_win), cols]
            zero = jnp.zeros_like(qp)
            q2 = jnp.concatenate([jnp.where(low_half, qp, zero), jnp.where(low_half, zero, qp)], axis=0)
            bias = jnp.concatenate(
                [jnp.concatenate([bias_ref[shift + 2 * p, 2 * hp + hh] for p in range(NA_WIN_ROWS // 2)], axis=-1)
                 for hh in range(2)], axis=0)
            scores.append(_dot_nt(q2, kp) + bias)
        probs = []
        for s in scores:
            e = jnp.exp(s - jnp.max(s, axis=-1, keepdims=True))
            probs.append((e.astype(jnp.bfloat16), jnp.sum(e, axis=-1, keepdims=True)))
        for hp, (p, l) in enumerate(probs):
            cols = slice(hp * LANES, (hp + 1) * LANES)
            o2 = _dot(p, v_ref[pl.ds(k_off, n_win), cols]) / l
            out_ref[pl.ds(q_off, GRID_W), cols] = jnp.where(
                low_half, o2[:GRID_W], o2[GRID_W:]).astype(jnp.bfloat16)
        return carry

    lax.fori_loop(0, NA_ROWS_PER_STEP, row_body, 0, unroll=4)


def _na_attn(na_qkv, bias_tab, batch, seq):
    rows = seq // GRID_W
    steps = rows // NA_ROWS_PER_STEP
    tq = NA_ROWS_PER_STEP * GRID_W
    n = na_qkv.shape[0]
    return pl.pallas_call(
        functools.partial(_na_kernel, rows=rows),
        grid=(batch, steps),
        in_specs=[pl.BlockSpec((tq, WIDTH_A), lambda b, i: (b * steps + i, 0)),
                  pl.BlockSpec((seq, WIDTH_A), lambda b, i: (b, 1)),
                  pl.BlockSpec((seq, WIDTH_A), lambda b, i: (b, 2)),
                  pl.BlockSpec(bias_tab.shape, lambda b, i: (0, 0, 0, 0))],
        out_specs=pl.BlockSpec((tq, WIDTH_A), lambda b, i: (b * steps + i, 0)),
        out_shape=jax.ShapeDtypeStruct((n, WIDTH_A), jnp.bfloat16),
        compiler_params=_cparams(2),
        name="na_attn",
    )(na_qkv, na_qkv, na_qkv, bias_tab)


def _t5_bucket_np(rel):
    half = T5_BUCKETS // 2
    max_exact = half // 2
    ret = np.where(rel > 0, half, 0)
    n = np.abs(rel)
    nf = np.maximum(n, 1).astype(np.float32)
    large = max_exact + (np.log(nf / np.float32(max_exact)) / np.float32(math.log(T5_MAX_DIST / max_exact))
                         * np.float32(half - max_exact)).astype(np.int32)
    large = np.minimum(large, half - 1)
    return ret + np.where(n < max_exact, n, large)


def _dil_bucket_table():
    q = np.arange(DIL_BLOCK)[:, None]
    j = np.arange(2 * DIL_BLOCK)[None, :]
    tabs = []
    for window, dil in DIL_GROUPS:
        half = (window // 2) // dil
        assert half == DIL_BLOCK // 2
        rel = (j - half) - q
        tabs.append(np.where(np.abs(rel) <= half, _t5_bucket_np(rel * dil), -1))
    return np.stack(tabs).astype(np.int32)


def _bias_prep_kernel(rpb_ref, t5_ref, bucket_ref, na_ref, dil_ref):
    n_dc = 2 * NA_WIN_COLS - 1
    n_dr = 2 * NA_WIN_ROWS - 1
    qc = lax.broadcasted_iota(jnp.int32, (GRID_W, 2 * GRID_W), 0)
    lane = lax.broadcasted_iota(jnp.int32, (GRID_W, 2 * GRID_W), 1)
    second = lane >= GRID_W
    kc = jnp.where(second, lane - GRID_W, lane)
    diff = kc - qc
    cs = jnp.clip(qc - NA_WIN_COLS // 2, 0, GRID_W - NA_WIN_COLS)
    valid = jnp.logical_and(kc >= cs, kc < cs + NA_WIN_COLS)

    def na_tile(mh, carry):
        m = mh // NA_HEADS
        h = mh - m * NA_HEADS
        base = (h * n_dr + m) * n_dc
        acc = jnp.zeros((GRID_W, 2 * GRID_W), jnp.float32)
        for d in range(n_dc):
            v = jnp.where(second, rpb_ref[base + n_dc + d], rpb_ref[base + d])
            acc = jnp.where(diff == d - (NA_WIN_COLS - 1), v, acc)
        na_ref[m, h] = jnp.where(valid, acc, NEG_INF)
        return carry

    lax.fori_loop(0, (n_dr - 1) * NA_HEADS, na_tile, 0)

    j = lax.broadcasted_iota(jnp.int32, (DIL_BLOCK, 2 * DIL_BLOCK), 1)
    half = DIL_BLOCK // 2
    n_heads = DIL_HEADS_PER_GROUP * len(DIL_GROUPS)
    for gi in range(len(DIL_GROUPS)):
        bucket = bucket_ref[gi]
        in_win = bucket >= 0
        for h in range(DIL_HEADS_PER_GROUP):
            acc = jnp.zeros((DIL_BLOCK, 2 * DIL_BLOCK), jnp.float32)
            for b in range(T5_BUCKETS):
                acc = jnp.where(bucket == b, t5_ref[b * n_heads + gi * DIL_HEADS_PER_GROUP + h], acc)
            for v, ok in enumerate((jnp.logical_and(in_win, j >= half), in_win,
                                    jnp.logical_and(in_win, j < 2 * DIL_BLOCK - half))):
                dil_ref[gi, v, h] = jnp.where(ok, acc, NEG_INF)


def _bias_prep(rpb, t5):
    smem = pl.BlockSpec(memory_space=pltpu.SMEM)
    return pl.pallas_call(
        _bias_prep_kernel,
        in_specs=[smem, smem, pl.BlockSpec(memory_space=pltpu.VMEM)],
        out_shape=[jax.ShapeDtypeStruct((2 * NA_WIN_ROWS - 2, NA_HEADS, GRID_W, 2 * GRID_W), jnp.float32),
                   jax.ShapeDtypeStruct((len(DIL_GROUPS), 3, DIL_HEADS_PER_GROUP, DIL_BLOCK, 2 * DIL_BLOCK),
                                        jnp.float32)],
        compiler_params=pltpu.CompilerParams(vmem_limit_bytes=VMEM_LIMIT),
        name="bias_prep",
    )(rpb.reshape(-1), t5.reshape(-1), jnp.asarray(_dil_bucket_table()))


def _dil_kernel(q_ref, kp_ref, kc_ref, kn_ref, vp_ref, vc_ref, vn_ref, bias_ref, o_ref, lse_ref, *, dil):
    lane = lax.broadcasted_iota(jnp.int32, (DIL_BLOCK, LANES), 1)
    low_half = lane < HEAD_DIM
    hb = DIL_BLOCK // 2

    span = pl.program_id(1)
    n_spans = pl.num_programs(1)
    nbs = kc_ref.shape[2] // DIL_BLOCK

    def window(prev_ref, cur_ref, next_ref, rho, blk, cols):
        lo = blk * DIL_BLOCK
        first = prev_ref[0, rho, hb:, cols] if blk == 0 else cur_ref[0, rho, lo - hb:lo, cols]
        last = (next_ref[0, rho, :hb, cols] if blk == nbs - 1
                else cur_ref[0, rho, lo + DIL_BLOCK:lo + DIL_BLOCK + hb, cols])
        return jnp.concatenate([first, cur_ref[0, rho, lo:lo + DIL_BLOCK, cols], last], axis=0)

    def unit(rho, blk):
        variant = 1
        if blk == nbs - 1:
            variant = jnp.where(span == n_spans - 1, 2, variant)
        if blk == 0:
            variant = jnp.where(span == 0, 0, variant)
        scores = []
        for hp in range(DIL_HEADS_PER_GROUP // 2):
            cols = slice(hp * LANES, (hp + 1) * LANES)
            qp = q_ref[0, rho, blk * DIL_BLOCK:(blk + 1) * DIL_BLOCK, cols]
            zero = jnp.zeros_like(qp)
            q2 = jnp.concatenate([jnp.where(low_half, qp, zero), jnp.where(low_half, zero, qp)], axis=0)
            kw = window(kp_ref, kc_ref, kn_ref, rho, blk, cols)
            bias = jnp.concatenate([bias_ref[variant, 2 * hp], bias_ref[variant, 2 * hp + 1]], axis=0)
            scores.append(_dot_nt(q2, kw) + bias)
        probs = []
        for s in scores:
            m = jnp.max(s, axis=-1, keepdims=True)
            e = jnp.exp(s - m)
            probs.append((e.astype(jnp.bfloat16), m, jnp.sum(e, axis=-1, keepdims=True)))
        for hp, (p, m, l) in enumerate(probs):
            cols = slice(hp * LANES, (hp + 1) * LANES)
            o2 = _dot(p, window(vp_ref, vc_ref, vn_ref, rho, blk, cols)) / l
            lse2 = jnp.broadcast_to(m + jnp.log(l), (2 * DIL_BLOCK, LANES))
            o_pair = jnp.where(low_half, o2[:DIL_BLOCK], o2[DIL_BLOCK:])
            lse_pair = jnp.where(low_half, lse2[:DIL_BLOCK], lse2[DIL_BLOCK:])
            if dil == 1:
                rows = slice(blk * DIL_BLOCK, (blk + 1) * DIL_BLOCK)
                o_ref[hp, rows, :] = o_pair
                lse_ref[hp, rows, :] = lse_pair
            else:
                rows = pl.ds(blk * DIL_BLOCK * dil + rho, DIL_BLOCK, stride=dil)
                o_ref[hp, rows, :] = o_pair
                lse_ref[hp, rows, :] = lse_pair

    def class_body(rho, carry):
        for blk in range(nbs):
            unit(rho, blk)
        return carry

    if dil == 1:
        class_body(0, 0)
    else:
        lax.fori_loop(0, dil, class_body, 0, unroll=4 if nbs == 1 else 2)


def _dil_attn(qkv, bias_tab, group, dil, batch, seq):
    L = seq // dil
    nb = L // DIL_BLOCK
    assert nb >= 2
    tq = DIL_STEP_TOKENS
    span = tq // dil
    nbs = span // DIL_BLOCK
    n_spans = L // span

    def cur(col):
        return pl.BlockSpec((1, dil, span, WIDTH_G), lambda b, n: (b, 0, n, col))

    def edge(col, shift):
        return pl.BlockSpec((1, dil, DIL_BLOCK, WIDTH_G),
                            lambda b, n: (b, 0, jnp.clip(n * nbs + (shift if shift < 0 else nbs), 0, nb - 1), col))

    n_pairs = WIDTH_G // LANES
    out_spec = pl.BlockSpec((n_pairs, tq, LANES), lambda b, n: (0, b * n_spans + n, 0))
    return pl.pallas_call(
        functools.partial(_dil_kernel, dil=dil),
        grid=(batch, n_spans),
        in_specs=[cur(0), edge(1, -1), cur(1), edge(1, 1), edge(2, -1), cur(2), edge(2, 1),
                  pl.BlockSpec((None, 3, DIL_HEADS_PER_GROUP, DIL_BLOCK, 2 * DIL_BLOCK),
                               lambda b, n: (group, 0, 0, 0, 0))],
        out_specs=[out_spec, out_spec],
        out_shape=[jax.ShapeDtypeStruct((n_pairs, batch * seq, LANES), jnp.float32)] * 2,
        compiler_params=_cparams(2),
        name=f"dil_attn_d{dil}",
    )(qkv, qkv, qkv, qkv, qkv, qkv, qkv, bias_tab)


def _mix_kernel(x_ref, na_ref, o0_ref, l0_ref, o1_ref, l1_ref, o2_ref, l2_ref, qc_ref, ag_ref, wg_ref,
                mkv_ref, wa_ref, wb_ref, wc_ref, wo_ref, fg_ref, rw_ref, rb_ref,
                x1_ref, h_ref, idx_ref, wt_ref, seg_ref):
    T = x_ref.shape[0]
    hx = _rms(x_ref[...], ag_ref[...]).astype(jnp.bfloat16)
    gate_first = OFF_GATE - IN_COLS // 2

    def gate(b):
        lo = gate_first + b * D_MODEL
        return _sigmoid(_dot(hx, wg_ref[:, lo:lo + D_MODEL])).astype(jnp.bfloat16)

    gate_a = gate(0)
    gate_b = gate(1)
    ob = []
    for hp in range(WIDTH_G // LANES):
        l0, l1, l2 = l0_ref[hp], l1_ref[hp], l2_ref[hp]
        m = jnp.maximum(jnp.maximum(l0, l1), l2)
        e0, e1, e2 = jnp.exp(l0 - m), jnp.exp(l1 - m), jnp.exp(l2 - m)
        ob.append((e0 * o0_ref[hp] + e1 * o1_ref[hp] + e2 * o2_ref[hp]) / (e0 + e1 + e2))
    ob = jnp.concatenate(ob, axis=-1)

    mem_scale = MEM_HEAD_DIM ** -0.5
    scores = []
    for h in range(MEM_HEADS):
        cols = slice(h * MEM_HEAD_DIM, (h + 1) * MEM_HEAD_DIM)
        scores.append(_dot_nt(qc_ref[:, cols], mkv_ref[:, cols]) * mem_scale)
    gate_c = gate(2)
    probs = []
    for s in scores:
        e = jnp.exp(s - jnp.max(s, axis=-1, keepdims=True))
        probs.append((e.astype(jnp.bfloat16), jnp.sum(e, axis=-1, keepdims=True)))
    oc = []
    for h, (p, l) in enumerate(probs):
        oc.append(_dot(p, mkv_ref[:, WIDTH_C + h * MEM_HEAD_DIM:WIDTH_C + (h + 1) * MEM_HEAD_DIM]) / l)
    oc = jnp.concatenate(oc, axis=-1).astype(jnp.bfloat16)

    y_a = _dot(na_ref[...], wa_ref[...])
    y_b = _dot(ob.astype(jnp.bfloat16), wb_ref[...])
    y_c = _dot(oc, wc_ref[...])
    merged = (gate_a.astype(jnp.float32) * y_a + gate_b.astype(jnp.float32) * y_b
              + gate_c.astype(jnp.float32) * y_c)
    x1 = x_ref[...] + _dot(merged.astype(jnp.bfloat16), wo_ref[...])
    x1_ref[...] = x1

    h = _rms(x1, fg_ref[...])
    h_ref[...] = h.astype(jnp.bfloat16)
    logits = lax.dot_general(rw_ref[...], h, (((1,), (1,)), ((), ())), preferred_element_type=jnp.float32,
                             precision=lax.Precision.HIGHEST) + rb_ref[...]
    expert = lax.broadcasted_iota(jnp.int32, (N_EXPERTS, T), 0)
    vals, idxs = [], []
    for _ in range(TOP_K):
        mx = jnp.max(logits, axis=0, keepdims=True)
        sel = jnp.min(jnp.where(logits == mx, expert, N_EXPERTS), axis=0, keepdims=True)
        vals.append(mx)
        idxs.append(sel)
        logits = jnp.where(expert == sel, -jnp.inf, logits)
    ev = [jnp.exp(v - vals[0]) for v in vals]
    den = ev[0] + ev[1] + ev[2] + ev[3]
    idx_ref[...] = jnp.concatenate(idxs, axis=0)
    wt_ref[...] = jnp.concatenate([e / den for e in ev], axis=0)
    member = sum((expert == sel).astype(jnp.float32) for sel in idxs)
    for s in range(T // MOE_TILE):
        cnt = jnp.sum(member[:, s * MOE_TILE:(s + 1) * MOE_TILE], axis=1, keepdims=True)
        seg = jnp.floor((cnt + (MOE_PIECE - 1)) * (1.0 / MOE_PIECE)) * MOE_PIECE
        seg_ref[s] = jnp.broadcast_to(seg, (N_EXPERTS, LANES))


def _mix_out(x2d, na_out, dil_outs, qc, attn_g, w_in, mkv, wa, wb, wc, wo, ffn_g, router_w, router_b, batch, seq):
    n = x2d.shape[0]
    T = PROJ_TILE
    tiles_per_batch = seq // T
    mem_len = mkv.shape[0] // batch
    row = lambda w: pl.BlockSpec((T, w), lambda i: (i, 0))
    full = lambda a: pl.BlockSpec(a.shape, lambda i: (0,) * a.ndim, pipeline_mode=pl.Buffered(1))
    in_specs = [row(D_MODEL), row(WIDTH_A)]
    args = [x2d, na_out]
    for o, l in dil_outs:
        pair_rows = pl.BlockSpec((WIDTH_G // LANES, T, LANES), lambda i: (0, i, 0))
        in_specs += [pair_rows, pair_rows]
        args += [o, l]
    assert (IN_COLS // 2) % LANES == 0 and IN_COLS // 2 <= OFF_GATE
    in_specs += [row(WIDTH_C), full(attn_g),
                 pl.BlockSpec((D_MODEL, IN_COLS // 2), lambda i: (0, 1), pipeline_mode=pl.Buffered(1)),
                 pl.BlockSpec((mem_len, 2 * WIDTH_C), lambda i: (i // tiles_per_batch, 0)),
                 full(wa), full(wb), full(wc), full(wo), full(ffn_g), full(router_w), full(router_b)]
    args += [qc, attn_g, w_in, mkv, wa, wb, wc, wo, ffn_g, router_w, router_b]
    return pl.pallas_call(
        _mix_kernel,
        grid=(n // T,),
        in_specs=in_specs,
        out_specs=[row(D_MODEL), row(D_MODEL), pl.BlockSpec((TOP_K, T), lambda i: (0, i)),
                   pl.BlockSpec((TOP_K, T), lambda i: (0, i)),
                   pl.BlockSpec((T // MOE_TILE, N_EXPERTS, LANES), lambda i: (i, 0, 0))],
        out_shape=[jax.ShapeDtypeStruct((n, D_MODEL), jnp.float32),
                   jax.ShapeDtypeStruct((n, D_MODEL), jnp.bfloat16),
                   jax.ShapeDtypeStruct((TOP_K, n), jnp.int32),
                   jax.ShapeDtypeStruct((TOP_K, n), jnp.float32),
                   jax.ShapeDtypeStruct((n // MOE_TILE, N_EXPERTS, LANES), jnp.float32)],
        compiler_params=_cparams(1),
        name="mix_out",
    )(*args)


def _route_kernel(idx_ref, segs_ref, lpos_ref, ptab_ref, be_ref, pe_ref, carry_ref, pstart_ref):
    i = pl.program_id(0)
    T = idx_ref.shape[1]
    n_blocks = be_ref.shape[1]
    expert = lax.broadcasted_iota(jnp.int32, (N_EXPERTS, T), 0)
    idx = idx_ref[...]
    onehots = [expert == idx[k:k + 1, :] for k in range(TOP_K)]
    member = sum(o.astype(jnp.float32) for o in onehots)
    seg_b = segs_ref[i]
    seg = seg_b[:, 0:1]
    r = lax.broadcasted_iota(jnp.int32, (N_EXPERTS, N_EXPERTS), 0)
    c = lax.broadcasted_iota(jnp.int32, (N_EXPERTS, N_EXPERTS), 1)
    lower_incl = (r >= c).astype(jnp.float32)

    def cumsum_experts(v):
        return jnp.dot(lower_incl, v, preferred_element_type=jnp.float32, precision=lax.Precision.HIGHEST)

    @pl.when(i == 0)
    def _():
        counts = jnp.sum(segs_ref[...], axis=0)
        padded = jnp.floor((counts + (MOE_BLOCK - 1)) * (1.0 / MOE_BLOCK)) * MOE_BLOCK
        pad_ends = cumsum_experts(padded)
        pstart_ref[...] = pad_ends - padded
        carry_ref[...] = jnp.zeros_like(carry_ref)
        pe_ref[...] = pad_ends[:, 0:1].astype(jnp.int32)
        blk_start = (lax.broadcasted_iota(jnp.int32, (N_EXPERTS, n_blocks), 1) * MOE_BLOCK).astype(jnp.float32)
        be = jnp.sum((pad_ends[:, 0:1] <= blk_start).astype(jnp.int32), axis=0, keepdims=True)
        be_ref[...] = jnp.minimum(be, N_EXPERTS - 1)

    local_end = cumsum_experts(seg_b)[:, 0:1]
    local_start = local_end - seg
    global_start = pstart_ref[:, 0:1] + carry_ref[:, 0:1]
    rr = lax.broadcasted_iota(jnp.int32, (T, T), 0)
    cc = lax.broadcasted_iota(jnp.int32, (T, T), 1)
    earlier = (rr < cc).astype(jnp.bfloat16)
    before = _dot(member.astype(jnp.bfloat16), earlier)
    base = local_start + before
    lpos = [jnp.sum(jnp.where(o, base, 0.0), axis=0, keepdims=True) for o in onehots]
    lpos_ref[...] = jnp.concatenate(lpos, axis=0).astype(jnp.int32)

    inv = 1.0 / MOE_PIECE
    piece = lax.broadcasted_iota(jnp.int32, (N_EXPERTS, MOE_PIECE_TAB), 1).astype(jnp.float32)
    first = local_start * inv
    inside = jnp.logical_and(piece >= first, piece < local_end * inv)
    dst = jnp.sum(jnp.where(inside, global_start * inv + piece - first, 0.0), axis=0, keepdims=True)
    n_pieces = local_end[N_EXPERTS - 1:N_EXPERTS, :] * inv
    spare = (n_blocks * MOE_BLOCK // MOE_PIECE
             + (i % 2) * (MOE_TILE_ROWS // MOE_PIECE)).astype(jnp.float32)
    dst = jnp.where(piece[0:1, :] >= n_pieces, spare + piece[0:1, :], dst)
    dst = jnp.where(piece[0:1, :] == MOE_PIECE_TAB - 1, n_pieces, dst)
    ptab_ref[...] = dst.astype(jnp.int32)
    carry_ref[...] += seg


def _moe_route(topk_idx, segs, n_blocks):
    n = topk_idx.shape[1]
    T = MOE_TILE
    acc = pltpu.VMEM((N_EXPERTS, LANES), jnp.float32)
    return pl.pallas_call(
        _route_kernel,
        grid=(n // T,),
        in_specs=[pl.BlockSpec((TOP_K, T), lambda i: (0, i)),
                  pl.BlockSpec(segs.shape, lambda i: (0, 0, 0))],
        out_specs=[pl.BlockSpec((TOP_K, T), lambda i: (0, i)),
                   pl.BlockSpec((None, 1, MOE_PIECE_TAB), lambda i: (i, 0, 0)),
                   pl.BlockSpec((1, n_blocks), lambda i: (0, 0)),
                   pl.BlockSpec((N_EXPERTS, 1), lambda i: (0, 0))],
        out_shape=[jax.ShapeDtypeStruct((TOP_K, n), jnp.int32),
                   jax.ShapeDtypeStruct((n // T, 1, MOE_PIECE_TAB), jnp.int32),
                   jax.ShapeDtypeStruct((1, n_blocks), jnp.int32),
                   jax.ShapeDtypeStruct((N_EXPERTS, 1), jnp.int32)],
        scratch_shapes=[acc, acc],
        compiler_params=_cparams(1),
        name="moe_route",
    )(topk_idx, segs)


def _piece(ref, p):
    start = p * MOE_PIECE if isinstance(p, int) else pl.multiple_of(p * MOE_PIECE, MOE_PIECE)
    return ref.at[pl.ds(start, MOE_PIECE), :]


def _dispatch_kernel(pe_ref, ptab_ref, lpos_ref, h_ref, xs_hbm, zbuf, pbuf, sem, zsem):
    i = pl.program_id(0)
    T = h_ref.shape[0]
    half = h_ref.shape[1] // 2

    def zero_copy(e):
        end = pe_ref[e]
        start = pl.multiple_of(end - MOE_BLOCK, MOE_BLOCK)
        return pltpu.make_async_copy(zbuf, xs_hbm.at[pl.ds(start, MOE_BLOCK), :], zsem)

    def nonempty(e):
        return pe_ref[e] > (pe_ref[e - 1] if e else 0)

    @pl.when(i == 0)
    def _():
        zbuf[...] = jnp.zeros_like(zbuf)
        for e in range(N_EXPERTS):
            @pl.when(nonempty(e))
            def _():
                zero_copy(e).start()
        for e in range(N_EXPERTS):
            @pl.when(nonempty(e))
            def _():
                zero_copy(e).wait()

        def tail_copy(b):
            return pltpu.make_async_copy(
                zbuf, xs_hbm.at[pl.ds(pl.multiple_of(b * MOE_BLOCK, MOE_BLOCK), MOE_BLOCK), :], zsem)
        first_tail = pe_ref[N_EXPERTS - 1] // MOE_BLOCK
        n_blocks = xs_hbm.shape[0] // MOE_BLOCK
        lax.fori_loop(first_tail, n_blocks, lambda b, c: (tail_copy(b).start(), c)[1], 0)
        lax.fori_loop(first_tail, n_blocks, lambda b, c: (tail_copy(b).wait(), c)[1], 0)

    row = lax.broadcasted_iota(jnp.int32, (MOE_TILE_ROWS, T), 0)
    lpos = lpos_ref[...]
    perm = sum((row == lpos[k:k + 1, :]).astype(jnp.bfloat16) for k in range(TOP_K))
    lo = lax.bitcast_convert_type(_dot(perm, h_ref[:, :half]), jnp.uint32)
    hi = lax.bitcast_convert_type(_dot(perm, h_ref[:, half:]), jnp.uint32)
    slot = i % 2
    pbuf[slot] = hi | (lo >> 16)

    n_tile_pieces = MOE_TILE_ROWS // MOE_PIECE

    def piece_copy(s, p):
        return pltpu.make_async_copy(_piece(pbuf.at[s], p), _piece(xs_hbm, ptab_ref[p]), sem.at[s])

    def wait_all(s):
        for _ in range(n_tile_pieces):
            piece_copy(s, 0).wait()

    for p in range(n_tile_pieces):
        piece_copy(slot, p).start()

    @pl.when(i > 0)
    def _():
        wait_all(1 - slot)

    @pl.when(i == pl.num_programs(0) - 1)
    def _():
        wait_all(slot)


def _moe_dispatch(h, lpos, ptab, pad_ends, n_blocks):
    n, width = h.shape
    T = MOE_TILE
    n_steps = n // T

    def kernel(pe_ref, ptab_ref, *rest):
        _dispatch_kernel(pe_ref, ptab_ref.at[0], *rest)

    grid_spec = pltpu.PrefetchScalarGridSpec(
        num_scalar_prefetch=1,
        grid=(n_steps,),
        in_specs=[pl.BlockSpec((None, 1, MOE_PIECE_TAB), lambda i, pe: (i, 0, 0), memory_space=pltpu.SMEM),
                  pl.BlockSpec((TOP_K, T), lambda i, pe: (0, i)),
                  pl.BlockSpec((T, width), lambda i, pe: (i, 0))],
        out_specs=pl.BlockSpec(memory_space=pl.ANY),
        scratch_shapes=[pltpu.VMEM((MOE_BLOCK, width // 2), jnp.uint32),
                        pltpu.VMEM((2, MOE_TILE_ROWS, width // 2), jnp.uint32),
                        pltpu.SemaphoreType.DMA((2,)),
                        pltpu.SemaphoreType.DMA(())],
    )
    return pl.pallas_call(
        kernel,
        grid_spec=grid_spec,
        out_shape=jax.ShapeDtypeStruct(((n_blocks + MOE_SPARE_BLOCKS) * MOE_BLOCK, width // 2), jnp.uint32),
        compiler_params=_cparams(1),
        name="moe_dispatch",
    )(pad_ends, ptab, lpos, h)


def _moe_kernel(be_ref, pe_ref, xs_ref, wgu_ref, bgu_ref, wdn_ref, bdn_ref, y_ref):
    del be_ref
    i = pl.program_id(0)
    n_valid = pe_ref[N_EXPERTS - 1] // MOE_BLOCK

    @pl.when(i < n_valid)
    def _():
        x = _unpack_bf16_pairs(xs_ref[...])
        gu = _dot(x, wgu_ref[0]) + bgu_ref[0]
        g = jnp.minimum(gu[:, :D_FF], SWIGLU_LIMIT)
        u = jnp.clip(gu[:, D_FF:], -SWIGLU_LIMIT, SWIGLU_LIMIT)
        a = (u + 1.0) * (g * _sigmoid(SWIGLU_ALPHA * g))
        y_ref[...] = _pack_bf16_pairs(_dot(a.astype(jnp.bfloat16), wdn_ref[0]) + bdn_ref[0])

    @pl.when(i >= n_valid)
    def _():
        y_ref[...] = jnp.zeros_like(y_ref)


def _moe_experts(xs, block_expert, pad_ends, wgu, bgu, wdn, bdn):
    n_blocks = block_expert.shape[0]

    def row_blk(i, be, pe):
        return (jnp.minimum(i, jnp.maximum(pe[N_EXPERTS - 1] // MOE_BLOCK - 1, 0)), 0)

    grid_spec = pltpu.PrefetchScalarGridSpec(
        num_scalar_prefetch=2,
        grid=(n_blocks,),
        in_specs=[pl.BlockSpec((MOE_BLOCK, D_MODEL // 2), row_blk),
                  pl.BlockSpec((1, D_MODEL, 2 * D_FF), lambda i, be, pe: (be[i], 0, 0)),
                  pl.BlockSpec((1, 1, 2 * D_FF), lambda i, be, pe: (be[i], 0, 0)),
                  pl.BlockSpec((1, D_FF, D_MODEL), lambda i, be, pe: (be[i], 0, 0)),
                  pl.BlockSpec((1, 1, D_MODEL), lambda i, be, pe: (be[i], 0, 0))],
        out_specs=pl.BlockSpec((MOE_BLOCK, D_MODEL // 2), lambda i, be, pe: (i, 0)),
    )
    return pl.pallas_call(
        _moe_kernel,
        grid_spec=grid_spec,
        out_shape=jax.ShapeDtypeStruct((n_blocks * MOE_BLOCK, D_MODEL // 2), jnp.uint32),
        compiler_params=_cparams(1),
        name="moe_experts",
    )(block_expert, pad_ends, xs, wgu, bgu, wdn, bdn)


def _gather_pieces(src_hbm, ptab_ref, dst_ref, sem):
    spare = src_hbm.shape[0] // MOE_PIECE

    for p in range(MOE_TILE_ROWS // MOE_PIECE):
        src = ptab_ref[p]
        src = jnp.where(src >= spare, 0, src)
        pltpu.make_async_copy(_piece(src_hbm, src), _piece(dst_ref, p), sem).start()


def _wait_pieces(src_hbm, dst_ref, sem):
    for _ in range(MOE_TILE_ROWS // MOE_PIECE):
        pltpu.make_async_copy(_piece(src_hbm, 0), _piece(dst_ref, 0), sem).wait()


def _combine_kernel(ptab_ref, ptab_next_ref, ys_hbm, lpos_ref, wt_ref, x1_ref, g_ref, out_ref, gbuf, sem):
    i = pl.program_id(0)
    n_steps = pl.num_programs(0)
    slot = i % 2
    T = x1_ref.shape[0]

    @pl.when(i == 0)
    def _():
        _gather_pieces(ys_hbm, ptab_ref, gbuf.at[0], sem.at[0])

    _wait_pieces(ys_hbm, gbuf.at[slot], sem.at[slot])
    rows = _unpack_bf16_pairs(gbuf[slot])
    lpos = lpos_ref[...]
    wt = wt_ref[...]
    x1 = x1_ref[...]
    _gather_pieces(ys_hbm, ptab_next_ref, gbuf.at[1 - slot], sem.at[1 - slot])
    col = lax.broadcasted_iota(jnp.int32, (T, MOE_TILE_ROWS), 1)
    cmb = jnp.zeros((T, MOE_TILE_ROWS), jnp.float32)
    for k in range(TOP_K):
        cmb = jnp.where(col == lpos[:, k:k + 1], wt[:, k:k + 1], cmb)
    hi = cmb.astype(jnp.bfloat16)
    lo = (cmb - hi.astype(jnp.float32)).astype(jnp.bfloat16)
    y2 = _dot(jnp.concatenate([hi, lo], axis=0), rows)
    y = x1 + y2[:T] + y2[T:]
    out_ref[...] = _rms(y, g_ref[...])

    @pl.when(i == n_steps - 1)
    def _():
        _wait_pieces(ys_hbm, gbuf.at[1 - slot], sem.at[1 - slot])


def _moe_combine(ys, ptab, lpos_t, wts, x1, final_g):
    n = x1.shape[0]
    T = MOE_TILE
    n_steps = n // T
    smem_blk = lambda shift: pl.BlockSpec(
        (None, 1, MOE_PIECE_TAB), lambda i: (jnp.minimum(i + shift, n_steps - 1), 0, 0),
        memory_space=pltpu.SMEM)

    def kernel(ptab_ref, ptab_next_ref, *rest):
        _combine_kernel(ptab_ref.at[0], ptab_next_ref.at[0], *rest)

    return pl.pallas_call(
        kernel,
        grid=(n_steps,),
        in_specs=[smem_blk(0), smem_blk(1),
                  pl.BlockSpec(memory_space=pl.ANY),
                  pl.BlockSpec((T, TOP_K), lambda i: (i, 0)),
                  pl.BlockSpec((T, TOP_K), lambda i: (i, 0)),
                  pl.BlockSpec((T, D_MODEL), lambda i: (i, 0)),
                  pl.BlockSpec((1, D_MODEL), lambda i: (0, 0))],
        out_specs=pl.BlockSpec((T, D_MODEL), lambda i: (i, 0)),
        out_shape=jax.ShapeDtypeStruct((n, D_MODEL), jnp.float32),
        scratch_shapes=[pltpu.VMEM((2, MOE_TILE_ROWS, D_MODEL // 2), jnp.uint32),
                        pltpu.SemaphoreType.DMA((2,))],
        compiler_params=_cparams(1),
        name="moe_combine",
    )(ptab, ptab, ys, lpos_t, wts, x1, final_g)


def kernel(x, mem, attn_norm_g, mem_norm_g, w_in, w_mem_kv, na_rpb, t5_rel_bias, w_branch_a, w_branch_b,
           w_branch_c, w_out, ffn_norm_g, router_w, router_b, expert_w_gate_up, expert_b_gate_up,
           expert_w_down, expert_b_down, final_norm_g):
    B, S, D = x.shape
    depth = w_in.shape[0]
    x2d = x.reshape(B * S, D)
    mem2d = mem.reshape(B * mem.shape[1], D)
    assert depth == 1, "single-layer block (per-layer parameters carry a leading axis of 1)"
    l = 0
    na_bias, dil_bias = _bias_prep(na_rpb[l], t5_rel_bias)
    mkv = _mem_kv(mem2d, mem_norm_g[l][None], w_mem_kv[l])
    na_qkv, d0, d1, d2, qc = _in_proj(x2d, attn_norm_g[l][None], w_in[l], B, S)
    na_out = _na_attn(na_qkv, na_bias, B, S)
    dil_outs = [_dil_attn(qkv, dil_bias, gi, dil, B, S)
                for gi, (qkv, (_, dil)) in enumerate(zip((d0, d1, d2), DIL_GROUPS))]
    x1, h, topk_idx, topk_w, segs = _mix_out(
        x2d, na_out, dil_outs, qc, attn_norm_g[l][None], w_in[l], mkv, w_branch_a[l], w_branch_b[l],
        w_branch_c[l], w_out[l], ffn_norm_g[l][None], router_w[l].T, router_b[l][:, None], B, S)

    n = B * S
    max_rows = n * TOP_K + (n // MOE_TILE) * N_EXPERTS * (MOE_PIECE - 1)
    n_blocks = -(-max_rows // MOE_BLOCK) + N_EXPERTS
    lpos, ptab, block_expert, pad_ends = _moe_route(topk_idx, segs, n_blocks)
    block_expert = block_expert.reshape(n_blocks)
    pad_ends = pad_ends.reshape(N_EXPERTS)
    xs = _moe_dispatch(h, lpos, ptab, pad_ends, n_blocks)
    ys = _moe_experts(xs, block_expert, pad_ends, expert_w_gate_up[l], expert_b_gate_up[l][:, None, :],
                      expert_w_down[l], expert_b_down[l][:, None, :])
    out = _moe_combine(ys, ptab, lpos.T, topk_w.T, x1, final_norm_g[None])
    return out.reshape(B, S, D)
```
